```python
import math
import jax
import jax.numpy as jnp
from jax import lax
import numpy as np

D_MODEL = 1024
BATCH = 16
SEQ = 4096
DEPTH = 4

GRID_W = 64
CTX_LEN = 256
EPS = 1e-6
F32 = jnp.float32

D_HY = 512
HY_SHORT = 3
HY_POS_BANDS = 16
HY_POS_DIM = 1 + 2 * HY_POS_BANDS
HY_FFN = 64
HY_N_FILT = 4

GLA_HEADS = 4
GLA_DK = 64
GLA_DV = 128
GLA_QK = GLA_HEADS * GLA_DK
GLA_V = GLA_HEADS * GLA_DV
GLA_LOWRANK = 16
GLA_TAU = 16.0
GLA_CHUNK = 64

S5_CH = 512
S5_GROUP = 16
S5_GROUPS = S5_CH // S5_GROUP
S5_STATE = 64

D_FF = ((8 * D_MODEL + 3 * 256 - 1) // (3 * 256)) * 256

COL_SIZES = (GLA_QK, GLA_V, 2 * GLA_LOWRANK, S5_CH, GLA_QK, GLA_V, 3 * D_HY, 3 * D_MODEL)
N_STATE_COLS = GLA_QK + GLA_V + 2 * GLA_LOWRANK + S5_CH
N_IN = N_STATE_COLS + GLA_QK + GLA_V + 3 * D_HY + 3 * D_MODEL

kernel_name = 'hybrid_hyena_gla_s5_prefix_dit'


def rmsnorm(x, g):
    xf = x.astype(F32)
    y = xf * lax.rsqrt(jnp.mean(xf * xf, axis=-1, keepdims=True) + EPS)
    return (y * g.astype(F32)).astype(x.dtype)


def modulation(cond, w_mod, b_mod, n_chunks):
    cols = n_chunks * D_MODEL
    m = jax.nn.silu(cond) @ w_mod[:, :cols] + b_mod[:cols]
    return jnp.split(m[:, None, :], n_chunks, axis=-1)


def split_cols(p, sizes):
    idx = [int(s) for s in np.cumsum(sizes)[:-1]]
    return jnp.split(p, idx, axis=-1)


def short_conv(u, w):
    K = w.shape[0]
    pad = K // 2
    L = u.shape[1]
    up = jnp.pad(u, ((0, 0), (pad, pad), (0, 0)))
    return sum(up[:, j:j + L] * w[j] for j in range(K))


def hyena_filters(L, params):
    w1, b1, w2, b2, w3, freq1, freq2, decay = (p.astype(F32) for p in params)
    t = jnp.linspace(0.0, 1.0, L, dtype=F32)[:, None]
    bands = jnp.linspace(1e-4, HY_POS_BANDS - 1, HY_POS_BANDS, dtype=F32)
    ang = (2.0 * math.pi / L) * jnp.arange(L, dtype=F32)[:, None] * bands[None]
    z = jnp.concatenate([t, jnp.cos(ang), -jnp.sin(ang)], axis=-1)
    hid = jnp.sin(freq1 * (z @ w1 + b1))
    hid = jnp.sin(freq2 * (hid @ w2 + b2))
    h = (hid @ w3) * jnp.exp(-t * jnp.abs(decay))
    return h.reshape(L, HY_N_FILT, D_HY)


def long_conv_bidir(u, h_fwd, h_bwd, bias):
    L = u.shape[1]
    k = jnp.concatenate([h_fwd[:1] + h_bwd[:1], h_fwd[1:], jnp.zeros_like(h_fwd[:1]), h_bwd[:0:-1]], axis=0)
    k = k * lax.rsqrt(jnp.sum(k * k, axis=0, keepdims=True) + EPS)
    u_f = jnp.fft.rfft(u, n=2 * L, axis=1)
    k_f = jnp.fft.rfft(k, axis=0)
    y = jnp.fft.irfft(u_f * k_f[None], n=2 * L, axis=1)[:, :L]
    return y + u * bias


def hyena_mix(cols, conv_w, filt_params, bias):
    u = short_conv(cols.astype(F32), conv_w.astype(F32))
    x1, x2, v = jnp.split(u, 3, axis=-1)
    h = hyena_filters(cols.shape[1], filt_params)
    bias = bias.astype(F32)
    z = x1 * long_conv_bidir(v, h[:, 0], h[:, 1], bias[0])
    return x2 * long_conv_bidir(z, h[:, 2], h[:, 3], bias[1])


def to_heads(t, d):
    B, L, _ = t.shape
    return t.astype(F32).reshape(B, L, GLA_HEADS, d).transpose(0, 2, 1, 3)


def gla_decays(a_lr, wa2, ba):
    B, L, _ = a_lr.shape
    z = jnp.einsum('bldr,drk->bldk', a_lr.astype(F32).reshape(B, L, 2, GLA_LOWRANK), wa2.astype(F32)) + ba.astype(F32)
    log_a = jax.nn.log_sigmoid(z) / GLA_TAU
    return to_heads(log_a[:, :, 0], GLA_DK), to_heads(log_a[:, :, 1], GLA_DK)


def gla_scan(k, v, log_a, S0, q=None):
    B, H, L, DK = k.shape
    DV = v.shape[-1]
    C = GLA_CHUNK
    N = L // C
    blk = lambda t: t.reshape(B, H, N, C, t.shape[-1])
    k, v = blk(k), blk(v)
    b = jnp.cumsum(blk(log_a), axis=3)
    b_last = b[:, :, :, -1:]
    dS = jnp.einsum('bhncd,bhnce->bhnde', k * jnp.exp(b_last - b), v)
    decay = jnp.exp(b_last[:, :, :, 0])
    want = q is not None

    def step(S, inp):
        dec, ds = inp
        return dec[..., None] * S + ds, (S if want else None)

    S_fin, S_prev = lax.scan(step, S0, (jnp.moveaxis(decay, 2, 0), jnp.moveaxis(dS, 2, 0)))
    if not want:
        return None, S_fin
    q_dec = blk(q) * jnp.exp(b)
    att = jnp.einsum('bhncd,bhnsd->bhncs', q_dec, k * jnp.exp(-b))
    att = jnp.where(jnp.tril(jnp.ones((C, C), bool)), att, 0.0)
    o = jnp.einsum('bhncs,bhnse->bhnce', att, v) + jnp.einsum('bhncd,bhnde->bhnce', q_dec, jnp.moveaxis(S_prev, 0, 2))
    return o.reshape(B, H, L, DV), S_fin


def gla_bidir(k, v, la_f, la_b, S0_f, S0_b, q=None):
    fl = lambda t: jnp.flip(t, axis=2)
    o_f, S_f = gla_scan(k, v, la_f, S0_f, q)
    o_b, S_b = gla_scan(fl(k), fl(v), fl(la_b), S0_b, None if q is None else fl(q))
    o = None if q is None else o_f + fl(o_b)
    return o, S_f, S_b


def gla_readout(o, r, gnorm):
    o = rmsnorm(o, gnorm)
    B, H, L, DV = o.shape
    return o.transpose(0, 2, 1, 3).reshape(B, L, H * DV) * jax.nn.silu(r.astype(F32))


def s5_discretize(a_re, a_im, log_dt, b_re, b_im):
    lam = lax.complex(jnp.minimum(a_re.astype(F32), -1e-4), a_im.astype(F32))
    lam_dt = lam * jnp.exp(log_dt.astype(F32))[:, None]
    b_bar = ((jnp.exp(lam_dt) - 1.0) / lam)[..., None] * lax.complex(b_re.astype(F32), b_im.astype(F32))
    return lam_dt, b_bar


def s5_scan(u, lam_dt, b_bar, x0, c_mat=None):
    Bn, L, G, I = u.shape
    T = GRID_W
    rows = L // T
    lam_bar = jnp.exp(lam_dt)
    pows = jnp.exp(lam_dt[None] * jnp.arange(1, T + 1, dtype=F32)[:, None, None])
    u_blocks = jnp.moveaxis(u.reshape(Bn, rows, T, G, I), 1, 0)

    def combine(e1, e2):
        a1, b1 = e1
        a2, b2 = e2
        return a1 * a2, a2 * b1 + b2

    def step(x, u_blk):
        bu = jnp.einsum('btgi,gpi->btgp', u_blk.astype(jnp.complex64), b_bar)
        _, xs = lax.associative_scan(combine, (jnp.broadcast_to(lam_bar, bu.shape), bu), axis=1)
        xs = xs + pows[None] * x[:, None]
        y = None if c_mat is None else jnp.einsum('btgp,gip->btgi', xs, c_mat).real
        return xs[:, -1], y

    x_fin, ys = lax.scan(step, x0, u_blocks)
    if c_mat is None:
        return None, x_fin
    return jnp.moveaxis(ys, 0, 1).reshape(Bn, L, G, I), x_fin


def s5_bidir(u, disc_f, disc_b, x0_f, x0_b, c_mat=None):
    y_f, x_f = s5_scan(u, disc_f[0], disc_f[1], x0_f, c_mat)
    y_b, x_b = s5_scan(jnp.flip(u, 1), disc_b[0], disc_b[1], x0_b, c_mat)
    y = None if c_mat is None else y_f + jnp.flip(y_b, 1)
    return y, x_f, x_b


def to_groups(u):
    B, L, _ = u.shape
    return u.astype(F32).reshape(B, L, S5_GROUPS, S5_GROUP)


def s5_readout(y, u, d, w_glu, b_glu):
    B, L, _ = u.shape
    y = y.reshape(B, L, S5_CH) + d.astype(F32) * u.astype(F32)
    g = jax.nn.gelu(y)
    return g * jax.nn.sigmoid(g @ w_glu.astype(F32) + b_glu.astype(F32))


def merge_branches(y_hy, y_gla, y_s5, gates, w_hy, w_gla, w_s5, w_out):
    g_hy, g_gla, g_s5 = jnp.split(gates.astype(F32), 3, axis=-1)
    m = (jax.nn.sigmoid(g_hy) * (y_hy @ w_hy.astype(F32))
         + jax.nn.sigmoid(g_gla) * (y_gla @ w_gla.astype(F32))
         + jax.nn.sigmoid(g_s5) * (y_s5 @ w_s5.astype(F32)))
    return m @ w_out.astype(F32)


def swiglu(h, w_in, w_out):
    a, b = jnp.split(h @ w_in, 2, axis=-1)
    return (jax.nn.silu(a) * b) @ w_out


def setup_inputs(seed: int = 0) -> dict:
    key = jax.random.key(seed)
    ks = iter(jax.random.split(key, 48))
    nrm = lambda shape, scale: scale * jax.random.normal(next(ks), shape, F32)
    G, P, I = S5_GROUPS, S5_STATE, S5_GROUP
    n_idx = jnp.arange(P, dtype=F32)
    decay_init = jnp.linspace(math.log(1e-2) / 0.3, math.log(1e-2) / 1.5, HY_N_FILT * D_HY, dtype=F32)
    return {
        'x': nrm((BATCH, SEQ, D_MODEL), 1.0),
        'c': nrm((BATCH, D_MODEL), 1.0),
        'ctx': nrm((BATCH, CTX_LEN, D_MODEL), 1.0),
        'c_ctx': nrm((D_MODEL,), 1.0),
        'w_mod': nrm((DEPTH, D_MODEL, 6 * D_MODEL), 0.5 * D_MODEL ** -0.5),
        'b_mod': nrm((DEPTH, 6 * D_MODEL), 0.02),
        'g_norm1': 1.0 + nrm((DEPTH, D_MODEL), 0.02),
        'g_norm2': 1.0 + nrm((DEPTH, D_MODEL), 0.02),
        'w_in': nrm((DEPTH, D_MODEL, N_IN), D_MODEL ** -0.5),
        'hy_conv': nrm((DEPTH, HY_SHORT, 3 * D_HY), HY_SHORT ** -0.5),
        'hy_w1': nrm((DEPTH, HY_POS_DIM, HY_FFN), HY_POS_DIM ** -0.5),
        'hy_b1': nrm((DEPTH, HY_FFN), 0.02),
        'hy_w2': nrm((DEPTH, HY_FFN, HY_FFN), HY_FFN ** -0.5),
        'hy_b2': nrm((DEPTH, HY_FFN), 0.02),
        'hy_w3': nrm((DEPTH, HY_FFN, HY_N_FILT * D_HY), HY_FFN ** -0.5),
        'hy_freq1': 1.0 + nrm((DEPTH, HY_FFN), 0.02),
        'hy_freq2': 1.0 + nrm((DEPTH, HY_FFN), 0.02),
        'hy_decay': decay_init + nrm((DEPTH, HY_N_FILT * D_HY), 0.1),
        'hy_bias': nrm((DEPTH, 2, D_HY), 0.3),
        'gla_wa2': nrm((DEPTH, 2, GLA_LOWRANK, GLA_QK), GLA_LOWRANK ** -0.5),
        'gla_ba': nrm((DEPTH, 2, GLA_QK), 0.02),
        'gla_gnorm': 1.0 + nrm((DEPTH, GLA_DV), 0.02),
        's5_a_re': -0.5 + nrm((DEPTH, 2, G, P), 0.01),
        's5_a_im': math.pi * n_idx + nrm((DEPTH, 2, G, P), 0.01),
        's5_log_dt': jax.random.uniform(next(ks), (DEPTH, 2, G), F32, math.log(1e-3), math.log(1e-1)),
        's5_b_re': nrm((DEPTH, G, P, I), (2 * I) ** -0.5),
        's5_b_im': nrm((DEPTH, G, P, I), (2 * I) ** -0.5),
        's5_c_re': nrm((DEPTH, G, I, P), P ** -0.5),
        's5_c_im': nrm((DEPTH, G, I, P), P ** -0.5),
        's5_d': nrm((DEPTH, S5_CH), 0.5),
        's5_w_glu': nrm((DEPTH, S5_CH, S5_CH), S5_CH ** -0.5),
        's5_b_glu': nrm((DEPTH, S5_CH), 0.02),
        'w_br_hy': nrm((DEPTH, D_HY, D_MODEL), D_HY ** -0.5),
        'w_br_gla': nrm((DEPTH, GLA_V, D_MODEL), GLA_V ** -0.5),
        'w_br_s5': nrm((DEPTH, S5_CH, D_MODEL), S5_CH ** -0.5),
        'w_out': nrm((DEPTH, D_MODEL, D_MODEL), D_MODEL ** -0.5),
        'w_ffn_in': nrm((DEPTH, D_MODEL, 2 * D_FF), D_MODEL ** -0.5),
        'w_ffn_out': nrm((DEPTH, D_FF, D_MODEL), D_FF ** -0.5),
        'g_final': 1.0 + nrm((D_MODEL,), 0.02),
    }


def reference(x, c, ctx, c_ctx, w_mod, b_mod, g_norm1, g_norm2, w_in, hy_conv, hy_w1, hy_b1, hy_w2, hy_b2,
              hy_w3, hy_freq1, hy_freq2, hy_decay, hy_bias, gla_wa2, gla_ba, gla_gnorm, s5_a_re, s5_a_im,
              s5_log_dt, s5_b_re, s5_b_im, s5_c_re, s5_c_im, s5_d, s5_w_glu, s5_b_glu, w_br_hy, w_br_gla,
              w_br_s5, w_out, w_ffn_in, w_ffn_out, g_final):
    Bn = x.shape[0]
    zeros_S = jnp.zeros((Bn, GLA_HEADS, GLA_DK, GLA_DV), F32)
    zeros_x = jnp.zeros((Bn, S5_GROUPS, S5_STATE), jnp.complex64)
    q_scale = GLA_DK ** -0.5
    for l in range(DEPTH):
        last = l == DEPTH - 1
        sh1, sc1, gt1, sh2, sc2, gt2 = modulation(c, w_mod[l], b_mod[l], 6)
        mods_c = modulation(c_ctx[None], w_mod[l], b_mod[l], 2 if last else 6)
        h = rmsnorm(x, g_norm1[l]) * (1.0 + sc1) + sh1
        k_l, v_l, a_l, u_l, q_l, r_l, hy_l, gate_l = split_cols(h @ w_in[l], COL_SIZES)
        hc = rmsnorm(ctx, g_norm1[l]) * (1.0 + mods_c[1]) + mods_c[0]
        if last:
            pc = split_cols(hc @ w_in[l][:, :N_STATE_COLS], COL_SIZES[:4])
        else:
            pc = split_cols(hc @ w_in[l], COL_SIZES)

        lf_c, lb_c = gla_decays(pc[2], gla_wa2[l], gla_ba[l])
        q_c = None if last else to_heads(pc[4], GLA_DK) * q_scale
        o_c, S_f, S_b = gla_bidir(to_heads(pc[0], GLA_DK), to_heads(pc[1], GLA_DV), lf_c, lb_c, zeros_S, zeros_S, q_c)
        lf, lb = gla_decays(a_l, gla_wa2[l], gla_ba[l])
        o_l, _, _ = gla_bidir(to_heads(k_l, GLA_DK), to_heads(v_l, GLA_DV), lf, lb, S_f, S_b,
                              to_heads(q_l, GLA_DK) * q_scale)
        y_gla = gla_readout(o_l, r_l, gla_gnorm[l])

        disc_f = s5_discretize(s5_a_re[l, 0], s5_a_im[l, 0], s5_log_dt[l, 0], s5_b_re[l], s5_b_im[l])
        disc_b = s5_discretize(s5_a_re[l, 1], s5_a_im[l, 1], s5_log_dt[l, 1], s5_b_re[l], s5_b_im[l])
        c_mat = lax.complex(s5_c_re[l].astype(F32), s5_c_im[l].astype(F32))
        ys_c, x_f, x_b = s5_bidir(to_groups(pc[3]), disc_f, disc_b, zeros_x, zeros_x, None if last else c_mat)
        ys_l, _, _ = s5_bidir(to_groups(u_l), disc_f, disc_b, x_f, x_b, c_mat)
        y_s5 = s5_readout(ys_l, u_l, s5_d[l], s5_w_glu[l], s5_b_glu[l])

        hy_p = (hy_w1[l], hy_b1[l], hy_w2[l], hy_b2[l], hy_w3[l], hy_freq1[l], hy_freq2[l], hy_decay[l])
        y_hy = hyena_mix(hy_l, hy_conv[l], hy_p, hy_bias[l])

        mix = merge_branches(y_hy, y_gla, y_s5, gate_l, w_br_hy[l], w_br_gla[l], w_br_s5[l], w_out[l])
        x = x + (gt1 * mix).astype(x.dtype)
        h2 = rmsnorm(x, g_norm2[l]) * (1.0 + sc2) + sh2
        x = x + (gt2 * swiglu(h2, w_ffn_in[l], w_ffn_out[l])).astype(x.dtype)

        if not last:
            yc_hy = hyena_mix(pc[6], hy_conv[l], hy_p, hy_bias[l])
            yc_gla = gla_readout(o_c, pc[5], gla_gnorm[l])
            yc_s5 = s5_readout(ys_c, pc[3], s5_d[l], s5_w_glu[l], s5_b_glu[l])
            mix_c = merge_branches(yc_hy, yc_gla, yc_s5, pc[7], w_br_hy[l], w_br_gla[l], w_br_s5[l], w_out[l])
            ctx = ctx + (mods_c[2] * mix_c).astype(ctx.dtype)
            hc2 = rmsnorm(ctx, g_norm2[l]) * (1.0 + mods_c[4]) + mods_c[3]
            ctx = ctx + (mods_c[5] * swiglu(hc2, w_ffn_in[l], w_ffn_out[l])).astype(ctx.dtype)
    return rmsnorm(x, g_final)
```

```python
import functools
import math

import jax
import jax.numpy as jnp
import numpy as np
from jax import lax
from jax.experimental import pallas as pl
from jax.experimental.pallas import tpu as pltpu

F32 = jnp.float32
BF16 = jnp.bfloat16
EPS = 1e-6

GRID_W = 64
HY_SHORT = 3
HY_POS_BANDS = 16
HY_N_FILT = 4
GLA_HEADS = 4
GLA_DK = 64
GLA_DV = 128
GLA_QK = GLA_HEADS * GLA_DK
GLA_V = GLA_HEADS * GLA_DV
GLA_LOWRANK = 16
GLA_TAU = 16.0
GLA_CHUNK = 64
S5_GROUP = 16
S5_STATE = 64
A_PAD = 128

ROW_TILE = 256
VMEM_LIMIT = 56 * 1024 * 1024


def _const_spec(shape):
    nd = len(shape)
    return pl.BlockSpec(shape, lambda *_: (0,) * nd, pipeline_mode=pl.Buffered(1))


def _params(n_axes):
    return pltpu.CompilerParams(dimension_semantics=("parallel",) * n_axes, vmem_limit_bytes=VMEM_LIMIT)


def _mod_kernel(c_ref, w_ref, b_ref, o_ref):
    c = c_ref[...]
    s = c * jax.nn.sigmoid(c)
    o_ref[0] = jnp.dot(s.astype(BF16), w_ref[0], preferred_element_type=F32) + b_ref[0]


def _modulation_all(cond, w_mod, b_mod):
    depth, d, n = w_mod.shape
    r = cond.shape[0]
    tn = 1536
    return pl.pallas_call(
        _mod_kernel,
        grid=(depth, n // tn),
        in_specs=[
            pl.BlockSpec((r, d), lambda l, j: (0, 0)),
            pl.BlockSpec((1, d, tn), lambda l, j: (l, 0, j)),
            pl.BlockSpec((1, 1, tn), lambda l, j: (l, 0, j)),
        ],
        out_specs=pl.BlockSpec((1, r, tn), lambda l, j: (l, 0, j)),
        out_shape=jax.ShapeDtypeStruct((depth, r, n), F32),
        compiler_params=_params(2),
        name="modulation",
    )(cond, w_mod.astype(BF16), b_mod.reshape(depth, 1, n))


def _norm_mod(x, g, shift, scale):
    y = x * lax.rsqrt(jnp.mean(x * x, axis=-1, keepdims=True) + EPS)
    return (y * g) * (1.0 + scale) + shift


def _proj_kernel(col_sizes, d, x_ref, mod_ref, g_ref, w_ref, *o_refs):
    m = mod_ref[0, 0]
    h = _norm_mod(x_ref[0], g_ref[...], m[:, 0:d], m[:, d:2 * d]).astype(BF16)
    off = 0
    for o_ref, n in zip(o_refs, col_sizes):
        o_ref[0] = jnp.dot(h, w_ref[:, off:off + n], preferred_element_type=F32).astype(o_ref.dtype)
        off += n


def _project(xc, mods, g, w, col_sizes):
    b, t, d = xc.shape
    tm = ROW_TILE
    n_tot = sum(col_sizes)
    out_shape = [jax.ShapeDtypeStruct((b, t, n), BF16) for n in col_sizes]
    out_specs = [pl.BlockSpec((1, tm, n), lambda i, j: (i, j, 0)) for n in col_sizes]
    return pl.pallas_call(
        functools.partial(_proj_kernel, tuple(col_sizes), d),
        grid=(b, t // tm),
        in_specs=[
            pl.BlockSpec((1, tm, d), lambda i, j: (i, j, 0)),
            pl.BlockSpec((1, 1, 1, 6 * d), lambda i, j: (i, jnp.minimum(j, 1), 0, 0)),
            _const_spec((1, d)),
            _const_spec((d, n_tot)),
        ],
        out_specs=out_specs,
        out_shape=out_shape,
        compiler_params=_params(2),
        name="in_proj",
    )(xc, mods, g, w)


def _gelu_tanh(x):
    return 0.5 * x * (1.0 + jnp.tanh(math.sqrt(2.0 / math.pi) * (x + 0.044715 * (x * x * x))))


def _merge_kernel(d, x_ref, mod_ref, g_ref, yhy_ref, ygla_ref, ys5_ref, u_ref, wg_ref, whb_ref, wgb_ref, wsb_ref,
                  wo_ref, s5d_ref, wglu_ref, bglu_ref, o_ref):
    x = x_ref[0]
    m = mod_ref[0, 0]
    h = _norm_mod(x, g_ref[...], m[:, 0:d], m[:, d:2 * d]).astype(BF16)
    y5 = ys5_ref[0].astype(F32) + s5d_ref[...] * u_ref[0].astype(F32)
    g5 = _gelu_tanh(y5)
    y_s5 = g5 * jax.nn.sigmoid(jnp.dot(g5.astype(BF16), wglu_ref[...], preferred_element_type=F32) + bglu_ref[...])

    def branch(k, y, wb_ref):
        gate = jnp.dot(h, wg_ref[:, k * d:(k + 1) * d], preferred_element_type=F32)
        return jax.nn.sigmoid(gate) * jnp.dot(y, wb_ref[...], preferred_element_type=F32)

    mix = branch(0, yhy_ref[0], whb_ref) + branch(1, ygla_ref[0], wgb_ref) + branch(2, y_s5.astype(BF16), wsb_ref)
    out = jnp.dot(mix.astype(BF16), wo_ref[...], preferred_element_type=F32)
    o_ref[0] = x + m[:, 2 * d:3 * d] * out


def _merge(xc, mods, g, y_hy, y_gla, ys_s5, u_s5, wts, row_off):
    b, t, d = xc.shape
    tm = ROW_TILE
    nj = t // tm - row_off
    ch = y_hy.shape[-1]
    row = lambda n: pl.BlockSpec((1, tm, n), lambda i, j: (i, j + row_off, 0))
    w_gate, w_hy, w_gla, w_s5, w_out, s5_d, w_glu, b_glu = wts
    return pl.pallas_call(
        functools.partial(_merge_kernel, d),
        grid=(b, nj),
        in_specs=[
            row(d),
            pl.BlockSpec((1, 1, 1, 6 * d), lambda i, j: (i, jnp.minimum(j + row_off, 1), 0, 0)),
            _const_spec((1, d)),
            row(ch), row(ch), row(ch), row(ch),
            _const_spec(w_gate.shape), _const_spec(w_hy.shape), _const_spec(w_gla.shape), _const_spec(w_s5.shape),
            _const_spec(w_out.shape), _const_spec(s5_d.shape), _const_spec(w_glu.shape), _const_spec(b_glu.shape),
        ],
        out_specs=pl.BlockSpec((1, tm, d), lambda i, j: (i, j, 0)),
        out_shape=jax.ShapeDtypeStruct((b, nj * tm, d), F32),
        compiler_params=_params(2),
        name="merge",
    )(xc, mods, g, y_hy, y_gla, ys_s5, u_s5, w_gate, w_hy, w_gla, w_s5, w_out, s5_d, w_glu, b_glu)


def _ffn_kernel(d, final, x_ref, mod_ref, g_ref, wa_ref, wb_ref, wo_ref, gf_ref, o_ref):
    x = x_ref[0]
    m = mod_ref[0, 0]
    h = _norm_mod(x, g_ref[...], m[:, 3 * d:4 * d], m[:, 4 * d:5 * d]).astype(BF16)
    a = jnp.dot(h, wa_ref[...], preferred_element_type=F32)
    bb = jnp.dot(h, wb_ref[...], preferred_element_type=F32)
    act = (a * jax.nn.sigmoid(a) * bb).astype(BF16)
    y = x + m[:, 5 * d:6 * d] * jnp.dot(act, wo_ref[...], preferred_element_type=F32)
    if final:
        y = y * lax.rsqrt(jnp.mean(y * y, axis=-1, keepdims=True) + EPS) * gf_ref[...]
    o_ref[0] = y


def _ffn(x, mods, g, wa, wb, wo, g_final, mod_row_off, final):
    b, t, d = x.shape
    tm = ROW_TILE
    return pl.pallas_call(
        functools.partial(_ffn_kernel, d, final),
        grid=(b, t // tm),
        in_specs=[
            pl.BlockSpec((1, tm, d), lambda i, j: (i, j, 0)),
            pl.BlockSpec((1, 1, 1, 6 * d), lambda i, j: (i, jnp.minimum(j + mod_row_off, 1), 0, 0)),
            _const_spec((1, d)),
            _const_spec(wa.shape), _const_spec(wb.shape), _const_spec(wo.shape),
            _const_spec((1, d)),
        ],
        out_specs=pl.BlockSpec((1, tm, d), lambda i, j: (i, j, 0)),
        out_shape=jax.ShapeDtypeStruct((b, t, d), F32),
        compiler_params=_params(2),
        name="ffn",
    )(x, mods, g, wa, wb, wo, g_final)


def _rmsnorm(x, g):
    y = x * lax.rsqrt(jnp.mean(x * x, axis=-1, keepdims=True) + EPS)
    return y * g


def _short_conv(u, w):
    k = w.shape[0]
    pad = k // 2
    n = u.shape[1]
    up = jnp.pad(u, ((0, 0), (pad, pad), (0, 0)))
    return sum(up[:, j:j + n] * w[j] for j in range(k))


def _hyena_filters(n, params):
    w1, b1, w2, b2, w3, freq1, freq2, decay = (p.astype(F32) for p in params)
    t = jnp.linspace(0.0, 1.0, n, dtype=F32)[:, None]
    bands = jnp.linspace(1e-4, HY_POS_BANDS - 1, HY_POS_BANDS, dtype=F32)
    ang = (2.0 * math.pi / n) * jnp.arange(n, dtype=F32)[:, None] * bands[None]
    z = jnp.concatenate([t, jnp.cos(ang), -jnp.sin(ang)], axis=-1)
    hid = jnp.sin(freq1 * (z @ w1 + b1))
    hid = jnp.sin(freq2 * (hid @ w2 + b2))
    h = (hid @ w3) * jnp.exp(-t * jnp.abs(decay))
    return h.reshape(n, HY_N_FILT, -1)


def _long_conv_bidir(u, h_fwd, h_bwd, bias):
    n = u.shape[1]
    k = jnp.concatenate([h_fwd[:1] + h_bwd[:1], h_fwd[1:], jnp.zeros_like(h_fwd[:1]), h_bwd[:0:-1]], axis=0)
    k = k * lax.rsqrt(jnp.sum(k * k, axis=0, keepdims=True) + EPS)
    u_f = jnp.fft.rfft(u, n=2 * n, axis=1)
    k_f = jnp.fft.rfft(k, axis=0)
    y = jnp.fft.irfft(u_f * k_f[None], n=2 * n, axis=1)[:, :n]
    return y + u * bias


def _hyena_mix(cols, conv_w, filt_params, bias):
    u = _short_conv(cols.astype(F32), conv_w.astype(F32))
    x1, x2, v = jnp.split(u, 3, axis=-1)
    h = _hyena_filters(cols.shape[1], filt_params)
    z = x1 * _long_conv_bidir(v, h[:, 0], h[:, 1], bias[0])
    return x2 * _long_conv_bidir(z, h[:, 2], h[:, 3], bias[1])


def _to_heads(t, d):
    b, n, _ = t.shape
    return t.astype(F32).reshape(b, n, GLA_HEADS, d).transpose(0, 2, 1, 3)


def _gla_decays(a_lr, wa2, ba):
    b, n, _ = a_lr.shape
    z = jnp.einsum('bldr,drk->bldk', a_lr.astype(F32).reshape(b, n, 2, GLA_LOWRANK), wa2.astype(F32)) + ba
    log_a = jax.nn.log_sigmoid(z) / GLA_TAU
    return _to_heads(log_a[:, :, 0], GLA_DK), _to_heads(log_a[:, :, 1], GLA_DK)


def _gla_scan(k, v, log_a, s0, q=None):
    b_, h_, n_, dk = k.shape
    c = GLA_CHUNK
    nb = n_ // c
    blk = lambda t: t.reshape(b_, h_, nb, c, t.shape[-1])
    k, v = blk(k), blk(v)
    b = jnp.cumsum(blk(log_a), axis=3)
    b_last = b[:, :, :, -1:]
    ds = jnp.einsum('bhncd,bhnce->bhnde', k * jnp.exp(b_last - b), v)
    decay = jnp.exp(b_last[:, :, :, 0])
    want = q is not None

    def step(s, inp):
        dec, d_s = inp
        return dec[..., None] * s + d_s, (s if want else None)

    s_fin, s_prev = lax.scan(step, s0, (jnp.moveaxis(decay, 2, 0), jnp.moveaxis(ds, 2, 0)))
    if not want:
        return None, s_fin
    q_dec = blk(q) * jnp.exp(b)
    att = jnp.einsum('bhncd,bhnsd->bhncs', q_dec, k * jnp.exp(-b))
    att = jnp.where(jnp.tril(jnp.ones((c, c), bool)), att, 0.0)
    o = jnp.einsum('bhncs,bhnse->bhnce', att, v) + jnp.einsum('bhncd,bhnde->bhnce', q_dec, jnp.moveaxis(s_prev, 0, 2))
    return o.reshape(b_, h_, n_, -1), s_fin


def _gla_bidir(k, v, la_f, la_b, s0_f, s0_b, q=None):
    fl = lambda t: jnp.flip(t, axis=2)
    o_f, s_f = _gla_scan(k, v, la_f, s0_f, q)
    o_b, s_b = _gla_scan(fl(k), fl(v), fl(la_b), s0_b, None if q is None else fl(q))
    o = None if q is None else o_f + fl(o_b)
    return o, s_f, s_b


def _gla_readout(o, r, gnorm):
    o = _rmsnorm(o, gnorm)
    b, h, n, dv = o.shape
    return o.transpose(0, 2, 1, 3).reshape(b, n, h * dv) * jax.nn.silu(r.astype(F32))


def _s5_discretize(a_re, a_im, log_dt, b_re, b_im):
    lam = lax.complex(jnp.minimum(a_re, -1e-4), a_im)
    lam_dt = lam * jnp.exp(log_dt)[:, None]
    b_bar = ((jnp.exp(lam_dt) - 1.0) / lam)[..., None] * lax.complex(b_re, b_im)
    return lam_dt, b_bar


def _s5_scan(u, lam_dt, b_bar, x0, c_mat=None):
    bn, n, g, i = u.shape
    t = GRID_W
    rows = n // t
    lam_bar = jnp.exp(lam_dt)
    pows = jnp.exp(lam_dt[None] * jnp.arange(1, t + 1, dtype=F32)[:, None, None])
    u_blocks = jnp.moveaxis(u.reshape(bn, rows, t, g, i), 1, 0)

    def combine(e1, e2):
        a1, b1 = e1
        a2, b2 = e2
        return a1 * a2, a2 * b1 + b2

    def step(x, u_blk):
        bu = jnp.einsum('btgi,gpi->btgp', u_blk.astype(jnp.complex64), b_bar)
        _, xs = lax.associative_scan(combine, (jnp.broadcast_to(lam_bar, bu.shape), bu), axis=1)
        xs = xs + pows[None] * x[:, None]
        y = None if c_mat is None else jnp.einsum('btgp,gip->btgi', xs, c_mat).real
        return xs[:, -1], y

    x_fin, ys = lax.scan(step, x0, u_blocks)
    if c_mat is None:
        return None, x_fin
    return jnp.moveaxis(ys, 0, 1).reshape(bn, n, g, i), x_fin


def _s5_bidir(u, disc_f, disc_b, x0_f, x0_b, c_mat=None):
    y_f, x_f = _s5_scan(u, disc_f[0], disc_f[1], x0_f, c_mat)
    y_b, x_b = _s5_scan(jnp.flip(u, 1), disc_b[0], disc_b[1], x0_b, c_mat)
    y = None if c_mat is None else y_f + jnp.flip(y_b, 1)
    return y, x_f, x_b


def _to_groups(u):
    b, n, ch = u.shape
    return u.astype(F32).reshape(b, n, ch // S5_GROUP, S5_GROUP)


def kernel(x, c, ctx, c_ctx, w_mod, b_mod, g_norm1, g_norm2, w_in, hy_conv, hy_w1, hy_b1, hy_w2, hy_b2, hy_w3,
           hy_freq1, hy_freq2, hy_decay, hy_bias, gla_wa2, gla_ba, gla_gnorm, s5_a_re, s5_a_im, s5_log_dt, s5_b_re,
           s5_b_im, s5_c_re, s5_c_im, s5_d, s5_w_glu, s5_b_glu, w_br_hy, w_br_gla, w_br_s5, w_out, w_ffn_in,
           w_ffn_out, g_final):
    bn, n_lat, d = x.shape
    n_ctx = ctx.shape[1]
    depth = w_in.shape[0]
    d_hy = w_br_hy.shape[1]
    s5_ch = w_br_s5.shape[1]
    d_ff = w_ffn_out.shape[1]
    assert n_ctx == ROW_TILE and n_lat % ROW_TILE == 0
    groups = s5_ch // S5_GROUP
    n_low = 2 * GLA_LOWRANK
    o_k, o_v, o_a, o_u = 0, GLA_QK, GLA_QK + GLA_V, GLA_QK + GLA_V + n_low
    o_q = o_u + s5_ch
    o_r = o_q + GLA_QK
    o_hy = o_r + GLA_V
    o_gate = o_hy + 3 * d_hy
    col_sizes = (GLA_QK, GLA_V, A_PAD, s5_ch, GLA_QK, GLA_V, 3 * d_hy)

    cond = jnp.concatenate([c_ctx[None], c], axis=0)
    pad_rows = (-cond.shape[0]) % 8
    cond = jnp.pad(cond, ((0, pad_rows), (0, 0)))
    mod_all = _modulation_all(cond, w_mod, b_mod)

    xc = jnp.concatenate([ctx, x], axis=1)
    zeros_s = jnp.zeros((bn, GLA_HEADS, GLA_DK, GLA_DV), F32)
    zeros_x = jnp.zeros((bn, groups, S5_STATE), jnp.complex64)
    q_scale = GLA_DK ** -0.5
    out = None
    for l in range(depth):
        last = l == depth - 1
        mods = jnp.stack([jnp.broadcast_to(mod_all[l, 0], (bn, 6 * d)), mod_all[l, 1:1 + bn]], axis=1)
        mods = mods[:, :, None, :]
        wl = w_in[l]
        w_proj = jnp.concatenate(
            [wl[:, o_k:o_a + n_low], jnp.zeros((d, A_PAD - n_low), F32), wl[:, o_u:o_gate]], axis=1).astype(BF16)
        g1 = g_norm1[l][None]
        k_a, v_a, a_a, u_a, q_a, r_a, hy_a = _project(xc, mods, g1, w_proj, col_sizes)
        sp = lambda t: (t[:, :n_ctx], t[:, n_ctx:])
        (k_c, k_l), (v_c, v_l), (a_c, a_l), (u_c, u_l) = sp(k_a), sp(v_a), sp(a_a[..., :n_low]), sp(u_a)
        (q_c, q_l), (r_c, r_l), (hy_c, hy_l) = sp(q_a), sp(r_a), sp(hy_a)

        lf_c, lb_c = _gla_decays(a_c, gla_wa2[l], gla_ba[l])
        qh_c = None if last else _to_heads(q_c, GLA_DK) * q_scale
        o_c, s_f, s_b = _gla_bidir(_to_heads(k_c, GLA_DK), _to_heads(v_c, GLA_DV), lf_c, lb_c, zeros_s, zeros_s, qh_c)
        lf, lb = _gla_decays(a_l, gla_wa2[l], gla_ba[l])
        o_l, _, _ = _gla_bidir(_to_heads(k_l, GLA_DK), _to_heads(v_l, GLA_DV), lf, lb, s_f, s_b,
                               _to_heads(q_l, GLA_DK) * q_scale)
        y_gla = _gla_readout(o_l, r_l, gla_gnorm[l])

        disc_f = _s5_discretize(s5_a_re[l, 0], s5_a_im[l, 0], s5_log_dt[l, 0], s5_b_re[l], s5_b_im[l])
        disc_b = _s5_discretize(s5_a_re[l, 1], s5_a_im[l, 1], s5_log_dt[l, 1], s5_b_re[l], s5_b_im[l])
        c_mat = lax.complex(s5_c_re[l], s5_c_im[l])
        ys_c, x_f, x_b = _s5_bidir(_to_groups(u_c), disc_f, disc_b, zeros_x, zeros_x, None if last else c_mat)
        ys_l, _, _ = _s5_bidir(_to_groups(u_l), disc_f, disc_b, x_f, x_b, c_mat)
        ys_l = ys_l.reshape(bn, n_lat, s5_ch)

        hy_p = (hy_w1[l], hy_b1[l], hy_w2[l], hy_b2[l], hy_w3[l], hy_freq1[l], hy_freq2[l], hy_decay[l])
        y_hy = _hyena_mix(hy_l, hy_conv[l], hy_p, hy_bias[l])

        if last:
            pad_c = lambda t: jnp.pad(t.astype(BF16), ((0, 0), (n_ctx, 0), (0, 0)))
            y_hy_a, y_gla_a, ys_a = pad_c(y_hy), pad_c(y_gla), pad_c(ys_l)
        else:
            yc_hy = _hyena_mix(hy_c, hy_conv[l], hy_p, hy_bias[l])
            yc_gla = _gla_readout(o_c, r_c, gla_gnorm[l])
            cat = lambda a, b: jnp.concatenate([a.astype(BF16), b.astype(BF16)], axis=1)
            y_hy_a, y_gla_a, ys_a = cat(yc_hy, y_hy), cat(yc_gla, y_gla), cat(ys_c.reshape(bn, n_ctx, s5_ch), ys_l)

        wts = (wl[:, o_gate:].astype(BF16), w_br_hy[l].astype(BF16), w_br_gla[l].astype(BF16),
               w_br_s5[l].astype(BF16), w_out[l].astype(BF16), s5_d[l][None], s5_w_glu[l].astype(BF16),
               s5_b_glu[l][None])
        row_off = 1 if last else 0
        x_mid = _merge(xc, mods, g1, y_hy_a, y_gla_a, ys_a, u_a, wts, row_off)
        wf = w_ffn_in[l]
        res = _ffn(x_mid, mods, g_norm2[l][None], wf[:, :d_ff].astype(BF16), wf[:, d_ff:].astype(BF16),
                   w_ffn_out[l].astype(BF16), g_final[None], row_off, last)
        if last:
            out = res
        else:
            xc = res
    return out
```

```python
import functools
import math

import jax
import jax.numpy as jnp
import numpy as np
from jax import lax
from jax.experimental import pallas as pl
from jax.experimental.pallas import tpu as pltpu

F32 = jnp.float32
BF16 = jnp.bfloat16
EPS = 1e-6

GRID_W = 64
HY_SHORT = 3
HY_POS_BANDS = 16
HY_N_FILT = 4
GLA_HEADS = 4
GLA_DK = 64
GLA_DV = 128
GLA_QK = GLA_HEADS * GLA_DK
GLA_V = GLA_HEADS * GLA_DV
GLA_LOWRANK = 16
GLA_TAU = 16.0
GLA_CHUNK = 64
S5_GROUP = 16
S5_STATE = 64
A_PAD = 128

ROW_TILE = 256
VMEM_LIMIT = 56 * 1024 * 1024


def _const_spec(shape):
    nd = len(shape)
    return pl.BlockSpec(shape, lambda *_: (0,) * nd, pipeline_mode=pl.Buffered(1))


def _params(n_axes):
    return pltpu.CompilerParams(dimension_semantics=("parallel",) * n_axes, vmem_limit_bytes=VMEM_LIMIT)


def _mod_kernel(c_ref, w_ref, b_ref, o_ref):
    c = c_ref[...]
    s = c * jax.nn.sigmoid(c)
    o_ref[0] = jnp.dot(s.astype(BF16), w_ref[0], preferred_element_type=F32) + b_ref[0]


def _modulation_all(cond, w_mod, b_mod):
    depth, d, n = w_mod.shape
    r = cond.shape[0]
    tn = 1536
    return pl.pallas_call(
        _mod_kernel,
        grid=(depth, n // tn),
        in_specs=[
            pl.BlockSpec((r, d), lambda l, j: (0, 0)),
            pl.BlockSpec((1, d, tn), lambda l, j: (l, 0, j)),
            pl.BlockSpec((1, 1, tn), lambda l, j: (l, 0, j)),
        ],
        out_specs=pl.BlockSpec((1, r, tn), lambda l, j: (l, 0, j)),
        out_shape=jax.ShapeDtypeStruct((depth, r, n), F32),
        compiler_params=_params(2),
        name="modulation",
    )(cond, w_mod.astype(BF16), b_mod.reshape(depth, 1, n))


def _norm_mod(x, g, shift, scale):
    y = x * lax.rsqrt(jnp.mean(x * x, axis=-1, keepdims=True) + EPS)
    return (y * g) * (1.0 + scale) + shift


def _proj_kernel(col_sizes, d, x_ref, mod_ref, g_ref, w_ref, *o_refs):
    m = mod_ref[0, 0]
    h = _norm_mod(x_ref[0], g_ref[...], m[:, 0:d], m[:, d:2 * d]).astype(BF16)
    off = 0
    for o_ref, n in zip(o_refs, col_sizes):
        o_ref[0] = jnp.dot(h, w_ref[:, off:off + n], preferred_element_type=F32).astype(o_ref.dtype)
        off += n


def _project(xc, mods, g, w, col_sizes):
    b, t, d = xc.shape
    tm = ROW_TILE
    n_tot = sum(col_sizes)
    out_shape = [jax.ShapeDtypeStruct((b, t, n), BF16) for n in col_sizes]
    out_specs = [pl.BlockSpec((1, tm, n), lambda i, j: (i, j, 0)) for n in col_sizes]
    return pl.pallas_call(
        functools.partial(_proj_kernel, tuple(col_sizes), d),
        grid=(b, t // tm),
        in_specs=[
            pl.BlockSpec((1, tm, d), lambda i, j: (i, j, 0)),
            pl.BlockSpec((1, 1, 1, 6 * d), lambda i, j: (i, jnp.minimum(j, 1), 0, 0)),
            _const_spec((1, d)),
            _const_spec((d, n_tot)),
        ],
        out_specs=out_specs,
        out_shape=out_shape,
        compiler_params=_params(2),
        name="in_proj",
    )(xc, mods, g, w)


def _gelu_tanh(x):
    return 0.5 * x * (1.0 + jnp.tanh(math.sqrt(2.0 / math.pi) * (x + 0.044715 * (x * x * x))))


def _merge_kernel(d, x_ref, mod_ref, g_ref, yhy_ref, ygla_ref, ys5_ref, u_ref, wg_ref, whb_ref, wgb_ref, wsb_ref,
                  wo_ref, s5d_ref, wglu_ref, bglu_ref, o_ref):
    x = x_ref[0]
    m = mod_ref[0, 0]
    h = _norm_mod(x, g_ref[...], m[:, 0:d], m[:, d:2 * d]).astype(BF16)
    y5 = ys5_ref[0].astype(F32) + s5d_ref[...] * u_ref[0].astype(F32)
    g5 = _gelu_tanh(y5)
    y_s5 = g5 * jax.nn.sigmoid(jnp.dot(g5.astype(BF16), wglu_ref[...], preferred_element_type=F32) + bglu_ref[...])

    def branch(k, y, wb_ref):
        gate = jnp.dot(h, wg_ref[:, k * d:(k + 1) * d], preferred_element_type=F32)
        return jax.nn.sigmoid(gate) * jnp.dot(y, wb_ref[...], preferred_element_type=F32)

    mix = branch(0, yhy_ref[0], whb_ref) + branch(1, ygla_ref[0], wgb_ref) + branch(2, y_s5.astype(BF16), wsb_ref)
    out = jnp.dot(mix.astype(BF16), wo_ref[...], preferred_element_type=F32)
    o_ref[0] = x + m[:, 2 * d:3 * d] * out


def _merge(xc, mods, g, y_hy, y_gla, ys_s5, u_s5, wts, row_off):
    b, t, d = xc.shape
    tm = ROW_TILE
    nj = t // tm - row_off
    ch = y_hy.shape[-1]
    row = lambda n: pl.BlockSpec((1, tm, n), lambda i, j: (i, j + row_off, 0))
    w_gate, w_hy, w_gla, w_s5, w_out, s5_d, w_glu, b_glu = wts
    return pl.pallas_call(
        functools.partial(_merge_kernel, d),
        grid=(b, nj),
        in_specs=[
            row(d),
            pl.BlockSpec((1, 1, 1, 6 * d), lambda i, j: (i, jnp.minimum(j + row_off, 1), 0, 0)),
            _const_spec((1, d)),
            row(ch), row(ch), row(ch), row(ch),
            _const_spec(w_gate.shape), _const_spec(w_hy.shape), _const_spec(w_gla.shape), _const_spec(w_s5.shape),
            _const_spec(w_out.shape), _const_spec(s5_d.shape), _const_spec(w_glu.shape), _const_spec(b_glu.shape),
        ],
        out_specs=pl.BlockSpec((1, tm, d), lambda i, j: (i, j, 0)),
        out_shape=jax.ShapeDtypeStruct((b, nj * tm, d), F32),
        compiler_params=_params(2),
        name="merge",
    )(xc, mods, g, y_hy, y_gla, ys_s5, u_s5, w_gate, w_hy, w_gla, w_s5, w_out, s5_d, w_glu, b_glu)


def _ffn_kernel(d, final, x_ref, mod_ref, g_ref, wa_ref, wb_ref, wo_ref, gf_ref, o_ref):
    x = x_ref[0]
    m = mod_ref[0, 0]
    h = _norm_mod(x, g_ref[...], m[:, 3 * d:4 * d], m[:, 4 * d:5 * d]).astype(BF16)
    a = jnp.dot(h, wa_ref[...], preferred_element_type=F32)
    bb = jnp.dot(h, wb_ref[...], preferred_element_type=F32)
    act = (a * jax.nn.sigmoid(a) * bb).astype(BF16)
    y = x + m[:, 5 * d:6 * d] * jnp.dot(act, wo_ref[...], preferred_element_type=F32)
    if final:
        y = y * lax.rsqrt(jnp.mean(y * y, axis=-1, keepdims=True) + EPS) * gf_ref[...]
    o_ref[0] = y


def _ffn(x, mods, g, wa, wb, wo, g_final, mod_row_off, final):
    b, t, d = x.shape
    tm = ROW_TILE
    return pl.pallas_call(
        functools.partial(_ffn_kernel, d, final),
        grid=(b, t // tm),
        in_specs=[
            pl.BlockSpec((1, tm, d), lambda i, j: (i, j, 0)),
            pl.BlockSpec((1, 1, 1, 6 * d), lambda i, j: (i, jnp.minimum(j + mod_row_off, 1), 0, 0)),
            _const_spec((1, d)),
            _const_spec(wa.shape), _const_spec(wb.shape), _const_spec(wo.shape),
            _const_spec((1, d)),
        ],
        out_specs=pl.BlockSpec((1, tm, d), lambda i, j: (i, j, 0)),
        out_shape=jax.ShapeDtypeStruct((b, t, d), F32),
        compiler_params=_params(2),
        name="ffn",
    )(x, mods, g, wa, wb, wo, g_final)


S5_CHUNK = 64


def _s5_weights(a_re, a_im, log_dt, b_re, b_im, c_re, c_im, tc):
    g, p = a_re.shape[1:]
    i = b_re.shape[-1]
    lam_c = lax.complex(jnp.minimum(a_re, -1e-4), a_im)
    lam_dt = lam_c * jnp.exp(log_dt)[..., None]
    b_bar = ((jnp.exp(lam_dt) - 1.0) / lam_c)[..., None] * lax.complex(b_re, b_im)[None]
    c_mat = lax.complex(c_re, c_im)
    tau = jnp.arange(tc, dtype=F32)
    pw = jnp.exp(lam_dt[:, :, None, :] * tau[None, None, :, None])
    pw1 = pw * jnp.exp(lam_dt)[:, :, None, :]
    kern = jnp.einsum('gip,dgtp,dgpj->dgtij', c_mat, pw, b_bar).real
    lag = jnp.arange(tc)[None, :] - jnp.arange(tc)[:, None]
    k_f = jnp.where((lag >= 0)[None, :, :, None, None], kern[0][:, jnp.clip(lag, 0, tc - 1)], 0.0)
    k_b = jnp.where((lag <= 0)[None, :, :, None, None], kern[1][:, jnp.clip(-lag, 0, tc - 1)], 0.0)
    m = (k_f + k_b).transpose(0, 1, 4, 2, 3).reshape(g, tc * i, tc * i)

    inc_f = pw[0][:, ::-1, None, :] * b_bar[0].transpose(0, 2, 1)[:, None]
    inc_b = pw[1][:, :, None, :] * b_bar[1].transpose(0, 2, 1)[:, None]
    bm = jnp.concatenate([inc_f.real, inc_b.real, inc_f.imag, inc_b.imag], axis=-1).reshape(g, tc * i, 4 * p)

    out_f = c_mat.transpose(0, 2, 1)[:, :, None, :] * pw1[0].transpose(0, 2, 1)[..., None]
    out_b = c_mat.transpose(0, 2, 1)[:, :, None, :] * pw1[1][:, ::-1].transpose(0, 2, 1)[..., None]
    z = jnp.zeros((g, p, tc * i), F32)
    fl = lambda t: t.reshape(g, p, tc * i)
    cm = jnp.concatenate([fl(out_f.real), z, fl(-out_f.imag), z, z, fl(out_b.real), z, fl(-out_b.imag)], axis=1)

    lam_t = jnp.exp(lam_dt * float(tc))
    lam = jnp.stack([jnp.concatenate([lam_t[0].real, lam_t[1].real], -1),
                     jnp.concatenate([lam_t[0].imag, lam_t[1].imag], -1)], axis=1)
    return m.astype(BF16), bm.astype(BF16), cm.astype(BF16), lam


def _s5_kernel(nc, nc_ctx, bn, u_ref, m_ref, bm_ref, cm_ref, lam_ref, y_ref, dx_ref, p_ref):
    u = u_ref[0]
    dx_ref[...] = jnp.dot(u, bm_ref[0], preferred_element_type=F32)
    lam = lam_ref[0]
    lr, li = lam[0:1], lam[1:2]
    half = lam.shape[-1] // 2
    is_f = lax.broadcasted_iota(jnp.int32, (bn, 2 * half), 1) < half

    def step(s, carry):
        sr, si = carry
        nb = jnp.where(s < nc_ctx, nc_ctx - 1 - s, nc - 1 - (s - nc_ctx))
        rf = pl.multiple_of(s * bn, bn)
        rb = pl.multiple_of(nb * bn, bn)
        p_ref[pl.ds(rf, bn), 0:128] = sr
        p_ref[pl.ds(rf, bn), 128:256] = si
        p_ref[pl.ds(rb, bn), 256:384] = sr
        p_ref[pl.ds(rb, bn), 384:512] = si
        d_re = jnp.where(is_f, dx_ref[pl.ds(rf, bn), 0:128], dx_ref[pl.ds(rb, bn), 0:128])
        d_im = jnp.where(is_f, dx_ref[pl.ds(rf, bn), 128:256], dx_ref[pl.ds(rb, bn), 128:256])
        return lr * sr - li * si + d_re, lr * si + li * sr + d_im

    zero = jnp.zeros((bn, 2 * half), F32)
    lax.fori_loop(0, nc, step, (zero, zero))
    y = jnp.dot(u, m_ref[0], preferred_element_type=F32)
    y = y + jnp.dot(p_ref[...].astype(BF16), cm_ref[0], preferred_element_type=F32)
    y_ref[0] = y.astype(y_ref.dtype)


def _s5_mix(u_all, n_ctx, wts):
    m, bm, cm, lam = wts
    bn, t, ch = u_all.shape
    g = m.shape[0]
    i = ch // g
    tc = m.shape[1] // i
    assert S5_STATE == 64 and bn % 8 == 0 and n_ctx % tc == 0 and t % tc == 0
    nc, nc_ctx = t // tc, n_ctx // tc
    k = tc * i
    r = nc * bn
    u_g = u_all.reshape(bn, nc, tc, g, i).transpose(3, 1, 0, 2, 4).reshape(g, r, k)
    y_g = pl.pallas_call(
        functools.partial(_s5_kernel, nc, nc_ctx, bn),
        grid=(g,),
        in_specs=[
            pl.BlockSpec((1, r, k), lambda j: (j, 0, 0)),
            pl.BlockSpec((1, k, k), lambda j: (j, 0, 0)),
            pl.BlockSpec((1, k, 4 * S5_STATE), lambda j: (j, 0, 0)),
            pl.BlockSpec((1, 8 * S5_STATE, k), lambda j: (j, 0, 0)),
            pl.BlockSpec((1, 2, 2 * S5_STATE), lambda j: (j, 0, 0)),
        ],
        out_specs=pl.BlockSpec((1, r, k), lambda j: (j, 0, 0)),
        out_shape=jax.ShapeDtypeStruct((g, r, k), BF16),
        scratch_shapes=[pltpu.VMEM((r, 4 * S5_STATE), F32), pltpu.VMEM((r, 8 * S5_STATE), F32)],
        compiler_params=_params(1),
        name="s5_mix",
    )(u_g, m, bm, cm, lam)
    return y_g.reshape(g, nc, bn, tc, i).transpose(2, 1, 3, 0, 4).reshape(bn, t, ch)


def _rmsnorm(x, g):
    y = x * lax.rsqrt(jnp.mean(x * x, axis=-1, keepdims=True) + EPS)
    return y * g


def _short_conv(u, w):
    k = w.shape[0]
    pad = k // 2
    n = u.shape[1]
    up = jnp.pad(u, ((0, 0), (pad, pad), (0, 0)))
    return sum(up[:, j:j + n] * w[j] for j in range(k))


def _hyena_filters(n, params):
    w1, b1, w2, b2, w3, freq1, freq2, decay = (p.astype(F32) for p in params)
    t = jnp.linspace(0.0, 1.0, n, dtype=F32)[:, None]
    bands = jnp.linspace(1e-4, HY_POS_BANDS - 1, HY_POS_BANDS, dtype=F32)
    ang = (2.0 * math.pi / n) * jnp.arange(n, dtype=F32)[:, None] * bands[None]
    z = jnp.concatenate([t, jnp.cos(ang), -jnp.sin(ang)], axis=-1)
    hid = jnp.sin(freq1 * (z @ w1 + b1))
    hid = jnp.sin(freq2 * (hid @ w2 + b2))
    h = (hid @ w3) * jnp.exp(-t * jnp.abs(decay))
    return h.reshape(n, HY_N_FILT, -1)


def _long_conv_bidir(u, h_fwd, h_bwd, bias):
    n = u.shape[1]
    k = jnp.concatenate([h_fwd[:1] + h_bwd[:1], h_fwd[1:], jnp.zeros_like(h_fwd[:1]), h_bwd[:0:-1]], axis=0)
    k = k * lax.rsqrt(jnp.sum(k * k, axis=0, keepdims=True) + EPS)
    u_f = jnp.fft.rfft(u, n=2 * n, axis=1)
    k_f = jnp.fft.rfft(k, axis=0)
    y = jnp.fft.irfft(u_f * k_f[None], n=2 * n, axis=1)[:, :n]
    return y + u * bias


def _hyena_mix(cols, conv_w, filt_params, bias):
    u = _short_conv(cols.astype(F32), conv_w.astype(F32))
    x1, x2, v = jnp.split(u, 3, axis=-1)
    h = _hyena_filters(cols.shape[1], filt_params)
    z = x1 * _long_conv_bidir(v, h[:, 0], h[:, 1], bias[0])
    return x2 * _long_conv_bidir(z, h[:, 2], h[:, 3], bias[1])


def _to_heads(t, d):
    b, n, _ = t.shape
    return t.astype(F32).reshape(b, n, GLA_HEADS, d).transpose(0, 2, 1, 3)


def _gla_decays(a_lr, wa2, ba):
    b, n, _ = a_lr.shape
    z = jnp.einsum('bldr,drk->bldk', a_lr.astype(F32).reshape(b, n, 2, GLA_LOWRANK), wa2.astype(F32)) + ba
    log_a = jax.nn.log_sigmoid(z) / GLA_TAU
    return _to_heads(log_a[:, :, 0], GLA_DK), _to_heads(log_a[:, :, 1], GLA_DK)


def _gla_scan(k, v, log_a, s0, q=None):
    b_, h_, n_, dk = k.shape
    c = GLA_CHUNK
    nb = n_ // c
    blk = lambda t: t.reshape(b_, h_, nb, c, t.shape[-1])
    k, v = blk(k), blk(v)
    b = jnp.cumsum(blk(log_a), axis=3)
    b_last = b[:, :, :, -1:]
    ds = jnp.einsum('bhncd,bhnce->bhnde', k * jnp.exp(b_last - b), v)
    decay = jnp.exp(b_last[:, :, :, 0])
    want = q is not None

    def step(s, inp):
        dec, d_s = inp
        return dec[..., None] * s + d_s, (s if want else None)

    s_fin, s_prev = lax.scan(step, s0, (jnp.moveaxis(decay, 2, 0), jnp.moveaxis(ds, 2, 0)))
    if not want:
        return None, s_fin
    q_dec = blk(q) * jnp.exp(b)
    att = jnp.einsum('bhncd,bhnsd->bhncs', q_dec, k * jnp.exp(-b))
    att = jnp.where(jnp.tril(jnp.ones((c, c), bool)), att, 0.0)
    o = jnp.einsum('bhncs,bhnse->bhnce', att, v) + jnp.einsum('bhncd,bhnde->bhnce', q_dec, jnp.moveaxis(s_prev, 0, 2))
    return o.reshape(b_, h_, n_, -1), s_fin


def _gla_bidir(k, v, la_f, la_b, s0_f, s0_b, q=None):
    fl = lambda t: jnp.flip(t, axis=2)
    o_f, s_f = _gla_scan(k, v, la_f, s0_f, q)
    o_b, s_b = _gla_scan(fl(k), fl(v), fl(la_b), s0_b, None if q is None else fl(q))
    o = None if q is None else o_f + fl(o_b)
    return o, s_f, s_b


def _gla_readout(o, r, gnorm):
    o = _rmsnorm(o, gnorm)
    b, h, n, dv = o.shape
    return o.transpose(0, 2, 1, 3).reshape(b, n, h * dv) * jax.nn.silu(r.astype(F32))


def _s5_discretize(a_re, a_im, log_dt, b_re, b_im):
    lam = lax.complex(jnp.minimum(a_re, -1e-4), a_im)
    lam_dt = lam * jnp.exp(log_dt)[:, None]
    b_bar = ((jnp.exp(lam_dt) - 1.0) / lam)[..., None] * lax.complex(b_re, b_im)
    return lam_dt, b_bar


def _s5_scan(u, lam_dt, b_bar, x0, c_mat=None):
    bn, n, g, i = u.shape
    t = GRID_W
    rows = n // t
    lam_bar = jnp.exp(lam_dt)
    pows = jnp.exp(lam_dt[None] * jnp.arange(1, t + 1, dtype=F32)[:, None, None])
    u_blocks = jnp.moveaxis(u.reshape(bn, rows, t, g, i), 1, 0)

    def combine(e1, e2):
        a1, b1 = e1
        a2, b2 = e2
        return a1 * a2, a2 * b1 + b2

    def step(x, u_blk):
        bu = jnp.einsum('btgi,gpi->btgp', u_blk.astype(jnp.complex64), b_bar)
        _, xs = lax.associative_scan(combine, (jnp.broadcast_to(lam_bar, bu.shape), bu), axis=1)
        xs = xs + pows[None] * x[:, None]
        y = None if c_mat is None else jnp.einsum('btgp,gip->btgi', xs, c_mat).real
        return xs[:, -1], y

    x_fin, ys = lax.scan(step, x0, u_blocks)
    if c_mat is None:
        return None, x_fin
    return jnp.moveaxis(ys, 0, 1).reshape(bn, n, g, i), x_fin


def _s5_bidir(u, disc_f, disc_b, x0_f, x0_b, c_mat=None):
    y_f, x_f = _s5_scan(u, disc_f[0], disc_f[1], x0_f, c_mat)
    y_b, x_b = _s5_scan(jnp.flip(u, 1), disc_b[0], disc_b[1], x0_b, c_mat)
    y = None if c_mat is None else y_f + jnp.flip(y_b, 1)
    return y, x_f, x_b


def _to_groups(u):
    b, n, ch = u.shape
    return u.astype(F32).reshape(b, n, ch // S5_GROUP, S5_GROUP)


def kernel(x, c, ctx, c_ctx, w_mod, b_mod, g_norm1, g_norm2, w_in, hy_conv, hy_w1, hy_b1, hy_w2, hy_b2, hy_w3,
           hy_freq1, hy_freq2, hy_decay, hy_bias, gla_wa2, gla_ba, gla_gnorm, s5_a_re, s5_a_im, s5_log_dt, s5_b_re,
           s5_b_im, s5_c_re, s5_c_im, s5_d, s5_w_glu, s5_b_glu, w_br_hy, w_br_gla, w_br_s5, w_out, w_ffn_in,
           w_ffn_out, g_final):
    bn, n_lat, d = x.shape
    n_ctx = ctx.shape[1]
    depth = w_in.shape[0]
    d_hy = w_br_hy.shape[1]
    s5_ch = w_br_s5.shape[1]
    d_ff = w_ffn_out.shape[1]
    assert n_ctx == ROW_TILE and n_lat % ROW_TILE == 0
    groups = s5_ch // S5_GROUP
    n_low = 2 * GLA_LOWRANK
    o_k, o_v, o_a, o_u = 0, GLA_QK, GLA_QK + GLA_V, GLA_QK + GLA_V + n_low
    o_q = o_u + s5_ch
    o_r = o_q + GLA_QK
    o_hy = o_r + GLA_V
    o_gate = o_hy + 3 * d_hy
    col_sizes = (GLA_QK, GLA_V, A_PAD, s5_ch, GLA_QK, GLA_V, 3 * d_hy)

    cond = jnp.concatenate([c_ctx[None], c], axis=0)
    pad_rows = (-cond.shape[0]) % 8
    cond = jnp.pad(cond, ((0, pad_rows), (0, 0)))
    mod_all = _modulation_all(cond, w_mod, b_mod)

    xc = jnp.concatenate([ctx, x], axis=1)
    zeros_s = jnp.zeros((bn, GLA_HEADS, GLA_DK, GLA_DV), F32)
    zeros_x = jnp.zeros((bn, groups, S5_STATE), jnp.complex64)
    q_scale = GLA_DK ** -0.5
    out = None
    for l in range(depth):
        last = l == depth - 1
        mods = jnp.stack([jnp.broadcast_to(mod_all[l, 0], (bn, 6 * d)), mod_all[l, 1:1 + bn]], axis=1)
        mods = mods[:, :, None, :]
        wl = w_in[l]
        w_proj = jnp.concatenate(
            [wl[:, o_k:o_a + n_low], jnp.zeros((d, A_PAD - n_low), F32), wl[:, o_u:o_gate]], axis=1).astype(BF16)
        g1 = g_norm1[l][None]
        k_a, v_a, a_a, u_a, q_a, r_a, hy_a = _project(xc, mods, g1, w_proj, col_sizes)
        sp = lambda t: (t[:, :n_ctx], t[:, n_ctx:])
        (k_c, k_l), (v_c, v_l), (a_c, a_l), (u_c, u_l) = sp(k_a), sp(v_a), sp(a_a[..., :n_low]), sp(u_a)
        (q_c, q_l), (r_c, r_l), (hy_c, hy_l) = sp(q_a), sp(r_a), sp(hy_a)

        lf_c, lb_c = _gla_decays(a_c, gla_wa2[l], gla_ba[l])
        qh_c = None if last else _to_heads(q_c, GLA_DK) * q_scale
        o_c, s_f, s_b = _gla_bidir(_to_heads(k_c, GLA_DK), _to_heads(v_c, GLA_DV), lf_c, lb_c, zeros_s, zeros_s, qh_c)
        lf, lb = _gla_decays(a_l, gla_wa2[l], gla_ba[l])
        o_l, _, _ = _gla_bidir(_to_heads(k_l, GLA_DK), _to_heads(v_l, GLA_DV), lf, lb, s_f, s_b,
                               _to_heads(q_l, GLA_DK) * q_scale)
        y_gla = _gla_readout(o_l, r_l, gla_gnorm[l])

        s5w = _s5_weights(s5_a_re[l], s5_a_im[l], s5_log_dt[l], s5_b_re[l], s5_b_im[l], s5_c_re[l], s5_c_im[l],
                          S5_CHUNK)
        ys_a = _s5_mix(u_a, n_ctx, s5w)

        hy_p = (hy_w1[l], hy_b1[l], hy_w2[l], hy_b2[l], hy_w3[l], hy_freq1[l], hy_freq2[l], hy_decay[l])
        y_hy = _hyena_mix(hy_l, hy_conv[l], hy_p, hy_bias[l])

        if last:
            pad_c = lambda t: jnp.pad(t.astype(BF16), ((0, 0), (n_ctx, 0), (0, 0)))
            y_hy_a, y_gla_a = pad_c(y_hy), pad_c(y_gla)
        else:
            yc_hy = _hyena_mix(hy_c, hy_conv[l], hy_p, hy_bias[l])
            yc_gla = _gla_readout(o_c, r_c, gla_gnorm[l])
            cat = lambda a, b: jnp.concatenate([a.astype(BF16), b.astype(BF16)], axis=1)
            y_hy_a, y_gla_a = cat(yc_hy, y_hy), cat(yc_gla, y_gla)

        wts = (wl[:, o_gate:].astype(BF16), w_br_hy[l].astype(BF16), w_br_gla[l].astype(BF16),
               w_br_s5[l].astype(BF16), w_out[l].astype(BF16), s5_d[l][None], s5_w_glu[l].astype(BF16),
               s5_b_glu[l][None])
        row_off = 1 if last else 0
        x_mid = _merge(xc, mods, g1, y_hy_a, y_gla_a, ys_a, u_a, wts, row_off)
        wf = w_ffn_in[l]
        res = _ffn(x_mid, mods, g_norm2[l][None], wf[:, :d_ff].astype(BF16), wf[:, d_ff:].astype(BF16),
                   w_ffn_out[l].astype(BF16), g_final[None], row_off, last)
        if last:
            out = res
        else:
            xc = res
    return out
```

```python
import functools
import math

import jax
import jax.numpy as jnp
import numpy as np
from jax import lax
from jax.experimental import pallas as pl
from jax.experimental.pallas import tpu as pltpu

F32 = jnp.float32
BF16 = jnp.bfloat16
EPS = 1e-6

GRID_W = 64
HY_SHORT = 3
HY_POS_BANDS = 16
HY_N_FILT = 4
GLA_HEADS = 4
GLA_DK = 64
GLA_DV = 128
GLA_QK = GLA_HEADS * GLA_DK
GLA_V = GLA_HEADS * GLA_DV
GLA_LOWRANK = 16
GLA_TAU = 16.0
GLA_CHUNK = 64
S5_GROUP = 16
S5_STATE = 64
A_PAD = 128

ROW_TILE = 256
VMEM_LIMIT = 56 * 1024 * 1024


def _const_spec(shape):
    nd = len(shape)
    return pl.BlockSpec(shape, lambda *_: (0,) * nd, pipeline_mode=pl.Buffered(1))


def _params(n_axes):
    return pltpu.CompilerParams(dimension_semantics=("parallel",) * n_axes, vmem_limit_bytes=VMEM_LIMIT)


def _mod_kernel(c_ref, w_ref, b_ref, o_ref):
    c = c_ref[...]
    s = c * jax.nn.sigmoid(c)
    o_ref[0] = jnp.dot(s.astype(BF16), w_ref[0], preferred_element_type=F32) + b_ref[0]


def _modulation_all(cond, w_mod, b_mod):
    depth, d, n = w_mod.shape
    r = cond.shape[0]
    tn = 1536
    return pl.pallas_call(
        _mod_kernel,
        grid=(depth, n // tn),
        in_specs=[
            pl.BlockSpec((r, d), lambda l, j: (0, 0)),
            pl.BlockSpec((1, d, tn), lambda l, j: (l, 0, j)),
            pl.BlockSpec((1, 1, tn), lambda l, j: (l, 0, j)),
        ],
        out_specs=pl.BlockSpec((1, r, tn), lambda l, j: (l, 0, j)),
        out_shape=jax.ShapeDtypeStruct((depth, r, n), F32),
        compiler_params=_params(2),
        name="modulation",
    )(cond, w_mod.astype(BF16), b_mod.reshape(depth, 1, n))


def _norm_mod(x, g, shift, scale):
    y = x * lax.rsqrt(jnp.mean(x * x, axis=-1, keepdims=True) + EPS)
    return (y * g) * (1.0 + scale) + shift


def _proj_kernel(col_sizes, d, x_ref, mod_ref, g_ref, w_ref, *o_refs):
    m = mod_ref[0, 0]
    h = _norm_mod(x_ref[0], g_ref[...], m[:, 0:d], m[:, d:2 * d]).astype(BF16)
    off = 0
    for o_ref, n in zip(o_refs, col_sizes):
        o_ref[0] = jnp.dot(h, w_ref[:, off:off + n], preferred_element_type=F32).astype(o_ref.dtype)
        off += n


def _mod_spec(d, n_lat_tiles):
    return pl.BlockSpec((1, 1, 1, 6 * d), lambda i, j: (i, (j < n_lat_tiles).astype(jnp.int32), 0, 0))


def _project(xc, mods, g, w, col_sizes, n_lat_tiles):
    b, t, d = xc.shape
    tm = ROW_TILE
    n_tot = sum(col_sizes)
    out_shape = [jax.ShapeDtypeStruct((b, t, n), BF16) for n in col_sizes]
    out_specs = [pl.BlockSpec((1, tm, n), lambda i, j: (i, j, 0)) for n in col_sizes]
    return pl.pallas_call(
        functools.partial(_proj_kernel, tuple(col_sizes), d),
        grid=(b, t // tm),
        in_specs=[
            pl.BlockSpec((1, tm, d), lambda i, j: (i, j, 0)),
            _mod_spec(d, n_lat_tiles),
            _const_spec((1, d)),
            _const_spec((d, n_tot)),
        ],
        out_specs=out_specs,
        out_shape=out_shape,
        compiler_params=_params(2),
        name="in_proj",
    )(xc, mods, g, w)


def _gelu_tanh(x):
    return 0.5 * x * (1.0 + jnp.tanh(math.sqrt(2.0 / math.pi) * (x + 0.044715 * (x * x * x))))


def _merge_kernel(d, x_ref, mod_ref, g_ref, yhy_ref, ygla_ref, ys5_ref, u_ref, wg_ref, whb_ref, wgb_ref, wsb_ref,
                  wo_ref, s5d_ref, wglu_ref, bglu_ref, o_ref):
    x = x_ref[0]
    m = mod_ref[0, 0]
    h = _norm_mod(x, g_ref[...], m[:, 0:d], m[:, d:2 * d]).astype(BF16)
    y5 = ys5_ref[0].astype(F32) + s5d_ref[...] * u_ref[0].astype(F32)
    g5 = _gelu_tanh(y5)
    y_s5 = g5 * jax.nn.sigmoid(jnp.dot(g5.astype(BF16), wglu_ref[...], preferred_element_type=F32) + bglu_ref[...])

    def branch(k, y, wb_ref):
        gate = jnp.dot(h, wg_ref[:, k * d:(k + 1) * d], preferred_element_type=F32)
        return jax.nn.sigmoid(gate) * jnp.dot(y, wb_ref[...], preferred_element_type=F32)

    mix = branch(0, yhy_ref[0], whb_ref) + branch(1, ygla_ref[0], wgb_ref) + branch(2, y_s5.astype(BF16), wsb_ref)
    out = jnp.dot(mix.astype(BF16), wo_ref[...], preferred_element_type=F32)
    o_ref[0] = x + m[:, 2 * d:3 * d] * out


def _merge(xc, mods, g, y_hy, y_gla, ys_s5, u_s5, wts, nj, n_lat_tiles):
    b, t, d = xc.shape
    tm = ROW_TILE
    ch = y_hy.shape[-1]
    row = lambda n: pl.BlockSpec((1, tm, n), lambda i, j: (i, j, 0))
    w_gate, w_hy, w_gla, w_s5, w_out, s5_d, w_glu, b_glu = wts
    return pl.pallas_call(
        functools.partial(_merge_kernel, d),
        grid=(b, nj),
        in_specs=[
            row(d),
            _mod_spec(d, n_lat_tiles),
            _const_spec((1, d)),
            row(ch), row(ch), row(ch), row(ch),
            _const_spec(w_gate.shape), _const_spec(w_hy.shape), _const_spec(w_gla.shape), _const_spec(w_s5.shape),
            _const_spec(w_out.shape), _const_spec(s5_d.shape), _const_spec(w_glu.shape), _const_spec(b_glu.shape),
        ],
        out_specs=pl.BlockSpec((1, tm, d), lambda i, j: (i, j, 0)),
        out_shape=jax.ShapeDtypeStruct((b, nj * tm, d), F32),
        compiler_params=_params(2),
        name="merge",
    )(xc, mods, g, y_hy, y_gla, ys_s5, u_s5, w_gate, w_hy, w_gla, w_s5, w_out, s5_d, w_glu, b_glu)


def _ffn_kernel(d, final, x_ref, mod_ref, g_ref, wa_ref, wb_ref, wo_ref, gf_ref, o_ref):
    x = x_ref[0]
    m = mod_ref[0, 0]
    h = _norm_mod(x, g_ref[...], m[:, 3 * d:4 * d], m[:, 4 * d:5 * d]).astype(BF16)
    a = jnp.dot(h, wa_ref[...], preferred_element_type=F32)
    bb = jnp.dot(h, wb_ref[...], preferred_element_type=F32)
    act = (a * jax.nn.sigmoid(a) * bb).astype(BF16)
    y = x + m[:, 5 * d:6 * d] * jnp.dot(act, wo_ref[...], preferred_element_type=F32)
    if final:
        y = y * lax.rsqrt(jnp.mean(y * y, axis=-1, keepdims=True) + EPS) * gf_ref[...]
    o_ref[0] = y


def _ffn(x, mods, g, wa, wb, wo, g_final, n_lat_tiles, final):
    b, t, d = x.shape
    tm = ROW_TILE
    return pl.pallas_call(
        functools.partial(_ffn_kernel, d, final),
        grid=(b, t // tm),
        in_specs=[
            pl.BlockSpec((1, tm, d), lambda i, j: (i, j, 0)),
            _mod_spec(d, n_lat_tiles),
            _const_spec((1, d)),
            _const_spec(wa.shape), _const_spec(wb.shape), _const_spec(wo.shape),
            _const_spec((1, d)),
        ],
        out_specs=pl.BlockSpec((1, tm, d), lambda i, j: (i, j, 0)),
        out_shape=jax.ShapeDtypeStruct((b, t, d), F32),
        compiler_params=_params(2),
        name="ffn",
    )(x, mods, g, wa, wb, wo, g_final)


S5_CHUNK = 64


def _s5_weights(a_re, a_im, log_dt, b_re, b_im, c_re, c_im, tc):
    g, p = a_re.shape[1:]
    i = b_re.shape[-1]
    lam_c = lax.complex(jnp.minimum(a_re, -1e-4), a_im)
    lam_dt = lam_c * jnp.exp(log_dt)[..., None]
    b_bar = ((jnp.exp(lam_dt) - 1.0) / lam_c)[..., None] * lax.complex(b_re, b_im)[None]
    c_mat = lax.complex(c_re, c_im)
    tau = jnp.arange(tc, dtype=F32)
    pw = jnp.exp(lam_dt[:, :, None, :] * tau[None, None, :, None])
    pw1 = pw * jnp.exp(lam_dt)[:, :, None, :]
    kern = jnp.einsum('gip,dgtp,dgpj->dgtij', c_mat, pw, b_bar).real
    lag = jnp.arange(tc)[None, :] - jnp.arange(tc)[:, None]
    k_f = jnp.where((lag >= 0)[None, :, :, None, None], kern[0][:, jnp.clip(lag, 0, tc - 1)], 0.0)
    k_b = jnp.where((lag <= 0)[None, :, :, None, None], kern[1][:, jnp.clip(-lag, 0, tc - 1)], 0.0)
    m = (k_f + k_b).transpose(0, 1, 4, 2, 3).reshape(g, tc * i, tc * i)

    inc_f = pw[0][:, ::-1, None, :] * b_bar[0].transpose(0, 2, 1)[:, None]
    inc_b = pw[1][:, :, None, :] * b_bar[1].transpose(0, 2, 1)[:, None]
    bm = jnp.concatenate([inc_f.real, inc_b.real, inc_f.imag, inc_b.imag], axis=-1).reshape(g, tc * i, 4 * p)

    out_f = c_mat.transpose(0, 2, 1)[:, :, None, :] * pw1[0].transpose(0, 2, 1)[..., None]
    out_b = c_mat.transpose(0, 2, 1)[:, :, None, :] * pw1[1][:, ::-1].transpose(0, 2, 1)[..., None]
    z = jnp.zeros((g, p, tc * i), F32)
    fl = lambda t: t.reshape(g, p, tc * i)
    cm = jnp.concatenate([fl(out_f.real), z, fl(-out_f.imag), z, z, fl(out_b.real), z, fl(-out_b.imag)], axis=1)

    lam_t = jnp.exp(lam_dt * float(tc))
    lam = jnp.stack([jnp.concatenate([lam_t[0].real, lam_t[1].real], -1),
                     jnp.concatenate([lam_t[0].imag, lam_t[1].imag], -1)], axis=1)
    return m.astype(BF16), bm.astype(BF16), cm.astype(BF16), lam


def _s5_kernel(nc, nc_ctx, bn, u_ref, m_ref, bm_ref, cm_ref, lam_ref, y_ref, dx_ref, p_ref):
    nl = nc - nc_ctx
    u = u_ref[0]
    dx_ref[...] = jnp.dot(u, bm_ref[0], preferred_element_type=F32)
    lam = lam_ref[0]
    lr, li = lam[0:1], lam[1:2]
    half = lam.shape[-1] // 2
    is_f = lax.broadcasted_iota(jnp.int32, (bn, 2 * half), 1) < half

    def step(s, carry):
        sr, si = carry
        nf = jnp.where(s < nc_ctx, nl + s, s - nc_ctx)
        nb = nc - 1 - s
        rf = pl.multiple_of(nf * bn, bn)
        rb = pl.multiple_of(nb * bn, bn)
        p_ref[pl.ds(rf, bn), 0:128] = sr
        p_ref[pl.ds(rf, bn), 128:256] = si
        p_ref[pl.ds(rb, bn), 256:384] = sr
        p_ref[pl.ds(rb, bn), 384:512] = si
        d_re = jnp.where(is_f, dx_ref[pl.ds(rf, bn), 0:128], dx_ref[pl.ds(rb, bn), 0:128])
        d_im = jnp.where(is_f, dx_ref[pl.ds(rf, bn), 128:256], dx_ref[pl.ds(rb, bn), 128:256])
        return lr * sr - li * si + d_re, lr * si + li * sr + d_im

    zero = jnp.zeros((bn, 2 * half), F32)
    lax.fori_loop(0, nc, step, (zero, zero))
    y = jnp.dot(u, m_ref[0], preferred_element_type=F32)
    y = y + jnp.dot(p_ref[...].astype(BF16), cm_ref[0], preferred_element_type=F32)
    y_ref[0] = y.astype(y_ref.dtype)


def _s5_mix(u_all, n_ctx, wts):
    m, bm, cm, lam = wts
    bn, t, ch = u_all.shape
    g = m.shape[0]
    i = ch // g
    tc = m.shape[1] // i
    assert S5_STATE == 64 and bn % 8 == 0 and n_ctx % tc == 0 and t % tc == 0
    nc, nc_ctx = t // tc, n_ctx // tc
    k = tc * i
    r = nc * bn
    u_g = u_all.reshape(bn, nc, tc, g, i).transpose(3, 1, 0, 2, 4).reshape(g, r, k)
    y_g = pl.pallas_call(
        functools.partial(_s5_kernel, nc, nc_ctx, bn),
        grid=(g,),
        in_specs=[
            pl.BlockSpec((1, r, k), lambda j: (j, 0, 0)),
            pl.BlockSpec((1, k, k), lambda j: (j, 0, 0)),
            pl.BlockSpec((1, k, 4 * S5_STATE), lambda j: (j, 0, 0)),
            pl.BlockSpec((1, 8 * S5_STATE, k), lambda j: (j, 0, 0)),
            pl.BlockSpec((1, 2, 2 * S5_STATE), lambda j: (j, 0, 0)),
        ],
        out_specs=pl.BlockSpec((1, r, k), lambda j: (j, 0, 0)),
        out_shape=jax.ShapeDtypeStruct((g, r, k), BF16),
        scratch_shapes=[pltpu.VMEM((r, 4 * S5_STATE), F32), pltpu.VMEM((r, 8 * S5_STATE), F32)],
        compiler_params=_params(1),
        name="s5_mix",
    )(u_g, m, bm, cm, lam)
    return y_g.reshape(g, nc, bn, tc, i).transpose(2, 1, 3, 0, 4).reshape(bn, t, ch)


_NT = (((1,), (1,)), ((), ()))
_TN = (((0,), (0,)), ((), ()))


def _gla_kernel(nc, nc_ctx, k_ref, v_ref, a_ref, q_ref, r_ref, wa_ref, ba_ref, gn_ref, y_ref, of_ref):
    c, h_n, dk, dv = GLA_CHUNK, GLA_HEADS, GLA_DK, GLA_DV
    qk = h_n * dk
    q_scale = dk ** -0.5
    row_i = lax.broadcasted_iota(jnp.int32, (c, c), 0)
    col_i = lax.broadcasted_iota(jnp.int32, (c, c), 1)
    lower = row_i >= col_i
    tri = (lower.astype(BF16), (row_i <= col_i).astype(BF16))
    att_mask = (jnp.concatenate([lower] * h_n, axis=0), jnp.concatenate([row_i <= col_i] * h_n, axis=0))
    lane_head = lax.broadcasted_iota(jnp.int32, (1, qk), 1) // dk
    head_lanes = [lane_head == h for h in range(h_n)]

    def chunk(n, s_t, d):
        rows = pl.ds(pl.multiple_of(n * c, c), c)
        k = k_ref[0, rows, :].astype(F32)
        q = q_ref[0, rows, :].astype(F32) * q_scale
        v = v_ref[0, rows, :]
        z = jnp.dot(a_ref[0, rows, :], wa_ref[:, d * qk:(d + 1) * qk], preferred_element_type=F32) + ba_ref[d:d + 1, :]
        log_a = (jnp.minimum(z, 0.0) - jnp.log(1.0 + jnp.exp(-jnp.abs(z)))) * (1.0 / GLA_TAU)
        hi = log_a.astype(BF16)
        lo = (log_a - hi.astype(F32)).astype(BF16)
        cs = jnp.dot(tri[d], jnp.concatenate([hi, lo], axis=1), preferred_element_type=F32)
        b = cs[:, :qk] + cs[:, qk:]
        b_end = b[c - 1:c, :] if d == 0 else b[0:1, :]
        qd = (q * jnp.exp(b)).astype(BF16)
        kd = (k * jnp.exp(-b)).astype(BF16)
        kl = (k * jnp.exp(b_end - b)).astype(BF16)
        zero = jnp.zeros_like(qd)
        qm = jnp.concatenate([jnp.where(m, qd, zero) for m in head_lanes], axis=0)
        att = lax.dot_general(qm, kd, _NT, preferred_element_type=F32)
        att = jnp.where(att_mask[d], att, 0.0).astype(BF16)
        inter = lax.dot_general(qm, s_t.astype(BF16), _NT, preferred_element_type=F32)
        o = jnp.concatenate(
            [jnp.dot(att[h * c:(h + 1) * c], v[:, h * dv:(h + 1) * dv], preferred_element_type=F32)
             + inter[h * c:(h + 1) * c] for h in range(h_n)], axis=1)
        ds = lax.dot_general(v, kl, _TN, preferred_element_type=F32)
        s_new = s_t * jnp.exp(b_end)
        for h in range(h_n):
            s_new = s_new + jnp.where(head_lanes[h], ds[h * dv:(h + 1) * dv], 0.0)
        return rows, o, s_new

    def fwd_step(i, s_t, lo_chunk):
        rows, o, s_new = chunk(lo_chunk + i, s_t, 0)
        of_ref[rows, :] = o
        return s_new

    def bwd_step(i, s_t, hi_chunk):
        rows, o, s_new = chunk(hi_chunk - 1 - i, s_t, 1)
        o = o + of_ref[rows, :]
        r = r_ref[0, rows, :].astype(F32)
        gate = r * jax.nn.sigmoid(r)
        outs = []
        for h in range(h_n):
            oh = o[:, h * dv:(h + 1) * dv]
            oh = oh * lax.rsqrt(jnp.mean(oh * oh, axis=-1, keepdims=True) + EPS) * gn_ref[...]
            outs.append(oh * gate[:, h * dv:(h + 1) * dv])
        y_ref[0, rows, :] = jnp.concatenate(outs, axis=1).astype(y_ref.dtype)
        return s_new

    nl = nc - nc_ctx
    s0 = jnp.zeros((dv, qk), F32)
    s_f = lax.fori_loop(0, nc_ctx, functools.partial(fwd_step, lo_chunk=nl), s0)
    lax.fori_loop(0, nl, functools.partial(fwd_step, lo_chunk=0), s_f)
    s_b = lax.fori_loop(0, nc_ctx, functools.partial(bwd_step, hi_chunk=nc), s0)
    lax.fori_loop(0, nl, functools.partial(bwd_step, hi_chunk=nl), s_b)


def _gla_mix(k, v, a, q, r, wa, ba, gnorm, n_ctx):
    bn, t, qk = k.shape
    vd = v.shape[-1]
    c = GLA_CHUNK
    assert t % c == 0 and n_ctx % c == 0
    seq = lambda n: pl.BlockSpec((1, t, n), lambda i: (i, 0, 0), pipeline_mode=pl.Buffered(1))
    return pl.pallas_call(
        functools.partial(_gla_kernel, t // c, n_ctx // c),
        grid=(bn,),
        in_specs=[seq(qk), seq(vd), seq(a.shape[-1]), seq(qk), seq(vd),
                  _const_spec(wa.shape), _const_spec(ba.shape), _const_spec(gnorm.shape)],
        out_specs=pl.BlockSpec((1, t, vd), lambda i: (i, 0, 0)),
        out_shape=jax.ShapeDtypeStruct((bn, t, vd), BF16),
        scratch_shapes=[pltpu.VMEM((t, vd), F32)],
        compiler_params=_params(1),
        name="gla_mix",
    )(k, v, a, q, r, wa, ba, gnorm)


HY_NB = 128
HY_XPITCH = HY_NB + 8
HY_APITCH = 2 * HY_NB + 8
HY_CT = 128
HY_UNROLL_OUTER = 4
HY_UNROLL_INNER = 2


def _hyena_filters(n, params):
    w1, b1, w2, b2, w3, freq1, freq2, decay = (p.astype(F32) for p in params)
    t = jnp.linspace(0.0, 1.0, n, dtype=F32)[:, None]
    bands = jnp.linspace(1e-4, HY_POS_BANDS - 1, HY_POS_BANDS, dtype=F32)
    ang = (2.0 * math.pi / n) * jnp.arange(n, dtype=F32)[:, None] * bands[None]
    z = jnp.concatenate([t, jnp.cos(ang), -jnp.sin(ang)], axis=-1)
    hid = jnp.sin(freq1 * (z @ w1 + b1))
    hid = jnp.sin(freq2 * (hid @ w2 + b2))
    h = (hid @ w3) * jnp.exp(-t * jnp.abs(decay))
    return h.reshape(n, HY_N_FILT, -1)


def _hyena_spectra(n, params):
    h = _hyena_filters(n, params)

    def spectrum(h_fwd, h_bwd):
        k = jnp.concatenate([h_fwd[:1] + h_bwd[:1], h_fwd[1:], jnp.zeros_like(h_fwd[:1]), h_bwd[:0:-1]], axis=0)
        k = k * lax.rsqrt(jnp.sum(k * k, axis=0, keepdims=True) + EPS)
        return jnp.fft.fft(k, axis=0)

    return spectrum(h[:, 0], h[:, 1]), spectrum(h[:, 2], h[:, 3])


def _two_stage_layout(kf, na):
    n, ch = kf.shape
    kf = kf.reshape(n // na, na, ch).transpose(1, 0, 2)
    return jnp.stack([kf.real, kf.imag], axis=1).astype(BF16)


def _cplx_block(m):
    return np.block([[m.real, -m.imag], [m.imag, m.real]])


@functools.lru_cache(maxsize=None)
def _dft_tables(n_seq):
    nb = HY_NB
    n = 2 * n_seq
    na = n // nb
    ha = na // 2
    ka = np.arange(na)[:, None]
    b = np.arange(nb)
    tw = np.exp(-2j * np.pi * ka * b[None, :] / n)
    w_a = np.exp(-2j * np.pi * ka * np.arange(ha)[None, :] / na)
    f1 = np.stack([_cplx_block(tw[:, i:i + 1] * w_a) for i in range(nb)])
    g = np.stack([_cplx_block((np.conj(tw[:, i:i + 1] * w_a)).T / n) for i in range(nb)])
    w_b = np.exp(-2j * np.pi * b[:, None] * b[None, :] / nb)
    f2 = _cplx_block(w_b)
    f2i = _cplx_block(np.conj(w_b))
    return tuple(jnp.asarray(t, dtype=BF16) for t in (f1, f2, f2i, g))


def _hyena_kernel(na, x1_ref, x2_ref, v_ref, cw_ref, bias_ref, kf1_ref, kf2_ref, f1_ref, f2_ref, f2i_ref, g_ref,
                  o_ref, rawf, vp, gp, yp, a2):
    nb, xp, ap = HY_NB, HY_XPITCH, HY_APITCH
    ha = na // 2
    n_seq = ha * nb
    ct = o_ref.shape[-1]
    rawf[:, 0:8, :] = jnp.zeros((2, 8, ct), F32)
    rawf[:, 8 + n_seq:16 + n_seq, :] = jnp.zeros((2, 8, ct), F32)

    def short_conv_into(raw_ref, which, dst):
        w0, w1, w2 = (cw_ref[j, which:which + 1, :] for j in range(HY_SHORT))
        for e in range(2):
            rawf[e, 8:8 + n_seq, :] = raw_ref[e].astype(F32)

            def body(a, c):
                win = rawf[e, pl.ds(pl.multiple_of(a * nb, nb), nb + 16), :]
                u = w0 * win[7:7 + nb] + w1 * win[8:8 + nb] + w2 * win[9:9 + nb]
                dst[e, pl.ds(pl.multiple_of(a * xp, 8), nb), :] = u
                return c

            lax.fori_loop(0, ha, body, 0)

    def long_conv(kf_ref):
        def stage1(b, c):
            xs = jnp.concatenate([vp[0, pl.ds(b, ha, stride=xp), :], vp[1, pl.ds(b, ha, stride=xp), :]], axis=0)
            r = jnp.dot(f1_ref[b], xs.astype(BF16), preferred_element_type=F32)
            a2[pl.ds(b, na, stride=ap), :] = r[:na]
            a2[pl.ds(nb + b, na, stride=ap), :] = r[na:]
            return c

        def stage2(ka, c):
            rows = pl.ds(pl.multiple_of(ka * ap, 8), 2 * nb)
            xf = jnp.dot(f2_ref[...], a2[rows, :].astype(BF16), preferred_element_type=F32)
            xr, xi = xf[:nb], xf[nb:]
            kr, ki = kf_ref[ka, 0].astype(F32), kf_ref[ka, 1].astype(F32)
            z = jnp.concatenate([xr * kr - xi * ki, xr * ki + xi * kr], axis=0).astype(BF16)
            a2[rows, :] = jnp.dot(f2i_ref[...], z, preferred_element_type=F32)
            return c

        def stage3(b, c):
            s = jnp.concatenate([a2[pl.ds(b, na, stride=ap), :], a2[pl.ds(nb + b, na, stride=ap), :]], axis=0)
            y = jnp.dot(g_ref[b], s.astype(BF16), preferred_element_type=F32)
            yp[0, pl.ds(b, ha, stride=xp), :] = y[:ha]
            yp[1, pl.ds(b, ha, stride=xp), :] = y[ha:]
            return c

        lax.fori_loop(0, nb, stage1, 0, unroll=HY_UNROLL_OUTER)
        lax.fori_loop(0, na, stage2, 0, unroll=HY_UNROLL_INNER)
        lax.fori_loop(0, nb, stage3, 0, unroll=HY_UNROLL_OUTER)

    def gated(bias_row, write):
        for e in range(2):
            def body(a, c):
                rows = pl.ds(pl.multiple_of(a * xp, 8), nb)
                write(e, a, rows, gp[e, rows, :] * (yp[e, rows, :] + bias_row * vp[e, rows, :]))
                return c

            lax.fori_loop(0, ha, body, 0)

    def write_z(e, a, rows, val):
        vp[e, rows, :] = val

    def write_out(e, a, rows, val):
        o_ref[e, pl.ds(pl.multiple_of(a * nb, nb), nb), :] = val.astype(o_ref.dtype)

    short_conv_into(v_ref, 2, vp)
    short_conv_into(x1_ref, 0, gp)
    long_conv(kf1_ref)
    gated(bias_ref[0:1, :], write_z)
    short_conv_into(x2_ref, 1, gp)
    long_conv(kf2_ref)
    gated(bias_ref[1:2, :], write_out)


def _hyena_latent(hy_all, n_seq, conv_w, bias, kf1, kf2):
    bn, t_all = hy_all.shape[:2]
    ch = hy_all.shape[-1] // 3
    ct = HY_CT
    nct = ch // ct
    nb = HY_NB
    na = 2 * n_seq // nb
    assert bn % 2 == 0 and ch % ct == 0 and n_seq % (8 * nb) == 0
    f1, f2, f2i, g = _dft_tables(n_seq)
    col = lambda which: pl.BlockSpec((2, n_seq, ct), lambda j, p: (p, 0, which * nct + j), pipeline_mode=pl.Buffered(1))
    kf_spec = pl.BlockSpec((na, 2, nb, ct), lambda j, p: (0, 0, 0, j), pipeline_mode=pl.Buffered(1))
    pad_rows = (na // 2) * HY_XPITCH
    return pl.pallas_call(
        functools.partial(_hyena_kernel, na),
        grid=(nct, bn // 2),
        in_specs=[col(0), col(1), col(2),
                  pl.BlockSpec((HY_SHORT, 3, ct), lambda j, p: (0, 0, j)),
                  pl.BlockSpec((2, ct), lambda j, p: (0, j)),
                  kf_spec, kf_spec,
                  _const_spec(f1.shape), _const_spec(f2.shape), _const_spec(f2i.shape), _const_spec(g.shape)],
        out_specs=pl.BlockSpec((2, n_seq, ct), lambda j, p: (p, 0, j)),
        out_shape=jax.ShapeDtypeStruct((bn, t_all, ch), BF16),
        scratch_shapes=[pltpu.VMEM((2, n_seq + 16, ct), F32), pltpu.VMEM((2, pad_rows, ct), F32),
                        pltpu.VMEM((2, pad_rows, ct), F32), pltpu.VMEM((2, pad_rows, ct), F32),
                        pltpu.VMEM((na * HY_APITCH, ct), F32)],
        compiler_params=_params(2),
        name="hyena_latent",
    )(hy_all, hy_all, hy_all, conv_w.reshape(HY_SHORT, 3, ch), bias, _two_stage_layout(kf1, na),
      _two_stage_layout(kf2, na), f1, f2, f2i, g)


@functools.lru_cache(maxsize=None)
def _dense_dft_tables(n_seq):
    n = 2 * n_seq
    ang = 2.0 * np.pi * np.arange(n)[:, None] * np.arange(n_seq)[None, :] / n
    fd = np.concatenate([np.cos(ang), -np.sin(ang)], axis=0)
    gd = np.concatenate([np.cos(ang).T, -np.sin(ang).T], axis=1) / n
    return jnp.asarray(fd, dtype=BF16), jnp.asarray(gd, dtype=BF16)


def _hyena_ctx_kernel(n_seq, x1_ref, x2_ref, v_ref, cw_ref, bias_ref, kf1_ref, kf2_ref, fd_ref, gd_ref, buf_ref,
                      o_ref):
    del buf_ref
    n = 2 * n_seq

    def short_conv(raw_ref, which):
        x = raw_ref[0].astype(F32)
        zero = jnp.zeros((1, x.shape[-1]), F32)
        prev = jnp.concatenate([zero, x[:-1]], axis=0)
        nxt = jnp.concatenate([x[1:], zero], axis=0)
        return cw_ref[0, which:which + 1, :] * prev + cw_ref[1, which:which + 1, :] * x + cw_ref[2, which:which + 1, :] * nxt

    def long_conv(u, kf_ref, bias_row):
        xf = jnp.dot(fd_ref[...], u.astype(BF16), preferred_element_type=F32)
        xr, xi = xf[:n], xf[n:]
        kr, ki = kf_ref[0], kf_ref[1]
        z = jnp.concatenate([xr * kr - xi * ki, xr * ki + xi * kr], axis=0).astype(BF16)
        return jnp.dot(gd_ref[...], z, preferred_element_type=F32) + bias_row * u

    x1, x2, v = short_conv(x1_ref, 0), short_conv(x2_ref, 1), short_conv(v_ref, 2)
    z = x1 * long_conv(v, kf1_ref, bias_ref[0:1, :])
    o_ref[0] = (x2 * long_conv(z, kf2_ref, bias_ref[1:2, :])).astype(o_ref.dtype)


def _hyena_context(hy_all, y_buf, row_block, n_seq, conv_w, bias, kf1, kf2):
    bn = hy_all.shape[0]
    ch = hy_all.shape[-1] // 3
    fd, gd = _dense_dft_tables(n_seq)
    col = lambda which: pl.BlockSpec((1, n_seq, ch), lambda i: (i, row_block, which))
    split = lambda kf: jnp.stack([kf.real, kf.imag], axis=0)
    return pl.pallas_call(
        functools.partial(_hyena_ctx_kernel, n_seq),
        grid=(bn,),
        in_specs=[col(0), col(1), col(2), _const_spec((HY_SHORT, 3, ch)), _const_spec((2, ch)),
                  _const_spec((2, 2 * n_seq, ch)), _const_spec((2, 2 * n_seq, ch)),
                  _const_spec(fd.shape), _const_spec(gd.shape), pl.BlockSpec(memory_space=pl.ANY)],
        out_specs=pl.BlockSpec((1, n_seq, ch), lambda i: (i, row_block, 0)),
        out_shape=jax.ShapeDtypeStruct(y_buf.shape, y_buf.dtype),
        input_output_aliases={9: 0},
        compiler_params=_params(1),
        name="hyena_context",
    )(hy_all, hy_all, hy_all, conv_w.reshape(HY_SHORT, 3, ch), bias, split(kf1), split(kf2), fd, gd, y_buf)


_hyena_filters_new = _hyena_filters

def _rmsnorm(x, g):
    y = x * lax.rsqrt(jnp.mean(x * x, axis=-1, keepdims=True) + EPS)
    return y * g


def _short_conv(u, w):
    k = w.shape[0]
    pad = k // 2
    n = u.shape[1]
    up = jnp.pad(u, ((0, 0), (pad, pad), (0, 0)))
    return sum(up[:, j:j + n] * w[j] for j in range(k))


def _hyena_filters(n, params):
    w1, b1, w2, b2, w3, freq1, freq2, decay = (p.astype(F32) for p in params)
    t = jnp.linspace(0.0, 1.0, n, dtype=F32)[:, None]
    bands = jnp.linspace(1e-4, HY_POS_BANDS - 1, HY_POS_BANDS, dtype=F32)
    ang = (2.0 * math.pi / n) * jnp.arange(n, dtype=F32)[:, None] * bands[None]
    z = jnp.concatenate([t, jnp.cos(ang), -jnp.sin(ang)], axis=-1)
    hid = jnp.sin(freq1 * (z @ w1 + b1))
    hid = jnp.sin(freq2 * (hid @ w2 + b2))
    h = (hid @ w3) * jnp.exp(-t * jnp.abs(decay))
    return h.reshape(n, HY_N_FILT, -1)


def _long_conv_bidir(u, h_fwd, h_bwd, bias):
    n = u.shape[1]
    k = jnp.concatenate([h_fwd[:1] + h_bwd[:1], h_fwd[1:], jnp.zeros_like(h_fwd[:1]), h_bwd[:0:-1]], axis=0)
    k = k * lax.rsqrt(jnp.sum(k * k, axis=0, keepdims=True) + EPS)
    u_f = jnp.fft.rfft(u, n=2 * n, axis=1)
    k_f = jnp.fft.rfft(k, axis=0)
    y = jnp.fft.irfft(u_f * k_f[None], n=2 * n, axis=1)[:, :n]
    return y + u * bias


def _hyena_mix(cols, conv_w, filt_params, bias):
    u = _short_conv(cols.astype(F32), conv_w.astype(F32))
    x1, x2, v = jnp.split(u, 3, axis=-1)
    h = _hyena_filters(cols.shape[1], filt_params)
    z = x1 * _long_conv_bidir(v, h[:, 0], h[:, 1], bias[0])
    return x2 * _long_conv_bidir(z, h[:, 2], h[:, 3], bias[1])


def _to_heads(t, d):
    b, n, _ = t.shape
    return t.astype(F32).reshape(b, n, GLA_HEADS, d).transpose(0, 2, 1, 3)


def _gla_decays(a_lr, wa2, ba):
    b, n, _ = a_lr.shape
    z = jnp.einsum('bldr,drk->bldk', a_lr.astype(F32).reshape(b, n, 2, GLA_LOWRANK), wa2.astype(F32)) + ba
    log_a = jax.nn.log_sigmoid(z) / GLA_TAU
    return _to_heads(log_a[:, :, 0], GLA_DK), _to_heads(log_a[:, :, 1], GLA_DK)


def _gla_scan(k, v, log_a, s0, q=None):
    b_, h_, n_, dk = k.shape
    c = GLA_CHUNK
    nb = n_ // c
    blk = lambda t: t.reshape(b_, h_, nb, c, t.shape[-1])
    k, v = blk(k), blk(v)
    b = jnp.cumsum(blk(log_a), axis=3)
    b_last = b[:, :, :, -1:]
    ds = jnp.einsum('bhncd,bhnce->bhnde', k * jnp.exp(b_last - b), v)
    decay = jnp.exp(b_last[:, :, :, 0])
    want = q is not None

    def step(s, inp):
        dec, d_s = inp
        return dec[..., None] * s + d_s, (s if want else None)

    s_fin, s_prev = lax.scan(step, s0, (jnp.moveaxis(decay, 2, 0), jnp.moveaxis(ds, 2, 0)))
    if not want:
        return None, s_fin
    q_dec = blk(q) * jnp.exp(b)
    att = jnp.einsum('bhncd,bhnsd->bhncs', q_dec, k * jnp.exp(-b))
    att = jnp.where(jnp.tril(jnp.ones((c, c), bool)), att, 0.0)
    o = jnp.einsum('bhncs,bhnse->bhnce', att, v) + jnp.einsum('bhncd,bhnde->bhnce', q_dec, jnp.moveaxis(s_prev, 0, 2))
    return o.reshape(b_, h_, n_, -1), s_fin


def _gla_bidir(k, v, la_f, la_b, s0_f, s0_b, q=None):
    fl = lambda t: jnp.flip(t, axis=2)
    o_f, s_f = _gla_scan(k, v, la_f, s0_f, q)
    o_b, s_b = _gla_scan(fl(k), fl(v), fl(la_b), s0_b, None if q is None else fl(q))
    o = None if q is None else o_f + fl(o_b)
    return o, s_f, s_b


def _gla_readout(o, r, gnorm):
    o = _rmsnorm(o, gnorm)
    b, h, n, dv = o.shape
    return o.transpose(0, 2, 1, 3).reshape(b, n, h * dv) * jax.nn.silu(r.astype(F32))


def _s5_discretize(a_re, a_im, log_dt, b_re, b_im):
    lam = lax.complex(jnp.minimum(a_re, -1e-4), a_im)
    lam_dt = lam * jnp.exp(log_dt)[:, None]
    b_bar = ((jnp.exp(lam_dt) - 1.0) / lam)[..., None] * lax.complex(b_re, b_im)
    return lam_dt, b_bar


def _s5_scan(u, lam_dt, b_bar, x0, c_mat=None):
    bn, n, g, i = u.shape
    t = GRID_W
    rows = n // t
    lam_bar = jnp.exp(lam_dt)
    pows = jnp.exp(lam_dt[None] * jnp.arange(1, t + 1, dtype=F32)[:, None, None])
    u_blocks = jnp.moveaxis(u.reshape(bn, rows, t, g, i), 1, 0)

    def combine(e1, e2):
        a1, b1 = e1
        a2, b2 = e2
        return a1 * a2, a2 * b1 + b2

    def step(x, u_blk):
        bu = jnp.einsum('btgi,gpi->btgp', u_blk.astype(jnp.complex64), b_bar)
        _, xs = lax.associative_scan(combine, (jnp.broadcast_to(lam_bar, bu.shape), bu), axis=1)
        xs = xs + pows[None] * x[:, None]
        y = None if c_mat is None else jnp.einsum('btgp,gip->btgi', xs, c_mat).real
        return xs[:, -1], y

    x_fin, ys = lax.scan(step, x0, u_blocks)
    if c_mat is None:
        return None, x_fin
    return jnp.moveaxis(ys, 0, 1).reshape(bn, n, g, i), x_fin


def _s5_bidir(u, disc_f, disc_b, x0_f, x0_b, c_mat=None):
    y_f, x_f = _s5_scan(u, disc_f[0], disc_f[1], x0_f, c_mat)
    y_b, x_b = _s5_scan(jnp.flip(u, 1), disc_b[0], disc_b[1], x0_b, c_mat)
    y = None if c_mat is None else y_f + jnp.flip(y_b, 1)
    return y, x_f, x_b


def _to_groups(u):
    b, n, ch = u.shape
    return u.astype(F32).reshape(b, n, ch // S5_GROUP, S5_GROUP)


def _kernel_old(x, c, ctx, c_ctx, w_mod, b_mod, g_norm1, g_norm2, w_in, hy_conv, hy_w1, hy_b1, hy_w2, hy_b2, hy_w3,
           hy_freq1, hy_freq2, hy_decay, hy_bias, gla_wa2, gla_ba, gla_gnorm, s5_a_re, s5_a_im, s5_log_dt, s5_b_re,
           s5_b_im, s5_c_re, s5_c_im, s5_d, s5_w_glu, s5_b_glu, w_br_hy, w_br_gla, w_br_s5, w_out, w_ffn_in,
           w_ffn_out, g_final):
    bn, n_lat, d = x.shape
    n_ctx = ctx.shape[1]
    depth = w_in.shape[0]
    d_hy = w_br_hy.shape[1]
    s5_ch = w_br_s5.shape[1]
    d_ff = w_ffn_out.shape[1]
    assert n_ctx == ROW_TILE and n_lat % ROW_TILE == 0
    groups = s5_ch // S5_GROUP
    n_low = 2 * GLA_LOWRANK
    o_k, o_v, o_a, o_u = 0, GLA_QK, GLA_QK + GLA_V, GLA_QK + GLA_V + n_low
    o_q = o_u + s5_ch
    o_r = o_q + GLA_QK
    o_hy = o_r + GLA_V
    o_gate = o_hy + 3 * d_hy
    col_sizes = (GLA_QK, GLA_V, A_PAD, s5_ch, GLA_QK, GLA_V, 3 * d_hy)

    cond = jnp.concatenate([c_ctx[None], c], axis=0)
    pad_rows = (-cond.shape[0]) % 8
    cond = jnp.pad(cond, ((0, pad_rows), (0, 0)))
    mod_all = _modulation_all(cond, w_mod, b_mod)

    xc = jnp.concatenate([ctx, x], axis=1)
    zeros_s = jnp.zeros((bn, GLA_HEADS, GLA_DK, GLA_DV), F32)
    zeros_x = jnp.zeros((bn, groups, S5_STATE), jnp.complex64)
    q_scale = GLA_DK ** -0.5
    out = None
    for l in range(depth):
        last = l == depth - 1
        mods = jnp.stack([jnp.broadcast_to(mod_all[l, 0], (bn, 6 * d)), mod_all[l, 1:1 + bn]], axis=1)
        mods = mods[:, :, None, :]
        wl = w_in[l]
        w_proj = jnp.concatenate(
            [wl[:, o_k:o_a + n_low], jnp.zeros((d, A_PAD - n_low), F32), wl[:, o_u:o_gate]], axis=1).astype(BF16)
        g1 = g_norm1[l][None]
        k_a, v_a, a_a, u_a, q_a, r_a, hy_a = _project(xc, mods, g1, w_proj, col_sizes)
        hy_c, hy_l = hy_a[:, :n_ctx], hy_a[:, n_ctx:]

        wa = jnp.zeros((A_PAD, 2 * GLA_QK), F32)
        wa = wa.at[:GLA_LOWRANK, :GLA_QK].set(gla_wa2[l, 0]).at[GLA_LOWRANK:n_low, GLA_QK:].set(gla_wa2[l, 1])
        y_gla_a = _gla_mix(k_a, v_a, a_a, q_a, r_a, wa.astype(BF16), gla_ba[l], gla_gnorm[l][None], n_ctx)

        s5w = _s5_weights(s5_a_re[l], s5_a_im[l], s5_log_dt[l], s5_b_re[l], s5_b_im[l], s5_c_re[l], s5_c_im[l],
                          S5_CHUNK)
        ys_a = _s5_mix(u_a, n_ctx, s5w)

        hy_p = (hy_w1[l], hy_b1[l], hy_w2[l], hy_b2[l], hy_w3[l], hy_freq1[l], hy_freq2[l], hy_decay[l])
        y_hy = _hyena_mix(hy_l, hy_conv[l], hy_p, hy_bias[l])

        if last:
            y_hy_a = jnp.pad(y_hy.astype(BF16), ((0, 0), (n_ctx, 0), (0, 0)))
        else:
            yc_hy = _hyena_mix(hy_c, hy_conv[l], hy_p, hy_bias[l])
            y_hy_a = jnp.concatenate([yc_hy.astype(BF16), y_hy.astype(BF16)], axis=1)

        wts = (wl[:, o_gate:].astype(BF16), w_br_hy[l].astype(BF16), w_br_gla[l].astype(BF16),
               w_br_s5[l].astype(BF16), w_out[l].astype(BF16), s5_d[l][None], s5_w_glu[l].astype(BF16),
               s5_b_glu[l][None])
        row_off = 1 if last else 0
        x_mid = _merge(xc, mods, g1, y_hy_a, y_gla_a, ys_a, u_a, wts, row_off)
        wf = w_ffn_in[l]
        res = _ffn(x_mid, mods, g_norm2[l][None], wf[:, :d_ff].astype(BF16), wf[:, d_ff:].astype(BF16),
                   w_ffn_out[l].astype(BF16), g_final[None], row_off, last)
        if last:
            out = res
        else:
            xc = res
    return out


def kernel(x, c, ctx, c_ctx, w_mod, b_mod, g_norm1, g_norm2, w_in, hy_conv, hy_w1, hy_b1, hy_w2, hy_b2, hy_w3,
           hy_freq1, hy_freq2, hy_decay, hy_bias, gla_wa2, gla_ba, gla_gnorm, s5_a_re, s5_a_im, s5_log_dt, s5_b_re,
           s5_b_im, s5_c_re, s5_c_im, s5_d, s5_w_glu, s5_b_glu, w_br_hy, w_br_gla, w_br_s5, w_out, w_ffn_in,
           w_ffn_out, g_final):
    bn, n_lat, d = x.shape
    n_ctx = ctx.shape[1]
    depth = w_in.shape[0]
    d_hy = w_br_hy.shape[1]
    s5_ch = w_br_s5.shape[1]
    d_ff = w_ffn_out.shape[1]
    assert n_ctx == ROW_TILE and n_lat % ROW_TILE == 0
    n_lat_tiles = n_lat // ROW_TILE
    n_low = 2 * GLA_LOWRANK
    o_a = GLA_QK + GLA_V
    o_u = o_a + n_low
    o_gate = o_u + s5_ch + GLA_QK + GLA_V + 3 * d_hy
    col_sizes = (GLA_QK, GLA_V, A_PAD, s5_ch, GLA_QK, GLA_V, 3 * d_hy)

    cond = jnp.concatenate([c_ctx[None], c], axis=0)
    cond = jnp.pad(cond, ((0, (-cond.shape[0]) % 8), (0, 0)))
    mod_all = _modulation_all(cond, w_mod, b_mod)

    xc = jnp.concatenate([x, ctx], axis=1)
    out = None
    for l in range(depth):
        last = l == depth - 1
        mods = jnp.stack([jnp.broadcast_to(mod_all[l, 0], (bn, 6 * d)), mod_all[l, 1:1 + bn]], axis=1)
        mods = mods[:, :, None, :]
        wl = w_in[l]
        w_proj = jnp.concatenate(
            [wl[:, :o_u], jnp.zeros((d, A_PAD - n_low), F32), wl[:, o_u:o_gate]], axis=1).astype(BF16)
        g1 = g_norm1[l][None]
        k_a, v_a, a_a, u_a, q_a, r_a, hy_a = _project(xc, mods, g1, w_proj, col_sizes, n_lat_tiles)

        wa = jnp.zeros((A_PAD, 2 * GLA_QK), F32)
        wa = wa.at[:GLA_LOWRANK, :GLA_QK].set(gla_wa2[l, 0]).at[GLA_LOWRANK:n_low, GLA_QK:].set(gla_wa2[l, 1])
        y_gla = _gla_mix(k_a, v_a, a_a, q_a, r_a, wa.astype(BF16), gla_ba[l], gla_gnorm[l][None], n_ctx)

        s5w = _s5_weights(s5_a_re[l], s5_a_im[l], s5_log_dt[l], s5_b_re[l], s5_b_im[l], s5_c_re[l], s5_c_im[l],
                          S5_CHUNK)
        ys = _s5_mix(u_a, n_ctx, s5w)

        hy_p = (hy_w1[l], hy_b1[l], hy_w2[l], hy_b2[l], hy_w3[l], hy_freq1[l], hy_freq2[l], hy_decay[l])
        y_hy = _hyena_latent(hy_a, n_lat, hy_conv[l], hy_bias[l], *_hyena_spectra(n_lat, hy_p))
        if not last:
            y_hy = _hyena_context(hy_a, y_hy, n_lat // n_ctx, n_ctx, hy_conv[l], hy_bias[l],
                                  *_hyena_spectra(n_ctx, hy_p))

        wts = (wl[:, o_gate:].astype(BF16), w_br_hy[l].astype(BF16), w_br_gla[l].astype(BF16),
               w_br_s5[l].astype(BF16), w_out[l].astype(BF16), s5_d[l][None], s5_w_glu[l].astype(BF16),
               s5_b_glu[l][None])
        n_tiles = n_lat_tiles if last else n_lat_tiles + n_ctx // ROW_TILE
        x_mid = _merge(xc, mods, g1, y_hy, y_gla, ys, u_a, wts, n_tiles, n_lat_tiles)
        wf = w_ffn_in[l]
        xc = _ffn(x_mid, mods, g_norm2[l][None], wf[:, :d_ff].astype(BF16), wf[:, d_ff:].astype(BF16),
                  w_ffn_out[l].astype(BF16), g_final[None], n_lat_tiles, last)
    return xc
```

```python
import functools
import math

import jax
import jax.numpy as jnp
import numpy as np
from jax import lax
from jax.experimental import pallas as pl
from jax.experimental.pallas import tpu as pltpu

F32 = jnp.float32
BF16 = jnp.bfloat16
EPS = 1e-6

GRID_W = 64
HY_SHORT = 3
HY_POS_BANDS = 16
HY_N_FILT = 4
GLA_HEADS = 4
GLA_DK = 64
GLA_DV = 128
GLA_QK = GLA_HEADS * GLA_DK
GLA_V = GLA_HEADS * GLA_DV
GLA_LOWRANK = 16
GLA_TAU = 16.0
GLA_CHUNK = 64
GLA_BLOCK = 4
S5_GROUP = 16
S5_STATE = 64
A_PAD = 128

ROW_TILE = 256
VMEM_LIMIT = 56 * 1024 * 1024


def _const_spec(shape):
    nd = len(shape)
    return pl.BlockSpec(shape, lambda *_: (0,) * nd, pipeline_mode=pl.Buffered(1))


def _params(n_axes):
    return pltpu.CompilerParams(dimension_semantics=("parallel",) * n_axes, vmem_limit_bytes=VMEM_LIMIT)


def _mod_kernel(c_ref, w_ref, b_ref, o_ref):
    c = c_ref[...]
    s = c * jax.nn.sigmoid(c)
    o_ref[0] = jnp.dot(s.astype(BF16), w_ref[0], preferred_element_type=F32) + b_ref[0]


def _modulation_all(cond, w_mod, b_mod):
    depth, d, n = w_mod.shape
    r = cond.shape[0]
    tn = 1536
    return pl.pallas_call(
        _mod_kernel,
        grid=(depth, n // tn),
        in_specs=[
            pl.BlockSpec((r, d), lambda l, j: (0, 0)),
            pl.BlockSpec((1, d, tn), lambda l, j: (l, 0, j)),
            pl.BlockSpec((1, 1, tn), lambda l, j: (l, 0, j)),
        ],
        out_specs=pl.BlockSpec((1, r, tn), lambda l, j: (l, 0, j)),
        out_shape=jax.ShapeDtypeStruct((depth, r, n), F32),
        compiler_params=_params(2),
        name="modulation",
    )(cond, w_mod.astype(BF16), b_mod.reshape(depth, 1, n))


def _norm_mod(x, g, shift, scale):
    y = x * lax.rsqrt(jnp.mean(x * x, axis=-1, keepdims=True) + EPS)
    return (y * g) * (1.0 + scale) + shift


def _proj_kernel(col_sizes, d, x_ref, mod_ref, g_ref, w_ref, *o_refs):
    m = mod_ref[0, 0]
    h = _norm_mod(x_ref[0], g_ref[...], m[:, 0:d], m[:, d:2 * d]).astype(BF16)
    off = 0
    for o_ref, n in zip(o_refs, col_sizes):
        o_ref[0] = jnp.dot(h, w_ref[:, off:off + n], preferred_element_type=F32).astype(o_ref.dtype)
        off += n


def _mod_spec(d, n_lat_tiles):
    return pl.BlockSpec((1, 1, 1, 6 * d), lambda i, j: (i, (j < n_lat_tiles).astype(jnp.int32), 0, 0))


def _project(xc, mods, g, w, col_sizes, n_lat_tiles):
    b, t, d = xc.shape
    tm = ROW_TILE
    n_tot = sum(col_sizes)
    out_shape = [jax.ShapeDtypeStruct((b, t, n), BF16) for n in col_sizes]
    out_specs = [pl.BlockSpec((1, tm, n), lambda i, j: (i, j, 0)) for n in col_sizes]
    return pl.pallas_call(
        functools.partial(_proj_kernel, tuple(col_sizes), d),
        grid=(b, t // tm),
        in_specs=[
            pl.BlockSpec((1, tm, d), lambda i, j: (i, j, 0)),
            _mod_spec(d, n_lat_tiles),
            _const_spec((1, d)),
            _const_spec((d, n_tot)),
        ],
        out_specs=out_specs,
        out_shape=out_shape,
        compiler_params=_params(2),
        name="in_proj",
    )(xc, mods, g, w)


def _gelu_tanh(x):
    return 0.5 * x * (1.0 + jnp.tanh(math.sqrt(2.0 / math.pi) * (x + 0.044715 * (x * x * x))))


def _merge_kernel(d, x_ref, mod_ref, g_ref, yhy_ref, ygla_ref, ys5_ref, u_ref, wg_ref, whb_ref, wgb_ref, wsb_ref,
                  wo_ref, s5d_ref, wglu_ref, bglu_ref, o_ref):
    x = x_ref[0]
    m = mod_ref[0, 0]
    h = _norm_mod(x, g_ref[...], m[:, 0:d], m[:, d:2 * d]).astype(BF16)
    y5 = ys5_ref[0].astype(F32) + s5d_ref[...] * u_ref[0].astype(F32)
    g5 = _gelu_tanh(y5)
    y_s5 = g5 * jax.nn.sigmoid(jnp.dot(g5.astype(BF16), wglu_ref[...], preferred_element_type=F32) + bglu_ref[...])

    def branch(k, y, wb_ref):
        gate = jnp.dot(h, wg_ref[:, k * d:(k + 1) * d], preferred_element_type=F32)
        return jax.nn.sigmoid(gate) * jnp.dot(y, wb_ref[...], preferred_element_type=F32)

    mix = branch(0, yhy_ref[0], whb_ref) + branch(1, ygla_ref[0], wgb_ref) + branch(2, y_s5.astype(BF16), wsb_ref)
    out = jnp.dot(mix.astype(BF16), wo_ref[...], preferred_element_type=F32)
    o_ref[0] = x + m[:, 2 * d:3 * d] * out


def _merge(xc, mods, g, y_hy, y_gla, ys_s5, u_s5, wts, nj, n_lat_tiles):
    b, t, d = xc.shape
    tm = ROW_TILE
    ch = y_hy.shape[-1]
    row = lambda n: pl.BlockSpec((1, tm, n), lambda i, j: (i, j, 0))
    w_gate, w_hy, w_gla, w_s5, w_out, s5_d, w_glu, b_glu = wts
    return pl.pallas_call(
        functools.partial(_merge_kernel, d),
        grid=(b, nj),
        in_specs=[
            row(d),
            _mod_spec(d, n_lat_tiles),
            _const_spec((1, d)),
            row(ch), row(ch), row(ch), row(ch),
            _const_spec(w_gate.shape), _const_spec(w_hy.shape), _const_spec(w_gla.shape), _const_spec(w_s5.shape),
            _const_spec(w_out.shape), _const_spec(s5_d.shape), _const_spec(w_glu.shape), _const_spec(b_glu.shape),
        ],
        out_specs=pl.BlockSpec((1, tm, d), lambda i, j: (i, j, 0)),
        out_shape=jax.ShapeDtypeStruct((b, nj * tm, d), F32),
        compiler_params=_params(2),
        name="merge",
    )(xc, mods, g, y_hy, y_gla, ys_s5, u_s5, w_gate, w_hy, w_gla, w_s5, w_out, s5_d, w_glu, b_glu)


def _ffn_kernel(d, final, x_ref, mod_ref, g_ref, wa_ref, wb_ref, wo_ref, gf_ref, o_ref):
    x = x_ref[0]
    m = mod_ref[0, 0]
    h = _norm_mod(x, g_ref[...], m[:, 3 * d:4 * d], m[:, 4 * d:5 * d]).astype(BF16)
    a = jnp.dot(h, wa_ref[...], preferred_element_type=F32)
    bb = jnp.dot(h, wb_ref[...], preferred_element_type=F32)
    act = (a * jax.nn.sigmoid(a) * bb).astype(BF16)
    y = x + m[:, 5 * d:6 * d] * jnp.dot(act, wo_ref[...], preferred_element_type=F32)
    if final:
        y = y * lax.rsqrt(jnp.mean(y * y, axis=-1, keepdims=True) + EPS) * gf_ref[...]
    o_ref[0] = y


def _ffn(x, mods, g, wa, wb, wo, g_final, n_lat_tiles, final):
    b, t, d = x.shape
    tm = ROW_TILE
    return pl.pallas_call(
        functools.partial(_ffn_kernel, d, final),
        grid=(b, t // tm),
        in_specs=[
            pl.BlockSpec((1, tm, d), lambda i, j: (i, j, 0)),
            _mod_spec(d, n_lat_tiles),
            _const_spec((1, d)),
            _const_spec(wa.shape), _const_spec(wb.shape), _const_spec(wo.shape),
            _const_spec((1, d)),
        ],
        out_specs=pl.BlockSpec((1, tm, d), lambda i, j: (i, j, 0)),
        out_shape=jax.ShapeDtypeStruct((b, t, d), F32),
        compiler_params=_params(2),
        name="ffn",
    )(x, mods, g, wa, wb, wo, g_final)


S5_CHUNK = 64


def _s5_weights(a_re, a_im, log_dt, b_re, b_im, c_re, c_im, tc):
    g, p = a_re.shape[1:]
    i = b_re.shape[-1]
    lam_c = lax.complex(jnp.minimum(a_re, -1e-4), a_im)
    lam_dt = lam_c * jnp.exp(log_dt)[..., None]
    b_bar = ((jnp.exp(lam_dt) - 1.0) / lam_c)[..., None] * lax.complex(b_re, b_im)[None]
    c_mat = lax.complex(c_re, c_im)
    tau = jnp.arange(tc, dtype=F32)
    pw = jnp.exp(lam_dt[:, :, None, :] * tau[None, None, :, None])
    pw1 = pw * jnp.exp(lam_dt)[:, :, None, :]
    kern = jnp.einsum('gip,dgtp,dgpj->dgtij', c_mat, pw, b_bar).real
    k_lag = jnp.concatenate([kern[1][:, :0:-1], kern[0][:, :1] + kern[1][:, :1], kern[0][:, 1:]], axis=1)
    kr = k_lag.transpose(0, 3, 1, 2).reshape(g, i, (2 * tc - 1) * i)
    kr = jnp.pad(kr, ((0, 0), (0, 0), (0, i)))

    inc_f = pw[0][:, ::-1, None, :] * b_bar[0].transpose(0, 2, 1)[:, None]
    inc_b = pw[1][:, :, None, :] * b_bar[1].transpose(0, 2, 1)[:, None]
    bm = jnp.concatenate([inc_f.real, inc_b.real, inc_f.imag, inc_b.imag], axis=-1).reshape(g, tc * i, 4 * p)

    out_f = c_mat.transpose(0, 2, 1)[:, :, None, :] * pw1[0].transpose(0, 2, 1)[..., None]
    out_b = c_mat.transpose(0, 2, 1)[:, :, None, :] * pw1[1][:, ::-1].transpose(0, 2, 1)[..., None]
    z = jnp.zeros((g, p, tc * i), F32)
    fl = lambda t: t.reshape(g, p, tc * i)
    cm = jnp.concatenate([fl(out_f.real), z, fl(-out_f.imag), z, z, fl(out_b.real), z, fl(-out_b.imag)], axis=1)

    lam_t = jnp.exp(lam_dt * float(tc))
    lam = jnp.stack([jnp.concatenate([lam_t[0].real, lam_t[1].real], -1),
                     jnp.concatenate([lam_t[0].imag, lam_t[1].imag], -1)], axis=1)
    return kr, bm.astype(BF16), cm.astype(BF16), lam


def _s5_kernel(nc, nc_ctx, bn, u_ref, kr_ref, bm_ref, cm_ref, lam_ref, y_ref, dx_ref, p_ref, m_ref):
    nl = nc - nc_ctx
    kr = kr_ref[0]
    i_sz, width = kr.shape
    k_sz = width // 2
    for s in range(k_sz // i_sz):
        shift = (width - (k_sz // i_sz - 1 - s) * i_sz) % width
        row = pltpu.roll(kr, shift, axis=1) if shift else kr
        m_ref[s * i_sz:(s + 1) * i_sz, :] = row[:, :k_sz].astype(BF16)
    u = u_ref[0]
    dx_ref[...] = jnp.dot(u, bm_ref[0], preferred_element_type=F32)
    lam = lam_ref[0]
    lr, li = lam[0:1], lam[1:2]
    half = lam.shape[-1] // 2
    is_f = lax.broadcasted_iota(jnp.int32, (bn, 2 * half), 1) < half

    def step(s, carry):
        sr, si = carry
        nf = jnp.where(s < nc_ctx, nl + s, s - nc_ctx)
        nb = nc - 1 - s
        rf = pl.multiple_of(nf * bn, bn)
        rb = pl.multiple_of(nb * bn, bn)
        p_ref[pl.ds(rf, bn), 0:128] = sr
        p_ref[pl.ds(rf, bn), 128:256] = si
        p_ref[pl.ds(rb, bn), 256:384] = sr
        p_ref[pl.ds(rb, bn), 384:512] = si
        d_re = jnp.where(is_f, dx_ref[pl.ds(rf, bn), 0:128], dx_ref[pl.ds(rb, bn), 0:128])
        d_im = jnp.where(is_f, dx_ref[pl.ds(rf, bn), 128:256], dx_ref[pl.ds(rb, bn), 128:256])
        return lr * sr - li * si + d_re, lr * si + li * sr + d_im

    zero = jnp.zeros((bn, 2 * half), F32)
    lax.fori_loop(0, nc, step, (zero, zero))
    y = jnp.dot(u, m_ref[...], preferred_element_type=F32)
    y = y + jnp.dot(p_ref[...].astype(BF16), cm_ref[0], preferred_element_type=F32)
    y_ref[0] = y.astype(y_ref.dtype)


def _s5_mix(u_all, n_ctx, wts):
    kr, bm, cm, lam = wts
    bn, t, ch = u_all.shape
    g, i = kr.shape[:2]
    tc = kr.shape[2] // (2 * i)
    assert S5_STATE == 64 and bn % 8 == 0 and n_ctx % tc == 0 and t % tc == 0
    nc, nc_ctx = t // tc, n_ctx // tc
    k = tc * i
    r = nc * bn
    u_g = u_all.reshape(bn, nc, tc, g, i).transpose(3, 1, 0, 2, 4).reshape(g, r, k)
    y_g = pl.pallas_call(
        functools.partial(_s5_kernel, nc, nc_ctx, bn),
        grid=(g,),
        in_specs=[
            pl.BlockSpec((1, r, k), lambda j: (j, 0, 0)),
            pl.BlockSpec((1, i, 2 * k), lambda j: (j, 0, 0)),
            pl.BlockSpec((1, k, 4 * S5_STATE), lambda j: (j, 0, 0)),
            pl.BlockSpec((1, 8 * S5_STATE, k), lambda j: (j, 0, 0)),
            pl.BlockSpec((1, 2, 2 * S5_STATE), lambda j: (j, 0, 0)),
        ],
        out_specs=pl.BlockSpec((1, r, k), lambda j: (j, 0, 0)),
        out_shape=jax.ShapeDtypeStruct((g, r, k), BF16),
        scratch_shapes=[pltpu.VMEM((r, 4 * S5_STATE), F32), pltpu.VMEM((r, 8 * S5_STATE), F32),
                        pltpu.VMEM((k, k), BF16)],
        compiler_params=_params(1),
        name="s5_mix",
    )(u_g, kr, bm, cm, lam)
    return y_g.reshape(g, nc, bn, tc, i).transpose(2, 1, 3, 0, 4).reshape(bn, t, ch)


_NT = (((1,), (1,)), ((), ()))
_TN = (((0,), (0,)), ((), ()))


def _gla_kernel(nc, nc_ctx, k_ref, v_ref, a_ref, q_ref, r_ref, wa_ref, ba_ref, gn_ref, y_ref, of_ref, ob_ref,
                s_ref):
    c, h_n, dk, dv = GLA_CHUNK, GLA_HEADS, GLA_DK, GLA_DV
    qk = h_n * dk
    q_scale = dk ** -0.5
    nbk = GLA_BLOCK
    rb = nbk * c
    row_i = lax.broadcasted_iota(jnp.int32, (rb, rb), 0)
    col_i = lax.broadcasted_iota(jnp.int32, (rb, rb), 1)
    same = (row_i // c) == (col_i // c)
    causal = (same & (row_i >= col_i), same & (row_i <= col_i))
    cum_ops = [jnp.concatenate([m.astype(BF16), same.astype(BF16)], axis=0) for m in causal]
    att_mask = [jnp.concatenate([m] * h_n, axis=0) for m in causal]
    lane_head = lax.broadcasted_iota(jnp.int32, (1, qk), 1) // dk
    head_lanes = [lane_head == h for h in range(h_n)]

    def by_head(x):
        zero = jnp.zeros_like(x)
        return jnp.concatenate([jnp.where(m, x, zero) for m in head_lanes], axis=0)

    def block(j, d):
        rows = pl.ds(pl.multiple_of(j * rb, rb), rb)
        k = k_ref[0, rows, :].astype(F32)
        q = q_ref[0, rows, :].astype(F32) * q_scale
        v = v_ref[0, rows, :]
        z = jnp.dot(a_ref[0, rows, :], wa_ref[:, d * qk:(d + 1) * qk], preferred_element_type=F32) + ba_ref[d:d + 1, :]
        log_a = (jnp.minimum(z, 0.0) - jnp.log(1.0 + jnp.exp(-jnp.abs(z)))) * (1.0 / GLA_TAU)
        hi = log_a.astype(BF16)
        lo = (log_a - hi.astype(F32)).astype(BF16)
        cs = jnp.dot(cum_ops[d], jnp.concatenate([hi, lo], axis=1), preferred_element_type=F32)
        cs = cs[:, :qk] + cs[:, qk:]
        b, b_tot = cs[:rb], cs[rb:]
        qd = (q * jnp.exp(b)).astype(BF16)
        kd = (k * jnp.exp(-b)).astype(BF16)
        kl = (k * jnp.exp(b_tot - b)).astype(BF16)
        qm = by_head(qd)
        att = lax.dot_general(qm, kd, _NT, preferred_element_type=F32)
        att = jnp.where(att_mask[d], att, 0.0).astype(BF16)
        intra = [jnp.dot(att[h * rb:(h + 1) * rb], v[:, h * dv:(h + 1) * dv], preferred_element_type=F32)
                 for h in range(h_n)]
        s_t = s_ref[d]
        inter = [None] * nbk
        for n in (range(nbk) if d == 0 else reversed(range(nbk))):
            r0 = n * c
            qm_n = jnp.concatenate([qm[h * rb + r0:h * rb + r0 + c] for h in range(h_n)], axis=0)
            inter[n] = lax.dot_general(qm_n, s_t.astype(BF16), _NT, preferred_element_type=F32)
            v_t = jnp.concatenate(
                [jnp.concatenate([v[r0:r0 + c, h * dv:(h + 1) * dv] for h in range(p, p + dv // c)], axis=0).T
                 for p in range(0, h_n, dv // c)], axis=1)
            s_t = s_t * jnp.exp(b_tot[r0:r0 + 1]) + jnp.dot(v_t, by_head(kl[r0:r0 + c]), preferred_element_type=F32)
        s_ref[d] = s_t
        o = jnp.concatenate(
            [intra[h] + jnp.concatenate([inter[n][h * c:(h + 1) * c] for n in range(nbk)], axis=0)
             for h in range(h_n)], axis=1)
        return rows, o

    def scan_step(i, carry, lo_block, hi_block):
        rows_f, o_f = block(lo_block + i, 0)
        rows_b, o_b = block(hi_block - 1 - i, 1)
        of_ref[rows_f, :] = o_f
        ob_ref[rows_b, :] = o_b
        return carry

    def readout(n, carry):
        rows = pl.ds(pl.multiple_of(n * c, c), c)
        o = of_ref[rows, :] + ob_ref[rows, :]
        r = r_ref[0, rows, :].astype(F32)
        gate = r * jax.nn.sigmoid(r)
        outs = []
        for h in range(h_n):
            oh = o[:, h * dv:(h + 1) * dv]
            oh = oh * lax.rsqrt(jnp.mean(oh * oh, axis=-1, keepdims=True) + EPS) * gn_ref[...]
            outs.append(oh * gate[:, h * dv:(h + 1) * dv])
        y_ref[0, rows, :] = jnp.concatenate(outs, axis=1).astype(y_ref.dtype)
        return carry

    nl, n_ctx = (nc - nc_ctx) // nbk, nc_ctx // nbk
    s_ref[...] = jnp.zeros(s_ref.shape, F32)
    lax.fori_loop(0, n_ctx, functools.partial(scan_step, lo_block=nl, hi_block=nl + n_ctx), 0)
    lax.fori_loop(0, nl, functools.partial(scan_step, lo_block=0, hi_block=nl), 0)
    lax.fori_loop(0, nc, readout, 0, unroll=2)


def _gla_mix(k, v, a, q, r, wa, ba, gnorm, n_ctx):
    bn, t, qk = k.shape
    vd = v.shape[-1]
    c = GLA_CHUNK
    assert t % (c * GLA_BLOCK) == 0 and n_ctx % (c * GLA_BLOCK) == 0 and GLA_DV % c == 0
    seq = lambda n: pl.BlockSpec((1, t, n), lambda i: (i, 0, 0), pipeline_mode=pl.Buffered(1))
    return pl.pallas_call(
        functools.partial(_gla_kernel, t // c, n_ctx // c),
        grid=(bn,),
        in_specs=[seq(qk), seq(vd), seq(a.shape[-1]), seq(qk), seq(vd),
                  _const_spec(wa.shape), _const_spec(ba.shape), _const_spec(gnorm.shape)],
        out_specs=pl.BlockSpec((1, t, vd), lambda i: (i, 0, 0)),
        out_shape=jax.ShapeDtypeStruct((bn, t, vd), BF16),
        scratch_shapes=[pltpu.VMEM((t, vd), F32), pltpu.VMEM((t, vd), F32), pltpu.VMEM((2, GLA_DV, qk), F32)],
        compiler_params=_params(1),
        name="gla_mix",
    )(k, v, a, q, r, wa, ba, gnorm)


HY_NB = 128
HY_XPITCH = HY_NB + 8
HY_APITCH = 2 * HY_NB + 8
HY_CT = 128
HY_UNROLL_OUTER = 16
HY_UNROLL_INNER = 8


def _hyena_filters(n, params):
    w1, b1, w2, b2, w3, freq1, freq2, decay = (p.astype(F32) for p in params)
    t = jnp.linspace(0.0, 1.0, n, dtype=F32)[:, None]
    bands = jnp.linspace(1e-4, HY_POS_BANDS - 1, HY_POS_BANDS, dtype=F32)
    ang = (2.0 * math.pi / n) * jnp.arange(n, dtype=F32)[:, None] * bands[None]
    z = jnp.concatenate([t, jnp.cos(ang), -jnp.sin(ang)], axis=-1)
    hid = jnp.sin(freq1 * (z @ w1 + b1))
    hid = jnp.sin(freq2 * (hid @ w2 + b2))
    h = (hid @ w3) * jnp.exp(-t * jnp.abs(decay))
    return h.reshape(n, HY_N_FILT, -1)


def _hyena_spectra(n, params):
    h = _hyena_filters(n, params)

    def spectrum(h_fwd, h_bwd):
        k = jnp.concatenate([h_fwd[:1] + h_bwd[:1], h_fwd[1:], jnp.zeros_like(h_fwd[:1]), h_bwd[:0:-1]], axis=0)
        k = k * lax.rsqrt(jnp.sum(k * k, axis=0, keepdims=True) + EPS)
        return jnp.fft.fft(k, axis=0)

    return spectrum(h[:, 0], h[:, 1]), spectrum(h[:, 2], h[:, 3])


def _two_stage_layout(kf, na):
    n, ch = kf.shape
    kf = kf.reshape(n // na, na, ch).transpose(1, 0, 2)
    return jnp.stack([kf.real, kf.imag], axis=1).astype(BF16)


def _cplx_block(m):
    return np.block([[m.real, -m.imag], [m.imag, m.real]])


@functools.lru_cache(maxsize=None)
def _dft_tables(n_seq):
    nb = HY_NB
    n = 2 * n_seq
    na = n // nb
    ha = na // 2
    ka = np.arange(na)[:, None]
    b = np.arange(nb)
    tw = np.exp(-2j * np.pi * ka * b[None, :] / n)
    w_a = np.exp(-2j * np.pi * ka * np.arange(ha)[None, :] / na)
    f1 = np.stack([_cplx_block(tw[:, i:i + 1] * w_a) for i in range(nb)])
    g = np.stack([_cplx_block((np.conj(tw[:, i:i + 1] * w_a)).T / n) for i in range(nb)])
    w_b = np.exp(-2j * np.pi * b[:, None] * b[None, :] / nb)
    f2 = _cplx_block(w_b)
    f2i = _cplx_block(np.conj(w_b))
    return tuple(jnp.asarray(t, dtype=BF16) for t in (f1, f2, f2i, g))


def _hyena_kernel(na, x1_ref, x2_ref, v_ref, cw_ref, bias_ref, kf1_ref, kf2_ref, f1_ref, f2_ref, f2i_ref, g_ref,
                  o_ref, rawf, vp, gp, yp, a2):
    nb, xp, ap = HY_NB, HY_XPITCH, HY_APITCH
    ha = na // 2
    n_seq = ha * nb
    ct = o_ref.shape[-1]
    rawf[:, 0:8, :] = jnp.zeros((2, 8, ct), F32)
    rawf[:, 8 + n_seq:16 + n_seq, :] = jnp.zeros((2, 8, ct), F32)

    def short_conv_into(raw_ref, which, dst):
        w0, w1, w2 = (cw_ref[j, which:which + 1, :] for j in range(HY_SHORT))
        for e in range(2):
            rawf[e, 8:8 + n_seq, :] = raw_ref[e].astype(F32)

            def body(a, c):
                win = rawf[e, pl.ds(pl.multiple_of(a * nb, nb), nb + 16), :]
                u = w0 * win[7:7 + nb] + w1 * win[8:8 + nb] + w2 * win[9:9 + nb]
                dst[e, pl.ds(pl.multiple_of(a * xp, 8), nb), :] = u
                return c

            lax.fori_loop(0, ha, body, 0)

    def long_conv(kf_ref):
        def stage1(b, c):
            xs = jnp.concatenate([vp[0, pl.ds(b, ha, stride=xp), :], vp[1, pl.ds(b, ha, stride=xp), :]], axis=0)
            r = jnp.dot(f1_ref[b], xs.astype(BF16), preferred_element_type=F32)
            a2[pl.ds(b, na, stride=ap), :] = r[:na]
            a2[pl.ds(nb + b, na, stride=ap), :] = r[na:]
            return c

        def stage2(kp, c):
            rows = [pl.ds(pl.multiple_of((2 * kp + e) * ap, 8), 2 * nb) for e in range(2)]
            slab = jnp.concatenate([a2[r, :] for r in rows], axis=1).astype(BF16)
            xf = jnp.dot(f2_ref[...], slab, preferred_element_type=F32)
            xr, xi = xf[:nb], xf[nb:]
            kr = jnp.concatenate([kf_ref[2 * kp + e, 0] for e in range(2)], axis=1).astype(F32)
            ki = jnp.concatenate([kf_ref[2 * kp + e, 1] for e in range(2)], axis=1).astype(F32)
            z = jnp.concatenate([xr * kr - xi * ki, xr * ki + xi * kr], axis=0).astype(BF16)
            back = jnp.dot(f2i_ref[...], z, preferred_element_type=F32)
            for e in range(2):
                a2[rows[e], :] = back[:, e * ct:(e + 1) * ct]
            return c

        def stage3(b, c):
            s = jnp.concatenate([a2[pl.ds(b, na, stride=ap), :], a2[pl.ds(nb + b, na, stride=ap), :]], axis=0)
            y = jnp.dot(g_ref[b], s.astype(BF16), preferred_element_type=F32)
            yp[0, pl.ds(b, ha, stride=xp), :] = y[:ha]
            yp[1, pl.ds(b, ha, stride=xp), :] = y[ha:]
            return c

        lax.fori_loop(0, nb, stage1, 0, unroll=HY_UNROLL_OUTER)
        lax.fori_loop(0, na // 2, stage2, 0, unroll=HY_UNROLL_INNER)
        lax.fori_loop(0, nb, stage3, 0, unroll=HY_UNROLL_OUTER)

    def gated(bias_row, write):
        for e in range(2):
            def body(a, c):
                rows = pl.ds(pl.multiple_of(a * xp, 8), nb)
                write(e, a, rows, gp[e, rows, :] * (yp[e, rows, :] + bias_row * vp[e, rows, :]))
                return c

            lax.fori_loop(0, ha, body, 0)

    def write_z(e, a, rows, val):
        vp[e, rows, :] = val

    def write_out(e, a, rows, val):
        o_ref[e, pl.ds(pl.multiple_of(a * nb, nb), nb), :] = val.astype(o_ref.dtype)

    short_conv_into(v_ref, 2, vp)
    short_conv_into(x1_ref, 0, gp)
    long_conv(kf1_ref)
    gated(bias_ref[0:1, :], write_z)
    short_conv_into(x2_ref, 1, gp)
    long_conv(kf2_ref)
    gated(bias_ref[1:2, :], write_out)


def _hyena_latent(hy_all, n_seq, conv_w, bias, kf1, kf2):
    bn, t_all = hy_all.shape[:2]
    ch = hy_all.shape[-1] // 3
    ct = HY_CT
    nct = ch // ct
    nb = HY_NB
    na = 2 * n_seq // nb
    assert bn % 2 == 0 and ch % ct == 0 and n_seq % (8 * nb) == 0
    f1, f2, f2i, g = _dft_tables(n_seq)
    col = lambda which: pl.BlockSpec((2, n_seq, ct), lambda j, p: (p, 0, which * nct + j), pipeline_mode=pl.Buffered(1))
    kf_spec = pl.BlockSpec((na, 2, nb, ct), lambda j, p: (0, 0, 0, j), pipeline_mode=pl.Buffered(1))
    pad_rows = (na // 2) * HY_XPITCH
    return pl.pallas_call(
        functools.partial(_hyena_kernel, na),
        grid=(nct, bn // 2),
        in_specs=[col(0), col(1), col(2),
                  pl.BlockSpec((HY_SHORT, 3, ct), lambda j, p: (0, 0, j)),
                  pl.BlockSpec((2, ct), lambda j, p: (0, j)),
                  kf_spec, kf_spec,
                  _const_spec(f1.shape), _const_spec(f2.shape), _const_spec(f2i.shape), _const_spec(g.shape)],
        out_specs=pl.BlockSpec((2, n_seq, ct), lambda j, p: (p, 0, j)),
        out_shape=jax.ShapeDtypeStruct((bn, t_all, ch), BF16),
        scratch_shapes=[pltpu.VMEM((2, n_seq + 16, ct), F32), pltpu.VMEM((2, pad_rows, ct), F32),
                        pltpu.VMEM((2, pad_rows, ct), F32), pltpu.VMEM((2, pad_rows, ct), F32),
                        pltpu.VMEM((na * HY_APITCH, ct), F32)],
        compiler_params=_params(2),
        name="hyena_latent",
    )(hy_all, hy_all, hy_all, conv_w.reshape(HY_SHORT, 3, ch), bias, _two_stage_layout(kf1, na),
      _two_stage_layout(kf2, na), f1, f2, f2i, g)


@functools.lru_cache(maxsize=None)
def _dense_dft_tables(n_seq):
    n = 2 * n_seq
    ang = 2.0 * np.pi * np.arange(n)[:, None] * np.arange(n_seq)[None, :] / n
    fd = np.concatenate([np.cos(ang), -np.sin(ang)], axis=0)
    gd = np.concatenate([np.cos(ang).T, -np.sin(ang).T], axis=1) / n
    return jnp.asarray(fd, dtype=BF16), jnp.asarray(gd, dtype=BF16)


def _hyena_ctx_kernel(n_seq, x1_ref, x2_ref, v_ref, cw_ref, bias_ref, kf1_ref, kf2_ref, fd_ref, gd_ref, buf_ref,
                      o_ref):
    del buf_ref
    n = 2 * n_seq

    def short_conv(raw_ref, which):
        x = raw_ref[0].astype(F32)
        zero = jnp.zeros((1, x.shape[-1]), F32)
        prev = jnp.concatenate([zero, x[:-1]], axis=0)
        nxt = jnp.concatenate([x[1:], zero], axis=0)
        return cw_ref[0, which:which + 1, :] * prev + cw_ref[1, which:which + 1, :] * x + cw_ref[2, which:which + 1, :] * nxt

    def long_conv(u, kf_ref, bias_row):
        xf = jnp.dot(fd_ref[...], u.astype(BF16), preferred_element_type=F32)
        xr, xi = xf[:n], xf[n:]
        kr, ki = kf_ref[0], kf_ref[1]
        z = jnp.concatenate([xr * kr - xi * ki, xr * ki + xi * kr], axis=0).astype(BF16)
        return jnp.dot(gd_ref[...], z, preferred_element_type=F32) + bias_row * u

    x1, x2, v = short_conv(x1_ref, 0), short_conv(x2_ref, 1), short_conv(v_ref, 2)
    z = x1 * long_conv(v, kf1_ref, bias_ref[0:1, :])
    o_ref[0] = (x2 * long_conv(z, kf2_ref, bias_ref[1:2, :])).astype(o_ref.dtype)


def _hyena_context(hy_all, y_buf, row_block, n_seq, conv_w, bias, kf1, kf2):
    bn = hy_all.shape[0]
    ch = hy_all.shape[-1] // 3
    fd, gd = _dense_dft_tables(n_seq)
    col = lambda which: pl.BlockSpec((1, n_seq, ch), lambda i: (i, row_block, which))
    split = lambda kf: jnp.stack([kf.real, kf.imag], axis=0)
    return pl.pallas_call(
        functools.partial(_hyena_ctx_kernel, n_seq),
        grid=(bn,),
        in_specs=[col(0), col(1), col(2), _const_spec((HY_SHORT, 3, ch)), _const_spec((2, ch)),
                  _const_spec((2, 2 * n_seq, ch)), _const_spec((2, 2 * n_seq, ch)),
                  _const_spec(fd.shape), _const_spec(gd.shape), pl.BlockSpec(memory_space=pl.ANY)],
        out_specs=pl.BlockSpec((1, n_seq, ch), lambda i: (i, row_block, 0)),
        out_shape=jax.ShapeDtypeStruct(y_buf.shape, y_buf.dtype),
        input_output_aliases={9: 0},
        compiler_params=_params(1),
        name="hyena_context",
    )(hy_all, hy_all, hy_all, conv_w.reshape(HY_SHORT, 3, ch), bias, split(kf1), split(kf2), fd, gd, y_buf)


def kernel(x, c, ctx, c_ctx, w_mod, b_mod, g_norm1, g_norm2, w_in, hy_conv, hy_w1, hy_b1, hy_w2, hy_b2, hy_w3,
           hy_freq1, hy_freq2, hy_decay, hy_bias, gla_wa2, gla_ba, gla_gnorm, s5_a_re, s5_a_im, s5_log_dt, s5_b_re,
           s5_b_im, s5_c_re, s5_c_im, s5_d, s5_w_glu, s5_b_glu, w_br_hy, w_br_gla, w_br_s5, w_out, w_ffn_in,
           w_ffn_out, g_final):
    bn, n_lat, d = x.shape
    n_ctx = ctx.shape[1]
    depth = w_in.shape[0]
    d_hy = w_br_hy.shape[1]
    s5_ch = w_br_s5.shape[1]
    d_ff = w_ffn_out.shape[1]
    assert n_ctx == ROW_TILE and n_lat % ROW_TILE == 0
    n_lat_tiles = n_lat // ROW_TILE
    n_low = 2 * GLA_LOWRANK
    o_a = GLA_QK + GLA_V
    o_u = o_a + n_low
    o_gate = o_u + s5_ch + GLA_QK + GLA_V + 3 * d_hy
    col_sizes = (GLA_QK, GLA_V, A_PAD, s5_ch, GLA_QK, GLA_V, 3 * d_hy)

    cond = jnp.concatenate([c_ctx[None], c], axis=0)
    cond = jnp.pad(cond, ((0, (-cond.shape[0]) % 8), (0, 0)))
    mod_all = _modulation_all(cond, w_mod, b_mod)

    xc = jnp.concatenate([x, ctx], axis=1)
    out = None
    for l in range(depth):
        last = l == depth - 1
        mods = jnp.stack([jnp.broadcast_to(mod_all[l, 0], (bn, 6 * d)), mod_all[l, 1:1 + bn]], axis=1)
        mods = mods[:, :, None, :]
        wl = w_in[l]
        w_proj = jnp.concatenate(
            [wl[:, :o_u], jnp.zeros((d, A_PAD - n_low), F32), wl[:, o_u:o_gate]], axis=1).astype(BF16)
        g1 = g_norm1[l][None]
        k_a, v_a, a_a, u_a, q_a, r_a, hy_a = _project(xc, mods, g1, w_proj, col_sizes, n_lat_tiles)

        wa = jnp.zeros((A_PAD, 2 * GLA_QK), F32)
        wa = wa.at[:GLA_LOWRANK, :GLA_QK].set(gla_wa2[l, 0]).at[GLA_LOWRANK:n_low, GLA_QK:].set(gla_wa2[l, 1])
        y_gla = _gla_mix(k_a, v_a, a_a, q_a, r_a, wa.astype(BF16), gla_ba[l], gla_gnorm[l][None], n_ctx)

        s5w = _s5_weights(s5_a_re[l], s5_a_im[l], s5_log_dt[l], s5_b_re[l], s5_b_im[l], s5_c_re[l], s5_c_im[l],
                          S5_CHUNK)
        ys = _s5_mix(u_a, n_ctx, s5w)

        hy_p = (hy_w1[l], hy_b1[l], hy_w2[l], hy_b2[l], hy_w3[l], hy_freq1[l], hy_freq2[l], hy_decay[l])
        y_hy = _hyena_latent(hy_a, n_lat, hy_conv[l], hy_bias[l], *_hyena_spectra(n_lat, hy_p))
        if not last:
            y_hy = _hyena_context(hy_a, y_hy, n_lat // n_ctx, n_ctx, hy_conv[l], hy_bias[l],
                                  *_hyena_spectra(n_ctx, hy_p))

        wts = (wl[:, o_gate:].astype(BF16), w_br_hy[l].astype(BF16), w_br_gla[l].astype(BF16),
               w_br_s5[l].astype(BF16), w_out[l].astype(BF16), s5_d[l][None], s5_w_glu[l].astype(BF16),
               s5_b_glu[l][None])
        n_tiles = n_lat_tiles if last else n_lat_tiles + n_ctx // ROW_TILE
        x_mid = _merge(xc, mods, g1, y_hy, y_gla, ys, u_a, wts, n_tiles, n_lat_tiles)
        wf = w_ffn_in[l]
        xc = _ffn(x_mid, mods, g_norm2[l][None], wf[:, :d_ff].astype(BF16), wf[:, d_ff:].astype(BF16),
                  w_ffn_out[l].astype(BF16), g_final[None], n_lat_tiles, last)
    return xc
```

```python
import functools
import math

import jax
import jax.numpy as jnp
import numpy as np
from jax import lax
from jax.experimental import pallas as pl
from jax.experimental.pallas import tpu as pltpu

F32 = jnp.float32
BF16 = jnp.bfloat16
EPS = 1e-6

GRID_W = 64
HY_SHORT = 3
HY_POS_BANDS = 16
HY_N_FILT = 4
GLA_HEADS = 4
GLA_DK = 64
GLA_DV = 128
GLA_QK = GLA_HEADS * GLA_DK
GLA_V = GLA_HEADS * GLA_DV
GLA_LOWRANK = 16
GLA_TAU = 16.0
GLA_CHUNK = 64
GLA_BLOCK = 4
S5_GROUP = 16
S5_STATE = 64
A_PAD = 128

ROW_TILE = 256
VMEM_LIMIT = 56 * 1024 * 1024


def _const_spec(shape):
    nd = len(shape)
    return pl.BlockSpec(shape, lambda *_: (0,) * nd, pipeline_mode=pl.Buffered(1))


def _params(n_axes):
    return pltpu.CompilerParams(dimension_semantics=("parallel",) * n_axes, vmem_limit_bytes=VMEM_LIMIT)


def _mod_kernel(c_ref, w_ref, b_ref, o_ref):
    c = c_ref[...]
    s = c * jax.nn.sigmoid(c)
    o_ref[0] = jnp.dot(s.astype(BF16), w_ref[0], preferred_element_type=F32) + b_ref[0]


def _modulation_all(cond, w_mod, b_mod):
    depth, d, n = w_mod.shape
    r = cond.shape[0]
    tn = 1536
    return pl.pallas_call(
        _mod_kernel,
        grid=(depth, n // tn),
        in_specs=[
            pl.BlockSpec((r, d), lambda l, j: (0, 0)),
            pl.BlockSpec((1, d, tn), lambda l, j: (l, 0, j)),
            pl.BlockSpec((1, 1, tn), lambda l, j: (l, 0, j)),
        ],
        out_specs=pl.BlockSpec((1, r, tn), lambda l, j: (l, 0, j)),
        out_shape=jax.ShapeDtypeStruct((depth, r, n), F32),
        compiler_params=_params(2),
        name="modulation",
    )(cond, w_mod.astype(BF16), b_mod.reshape(depth, 1, n))


def _norm_mod(x, g, shift, scale):
    y = x * lax.rsqrt(jnp.mean(x * x, axis=-1, keepdims=True) + EPS)
    return (y * g) * (1.0 + scale) + shift


_NT = (((1,), (1,)), ((), ()))


def _proj_kernel(col_sizes, d, x_ref, mod_ref, g_ref, w_ref, wt_ref, *o_refs):
    m = mod_ref[0, 0]
    h = _norm_mod(x_ref[0], g_ref[...], m[:, 0:d], m[:, d:2 * d]).astype(BF16)
    off = 0
    for o_ref, n in zip(o_refs[:-1], col_sizes):
        o_ref[0] = jnp.dot(h, w_ref[:, off:off + n], preferred_element_type=F32).astype(o_ref.dtype)
        off += n
    o_refs[-1][0] = lax.dot_general(wt_ref[...], h, _NT, preferred_element_type=F32)


def _mod_spec(d, n_lat_tiles):
    return pl.BlockSpec((1, 1, 1, 6 * d), lambda i, j: (i, (j < n_lat_tiles).astype(jnp.int32), 0, 0))


def _project(xc, mods, g, w, col_sizes, w_t, n_lat_tiles):
    b, t, d = xc.shape
    tm = ROW_TILE
    n_tot = sum(col_sizes)
    n_t = w_t.shape[0]
    out_shape = [jax.ShapeDtypeStruct((b, t, n), BF16) for n in col_sizes] + [jax.ShapeDtypeStruct((b, n_t, t), F32)]
    out_specs = [pl.BlockSpec((1, tm, n), lambda i, j: (i, j, 0)) for n in col_sizes]
    out_specs.append(pl.BlockSpec((1, n_t, tm), lambda i, j: (i, 0, j)))
    return pl.pallas_call(
        functools.partial(_proj_kernel, tuple(col_sizes), d),
        grid=(b, t // tm),
        in_specs=[
            pl.BlockSpec((1, tm, d), lambda i, j: (i, j, 0)),
            _mod_spec(d, n_lat_tiles),
            _const_spec((1, d)),
            _const_spec((d, n_tot)),
            _const_spec((n_t, d)),
        ],
        out_specs=out_specs,
        out_shape=out_shape,
        compiler_params=_params(2),
        name="in_proj",
    )(xc, mods, g, w, w_t)


def _gelu_tanh(x):
    return 0.5 * x * (1.0 + jnp.tanh(math.sqrt(2.0 / math.pi) * (x + 0.044715 * (x * x * x))))


def _merge_kernel(d, x_ref, mod_ref, g_ref, yhy_ref, ygla_ref, ys5_ref, u_ref, wg_ref, whb_ref, wgb_ref, wsb_ref,
                  wo_ref, s5d_ref, wglu_ref, bglu_ref, o_ref):
    x = x_ref[0]
    m = mod_ref[0, 0]
    h = _norm_mod(x, g_ref[...], m[:, 0:d], m[:, d:2 * d]).astype(BF16)
    y5 = ys5_ref[0].T + s5d_ref[...] * u_ref[0].astype(F32)
    g5 = _gelu_tanh(y5)
    y_s5 = g5 * jax.nn.sigmoid(jnp.dot(g5.astype(BF16), wglu_ref[...], preferred_element_type=F32) + bglu_ref[...])

    def branch(k, y, wb_ref):
        gate = jnp.dot(h, wg_ref[:, k * d:(k + 1) * d], preferred_element_type=F32)
        return jax.nn.sigmoid(gate) * jnp.dot(y, wb_ref[...], preferred_element_type=F32)

    mix = branch(0, yhy_ref[0], whb_ref) + branch(1, ygla_ref[0], wgb_ref) + branch(2, y_s5.astype(BF16), wsb_ref)
    out = jnp.dot(mix.astype(BF16), wo_ref[...], preferred_element_type=F32)
    o_ref[0] = x + m[:, 2 * d:3 * d] * out


def _merge(xc, mods, g, y_hy, y_gla, ys_s5, u_s5, wts, nj, n_lat_tiles):
    b, t, d = xc.shape
    tm = ROW_TILE
    ch = y_hy.shape[-1]
    row = lambda n: pl.BlockSpec((1, tm, n), lambda i, j: (i, j, 0))
    w_gate, w_hy, w_gla, w_s5, w_out, s5_d, w_glu, b_glu = wts
    return pl.pallas_call(
        functools.partial(_merge_kernel, d),
        grid=(b, nj),
        in_specs=[
            row(d),
            _mod_spec(d, n_lat_tiles),
            _const_spec((1, d)),
            row(ch), row(ch), pl.BlockSpec((1, ch, tm), lambda i, j: (i, 0, j)), row(ch),
            _const_spec(w_gate.shape), _const_spec(w_hy.shape), _const_spec(w_gla.shape), _const_spec(w_s5.shape),
            _const_spec(w_out.shape), _const_spec(s5_d.shape), _const_spec(w_glu.shape), _const_spec(b_glu.shape),
        ],
        out_specs=pl.BlockSpec((1, tm, d), lambda i, j: (i, j, 0)),
        out_shape=jax.ShapeDtypeStruct((b, nj * tm, d), F32),
        compiler_params=_params(2),
        name="merge",
    )(xc, mods, g, y_hy, y_gla, ys_s5, u_s5, w_gate, w_hy, w_gla, w_s5, w_out, s5_d, w_glu, b_glu)


def _ffn_kernel(d, final, x_ref, mod_ref, g_ref, wa_ref, wb_ref, wo_ref, gf_ref, o_ref):
    x = x_ref[0]
    m = mod_ref[0, 0]
    h = _norm_mod(x, g_ref[...], m[:, 3 * d:4 * d], m[:, 4 * d:5 * d]).astype(BF16)
    a = jnp.dot(h, wa_ref[...], preferred_element_type=F32)
    bb = jnp.dot(h, wb_ref[...], preferred_element_type=F32)
    act = (a * jax.nn.sigmoid(a) * bb).astype(BF16)
    y = x + m[:, 5 * d:6 * d] * jnp.dot(act, wo_ref[...], preferred_element_type=F32)
    if final:
        y = y * lax.rsqrt(jnp.mean(y * y, axis=-1, keepdims=True) + EPS) * gf_ref[...]
    o_ref[0] = y


def _ffn(x, mods, g, wa, wb, wo, g_final, n_lat_tiles, final):
    b, t, d = x.shape
    tm = ROW_TILE
    return pl.pallas_call(
        functools.partial(_ffn_kernel, d, final),
        grid=(b, t // tm),
        in_specs=[
            pl.BlockSpec((1, tm, d), lambda i, j: (i, j, 0)),
            _mod_spec(d, n_lat_tiles),
            _const_spec((1, d)),
            _const_spec(wa.shape), _const_spec(wb.shape), _const_spec(wo.shape),
            _const_spec((1, d)),
        ],
        out_specs=pl.BlockSpec((1, tm, d), lambda i, j: (i, j, 0)),
        out_shape=jax.ShapeDtypeStruct((b, t, d), F32),
        compiler_params=_params(2),
        name="ffn",
    )(x, mods, g, wa, wb, wo, g_final)


S5_CHUNK = 128


def _s5_weights(a_re, a_im, log_dt, b_re, b_im, c_re, c_im, tc):
    g, p = a_re.shape[1:]
    i = b_re.shape[-1]
    lam_c = lax.complex(jnp.minimum(a_re, -1e-4), a_im)
    lam_dt = lam_c * jnp.exp(log_dt)[..., None]
    b_bar = ((jnp.exp(lam_dt) - 1.0) / lam_c)[..., None] * lax.complex(b_re, b_im)[None]
    c_mat = lax.complex(c_re, c_im)
    tau = jnp.arange(tc, dtype=F32)
    pw = jnp.exp(lam_dt[:, :, None, :] * tau[None, None, :, None])
    pw1 = pw * jnp.exp(lam_dt)[:, :, None, :]
    kern = jnp.einsum('gip,dgtp,dgpj->dgtij', c_mat, pw, b_bar).real
    k_lag = jnp.concatenate([kern[1][:, :0:-1], kern[0][:, :1] + kern[1][:, :1], kern[0][:, 1:]], axis=1)
    kr = jnp.pad(k_lag.transpose(0, 3, 2, 1), ((0, 0), (0, 0), (0, 0), (0, 1)))
    kr = kr.reshape(g, i, i * 2 * tc)

    inc_f = pw[0][:, None, ::-1, :] * b_bar[0].transpose(0, 2, 1)[:, :, None]
    inc_b = pw[1][:, None, :, :] * b_bar[1].transpose(0, 2, 1)[:, :, None]
    bm = jnp.concatenate([inc_f.real, inc_b.real, inc_f.imag, inc_b.imag], axis=-1).reshape(g, tc * i, 4 * p)

    out_f = c_mat.transpose(0, 2, 1)[..., None] * pw1[0].transpose(0, 2, 1)[:, :, None, :]
    out_b = c_mat.transpose(0, 2, 1)[..., None] * pw1[1][:, ::-1].transpose(0, 2, 1)[:, :, None, :]
    z = jnp.zeros((g, p, tc * i), F32)
    fl = lambda t: t.reshape(g, p, tc * i)
    cm = jnp.concatenate([fl(out_f.real), z, fl(-out_f.imag), z, z, fl(out_b.real), z, fl(-out_b.imag)], axis=1)

    lam_t = jnp.exp(lam_dt * float(tc))
    lam = jnp.stack([jnp.concatenate([lam_t[0].real, lam_t[1].real], -1),
                     jnp.concatenate([lam_t[0].imag, lam_t[1].imag], -1)], axis=1)
    return kr, bm.astype(BF16), cm.astype(BF16), lam


def _s5_kernel(nc, nc_ctx, tc, u_ref, kr_ref, bm_ref, cm_ref, lam_ref, y_ref, up_ref, dx_ref, p_ref, m_ref):
    bn, i_sz = u_ref.shape[:2]
    nl = nc - nc_ctx
    width = kr_ref.shape[-1]
    seg = width // i_sz
    for j in range(i_sz):
        lag_rows = pltpu.roll(jnp.broadcast_to(kr_ref[0, j:j + 1, :], (tc, width)), width - (tc - 1), axis=1,
                              stride=1, stride_axis=0)
        m_ref[j * tc:(j + 1) * tc, :] = jnp.concatenate(
            [lag_rows[:, i * seg:i * seg + tc] for i in range(i_sz)], axis=1).astype(BF16)
    for n in range(nc):
        for i in range(i_sz):
            up_ref[n * bn:(n + 1) * bn, i * tc:(i + 1) * tc] = u_ref[:, i, n * tc:(n + 1) * tc]
    u = up_ref[...].astype(BF16)
    dx_ref[...] = jnp.dot(u, bm_ref[0], preferred_element_type=F32)
    lam = lam_ref[0]
    lr, li = lam[0:1], lam[1:2]
    half = lam.shape[-1] // 2
    is_f = lax.broadcasted_iota(jnp.int32, (bn, 2 * half), 1) < half

    def step(s, carry):
        sr, si = carry
        nf = jnp.where(s < nc_ctx, nl + s, s - nc_ctx)
        nb = nc - 1 - s
        rf = pl.multiple_of(nf * bn, bn)
        rb = pl.multiple_of(nb * bn, bn)
        p_ref[pl.ds(rf, bn), 0:128] = sr
        p_ref[pl.ds(rf, bn), 128:256] = si
        p_ref[pl.ds(rb, bn), 256:384] = sr
        p_ref[pl.ds(rb, bn), 384:512] = si
        d_re = jnp.where(is_f, dx_ref[pl.ds(rf, bn), 0:128], dx_ref[pl.ds(rb, bn), 0:128])
        d_im = jnp.where(is_f, dx_ref[pl.ds(rf, bn), 128:256], dx_ref[pl.ds(rb, bn), 128:256])
        return lr * sr - li * si + d_re, lr * si + li * sr + d_im

    zero = jnp.zeros((bn, 2 * half), F32)
    lax.fori_loop(0, nc, step, (zero, zero))
    y = jnp.dot(u, m_ref[...], preferred_element_type=F32)
    y = y + jnp.dot(p_ref[...].astype(BF16), cm_ref[0], preferred_element_type=F32)
    for n in range(nc):
        for i in range(i_sz):
            y_ref[:, i, n * tc:(n + 1) * tc] = y[n * bn:(n + 1) * bn, i * tc:(i + 1) * tc]


def _s5_mix(u_t, n_ctx, wts):
    kr, bm, cm, lam = wts
    bn, ch, t = u_t.shape
    g, i = kr.shape[:2]
    tc = kr.shape[2] // (2 * i)
    assert S5_STATE == 64 and bn % 8 == 0 and n_ctx % tc == 0 and t % tc == 0 and tc % 128 == 0
    nc, nc_ctx = t // tc, n_ctx // tc
    k = tc * i
    r = nc * bn
    return pl.pallas_call(
        functools.partial(_s5_kernel, nc, nc_ctx, tc),
        grid=(g,),
        in_specs=[
            pl.BlockSpec((bn, i, t), lambda j: (0, j, 0)),
            pl.BlockSpec((1, i, 2 * k), lambda j: (j, 0, 0)),
            pl.BlockSpec((1, k, 4 * S5_STATE), lambda j: (j, 0, 0)),
            pl.BlockSpec((1, 8 * S5_STATE, k), lambda j: (j, 0, 0)),
            pl.BlockSpec((1, 2, 2 * S5_STATE), lambda j: (j, 0, 0)),
        ],
        out_specs=pl.BlockSpec((bn, i, t), lambda j: (0, j, 0)),
        out_shape=jax.ShapeDtypeStruct((bn, ch, t), F32),
        scratch_shapes=[pltpu.VMEM((r, k), F32), pltpu.VMEM((r, 4 * S5_STATE), F32),
                        pltpu.VMEM((r, 8 * S5_STATE), F32), pltpu.VMEM((k, k), BF16)],
        compiler_params=_params(1),
        name="s5_mix",
    )(u_t, kr, bm, cm, lam)


def _gla_kernel(nc, nc_ctx, k_ref, v_ref, a_ref, q_ref, r_ref, wa_ref, ba_ref, gn_ref, y_ref, of_ref, ob_ref,
                s_ref):
    c, h_n, dk, dv = GLA_CHUNK, GLA_HEADS, GLA_DK, GLA_DV
    qk = h_n * dk
    q_scale = dk ** -0.5
    nbk = GLA_BLOCK
    rb = nbk * c
    row_i = lax.broadcasted_iota(jnp.int32, (rb, rb), 0)
    col_i = lax.broadcasted_iota(jnp.int32, (rb, rb), 1)
    same = (row_i // c) == (col_i // c)
    causal = (same & (row_i >= col_i), same & (row_i <= col_i))
    cum_ops = [jnp.concatenate([m.astype(BF16), same.astype(BF16)], axis=0) for m in causal]
    att_mask = [jnp.concatenate([m] * h_n, axis=0) for m in causal]
    lane_head = lax.broadcasted_iota(jnp.int32, (1, qk), 1) // dk
    head_lanes = [lane_head == h for h in range(h_n)]

    def by_head(x):
        zero = jnp.zeros_like(x)
        return jnp.concatenate([jnp.where(m, x, zero) for m in head_lanes], axis=0)

    def block(j, d):
        rows = pl.ds(pl.multiple_of(j * rb, rb), rb)
        k = k_ref[0, rows, :].astype(F32)
        q = q_ref[0, rows, :].astype(F32) * q_scale
        v = v_ref[0, rows, :]
        z = jnp.dot(a_ref[0, rows, :], wa_ref[:, d * qk:(d + 1) * qk], preferred_element_type=F32) + ba_ref[d:d + 1, :]
        log_a = (jnp.minimum(z, 0.0) - jnp.log(1.0 + jnp.exp(-jnp.abs(z)))) * (1.0 / GLA_TAU)
        hi = log_a.astype(BF16)
        lo = (log_a - hi.astype(F32)).astype(BF16)
        cs = jnp.dot(cum_ops[d], jnp.concatenate([hi, lo], axis=1), preferred_element_type=F32)
        cs = cs[:, :qk] + cs[:, qk:]
        b, b_tot = cs[:rb], cs[rb:]
        qd = (q * jnp.exp(b)).astype(BF16)
        kd = (k * jnp.exp(-b)).astype(BF16)
        kl = (k * jnp.exp(b_tot - b)).astype(BF16)
        qm = by_head(qd)
        intra = []
        for h in range(h_n):
            att = lax.dot_general(qm[h * rb:(h + 1) * rb], kd, _NT, preferred_element_type=F32)
            att = jnp.where(causal[d], att, 0.0).astype(BF16)
            intra.append(jnp.dot(att, v[:, h * dv:(h + 1) * dv], preferred_element_type=F32))
        s_t = s_ref[d]
        inter = [None] * nbk
        for n in (range(nbk) if d == 0 else reversed(range(nbk))):
            r0 = n * c
            qm_n = jnp.concatenate([qm[h * rb + r0:h * rb + r0 + c] for h in range(h_n)], axis=0)
            inter[n] = lax.dot_general(qm_n, s_t.astype(BF16), _NT, preferred_element_type=F32)
            v_t = jnp.concatenate(
                [jnp.concatenate([v[r0:r0 + c, h * dv:(h + 1) * dv] for h in range(p, p + dv // c)], axis=0).T
                 for p in range(0, h_n, dv // c)], axis=1)
            s_t = s_t * jnp.exp(b_tot[r0:r0 + 1]) + jnp.dot(v_t, by_head(kl[r0:r0 + c]), preferred_element_type=F32)
        s_ref[d] = s_t
        o = jnp.concatenate(
            [intra[h] + jnp.concatenate([inter[n][h * c:(h + 1) * c] for n in range(nbk)], axis=0)
             for h in range(h_n)], axis=1)
        return rows, o

    def scan_step(i, carry, lo_block, hi_block):
        rows_f, o_f = block(lo_block + i, 0)
        rows_b, o_b = block(hi_block - 1 - i, 1)
        of_ref[rows_f, :] = o_f
        ob_ref[rows_b, :] = o_b
        return carry

    def readout(n, carry):
        rows = pl.ds(pl.multiple_of(n * c, c), c)
        o = of_ref[rows, :] + ob_ref[rows, :]
        r = r_ref[0, rows, :].astype(F32)
        gate = r * jax.nn.sigmoid(r)
        outs = []
        for h in range(h_n):
            oh = o[:, h * dv:(h + 1) * dv]
            oh = oh * lax.rsqrt(jnp.mean(oh * oh, axis=-1, keepdims=True) + EPS) * gn_ref[...]
            outs.append(oh * gate[:, h * dv:(h + 1) * dv])
        y_ref[0, rows, :] = jnp.concatenate(outs, axis=1).astype(y_ref.dtype)
        return carry

    nl, n_ctx = (nc - nc_ctx) // nbk, nc_ctx // nbk
    s_ref[...] = jnp.zeros(s_ref.shape, F32)
    lax.fori_loop(0, n_ctx, functools.partial(scan_step, lo_block=nl, hi_block=nl + n_ctx), 0)
    lax.fori_loop(0, nl, functools.partial(scan_step, lo_block=0, hi_block=nl), 0)
    lax.fori_loop(0, nc, readout, 0, unroll=2)


def _gla_mix(k, v, a, q, r, wa, ba, gnorm, n_ctx):
    bn, t, qk = k.shape
    vd = v.shape[-1]
    c = GLA_CHUNK
    assert t % (c * GLA_BLOCK) == 0 and n_ctx % (c * GLA_BLOCK) == 0 and GLA_DV % c == 0
    seq = lambda n: pl.BlockSpec((1, t, n), lambda i: (i, 0, 0), pipeline_mode=pl.Buffered(1))
    return pl.pallas_call(
        functools.partial(_gla_kernel, t // c, n_ctx // c),
        grid=(bn,),
        in_specs=[seq(qk), seq(vd), seq(a.shape[-1]), seq(qk), seq(vd),
                  _const_spec(wa.shape), _const_spec(ba.shape), _const_spec(gnorm.shape)],
        out_specs=pl.BlockSpec((1, t, vd), lambda i: (i, 0, 0)),
        out_shape=jax.ShapeDtypeStruct((bn, t, vd), BF16),
        scratch_shapes=[pltpu.VMEM((t, vd), F32), pltpu.VMEM((t, vd), F32), pltpu.VMEM((2, GLA_DV, qk), F32)],
        compiler_params=_params(1),
        name="gla_mix",
    )(k, v, a, q, r, wa, ba, gnorm)


HY_NB = 128
HY_XPITCH = HY_NB + 8
HY_APITCH = 2 * HY_NB + 8
HY_CT = 128
HY_UNROLL_OUTER = 16
HY_UNROLL_INNER = 8


def _hyena_filters(n, params):
    w1, b1, w2, b2, w3, freq1, freq2, decay = (p.astype(F32) for p in params)
    t = jnp.linspace(0.0, 1.0, n, dtype=F32)[:, None]
    bands = jnp.linspace(1e-4, HY_POS_BANDS - 1, HY_POS_BANDS, dtype=F32)
    ang = (2.0 * math.pi / n) * jnp.arange(n, dtype=F32)[:, None] * bands[None]
    z = jnp.concatenate([t, jnp.cos(ang), -jnp.sin(ang)], axis=-1)
    hid = jnp.sin(freq1 * (z @ w1 + b1))
    hid = jnp.sin(freq2 * (hid @ w2 + b2))
    h = (hid @ w3) * jnp.exp(-t * jnp.abs(decay))
    return h.reshape(n, HY_N_FILT, -1)


def _hyena_spectra(n, params):
    h = _hyena_filters(n, params)

    def spectrum(h_fwd, h_bwd):
        k = jnp.concatenate([h_fwd[:1] + h_bwd[:1], h_fwd[1:], jnp.zeros_like(h_fwd[:1]), h_bwd[:0:-1]], axis=0)
        k = k * lax.rsqrt(jnp.sum(k * k, axis=0, keepdims=True) + EPS)
        return jnp.fft.fft(k, axis=0)

    return spectrum(h[:, 0], h[:, 1]), spectrum(h[:, 2], h[:, 3])


def _two_stage_layout(kf, na):
    n, ch = kf.shape
    kf = kf.reshape(n // na, na, ch).transpose(1, 0, 2)
    return jnp.stack([kf.real, kf.imag], axis=1).astype(BF16)


def _cplx_block(m):
    return np.block([[m.real, -m.imag], [m.imag, m.real]])


@functools.lru_cache(maxsize=None)
def _dft_tables(n_seq):
    nb = HY_NB
    n = 2 * n_seq
    na = n // nb
    ha = na // 2
    ka = np.arange(na)[:, None]
    b = np.arange(nb)
    tw = np.exp(-2j * np.pi * ka * b[None, :] / n)
    w_a = np.exp(-2j * np.pi * ka * np.arange(ha)[None, :] / na)
    f1 = np.stack([_cplx_block(tw[:, i:i + 1] * w_a) for i in range(nb)])
    g = np.stack([_cplx_block((np.conj(tw[:, i:i + 1] * w_a)).T / n) for i in range(nb)])
    w_b = np.exp(-2j * np.pi * b[:, None] * b[None, :] / nb)
    f2 = _cplx_block(w_b)
    f2i = _cplx_block(np.conj(w_b))
    return tuple(jnp.asarray(t, dtype=BF16) for t in (f1, f2, f2i, g))


def _hyena_kernel(na, x1_ref, x2_ref, v_ref, cw_ref, bias_ref, kf1_ref, kf2_ref, f1_ref, f2_ref, f2i_ref, g_ref,
                  o_ref, rawf, vp, gp, yp, a2):
    nb, xp, ap = HY_NB, HY_XPITCH, HY_APITCH
    ha = na // 2
    n_seq = ha * nb
    ct = o_ref.shape[-1]
    rawf[:, 0:8, :] = jnp.zeros((2, 8, ct), F32)
    rawf[:, 8 + n_seq:16 + n_seq, :] = jnp.zeros((2, 8, ct), F32)

    def short_conv_into(raw_ref, which, dst):
        w0, w1, w2 = (cw_ref[j, which:which + 1, :] for j in range(HY_SHORT))
        for e in range(2):
            rawf[e, 8:8 + n_seq, :] = raw_ref[e].astype(F32)

            def body(a, c):
                win = rawf[e, pl.ds(pl.multiple_of(a * nb, nb), nb + 16), :]
                u = w0 * win[7:7 + nb] + w1 * win[8:8 + nb] + w2 * win[9:9 + nb]
                dst[e, pl.ds(pl.multiple_of(a * xp, 8), nb), :] = u
                return c

            lax.fori_loop(0, ha, body, 0)

    def long_conv(kf_ref):
        def stage1(b, c):
            xs = jnp.concatenate([vp[0, pl.ds(b, ha, stride=xp), :], vp[1, pl.ds(b, ha, stride=xp), :]], axis=0)
            r = jnp.dot(f1_ref[b], xs.astype(BF16), preferred_element_type=F32)
            a2[pl.ds(b, na, stride=ap), :] = r[:na]
            a2[pl.ds(nb + b, na, stride=ap), :] = r[na:]
            return c

        def stage2(kp, c):
            rows = [pl.ds(pl.multiple_of((2 * kp + e) * ap, 8), 2 * nb) for e in range(2)]
            slab = jnp.concatenate([a2[r, :] for r in rows], axis=1).astype(BF16)
            xf = jnp.dot(f2_ref[...], slab, preferred_element_type=F32)
            xr, xi = xf[:nb], xf[nb:]
            kr = jnp.concatenate([kf_ref[2 * kp + e, 0] for e in range(2)], axis=1).astype(F32)
            ki = jnp.concatenate([kf_ref[2 * kp + e, 1] for e in range(2)], axis=1).astype(F32)
            z = jnp.concatenate([xr * kr - xi * ki, xr * ki + xi * kr], axis=0).astype(BF16)
            back = jnp.dot(f2i_ref[...], z, preferred_element_type=F32)
            for e in range(2):
                a2[rows[e], :] = back[:, e * ct:(e + 1) * ct]
            return c

        def stage3(b, c):
            s = jnp.concatenate([a2[pl.ds(b, na, stride=ap), :], a2[pl.ds(nb + b, na, stride=ap), :]], axis=0)
            y = jnp.dot(g_ref[b], s.astype(BF16), preferred_element_type=F32)
            yp[0, pl.ds(b, ha, stride=xp), :] = y[:ha]
            yp[1, pl.ds(b, ha, stride=xp), :] = y[ha:]
            return c

        lax.fori_loop(0, nb, stage1, 0, unroll=HY_UNROLL_OUTER)
        lax.fori_loop(0, na // 2, stage2, 0, unroll=HY_UNROLL_INNER)
        lax.fori_loop(0, nb, stage3, 0, unroll=HY_UNROLL_OUTER)

    def gated(bias_row, write):
        for e in range(2):
            def body(a, c):
                rows = pl.ds(pl.multiple_of(a * xp, 8), nb)
                write(e, a, rows, gp[e, rows, :] * (yp[e, rows, :] + bias_row * vp[e, rows, :]))
                return c

            lax.fori_loop(0, ha, body, 0)

    def write_z(e, a, rows, val):
        vp[e, rows, :] = val

    def write_out(e, a, rows, val):
        o_ref[e, pl.ds(pl.multiple_of(a * nb, nb), nb), :] = val.astype(o_ref.dtype)

    short_conv_into(v_ref, 2, vp)
    short_conv_into(x1_ref, 0, gp)
    long_conv(kf1_ref)
    gated(bias_ref[0:1, :], write_z)
    short_conv_into(x2_ref, 1, gp)
    long_conv(kf2_ref)
    gated(bias_ref[1:2, :], write_out)


def _hyena_latent(hy_all, n_seq, conv_w, bias, kf1, kf2):
    bn, t_all = hy_all.shape[:2]
    ch = hy_all.shape[-1] // 3
    ct = HY_CT
    nct = ch // ct
    nb = HY_NB
    na = 2 * n_seq // nb
    assert bn % 2 == 0 and ch % ct == 0 and n_seq % (8 * nb) == 0
    f1, f2, f2i, g = _dft_tables(n_seq)
    col = lambda which: pl.BlockSpec((2, n_seq, ct), lambda j, p: (p, 0, which * nct + j), pipeline_mode=pl.Buffered(1))
    kf_spec = pl.BlockSpec((na, 2, nb, ct), lambda j, p: (0, 0, 0, j), pipeline_mode=pl.Buffered(1))
    pad_rows = (na // 2) * HY_XPITCH
    return pl.pallas_call(
        functools.partial(_hyena_kernel, na),
        grid=(nct, bn // 2),
        in_specs=[col(0), col(1), col(2),
                  pl.BlockSpec((HY_SHORT, 3, ct), lambda j, p: (0, 0, j)),
                  pl.BlockSpec((2, ct), lambda j, p: (0, j)),
                  kf_spec, kf_spec,
                  _const_spec(f1.shape), _const_spec(f2.shape), _const_spec(f2i.shape), _const_spec(g.shape)],
        out_specs=pl.BlockSpec((2, n_seq, ct), lambda j, p: (p, 0, j)),
        out_shape=jax.ShapeDtypeStruct((bn, t_all, ch), BF16),
        scratch_shapes=[pltpu.VMEM((2, n_seq + 16, ct), F32), pltpu.VMEM((2, pad_rows, ct), F32),
                        pltpu.VMEM((2, pad_rows, ct), F32), pltpu.VMEM((2, pad_rows, ct), F32),
                        pltpu.VMEM((na * HY_APITCH, ct), F32)],
        compiler_params=_params(2),
        name="hyena_latent",
    )(hy_all, hy_all, hy_all, conv_w.reshape(HY_SHORT, 3, ch), bias, _two_stage_layout(kf1, na),
      _two_stage_layout(kf2, na), f1, f2, f2i, g)


@functools.lru_cache(maxsize=None)
def _dense_dft_tables(n_seq):
    n = 2 * n_seq
    ang = 2.0 * np.pi * np.arange(n)[:, None] * np.arange(n_seq)[None, :] / n
    fd = np.concatenate([np.cos(ang), -np.sin(ang)], axis=0)
    gd = np.concatenate([np.cos(ang).T, -np.sin(ang).T], axis=1) / n
    return jnp.asarray(fd, dtype=BF16), jnp.asarray(gd, dtype=BF16)


def _hyena_ctx_kernel(n_seq, x1_ref, x2_ref, v_ref, cw_ref, bias_ref, kf1_ref, kf2_ref, fd_ref, gd_ref, buf_ref,
                      o_ref):
    del buf_ref
    n = 2 * n_seq

    def short_conv(raw_ref, which):
        x = raw_ref[0].astype(F32)
        zero = jnp.zeros((1, x.shape[-1]), F32)
        prev = jnp.concatenate([zero, x[:-1]], axis=0)
        nxt = jnp.concatenate([x[1:], zero], axis=0)
        return cw_ref[0, which:which + 1, :] * prev + cw_ref[1, which:which + 1, :] * x + cw_ref[2, which:which + 1, :] * nxt

    def long_conv(u, kf_ref, bias_row):
        xf = jnp.dot(fd_ref[...], u.astype(BF16), preferred_element_type=F32)
        xr, xi = xf[:n], xf[n:]
        kr, ki = kf_ref[0], kf_ref[1]
        z = jnp.concatenate([xr * kr - xi * ki, xr * ki + xi * kr], axis=0).astype(BF16)
        return jnp.dot(gd_ref[...], z, preferred_element_type=F32) + bias_row * u

    x1, x2, v = short_conv(x1_ref, 0), short_conv(x2_ref, 1), short_conv(v_ref, 2)
    z = x1 * long_conv(v, kf1_ref, bias_ref[0:1, :])
    o_ref[0] = (x2 * long_conv(z, kf2_ref, bias_ref[1:2, :])).astype(o_ref.dtype)


def _hyena_context(hy_all, y_buf, row_block, n_seq, conv_w, bias, kf1, kf2):
    bn = hy_all.shape[0]
    ch = hy_all.shape[-1] // 3
    fd, gd = _dense_dft_tables(n_seq)
    col = lambda which: pl.BlockSpec((1, n_seq, ch), lambda i: (i, row_block, which))
    split = lambda kf: jnp.stack([kf.real, kf.imag], axis=0)
    return pl.pallas_call(
        functools.partial(_hyena_ctx_kernel, n_seq),
        grid=(bn,),
        in_specs=[col(0), col(1), col(2), _const_spec((HY_SHORT, 3, ch)), _const_spec((2, ch)),
                  _const_spec((2, 2 * n_seq, ch)), _const_spec((2, 2 * n_seq, ch)),
                  _const_spec(fd.shape), _const_spec(gd.shape), pl.BlockSpec(memory_space=pl.ANY)],
        out_specs=pl.BlockSpec((1, n_seq, ch), lambda i: (i, row_block, 0)),
        out_shape=jax.ShapeDtypeStruct(y_buf.shape, y_buf.dtype),
        input_output_aliases={9: 0},
        compiler_params=_params(1),
        name="hyena_context",
    )(hy_all, hy_all, hy_all, conv_w.reshape(HY_SHORT, 3, ch), bias, split(kf1), split(kf2), fd, gd, y_buf)


def kernel(x, c, ctx, c_ctx, w_mod, b_mod, g_norm1, g_norm2, w_in, hy_conv, hy_w1, hy_b1, hy_w2, hy_b2, hy_w3,
           hy_freq1, hy_freq2, hy_decay, hy_bias, gla_wa2, gla_ba, gla_gnorm, s5_a_re, s5_a_im, s5_log_dt, s5_b_re,
           s5_b_im, s5_c_re, s5_c_im, s5_d, s5_w_glu, s5_b_glu, w_br_hy, w_br_gla, w_br_s5, w_out, w_ffn_in,
           w_ffn_out, g_final):
    bn, n_lat, d = x.shape
    n_ctx = ctx.shape[1]
    depth = w_in.shape[0]
    d_hy = w_br_hy.shape[1]
    s5_ch = w_br_s5.shape[1]
    d_ff = w_ffn_out.shape[1]
    assert n_ctx == ROW_TILE and n_lat % ROW_TILE == 0
    n_lat_tiles = n_lat // ROW_TILE
    n_low = 2 * GLA_LOWRANK
    o_a = GLA_QK + GLA_V
    o_u = o_a + n_low
    o_gate = o_u + s5_ch + GLA_QK + GLA_V + 3 * d_hy
    col_sizes = (GLA_QK, GLA_V, A_PAD, s5_ch, GLA_QK, GLA_V, 3 * d_hy)

    cond = jnp.concatenate([c_ctx[None], c], axis=0)
    cond = jnp.pad(cond, ((0, (-cond.shape[0]) % 8), (0, 0)))
    mod_all = _modulation_all(cond, w_mod, b_mod)

    xc = jnp.concatenate([x, ctx], axis=1)
    out = None
    for l in range(depth):
        last = l == depth - 1
        mods = jnp.stack([jnp.broadcast_to(mod_all[l, 0], (bn, 6 * d)), mod_all[l, 1:1 + bn]], axis=1)
        mods = mods[:, :, None, :]
        wl = w_in[l]
        w_proj = jnp.concatenate(
            [wl[:, :o_u], jnp.zeros((d, A_PAD - n_low), F32), wl[:, o_u:o_gate]], axis=1).astype(BF16)
        g1 = g_norm1[l][None]
        w_u_t = wl[:, o_u:o_u + s5_ch].T.astype(BF16)
        k_a, v_a, a_a, u_a, q_a, r_a, hy_a, u_t = _project(xc, mods, g1, w_proj, col_sizes, w_u_t, n_lat_tiles)

        wa = jnp.zeros((A_PAD, 2 * GLA_QK), F32)
        wa = wa.at[:GLA_LOWRANK, :GLA_QK].set(gla_wa2[l, 0]).at[GLA_LOWRANK:n_low, GLA_QK:].set(gla_wa2[l, 1])
        y_gla = _gla_mix(k_a, v_a, a_a, q_a, r_a, wa.astype(BF16), gla_ba[l], gla_gnorm[l][None], n_ctx)

        s5w = _s5_weights(s5_a_re[l], s5_a_im[l], s5_log_dt[l], s5_b_re[l], s5_b_im[l], s5_c_re[l], s5_c_im[l],
                          S5_CHUNK)
        ys = _s5_mix(u_t, n_ctx, s5w)

        hy_p = (hy_w1[l], hy_b1[l], hy_w2[l], hy_b2[l], hy_w3[l], hy_freq1[l], hy_freq2[l], hy_decay[l])
        y_hy = _hyena_latent(hy_a, n_lat, hy_conv[l], hy_bias[l], *_hyena_spectra(n_lat, hy_p))
        if not last:
            y_hy = _hyena_context(hy_a, y_hy, n_lat // n_ctx, n_ctx, hy_conv[l], hy_bias[l],
                                  *_hyena_spectra(n_ctx, hy_p))

        wts = (wl[:, o_gate:].astype(BF16), w_br_hy[l].astype(BF16), w_br_gla[l].astype(BF16),
               w_br_s5[l].astype(BF16), w_out[l].astype(BF16), s5_d[l][None], s5_w_glu[l].astype(BF16),
               s5_b_glu[l][None])
        n_tiles = n_lat_tiles if last else n_lat_tiles + n_ctx // ROW_TILE
        x_mid = _merge(xc, mods, g1, y_hy, y_gla, ys, u_a, wts, n_tiles, n_lat_tiles)
        wf = w_ffn_in[l]
        xc = _ffn(x_mid, mods, g_norm2[l][None], wf[:, :d_ff].astype(BF16), wf[:, d_ff:].astype(BF16),
                  w_ffn_out[l].astype(BF16), g_final[None], n_lat_tiles, last)
    return xc
```

```python
import functools
import math

import jax
import jax.numpy as jnp
import numpy as np
from jax import lax
from jax.experimental import pallas as pl
from jax.experimental.pallas import tpu as pltpu

F32 = jnp.float32
BF16 = jnp.bfloat16
EPS = 1e-6

GRID_W = 64
HY_SHORT = 3
HY_POS_BANDS = 16
HY_N_FILT = 4
GLA_HEADS = 4
GLA_DK = 64
GLA_DV = 128
GLA_QK = GLA_HEADS * GLA_DK
GLA_V = GLA_HEADS * GLA_DV
GLA_LOWRANK = 16
GLA_TAU = 16.0
GLA_CHUNK = 64
GLA_BLOCK = 4
S5_GROUP = 16
S5_STATE = 64
A_PAD = 128

ROW_TILE = 256
VMEM_LIMIT = 56 * 1024 * 1024


def _const_spec(shape):
    nd = len(shape)
    return pl.BlockSpec(shape, lambda *_: (0,) * nd, pipeline_mode=pl.Buffered(1))


def _params(n_axes):
    return pltpu.CompilerParams(dimension_semantics=("parallel",) * n_axes, vmem_limit_bytes=VMEM_LIMIT)


def _mod_kernel(c_ref, w_ref, b_ref, o_ref):
    c = c_ref[...]
    s = c * jax.nn.sigmoid(c)
    o_ref[0] = jnp.dot(s.astype(BF16), w_ref[0], preferred_element_type=F32) + b_ref[0]


def _modulation_all(cond, w_mod, b_mod):
    depth, d, n = w_mod.shape
    r = cond.shape[0]
    tn = 1536
    return pl.pallas_call(
        _mod_kernel,
        grid=(depth, n // tn),
        in_specs=[
            pl.BlockSpec((r, d), lambda l, j: (0, 0)),
            pl.BlockSpec((1, d, tn), lambda l, j: (l, 0, j)),
            pl.BlockSpec((1, 1, tn), lambda l, j: (l, 0, j)),
        ],
        out_specs=pl.BlockSpec((1, r, tn), lambda l, j: (l, 0, j)),
        out_shape=jax.ShapeDtypeStruct((depth, r, n), F32),
        compiler_params=_params(2),
        name="modulation",
    )(cond, w_mod.astype(BF16), b_mod.reshape(depth, 1, n))


def _norm_mod(x, g, shift, scale):
    y = x * lax.rsqrt(jnp.mean(x * x, axis=-1, keepdims=True) + EPS)
    return (y * g) * (1.0 + scale) + shift


_NT = (((1,), (1,)), ((), ()))


def _proj_kernel(col_sizes, d, x_ref, mod_ref, g_ref, w_ref, wt_ref, *o_refs):
    m = mod_ref[0, 0]
    h = _norm_mod(x_ref[0], g_ref[...], m[:, 0:d], m[:, d:2 * d]).astype(BF16)
    off = 0
    for o_ref, n in zip(o_refs[:-1], col_sizes):
        o_ref[0] = jnp.dot(h, w_ref[:, off:off + n], preferred_element_type=F32).astype(o_ref.dtype)
        off += n
    o_refs[-1][0] = lax.dot_general(wt_ref[...], h, _NT, preferred_element_type=F32)


def _mod_spec(d, n_lat_tiles):
    return pl.BlockSpec((1, 1, 1, 6 * d), lambda i, j: (i, (j < n_lat_tiles).astype(jnp.int32), 0, 0))


def _project(xc, mods, g, w, col_sizes, w_t, n_lat_tiles):
    b, t, d = xc.shape
    tm = ROW_TILE
    n_tot = sum(col_sizes)
    n_t = w_t.shape[0]
    out_shape = [jax.ShapeDtypeStruct((b, t, n), BF16) for n in col_sizes] + [jax.ShapeDtypeStruct((b, n_t, t), F32)]
    out_specs = [pl.BlockSpec((1, tm, n), lambda i, j: (i, j, 0)) for n in col_sizes]
    out_specs.append(pl.BlockSpec((1, n_t, tm), lambda i, j: (i, 0, j)))
    return pl.pallas_call(
        functools.partial(_proj_kernel, tuple(col_sizes), d),
        grid=(b, t // tm),
        in_specs=[
            pl.BlockSpec((1, tm, d), lambda i, j: (i, j, 0)),
            _mod_spec(d, n_lat_tiles),
            _const_spec((1, d)),
            _const_spec((d, n_tot)),
            _const_spec((n_t, d)),
        ],
        out_specs=out_specs,
        out_shape=out_shape,
        compiler_params=_params(2),
        name="in_proj",
    )(xc, mods, g, w, w_t)


def _gelu_tanh(x):
    return 0.5 * x * (1.0 + jnp.tanh(math.sqrt(2.0 / math.pi) * (x + 0.044715 * (x * x * x))))


def _merge_kernel(d, x_ref, mod_ref, g_ref, yhy_ref, ygla_ref, ys5_ref, u_ref, wg_ref, whb_ref, wgb_ref, wsb_ref,
                  wo_ref, s5d_ref, wglu_ref, bglu_ref, o_ref):
    x = x_ref[0]
    m = mod_ref[0, 0]
    h = _norm_mod(x, g_ref[...], m[:, 0:d], m[:, d:2 * d]).astype(BF16)
    y5 = ys5_ref[0].T + s5d_ref[...] * u_ref[0].astype(F32)
    g5 = _gelu_tanh(y5)
    y_s5 = g5 * jax.nn.sigmoid(jnp.dot(g5.astype(BF16), wglu_ref[...], preferred_element_type=F32) + bglu_ref[...])

    def branch(k, y, wb_ref):
        gate = jnp.dot(h, wg_ref[:, k * d:(k + 1) * d], preferred_element_type=F32)
        return jax.nn.sigmoid(gate) * jnp.dot(y, wb_ref[...], preferred_element_type=F32)

    mix = branch(0, yhy_ref[0], whb_ref) + branch(1, ygla_ref[0], wgb_ref) + branch(2, y_s5.astype(BF16), wsb_ref)
    out = jnp.dot(mix.astype(BF16), wo_ref[...], preferred_element_type=F32)
    o_ref[0] = x + m[:, 2 * d:3 * d] * out


def _merge(xc, mods, g, y_hy, y_gla, ys_s5, u_s5, wts, nj, n_lat_tiles):
    b, t, d = xc.shape
    tm = ROW_TILE
    ch = y_hy.shape[-1]
    row = lambda n: pl.BlockSpec((1, tm, n), lambda i, j: (i, j, 0))
    w_gate, w_hy, w_gla, w_s5, w_out, s5_d, w_glu, b_glu = wts
    return pl.pallas_call(
        functools.partial(_merge_kernel, d),
        grid=(b, nj),
        in_specs=[
            row(d),
            _mod_spec(d, n_lat_tiles),
            _const_spec((1, d)),
            row(ch), row(ch), pl.BlockSpec((1, ch, tm), lambda i, j: (i, 0, j)), row(ch),
            _const_spec(w_gate.shape), _const_spec(w_hy.shape), _const_spec(w_gla.shape), _const_spec(w_s5.shape),
            _const_spec(w_out.shape), _const_spec(s5_d.shape), _const_spec(w_glu.shape), _const_spec(b_glu.shape),
        ],
        out_specs=pl.BlockSpec((1, tm, d), lambda i, j: (i, j, 0)),
        out_shape=jax.ShapeDtypeStruct((b, nj * tm, d), F32),
        compiler_params=_params(2),
        name="merge",
    )(xc, mods, g, y_hy, y_gla, ys_s5, u_s5, w_gate, w_hy, w_gla, w_s5, w_out, s5_d, w_glu, b_glu)


def _ffn_kernel(d, final, x_ref, mod_ref, g_ref, wa_ref, wb_ref, wo_ref, gf_ref, o_ref):
    x = x_ref[0]
    m = mod_ref[0, 0]
    h = _norm_mod(x, g_ref[...], m[:, 3 * d:4 * d], m[:, 4 * d:5 * d]).astype(BF16)
    a = jnp.dot(h, wa_ref[...], preferred_element_type=F32)
    bb = jnp.dot(h, wb_ref[...], preferred_element_type=F32)
    act = (a * jax.nn.sigmoid(a) * bb).astype(BF16)
    y = x + m[:, 5 * d:6 * d] * jnp.dot(act, wo_ref[...], preferred_element_type=F32)
    if final:
        y = y * lax.rsqrt(jnp.mean(y * y, axis=-1, keepdims=True) + EPS) * gf_ref[...]
    o_ref[0] = y


def _ffn(x, mods, g, wa, wb, wo, g_final, n_lat_tiles, final):
    b, t, d = x.shape
    tm = ROW_TILE
    return pl.pallas_call(
        functools.partial(_ffn_kernel, d, final),
        grid=(b, t // tm),
        in_specs=[
            pl.BlockSpec((1, tm, d), lambda i, j: (i, j, 0)),
            _mod_spec(d, n_lat_tiles),
            _const_spec((1, d)),
            _const_spec(wa.shape), _const_spec(wb.shape), _const_spec(wo.shape),
            _const_spec((1, d)),
        ],
        out_specs=pl.BlockSpec((1, tm, d), lambda i, j: (i, j, 0)),
        out_shape=jax.ShapeDtypeStruct((b, t, d), F32),
        compiler_params=_params(2),
        name="ffn",
    )(x, mods, g, wa, wb, wo, g_final)


S5_CHUNK = 128


def _s5_weights(a_re, a_im, log_dt, b_re, b_im, c_re, c_im, tc):
    g, p = a_re.shape[1:]
    i = b_re.shape[-1]
    lam_c = lax.complex(jnp.minimum(a_re, -1e-4), a_im)
    lam_dt = lam_c * jnp.exp(log_dt)[..., None]
    b_bar = ((jnp.exp(lam_dt) - 1.0) / lam_c)[..., None] * lax.complex(b_re, b_im)[None]
    c_mat = lax.complex(c_re, c_im)
    tau = jnp.arange(tc, dtype=F32)
    pw = jnp.exp(lam_dt[:, :, None, :] * tau[None, None, :, None])
    pw1 = pw * jnp.exp(lam_dt)[:, :, None, :]
    kern = jnp.einsum('gip,dgtp,dgpj->dgtij', c_mat, pw, b_bar).real
    k_lag = jnp.concatenate([kern[1][:, :0:-1], kern[0][:, :1] + kern[1][:, :1], kern[0][:, 1:]], axis=1)
    kr = jnp.pad(k_lag.transpose(0, 3, 2, 1), ((0, 0), (0, 0), (0, 0), (0, 1)))
    kr = kr.reshape(g, i, i * 2 * tc)

    inc_f = pw[0][:, None, ::-1, :] * b_bar[0].transpose(0, 2, 1)[:, :, None]
    inc_b = pw[1][:, None, :, :] * b_bar[1].transpose(0, 2, 1)[:, :, None]
    bm = jnp.concatenate([inc_f.real, inc_b.real, inc_f.imag, inc_b.imag], axis=-1).reshape(g, tc * i, 4 * p)

    out_f = c_mat.transpose(0, 2, 1)[..., None] * pw1[0].transpose(0, 2, 1)[:, :, None, :]
    out_b = c_mat.transpose(0, 2, 1)[..., None] * pw1[1][:, ::-1].transpose(0, 2, 1)[:, :, None, :]
    z = jnp.zeros((g, p, tc * i), F32)
    fl = lambda t: t.reshape(g, p, tc * i)
    cm = jnp.concatenate([fl(out_f.real), z, fl(-out_f.imag), z, z, fl(out_b.real), z, fl(-out_b.imag)], axis=1)

    lam_t = jnp.exp(lam_dt * float(tc))
    lam = jnp.stack([jnp.concatenate([lam_t[0].real, lam_t[1].real], -1),
                     jnp.concatenate([lam_t[0].imag, lam_t[1].imag], -1)], axis=1)
    return kr, bm.astype(BF16), cm.astype(BF16), lam


def _s5_kernel(nc, nc_ctx, tc, u_ref, kr_ref, bm_ref, cm_ref, lam_ref, y_ref, up_ref, dx_ref, p_ref, m_ref):
    bn, i_sz = u_ref.shape[:2]
    nl = nc - nc_ctx
    width = kr_ref.shape[-1]
    seg = width // i_sz
    for j in range(i_sz):
        lag_rows = pltpu.roll(jnp.broadcast_to(kr_ref[0, j:j + 1, :], (tc, width)), width - (tc - 1), axis=1,
                              stride=1, stride_axis=0)
        m_ref[j * tc:(j + 1) * tc, :] = jnp.concatenate(
            [lag_rows[:, i * seg:i * seg + tc] for i in range(i_sz)], axis=1).astype(BF16)
    for n in range(nc):
        for i in range(i_sz):
            up_ref[n * bn:(n + 1) * bn, i * tc:(i + 1) * tc] = u_ref[:, i, n * tc:(n + 1) * tc]
    u = up_ref[...].astype(BF16)
    dx_ref[...] = jnp.dot(u, bm_ref[0], preferred_element_type=F32)
    lam = lam_ref[0]
    lr, li = lam[0:1], lam[1:2]
    half = lam.shape[-1] // 2
    is_f = lax.broadcasted_iota(jnp.int32, (bn, 2 * half), 1) < half

    def step(s, carry):
        sr, si = carry
        nf = jnp.where(s < nc_ctx, nl + s, s - nc_ctx)
        nb = nc - 1 - s
        rf = pl.multiple_of(nf * bn, bn)
        rb = pl.multiple_of(nb * bn, bn)
        p_ref[pl.ds(rf, bn), 0:128] = sr
        p_ref[pl.ds(rf, bn), 128:256] = si
        p_ref[pl.ds(rb, bn), 256:384] = sr
        p_ref[pl.ds(rb, bn), 384:512] = si
        d_re = jnp.where(is_f, dx_ref[pl.ds(rf, bn), 0:128], dx_ref[pl.ds(rb, bn), 0:128])
        d_im = jnp.where(is_f, dx_ref[pl.ds(rf, bn), 128:256], dx_ref[pl.ds(rb, bn), 128:256])
        return lr * sr - li * si + d_re, lr * si + li * sr + d_im

    zero = jnp.zeros((bn, 2 * half), F32)
    lax.fori_loop(0, nc, step, (zero, zero))
    y = jnp.dot(u, m_ref[...], preferred_element_type=F32)
    y = y + jnp.dot(p_ref[...].astype(BF16), cm_ref[0], preferred_element_type=F32)
    for n in range(nc):
        for i in range(i_sz):
            y_ref[:, i, n * tc:(n + 1) * tc] = y[n * bn:(n + 1) * bn, i * tc:(i + 1) * tc]


def _s5_mix(u_t, n_ctx, wts):
    kr, bm, cm, lam = wts
    bn, ch, t = u_t.shape
    g, i = kr.shape[:2]
    tc = kr.shape[2] // (2 * i)
    assert S5_STATE == 64 and bn % 8 == 0 and n_ctx % tc == 0 and t % tc == 0 and tc % 128 == 0
    nc, nc_ctx = t // tc, n_ctx // tc
    k = tc * i
    r = nc * bn
    return pl.pallas_call(
        functools.partial(_s5_kernel, nc, nc_ctx, tc),
        grid=(g,),
        in_specs=[
            pl.BlockSpec((bn, i, t), lambda j: (0, j, 0)),
            pl.BlockSpec((1, i, 2 * k), lambda j: (j, 0, 0)),
            pl.BlockSpec((1, k, 4 * S5_STATE), lambda j: (j, 0, 0)),
            pl.BlockSpec((1, 8 * S5_STATE, k), lambda j: (j, 0, 0)),
            pl.BlockSpec((1, 2, 2 * S5_STATE), lambda j: (j, 0, 0)),
        ],
        out_specs=pl.BlockSpec((bn, i, t), lambda j: (0, j, 0)),
        out_shape=jax.ShapeDtypeStruct((bn, ch, t), F32),
        scratch_shapes=[pltpu.VMEM((r, k), F32), pltpu.VMEM((r, 4 * S5_STATE), F32),
                        pltpu.VMEM((r, 8 * S5_STATE), F32), pltpu.VMEM((k, k), BF16)],
        compiler_params=_params(1),
        name="s5_mix",
    )(u_t, kr, bm, cm, lam)


def _gla_kernel(nc, nc_ctx, k_ref, v_ref, a_ref, q_ref, r_ref, wa_ref, ba_ref, gn_ref, y_ref, of_ref, ob_ref,
                s_ref):
    c, h_n, dk, dv = GLA_CHUNK, GLA_HEADS, GLA_DK, GLA_DV
    qk = h_n * dk
    q_scale = dk ** -0.5
    nbk = GLA_BLOCK
    rb = nbk * c
    row_i = lax.broadcasted_iota(jnp.int32, (rb, rb), 0)
    col_i = lax.broadcasted_iota(jnp.int32, (rb, rb), 1)
    same = (row_i // c) == (col_i // c)
    causal = (same & (row_i >= col_i), same & (row_i <= col_i))
    cum_ops = [jnp.concatenate([m.astype(BF16), same.astype(BF16)], axis=0) for m in causal]
    att_mask = [jnp.concatenate([m] * h_n, axis=0) for m in causal]
    lane_head = lax.broadcasted_iota(jnp.int32, (1, qk), 1) // dk
    head_lanes = [lane_head == h for h in range(h_n)]

    def by_head(x):
        zero = jnp.zeros_like(x)
        return jnp.concatenate([jnp.where(m, x, zero) for m in head_lanes], axis=0)

    def block(j, d):
        rows = pl.ds(pl.multiple_of(j * rb, rb), rb)
        k = k_ref[0, rows, :].astype(F32)
        q = q_ref[0, rows, :].astype(F32) * q_scale
        v = v_ref[0, rows, :]
        z = jnp.dot(a_ref[0, rows, :], wa_ref[:, d * qk:(d + 1) * qk], preferred_element_type=F32) + ba_ref[d:d + 1, :]
        log_a = (jnp.minimum(z, 0.0) - jnp.log(1.0 + jnp.exp(-jnp.abs(z)))) * (1.0 / GLA_TAU)
        hi = log_a.astype(BF16)
        lo = (log_a - hi.astype(F32)).astype(BF16)
        cs = jnp.dot(cum_ops[d], jnp.concatenate([hi, lo], axis=1), preferred_element_type=F32)
        cs = cs[:, :qk] + cs[:, qk:]
        b, b_tot = cs[:rb], cs[rb:]
        qd = (q * jnp.exp(b)).astype(BF16)
        kd = (k * jnp.exp(-b)).astype(BF16)
        kl = (k * jnp.exp(b_tot - b)).astype(BF16)
        qm = by_head(qd)
        intra = []
        for h in range(h_n):
            att = lax.dot_general(qm[h * rb:(h + 1) * rb], kd, _NT, preferred_element_type=F32)
            att = jnp.where(causal[d], att, 0.0).astype(BF16)
            intra.append(jnp.dot(att, v[:, h * dv:(h + 1) * dv], preferred_element_type=F32))
        s_t = s_ref[d]
        inter = [None] * nbk
        for n in (range(nbk) if d == 0 else reversed(range(nbk))):
            r0 = n * c
            qm_n = jnp.concatenate([qm[h * rb + r0:h * rb + r0 + c] for h in range(h_n)], axis=0)
            inter[n] = lax.dot_general(qm_n, s_t.astype(BF16), _NT, preferred_element_type=F32)
            v_t = jnp.concatenate(
                [jnp.concatenate([v[r0:r0 + c, h * dv:(h + 1) * dv] for h in range(p, p + dv // c)], axis=0).T
                 for p in range(0, h_n, dv // c)], axis=1)
            s_t = s_t * jnp.exp(b_tot[r0:r0 + 1]) + jnp.dot(v_t, by_head(kl[r0:r0 + c]), preferred_element_type=F32)
        s_ref[d] = s_t
        o = jnp.concatenate(
            [intra[h] + jnp.concatenate([inter[n][h * c:(h + 1) * c] for n in range(nbk)], axis=0)
             for h in range(h_n)], axis=1)
        return rows, o

    def scan_step(i, carry, lo_block, hi_block):
        rows_f, o_f = block(lo_block + i, 0)
        rows_b, o_b = block(hi_block - 1 - i, 1)
        of_ref[rows_f, :] = o_f
        ob_ref[rows_b, :] = o_b
        return carry

    def readout(n, carry):
        rows = pl.ds(pl.multiple_of(n * c, c), c)
        o = of_ref[rows, :] + ob_ref[rows, :]
        r = r_ref[0, rows, :].astype(F32)
        gate = r * jax.nn.sigmoid(r)
        outs = []
        for h in range(h_n):
            oh = o[:, h * dv:(h + 1) * dv]
            oh = oh * lax.rsqrt(jnp.mean(oh * oh, axis=-1, keepdims=True) + EPS) * gn_ref[...]
            outs.append(oh * gate[:, h * dv:(h + 1) * dv])
        y_ref[0, rows, :] = jnp.concatenate(outs, axis=1).astype(y_ref.dtype)
        return carry

    nl, n_ctx = (nc - nc_ctx) // nbk, nc_ctx // nbk
    s_ref[...] = jnp.zeros(s_ref.shape, F32)
    lax.fori_loop(0, n_ctx, functools.partial(scan_step, lo_block=nl, hi_block=nl + n_ctx), 0)
    lax.fori_loop(0, nl, functools.partial(scan_step, lo_block=0, hi_block=nl), 0)
    lax.fori_loop(0, nc, readout, 0, unroll=2)


def _gla_mix(k, v, a, q, r, wa, ba, gnorm, n_ctx):
    bn, t, qk = k.shape
    vd = v.shape[-1]
    c = GLA_CHUNK
    assert t % (c * GLA_BLOCK) == 0 and n_ctx % (c * GLA_BLOCK) == 0 and GLA_DV % c == 0
    seq = lambda n: pl.BlockSpec((1, t, n), lambda i: (i, 0, 0), pipeline_mode=pl.Buffered(1))
    return pl.pallas_call(
        functools.partial(_gla_kernel, t // c, n_ctx // c),
        grid=(bn,),
        in_specs=[seq(qk), seq(vd), seq(a.shape[-1]), seq(qk), seq(vd),
                  _const_spec(wa.shape), _const_spec(ba.shape), _const_spec(gnorm.shape)],
        out_specs=pl.BlockSpec((1, t, vd), lambda i: (i, 0, 0)),
        out_shape=jax.ShapeDtypeStruct((bn, t, vd), BF16),
        scratch_shapes=[pltpu.VMEM((t, vd), F32), pltpu.VMEM((t, vd), F32), pltpu.VMEM((2, GLA_DV, qk), F32)],
        compiler_params=_params(1),
        name="gla_mix",
    )(k, v, a, q, r, wa, ba, gnorm)


HY_NB = 128
HY_XPITCH = HY_NB + 8
HY_APITCH = 2 * HY_NB + 8
HY_CT = 128
HY_UNROLL_OUTER = 16
HY_UNROLL_INNER = 8


def _hyena_filters(n, params, reverse):
    w1, b1, w2, b2, w3, freq1, freq2, decay = (p.astype(F32) for p in params)
    pos = jnp.arange(n, dtype=F32)
    t = jnp.linspace(0.0, 1.0, n, dtype=F32)
    if reverse:
        pos, t = (n - 1) - pos, jnp.linspace(1.0, 0.0, n, dtype=F32)
    t = t[:, None]
    bands = jnp.linspace(1e-4, HY_POS_BANDS - 1, HY_POS_BANDS, dtype=F32)
    ang = (2.0 * math.pi / n) * pos[:, None] * bands[None]
    z = jnp.concatenate([t, jnp.cos(ang), -jnp.sin(ang)], axis=-1)
    hid = jnp.sin(freq1 * (z @ w1 + b1))
    hid = jnp.sin(freq2 * (hid @ w2 + b2))
    h = (hid @ w3) * jnp.exp(-t * jnp.abs(decay))
    return h.reshape(n, HY_N_FILT, -1)


def _hyena_kernels(n, params):
    h = _hyena_filters(n, params, False)
    h_rev = _hyena_filters(n, params, True)

    def circular(h_fwd, h_bwd_rev):
        k = jnp.concatenate([h_fwd[:1] + h_bwd_rev[n - 1:], h_fwd[1:], jnp.zeros_like(h_fwd[:1]), h_bwd_rev[:n - 1]],
                            axis=0)
        return k * lax.rsqrt(jnp.sum(k * k, axis=0, keepdims=True) + EPS)

    return circular(h[:, 0], h_rev[:, 1]), circular(h[:, 2], h_rev[:, 3])


def _spectrum_kernel(na, k_ref, f1_ref, f2_ref, o_ref, a2):
    nb, ap = HY_NB, HY_APITCH

    def stage1(b, c):
        xs = k_ref[0, pl.ds(b, na, stride=nb), :].astype(BF16)
        r = jnp.dot(f1_ref[b], xs, preferred_element_type=F32)
        a2[pl.ds(b, na, stride=ap), :] = r[:na]
        a2[pl.ds(nb + b, na, stride=ap), :] = r[na:]
        return c

    def stage2(ka, c):
        rows = pl.ds(pl.multiple_of(ka * ap, 8), 2 * nb)
        xf = jnp.dot(f2_ref[...], a2[rows, :].astype(BF16), preferred_element_type=F32)
        o_ref[0, ka, 0] = xf[:nb].astype(o_ref.dtype)
        o_ref[0, ka, 1] = xf[nb:].astype(o_ref.dtype)
        return c

    lax.fori_loop(0, nb, stage1, 0, unroll=HY_UNROLL_INNER)
    lax.fori_loop(0, na, stage2, 0, unroll=HY_UNROLL_INNER)


def _hyena_spectra(kernels):
    nf, n, ch = kernels.shape
    nb, ct = HY_NB, HY_CT
    na = n // nb
    _, f2, _, _, f1_full = _dft_tables(n // 2)
    return pl.pallas_call(
        functools.partial(_spectrum_kernel, na),
        grid=(nf, ch // ct),
        in_specs=[pl.BlockSpec((1, n, ct), lambda f, j: (f, 0, j)), _const_spec(f1_full.shape), _const_spec(f2.shape)],
        out_specs=pl.BlockSpec((1, na, 2, nb, ct), lambda f, j: (f, 0, 0, 0, j)),
        out_shape=jax.ShapeDtypeStruct((nf, na, 2, nb, ch), BF16),
        scratch_shapes=[pltpu.VMEM((na * HY_APITCH, ct), F32)],
        compiler_params=_params(2),
        name="hyena_spectrum",
    )(kernels, f1_full, f2)


def _cplx_block(m):
    return np.block([[m.real, -m.imag], [m.imag, m.real]])


@functools.lru_cache(maxsize=None)
def _dft_tables(n_seq):
    nb = HY_NB
    n = 2 * n_seq
    na = n // nb
    ha = na // 2
    ka = np.arange(na)[:, None]
    b = np.arange(nb)
    tw = np.exp(-2j * np.pi * ka * b[None, :] / n)
    w_a = np.exp(-2j * np.pi * ka * np.arange(ha)[None, :] / na)
    f1 = np.stack([_cplx_block(tw[:, i:i + 1] * w_a) for i in range(nb)])
    g = np.stack([_cplx_block((np.conj(tw[:, i:i + 1] * w_a)).T / n) for i in range(nb)])
    w_b = np.exp(-2j * np.pi * b[:, None] * b[None, :] / nb)
    f2 = _cplx_block(w_b)
    f2i = _cplx_block(np.conj(w_b))
    w_full = np.exp(-2j * np.pi * ka * np.arange(na)[None, :] / na)
    f1_full = np.stack([np.concatenate([(tw[:, i:i + 1] * w_full).real, (tw[:, i:i + 1] * w_full).imag], axis=0)
                        for i in range(nb)])
    return tuple(jnp.asarray(t, dtype=BF16) for t in (f1, f2, f2i, g, f1_full))


def _hyena_kernel(na, x1_ref, x2_ref, v_ref, cw_ref, bias_ref, kf1_ref, kf2_ref, f1_ref, f2_ref, f2i_ref, g_ref,
                  o_ref, rawf, vp, gp, yp, a2):
    nb, xp, ap = HY_NB, HY_XPITCH, HY_APITCH
    ha = na // 2
    n_seq = ha * nb
    ct = o_ref.shape[-1]
    rawf[:, 0:8, :] = jnp.zeros((2, 8, ct), F32)
    rawf[:, 8 + n_seq:16 + n_seq, :] = jnp.zeros((2, 8, ct), F32)

    def short_conv_into(raw_ref, which, dst):
        w0, w1, w2 = (cw_ref[j, which:which + 1, :] for j in range(HY_SHORT))
        for e in range(2):
            rawf[e, 8:8 + n_seq, :] = raw_ref[e].astype(F32)

            def body(a, c):
                win = rawf[e, pl.ds(pl.multiple_of(a * nb, nb), nb + 16), :]
                u = w0 * win[7:7 + nb] + w1 * win[8:8 + nb] + w2 * win[9:9 + nb]
                dst[e, pl.ds(pl.multiple_of(a * xp, 8), nb), :] = u
                return c

            lax.fori_loop(0, ha, body, 0)

    def long_conv(kf_ref):
        def stage1(b, c):
            xs = jnp.concatenate([vp[0, pl.ds(b, ha, stride=xp), :], vp[1, pl.ds(b, ha, stride=xp), :]], axis=0)
            r = jnp.dot(f1_ref[b], xs.astype(BF16), preferred_element_type=F32)
            a2[pl.ds(b, na, stride=ap), :] = r[:na]
            a2[pl.ds(nb + b, na, stride=ap), :] = r[na:]
            return c

        def stage2(kp, c):
            rows = [pl.ds(pl.multiple_of((2 * kp + e) * ap, 8), 2 * nb) for e in range(2)]
            slab = jnp.concatenate([a2[r, :] for r in rows], axis=1).astype(BF16)
            xf = jnp.dot(f2_ref[...], slab, preferred_element_type=F32)
            xr, xi = xf[:nb], xf[nb:]
            kr = jnp.concatenate([kf_ref[0, 2 * kp + e, 0] for e in range(2)], axis=1).astype(F32)
            ki = jnp.concatenate([kf_ref[0, 2 * kp + e, 1] for e in range(2)], axis=1).astype(F32)
            z = jnp.concatenate([xr * kr - xi * ki, xr * ki + xi * kr], axis=0).astype(BF16)
            back = jnp.dot(f2i_ref[...], z, preferred_element_type=F32)
            for e in range(2):
                a2[rows[e], :] = back[:, e * ct:(e + 1) * ct]
            return c

        def stage3(b, c):
            s = jnp.concatenate([a2[pl.ds(b, na, stride=ap), :], a2[pl.ds(nb + b, na, stride=ap), :]], axis=0)
            y = jnp.dot(g_ref[b], s.astype(BF16), preferred_element_type=F32)
            yp[0, pl.ds(b, ha, stride=xp), :] = y[:ha]
            yp[1, pl.ds(b, ha, stride=xp), :] = y[ha:]
            return c

        lax.fori_loop(0, nb, stage1, 0, unroll=HY_UNROLL_OUTER)
        lax.fori_loop(0, na // 2, stage2, 0, unroll=HY_UNROLL_INNER)
        lax.fori_loop(0, nb, stage3, 0, unroll=HY_UNROLL_OUTER)

    def gated(bias_row, write):
        for e in range(2):
            def body(a, c):
                rows = pl.ds(pl.multiple_of(a * xp, 8), nb)
                write(e, a, rows, gp[e, rows, :] * (yp[e, rows, :] + bias_row * vp[e, rows, :]))
                return c

            lax.fori_loop(0, ha, body, 0)

    def write_z(e, a, rows, val):
        vp[e, rows, :] = val

    def write_out(e, a, rows, val):
        o_ref[e, pl.ds(pl.multiple_of(a * nb, nb), nb), :] = val.astype(o_ref.dtype)

    short_conv_into(v_ref, 2, vp)
    short_conv_into(x1_ref, 0, gp)
    long_conv(kf1_ref)
    gated(bias_ref[0:1, :], write_z)
    short_conv_into(x2_ref, 1, gp)
    long_conv(kf2_ref)
    gated(bias_ref[1:2, :], write_out)


def _hyena_latent(hy_all, n_seq, conv_w, bias, kf):
    bn, t_all = hy_all.shape[:2]
    ch = hy_all.shape[-1] // 3
    ct = HY_CT
    nct = ch // ct
    nb = HY_NB
    na = 2 * n_seq // nb
    assert bn % 2 == 0 and ch % ct == 0 and n_seq % (8 * nb) == 0
    f1, f2, f2i, g, _ = _dft_tables(n_seq)
    col = lambda which: pl.BlockSpec((2, n_seq, ct), lambda j, p: (p, 0, which * nct + j), pipeline_mode=pl.Buffered(1))
    kf_spec = lambda f: pl.BlockSpec((1, na, 2, nb, ct), lambda j, p: (f, 0, 0, 0, j), pipeline_mode=pl.Buffered(1))
    pad_rows = (na // 2) * HY_XPITCH
    return pl.pallas_call(
        functools.partial(_hyena_kernel, na),
        grid=(nct, bn // 2),
        in_specs=[col(0), col(1), col(2),
                  pl.BlockSpec((HY_SHORT, 3, ct), lambda j, p: (0, 0, j)),
                  pl.BlockSpec((2, ct), lambda j, p: (0, j)),
                  kf_spec(0), kf_spec(1),
                  _const_spec(f1.shape), _const_spec(f2.shape), _const_spec(f2i.shape), _const_spec(g.shape)],
        out_specs=pl.BlockSpec((2, n_seq, ct), lambda j, p: (p, 0, j)),
        out_shape=jax.ShapeDtypeStruct((bn, t_all, ch), BF16),
        scratch_shapes=[pltpu.VMEM((2, n_seq + 16, ct), F32), pltpu.VMEM((2, pad_rows, ct), F32),
                        pltpu.VMEM((2, pad_rows, ct), F32), pltpu.VMEM((2, pad_rows, ct), F32),
                        pltpu.VMEM((na * HY_APITCH, ct), F32)],
        compiler_params=_params(2),
        name="hyena_latent",
    )(hy_all, hy_all, hy_all, conv_w.reshape(HY_SHORT, 3, ch), bias, kf, kf, f1, f2, f2i, g)


@functools.lru_cache(maxsize=None)
def _dense_dft_tables(n_seq):
    n = 2 * n_seq
    ang = 2.0 * np.pi * np.arange(n)[:, None] * np.arange(n_seq)[None, :] / n
    fd = np.concatenate([np.cos(ang), -np.sin(ang)], axis=0)
    gd = np.concatenate([np.cos(ang).T, -np.sin(ang).T], axis=1) / n
    return jnp.asarray(fd, dtype=BF16), jnp.asarray(gd, dtype=BF16)


def _hyena_ctx_kernel(n_seq, x1_ref, x2_ref, v_ref, cw_ref, bias_ref, kf1_ref, kf2_ref, fd_ref, gd_ref, buf_ref,
                      o_ref):
    del buf_ref
    n = 2 * n_seq

    def short_conv(raw_ref, which):
        x = raw_ref[0].astype(F32)
        zero = jnp.zeros((1, x.shape[-1]), F32)
        prev = jnp.concatenate([zero, x[:-1]], axis=0)
        nxt = jnp.concatenate([x[1:], zero], axis=0)
        return cw_ref[0, which:which + 1, :] * prev + cw_ref[1, which:which + 1, :] * x + cw_ref[2, which:which + 1, :] * nxt

    def long_conv(u, kf_ref, bias_row):
        xf = jnp.dot(fd_ref[...], u.astype(BF16), preferred_element_type=F32)
        xr, xi = xf[:n], xf[n:]
        kr, ki = kf_ref[0], kf_ref[1]
        z = jnp.concatenate([xr * kr - xi * ki, xr * ki + xi * kr], axis=0).astype(BF16)
        return jnp.dot(gd_ref[...], z, preferred_element_type=F32) + bias_row * u

    x1, x2, v = short_conv(x1_ref, 0), short_conv(x2_ref, 1), short_conv(v_ref, 2)
    z = x1 * long_conv(v, kf1_ref, bias_ref[0:1, :])
    o_ref[0] = (x2 * long_conv(z, kf2_ref, bias_ref[1:2, :])).astype(o_ref.dtype)


def _hyena_context(hy_all, y_buf, row_block, n_seq, conv_w, bias, kf1, kf2):
    bn = hy_all.shape[0]
    ch = hy_all.shape[-1] // 3
    fd, gd = _dense_dft_tables(n_seq)
    col = lambda which: pl.BlockSpec((1, n_seq, ch), lambda i: (i, row_block, which))
    split = lambda kf: jnp.stack([kf.real, kf.imag], axis=0)
    return pl.pallas_call(
        functools.partial(_hyena_ctx_kernel, n_seq),
        grid=(bn,),
        in_specs=[col(0), col(1), col(2), _const_spec((HY_SHORT, 3, ch)), _const_spec((2, ch)),
                  _const_spec((2, 2 * n_seq, ch)), _const_spec((2, 2 * n_seq, ch)),
                  _const_spec(fd.shape), _const_spec(gd.shape), pl.BlockSpec(memory_space=pl.ANY)],
        out_specs=pl.BlockSpec((1, n_seq, ch), lambda i: (i, row_block, 0)),
        out_shape=jax.ShapeDtypeStruct(y_buf.shape, y_buf.dtype),
        input_output_aliases={9: 0},
        compiler_params=_params(1),
        name="hyena_context",
    )(hy_all, hy_all, hy_all, conv_w.reshape(HY_SHORT, 3, ch), bias, split(kf1), split(kf2), fd, gd, y_buf)


def kernel(x, c, ctx, c_ctx, w_mod, b_mod, g_norm1, g_norm2, w_in, hy_conv, hy_w1, hy_b1, hy_w2, hy_b2, hy_w3,
           hy_freq1, hy_freq2, hy_decay, hy_bias, gla_wa2, gla_ba, gla_gnorm, s5_a_re, s5_a_im, s5_log_dt, s5_b_re,
           s5_b_im, s5_c_re, s5_c_im, s5_d, s5_w_glu, s5_b_glu, w_br_hy, w_br_gla, w_br_s5, w_out, w_ffn_in,
           w_ffn_out, g_final):
    bn, n_lat, d = x.shape
    n_ctx = ctx.shape[1]
    depth = w_in.shape[0]
    d_hy = w_br_hy.shape[1]
    s5_ch = w_br_s5.shape[1]
    d_ff = w_ffn_out.shape[1]
    assert n_ctx == ROW_TILE and n_lat % ROW_TILE == 0
    n_lat_tiles = n_lat // ROW_TILE
    n_low = 2 * GLA_LOWRANK
    o_a = GLA_QK + GLA_V
    o_u = o_a + n_low
    o_gate = o_u + s5_ch + GLA_QK + GLA_V + 3 * d_hy
    col_sizes = (GLA_QK, GLA_V, A_PAD, s5_ch, GLA_QK, GLA_V, 3 * d_hy)

    cond = jnp.concatenate([c_ctx[None], c], axis=0)
    cond = jnp.pad(cond, ((0, (-cond.shape[0]) % 8), (0, 0)))
    mod_all = _modulation_all(cond, w_mod, b_mod)

    xc = jnp.concatenate([x, ctx], axis=1)
    out = None
    for l in range(depth):
        last = l == depth - 1
        mods = jnp.stack([jnp.broadcast_to(mod_all[l, 0], (bn, 6 * d)), mod_all[l, 1:1 + bn]], axis=1)
        mods = mods[:, :, None, :]
        wl = w_in[l]
        w_proj = jnp.concatenate(
            [wl[:, :o_u], jnp.zeros((d, A_PAD - n_low), F32), wl[:, o_u:o_gate]], axis=1).astype(BF16)
        g1 = g_norm1[l][None]
        w_u_t = wl[:, o_u:o_u + s5_ch].T.astype(BF16)
        k_a, v_a, a_a, u_a, q_a, r_a, hy_a, u_t = _project(xc, mods, g1, w_proj, col_sizes, w_u_t, n_lat_tiles)

        wa = jnp.zeros((A_PAD, 2 * GLA_QK), F32)
        wa = wa.at[:GLA_LOWRANK, :GLA_QK].set(gla_wa2[l, 0]).at[GLA_LOWRANK:n_low, GLA_QK:].set(gla_wa2[l, 1])
        y_gla = _gla_mix(k_a, v_a, a_a, q_a, r_a, wa.astype(BF16), gla_ba[l], gla_gnorm[l][None], n_ctx)

        s5w = _s5_weights(s5_a_re[l], s5_a_im[l], s5_log_dt[l], s5_b_re[l], s5_b_im[l], s5_c_re[l], s5_c_im[l],
                          S5_CHUNK)
        ys = _s5_mix(u_t, n_ctx, s5w)

        hy_p = (hy_w1[l], hy_b1[l], hy_w2[l], hy_b2[l], hy_w3[l], hy_freq1[l], hy_freq2[l], hy_decay[l])
        y_hy = _hyena_latent(hy_a, n_lat, hy_conv[l], hy_bias[l],
                             _hyena_spectra(jnp.stack(_hyena_kernels(n_lat, hy_p))))
        if not last:
            y_hy = _hyena_context(hy_a, y_hy, n_lat // n_ctx, n_ctx, hy_conv[l], hy_bias[l],
                                  *(jnp.fft.fft(k, axis=0) for k in _hyena_kernels(n_ctx, hy_p)))

        wts = (wl[:, o_gate:].astype(BF16), w_br_hy[l].astype(BF16), w_br_gla[l].astype(BF16),
               w_br_s5[l].astype(BF16), w_out[l].astype(BF16), s5_d[l][None], s5_w_glu[l].astype(BF16),
               s5_b_glu[l][None])
        n_tiles = n_lat_tiles if last else n_lat_tiles + n_ctx // ROW_TILE
        x_mid = _merge(xc, mods, g1, y_hy, y_gla, ys, u_a, wts, n_tiles, n_lat_tiles)
        wf = w_ffn_in[l]
        xc = _ffn(x_mid, mods, g_norm2[l][None], wf[:, :d_ff].astype(BF16), wf[:, d_ff:].astype(BF16),
                  w_ffn_out[l].astype(BF16), g_final[None], n_lat_tiles, last)
    return xc
```

```python
import functools
import math

import jax
import jax.numpy as jnp
import numpy as np
from jax import lax
from jax.experimental import pallas as pl
from jax.experimental.pallas import tpu as pltpu

F32 = jnp.float32
BF16 = jnp.bfloat16
EPS = 1e-6

HY_SHORT = 3
HY_POS_BANDS = 16
HY_N_FILT = 4
GLA_HEADS = 4
GLA_DK = 64
GLA_DV = 128
GLA_QK = GLA_HEADS * GLA_DK
GLA_V = GLA_HEADS * GLA_DV
GLA_LOWRANK = 16
GLA_TAU = 16.0
GLA_CHUNK = 64
S5_STATE = 64

LANES = 128
SUBLANES = 8
VMEM_LIMIT = 56 * 1024 * 1024

GLA_BLOCK = 4
A_PAD = LANES
ROW_TILE = 256


def _const_spec(shape):
    nd = len(shape)
    return pl.BlockSpec(shape, lambda *_: (0,) * nd, pipeline_mode=pl.Buffered(1))


def _params(n_axes):
    return pltpu.CompilerParams(dimension_semantics=("parallel",) * n_axes, vmem_limit_bytes=VMEM_LIMIT)


def _mod_kernel(c_ref, w_ref, b_ref, o_ref):
    c = c_ref[...]
    s = c * jax.nn.sigmoid(c)
    o_ref[0] = jnp.dot(s.astype(BF16), w_ref[0], preferred_element_type=F32) + b_ref[0]


def _modulation_all(cond, w_mod, b_mod):
    depth, d, n = w_mod.shape
    r = cond.shape[0]
    tn = 1536
    return pl.pallas_call(
        _mod_kernel,
        grid=(depth, n // tn),
        in_specs=[
            pl.BlockSpec((r, d), lambda l, j: (0, 0)),
            pl.BlockSpec((1, d, tn), lambda l, j: (l, 0, j)),
            pl.BlockSpec((1, 1, tn), lambda l, j: (l, 0, j)),
        ],
        out_specs=pl.BlockSpec((1, r, tn), lambda l, j: (l, 0, j)),
        out_shape=jax.ShapeDtypeStruct((depth, r, n), F32),
        compiler_params=_params(2),
        name="modulation",
    )(cond, w_mod.astype(BF16), b_mod.reshape(depth, 1, n))


def _norm_mod(x, g, shift, scale):
    y = x * lax.rsqrt(jnp.mean(x * x, axis=-1, keepdims=True) + EPS)
    return (y * g) * (1.0 + scale) + shift


_NT = (((1,), (1,)), ((), ()))


def _proj_kernel(col_sizes, d, x_ref, mod_ref, g_ref, w_ref, wt_ref, *o_refs):
    m = mod_ref[0, 0]
    h = _norm_mod(x_ref[0], g_ref[...], m[:, 0:d], m[:, d:2 * d]).astype(BF16)
    off = 0
    for o_ref, n in zip(o_refs[:-1], col_sizes):
        o_ref[0] = jnp.dot(h, w_ref[:, off:off + n], preferred_element_type=F32).astype(o_ref.dtype)
        off += n
    o_refs[-1][0] = lax.dot_general(wt_ref[...], h, _NT, preferred_element_type=F32)


def _mod_spec(d, n_lat_tiles):
    return pl.BlockSpec((1, 1, 1, 6 * d), lambda i, j: (i, (j < n_lat_tiles).astype(jnp.int32), 0, 0))


def _project(xc, mods, g, w, col_sizes, w_t, n_lat_tiles):
    b, t, d = xc.shape
    tm = ROW_TILE
    n_tot = sum(col_sizes)
    n_t = w_t.shape[0]
    out_shape = [jax.ShapeDtypeStruct((b, t, n), BF16) for n in col_sizes] + [jax.ShapeDtypeStruct((b, n_t, t), F32)]
    out_specs = [pl.BlockSpec((1, tm, n), lambda i, j: (i, j, 0)) for n in col_sizes]
    out_specs.append(pl.BlockSpec((1, n_t, tm), lambda i, j: (i, 0, j)))
    return pl.pallas_call(
        functools.partial(_proj_kernel, tuple(col_sizes), d),
        grid=(b, t // tm),
        in_specs=[
            pl.BlockSpec((1, tm, d), lambda i, j: (i, j, 0)),
            _mod_spec(d, n_lat_tiles),
            _const_spec((1, d)),
            _const_spec((d, n_tot)),
            _const_spec((n_t, d)),
        ],
        out_specs=out_specs,
        out_shape=out_shape,
        compiler_params=_params(2),
        name="in_proj",
    )(xc, mods, g, w, w_t)


def _gelu_tanh(x):
    return 0.5 * x * (1.0 + jnp.tanh(math.sqrt(2.0 / math.pi) * (x + 0.044715 * (x * x * x))))


def _merge_kernel(d, x_ref, mod_ref, g_ref, yhy_ref, ygla_ref, ys5_ref, u_ref, wg_ref, whb_ref, wgb_ref, wsb_ref,
                  wo_ref, s5d_ref, wglu_ref, bglu_ref, o_ref):
    x = x_ref[0]
    m = mod_ref[0, 0]
    h = _norm_mod(x, g_ref[...], m[:, 0:d], m[:, d:2 * d]).astype(BF16)
    y5 = (ys5_ref[0] + s5d_ref[...] * u_ref[0]).T
    g5 = _gelu_tanh(y5)
    y_s5 = g5 * jax.nn.sigmoid(jnp.dot(g5.astype(BF16), wglu_ref[...], preferred_element_type=F32) + bglu_ref[...])

    def branch(k, y, wb_ref):
        gate = jnp.dot(h, wg_ref[:, k * d:(k + 1) * d], preferred_element_type=F32)
        return jax.nn.sigmoid(gate) * jnp.dot(y, wb_ref[...], preferred_element_type=F32)

    mix = branch(0, yhy_ref[0], whb_ref) + branch(1, ygla_ref[0], wgb_ref) + branch(2, y_s5.astype(BF16), wsb_ref)
    out = jnp.dot(mix.astype(BF16), wo_ref[...], preferred_element_type=F32)
    o_ref[0] = x + m[:, 2 * d:3 * d] * out


def _merge(xc, mods, g, y_hy, y_gla, ys_s5, u_s5, wts, nj, n_lat_tiles):
    b, t, d = xc.shape
    tm = ROW_TILE
    ch = y_hy.shape[-1]
    row = lambda n: pl.BlockSpec((1, tm, n), lambda i, j: (i, j, 0))
    col = lambda n: pl.BlockSpec((1, n, tm), lambda i, j: (i, 0, j))
    w_gate, w_hy, w_gla, w_s5, w_out, s5_d, w_glu, b_glu = wts
    return pl.pallas_call(
        functools.partial(_merge_kernel, d),
        grid=(b, nj),
        in_specs=[
            row(d),
            _mod_spec(d, n_lat_tiles),
            _const_spec((1, d)),
            row(ch), row(ch), col(ch), col(ch),
            _const_spec(w_gate.shape), _const_spec(w_hy.shape), _const_spec(w_gla.shape), _const_spec(w_s5.shape),
            _const_spec(w_out.shape), _const_spec(s5_d.shape), _const_spec(w_glu.shape), _const_spec(b_glu.shape),
        ],
        out_specs=pl.BlockSpec((1, tm, d), lambda i, j: (i, j, 0)),
        out_shape=jax.ShapeDtypeStruct((b, nj * tm, d), F32),
        compiler_params=_params(2),
        name="merge",
    )(xc, mods, g, y_hy, y_gla, ys_s5, u_s5, w_gate, w_hy, w_gla, w_s5, w_out, s5_d, w_glu, b_glu)


def _ffn_kernel(d, final, x_ref, mod_ref, g_ref, wa_ref, wb_ref, wo_ref, gf_ref, o_ref):
    x = x_ref[0]
    m = mod_ref[0, 0]
    h = _norm_mod(x, g_ref[...], m[:, 3 * d:4 * d], m[:, 4 * d:5 * d]).astype(BF16)
    a = jnp.dot(h, wa_ref[...], preferred_element_type=F32)
    bb = jnp.dot(h, wb_ref[...], preferred_element_type=F32)
    act = (a * jax.nn.sigmoid(a) * bb).astype(BF16)
    y = x + m[:, 5 * d:6 * d] * jnp.dot(act, wo_ref[...], preferred_element_type=F32)
    if final:
        y = y * lax.rsqrt(jnp.mean(y * y, axis=-1, keepdims=True) + EPS) * gf_ref[...]
    o_ref[0] = y


def _ffn(x, mods, g, wa, wb, wo, g_final, n_lat_tiles, final):
    b, t, d = x.shape
    tm = ROW_TILE
    return pl.pallas_call(
        functools.partial(_ffn_kernel, d, final),
        grid=(b, t // tm),
        in_specs=[
            pl.BlockSpec((1, tm, d), lambda i, j: (i, j, 0)),
            _mod_spec(d, n_lat_tiles),
            _const_spec((1, d)),
            _const_spec(wa.shape), _const_spec(wb.shape), _const_spec(wo.shape),
            _const_spec((1, d)),
        ],
        out_specs=pl.BlockSpec((1, tm, d), lambda i, j: (i, j, 0)),
        out_shape=jax.ShapeDtypeStruct((b, t, d), F32),
        compiler_params=_params(2),
        name="ffn",
    )(x, mods, g, wa, wb, wo, g_final)


S5_CHUNK = LANES


def _s5_weights(a_re, a_im, log_dt, b_re, b_im, c_re, c_im, tc):
    g, p = a_re.shape[1:]
    i = b_re.shape[-1]
    lam_c = lax.complex(jnp.minimum(a_re, -1e-4), a_im)
    lam_dt = lam_c * jnp.exp(log_dt)[..., None]
    b_bar = ((jnp.exp(lam_dt) - 1.0) / lam_c)[..., None] * lax.complex(b_re, b_im)[None]
    c_mat = lax.complex(c_re, c_im)
    tau = jnp.arange(tc, dtype=F32)
    pw = jnp.exp(lam_dt[:, :, None, :] * tau[None, None, :, None])
    pw1 = pw * jnp.exp(lam_dt)[:, :, None, :]
    kern = jnp.einsum('gip,dgtp,dgpj->dgtij', c_mat, pw, b_bar).real
    k_lag = jnp.concatenate([kern[1][:, :0:-1], kern[0][:, :1] + kern[1][:, :1], kern[0][:, 1:]], axis=1)
    kr = jnp.pad(k_lag.transpose(0, 3, 2, 1), ((0, 0), (0, 0), (0, 0), (0, 1)))
    kr = kr.reshape(g, i, i * 2 * tc)

    inc_f = pw[0][:, None, ::-1, :] * b_bar[0].transpose(0, 2, 1)[:, :, None]
    inc_b = pw[1][:, None, :, :] * b_bar[1].transpose(0, 2, 1)[:, :, None]
    bm = jnp.concatenate([inc_f.real, inc_b.real, inc_f.imag, inc_b.imag], axis=-1).reshape(g, tc * i, 4 * p)

    out_f = c_mat.transpose(0, 2, 1)[..., None] * pw1[0].transpose(0, 2, 1)[:, :, None, :]
    out_b = c_mat.transpose(0, 2, 1)[..., None] * pw1[1][:, ::-1].transpose(0, 2, 1)[:, :, None, :]
    z = jnp.zeros((g, p, tc * i), F32)
    fl = lambda t: t.reshape(g, p, tc * i)
    cm = jnp.concatenate([fl(out_f.real), z, fl(-out_f.imag), z, z, fl(out_b.real), z, fl(-out_b.imag)], axis=1)

    lam_t = jnp.exp(lam_dt * float(tc))
    lam = jnp.stack([jnp.concatenate([lam_t[0].real, lam_t[1].real], -1),
                     jnp.concatenate([lam_t[0].imag, lam_t[1].imag], -1)], axis=1)
    return kr, bm.astype(BF16), cm.astype(BF16), lam


def _s5_kernel(nc, nc_ctx, tc, u_ref, kr_ref, bm_ref, cm_ref, lam_ref, y_ref, up_ref, dx_ref, p_ref, m_ref):
    bn, i_sz = u_ref.shape[:2]
    nl = nc - nc_ctx
    width = kr_ref.shape[-1]
    seg = width // i_sz
    for j in range(i_sz):
        lag_rows = pltpu.roll(jnp.broadcast_to(kr_ref[0, j:j + 1, :], (tc, width)), width - (tc - 1), axis=1,
                              stride=1, stride_axis=0)
        m_ref[j * tc:(j + 1) * tc, :] = jnp.concatenate(
            [lag_rows[:, i * seg:i * seg + tc] for i in range(i_sz)], axis=1).astype(BF16)
    for n in range(nc):
        for i in range(i_sz):
            up_ref[n * bn:(n + 1) * bn, i * tc:(i + 1) * tc] = u_ref[:, i, n * tc:(n + 1) * tc]
    u = up_ref[...].astype(BF16)
    dx_ref[...] = jnp.dot(u, bm_ref[0], preferred_element_type=F32)
    lam = lam_ref[0]
    lr, li = lam[0:1], lam[1:2]
    w = lam.shape[-1]
    half = w // 2
    is_f = lax.broadcasted_iota(jnp.int32, (bn, w), 1) < half

    def step(s, carry):
        sr, si = carry
        nf = jnp.where(s < nc_ctx, nl + s, s - nc_ctx)
        nb = nc - 1 - s
        rf = pl.multiple_of(nf * bn, bn)
        rb = pl.multiple_of(nb * bn, bn)
        p_ref[pl.ds(rf, bn), 0:w] = sr
        p_ref[pl.ds(rf, bn), w:2 * w] = si
        p_ref[pl.ds(rb, bn), 2 * w:3 * w] = sr
        p_ref[pl.ds(rb, bn), 3 * w:4 * w] = si
        d_re = jnp.where(is_f, dx_ref[pl.ds(rf, bn), 0:w], dx_ref[pl.ds(rb, bn), 0:w])
        d_im = jnp.where(is_f, dx_ref[pl.ds(rf, bn), w:2 * w], dx_ref[pl.ds(rb, bn), w:2 * w])
        return lr * sr - li * si + d_re, lr * si + li * sr + d_im

    zero = jnp.zeros((bn, 2 * half), F32)
    lax.fori_loop(0, nc, step, (zero, zero))
    y = jnp.dot(u, m_ref[...], preferred_element_type=F32)
    y = y + jnp.dot(p_ref[...].astype(BF16), cm_ref[0], preferred_element_type=F32)
    for n in range(nc):
        for i in range(i_sz):
            y_ref[:, i, n * tc:(n + 1) * tc] = y[n * bn:(n + 1) * bn, i * tc:(i + 1) * tc]


def _s5_mix(u_t, n_ctx, wts):
    kr, bm, cm, lam = wts
    bn, ch, t = u_t.shape
    g, i = kr.shape[:2]
    tc = kr.shape[2] // (2 * i)
    assert 2 * S5_STATE == LANES and bn % SUBLANES == 0 and n_ctx % tc == 0 and t % tc == 0 and tc % LANES == 0
    nc, nc_ctx = t // tc, n_ctx // tc
    k = tc * i
    r = nc * bn
    return pl.pallas_call(
        functools.partial(_s5_kernel, nc, nc_ctx, tc),
        grid=(g,),
        in_specs=[
            pl.BlockSpec((bn, i, t), lambda j: (0, j, 0)),
            pl.BlockSpec((1, i, 2 * k), lambda j: (j, 0, 0)),
            pl.BlockSpec((1, k, 4 * S5_STATE), lambda j: (j, 0, 0)),
            pl.BlockSpec((1, 8 * S5_STATE, k), lambda j: (j, 0, 0)),
            pl.BlockSpec((1, 2, 2 * S5_STATE), lambda j: (j, 0, 0)),
        ],
        out_specs=pl.BlockSpec((bn, i, t), lambda j: (0, j, 0)),
        out_shape=jax.ShapeDtypeStruct((bn, ch, t), F32),
        scratch_shapes=[pltpu.VMEM((r, k), F32), pltpu.VMEM((r, 4 * S5_STATE), F32),
                        pltpu.VMEM((r, 8 * S5_STATE), F32), pltpu.VMEM((k, k), BF16)],
        compiler_params=_params(1),
        name="s5_mix",
    )(u_t, kr, bm, cm, lam)


def _gla_kernel(nc, nc_ctx, k_ref, v_ref, a_ref, q_ref, r_ref, wa_ref, ba_ref, gn_ref, y_ref, of_ref, ob_ref,
                s_ref):
    c, h_n, dk, dv = GLA_CHUNK, GLA_HEADS, GLA_DK, GLA_DV
    qk = h_n * dk
    q_scale = dk ** -0.5
    nbk = GLA_BLOCK
    rb = nbk * c
    row_i = lax.broadcasted_iota(jnp.int32, (rb, rb), 0)
    col_i = lax.broadcasted_iota(jnp.int32, (rb, rb), 1)
    same = (row_i // c) == (col_i // c)
    causal = (same & (row_i >= col_i), same & (row_i <= col_i))
    cum_ops = [jnp.concatenate([m.astype(BF16), same.astype(BF16)], axis=0) for m in causal]
    att_mask = [jnp.concatenate([m] * h_n, axis=0) for m in causal]
    lane_head = lax.broadcasted_iota(jnp.int32, (1, qk), 1) // dk
    head_lanes = [lane_head == h for h in range(h_n)]

    def by_head(x):
        zero = jnp.zeros_like(x)
        return jnp.concatenate([jnp.where(m, x, zero) for m in head_lanes], axis=0)

    def block(j, d):
        rows = pl.ds(pl.multiple_of(j * rb, rb), rb)
        k = k_ref[0, rows, :].astype(F32)
        q = q_ref[0, rows, :].astype(F32) * q_scale
        v = v_ref[0, rows, :]
        z = jnp.dot(a_ref[0, rows, :], wa_ref[:, d * qk:(d + 1) * qk], preferred_element_type=F32) + ba_ref[d:d + 1, :]
        log_a = (jnp.minimum(z, 0.0) - jnp.log(1.0 + jnp.exp(-jnp.abs(z)))) * (1.0 / GLA_TAU)
        hi = log_a.astype(BF16)
        lo = (log_a - hi.astype(F32)).astype(BF16)
        cs = jnp.dot(cum_ops[d], jnp.concatenate([hi, lo], axis=1), preferred_element_type=F32)
        cs = cs[:, :qk] + cs[:, qk:]
        b, b_tot = cs[:rb], cs[rb:]
        qd = (q * jnp.exp(b)).astype(BF16)
        kd = (k * jnp.exp(-b)).astype(BF16)
        kl = (k * jnp.exp(b_tot - b)).astype(BF16)
        qm = by_head(qd)
        intra = []
        for h in range(h_n):
            att = lax.dot_general(qm[h * rb:(h + 1) * rb], kd, _NT, preferred_element_type=F32)
            att = jnp.where(causal[d], att, 0.0).astype(BF16)
            intra.append(jnp.dot(att, v[:, h * dv:(h + 1) * dv], preferred_element_type=F32))
        s_t = s_ref[d]
        inter = [None] * nbk
        for n in (range(nbk) if d == 0 else reversed(range(nbk))):
            r0 = n * c
            qm_n = jnp.concatenate([qm[h * rb + r0:h * rb + r0 + c] for h in range(h_n)], axis=0)
            inter[n] = lax.dot_general(qm_n, s_t.astype(BF16), _NT, preferred_element_type=F32)
            v_t = jnp.concatenate(
                [jnp.concatenate([v[r0:r0 + c, h * dv:(h + 1) * dv] for h in range(p, p + dv // c)], axis=0).T
                 for p in range(0, h_n, dv // c)], axis=1)
            s_t = s_t * jnp.exp(b_tot[r0:r0 + 1]) + jnp.dot(v_t, by_head(kl[r0:r0 + c]), preferred_element_type=F32)
        s_ref[d] = s_t
        o = jnp.concatenate(
            [intra[h] + jnp.concatenate([inter[n][h * c:(h + 1) * c] for n in range(nbk)], axis=0)
             for h in range(h_n)], axis=1)
        return rows, o

    def scan_step(i, carry, lo_block, hi_block):
        rows_f, o_f = block(lo_block + i, 0)
        rows_b, o_b = block(hi_block - 1 - i, 1)
        of_ref[rows_f, :] = o_f
        ob_ref[rows_b, :] = o_b
        return carry

    def readout(n, carry):
        rows = pl.ds(pl.multiple_of(n * c, c), c)
        o = of_ref[rows, :] + ob_ref[rows, :]
        r = r_ref[0, rows, :].astype(F32)
        gate = r * jax.nn.sigmoid(r)
        outs = []
        for h in range(h_n):
            oh = o[:, h * dv:(h + 1) * dv]
            oh = oh * lax.rsqrt(jnp.mean(oh * oh, axis=-1, keepdims=True) + EPS) * gn_ref[...]
            outs.append(oh * gate[:, h * dv:(h + 1) * dv])
        y_ref[0, rows, :] = jnp.concatenate(outs, axis=1).astype(y_ref.dtype)
        return carry

    nl, n_ctx = (nc - nc_ctx) // nbk, nc_ctx // nbk
    s_ref[...] = jnp.zeros(s_ref.shape, F32)
    lax.fori_loop(0, n_ctx, functools.partial(scan_step, lo_block=nl, hi_block=nl + n_ctx), 0)
    lax.fori_loop(0, nl, functools.partial(scan_step, lo_block=0, hi_block=nl), 0)
    lax.fori_loop(0, nc, readout, 0, unroll=2)


def _gla_mix(k, v, a, q, r, wa, ba, gnorm, n_ctx):
    bn, t, qk = k.shape
    vd = v.shape[-1]
    c = GLA_CHUNK
    assert t % (c * GLA_BLOCK) == 0 and n_ctx % (c * GLA_BLOCK) == 0 and GLA_DV % c == 0
    seq = lambda n: pl.BlockSpec((1, t, n), lambda i: (i, 0, 0), pipeline_mode=pl.Buffered(1))
    return pl.pallas_call(
        functools.partial(_gla_kernel, t // c, n_ctx // c),
        grid=(bn,),
        in_specs=[seq(qk), seq(vd), seq(a.shape[-1]), seq(qk), seq(vd),
                  _const_spec(wa.shape), _const_spec(ba.shape), _const_spec(gnorm.shape)],
        out_specs=pl.BlockSpec((1, t, vd), lambda i: (i, 0, 0)),
        out_shape=jax.ShapeDtypeStruct((bn, t, vd), BF16),
        scratch_shapes=[pltpu.VMEM((t, vd), F32), pltpu.VMEM((t, vd), F32), pltpu.VMEM((2, GLA_DV, qk), F32)],
        compiler_params=_params(1),
        name="gla_mix",
    )(k, v, a, q, r, wa, ba, gnorm)


HY_NB = LANES
HY_XPITCH = HY_NB + SUBLANES
HY_APITCH = 2 * HY_NB + SUBLANES
HY_CT = LANES
HY_UNROLL_OUTER = 16
HY_UNROLL_INNER = 8


def _hyena_filters(n, params, reverse):
    w1, b1, w2, b2, w3, freq1, freq2, decay = (p.astype(F32) for p in params)
    pos = jnp.arange(n, dtype=F32)
    t = jnp.linspace(0.0, 1.0, n, dtype=F32)
    if reverse:
        pos, t = (n - 1) - pos, jnp.linspace(1.0, 0.0, n, dtype=F32)
    t = t[:, None]
    bands = jnp.linspace(1e-4, HY_POS_BANDS - 1, HY_POS_BANDS, dtype=F32)
    ang = (2.0 * math.pi / n) * pos[:, None] * bands[None]
    z = jnp.concatenate([t, jnp.cos(ang), -jnp.sin(ang)], axis=-1)
    hid = jnp.sin(freq1 * (z @ w1 + b1))
    hid = jnp.sin(freq2 * (hid @ w2 + b2))
    w3 = w3.reshape(w3.shape[0], HY_N_FILT, -1)
    window = jnp.exp(-t[None] * jnp.abs(decay).reshape(HY_N_FILT, 1, -1))
    return jnp.einsum('nk,kfc->fnc', hid, w3) * window


def _hyena_kernels(n, params):
    h_fwd = _hyena_filters(n, params, False)[0::2]
    h_bwd_rev = _hyena_filters(n, params, True)[1::2]
    k = jnp.concatenate([h_fwd[:, :1] + h_bwd_rev[:, n - 1:], h_fwd[:, 1:], jnp.zeros_like(h_fwd[:, :1]),
                         h_bwd_rev[:, :n - 1]], axis=1)
    return k * lax.rsqrt(jnp.sum(k * k, axis=1, keepdims=True) + EPS)


def _spectrum_kernel(na, k_ref, f1_ref, f2_ref, o_ref, a2):
    nb, ap = HY_NB, HY_APITCH

    def stage1(b, c):
        xs = k_ref[0, pl.ds(b, na, stride=nb), :].astype(BF16)
        r = jnp.dot(f1_ref[b], xs, preferred_element_type=F32)
        a2[pl.ds(b, na, stride=ap), :] = r[:na]
        a2[pl.ds(nb + b, na, stride=ap), :] = r[na:]
        return c

    def stage2(ka, c):
        rows = pl.ds(pl.multiple_of(ka * ap, 8), 2 * nb)
        xf = jnp.dot(f2_ref[...], a2[rows, :].astype(BF16), preferred_element_type=F32)
        o_ref[0, ka, 0] = xf[:nb].astype(o_ref.dtype)
        o_ref[0, ka, 1] = xf[nb:].astype(o_ref.dtype)
        return c

    lax.fori_loop(0, nb, stage1, 0, unroll=HY_UNROLL_INNER)
    lax.fori_loop(0, na, stage2, 0, unroll=HY_UNROLL_INNER)


def _hyena_spectra(kernels):
    nf, n, ch = kernels.shape
    nb, ct = HY_NB, HY_CT
    na = n // nb
    _, f2, _, _, f1_full = _dft_tables(n // 2)
    return pl.pallas_call(
        functools.partial(_spectrum_kernel, na),
        grid=(nf, ch // ct),
        in_specs=[pl.BlockSpec((1, n, ct), lambda f, j: (f, 0, j)), _const_spec(f1_full.shape), _const_spec(f2.shape)],
        out_specs=pl.BlockSpec((1, na, 2, nb, ct), lambda f, j: (f, 0, 0, 0, j)),
        out_shape=jax.ShapeDtypeStruct((nf, na, 2, nb, ch), BF16),
        scratch_shapes=[pltpu.VMEM((na * HY_APITCH, ct), F32)],
        compiler_params=_params(2),
        name="hyena_spectrum",
    )(kernels, f1_full, f2)


def _cplx_block(m):
    return np.block([[m.real, -m.imag], [m.imag, m.real]])


@functools.lru_cache(maxsize=None)
def _dft_tables(n_seq):
    nb = HY_NB
    n = 2 * n_seq
    na = n // nb
    ha = na // 2
    ka = np.arange(na)[:, None]
    b = np.arange(nb)
    tw = np.exp(-2j * np.pi * ka * b[None, :] / n)
    w_a = np.exp(-2j * np.pi * ka * np.arange(ha)[None, :] / na)
    f1 = np.stack([_cplx_block(tw[:, i:i + 1] * w_a) for i in range(nb)])
    g = np.stack([_cplx_block((np.conj(tw[:, i:i + 1] * w_a)).T / n) for i in range(nb)])
    w_b = np.exp(-2j * np.pi * b[:, None] * b[None, :] / nb)
    f2 = _cplx_block(w_b)
    f2i = _cplx_block(np.conj(w_b))
    w_full = np.exp(-2j * np.pi * ka * np.arange(na)[None, :] / na)
    f1_full = np.stack([np.concatenate([(tw[:, i:i + 1] * w_full).real, (tw[:, i:i + 1] * w_full).imag], axis=0)
                        for i in range(nb)])
    return tuple(jnp.asarray(t, dtype=BF16) for t in (f1, f2, f2i, g, f1_full))


def _hyena_kernel(na, x1_ref, x2_ref, v_ref, cw_ref, bias_ref, kf1_ref, kf2_ref, f1_ref, f2_ref, f2i_ref, g_ref,
                  o_ref, rawf, vp, gp, yp, a2):
    nb, xp, ap = HY_NB, HY_XPITCH, HY_APITCH
    ha = na // 2
    n_seq = ha * nb
    ct = o_ref.shape[-1]
    halo = SUBLANES
    rawf[:, 0:halo, :] = jnp.zeros((2, halo, ct), F32)
    rawf[:, halo + n_seq:2 * halo + n_seq, :] = jnp.zeros((2, halo, ct), F32)

    def short_conv_into(raw_ref, which, dst):
        w0, w1, w2 = (cw_ref[j, which:which + 1, :] for j in range(HY_SHORT))
        for e in range(2):
            rawf[e, halo:halo + n_seq, :] = raw_ref[e].astype(F32)

            def body(a, c):
                base = pl.multiple_of(a * nb, nb) + halo
                u = (w0 * rawf[e, pl.ds(base - 1, nb), :] + w1 * rawf[e, pl.ds(base, nb), :]
                     + w2 * rawf[e, pl.ds(base + 1, nb), :])
                dst[e, pl.ds(pl.multiple_of(a * xp, 8), nb), :] = u
                return c

            lax.fori_loop(0, ha, body, 0)

    def long_conv(kf_ref):
        def stage1(b, c):
            xs = jnp.concatenate([vp[0, pl.ds(b, ha, stride=xp), :], vp[1, pl.ds(b, ha, stride=xp), :]], axis=0)
            r = jnp.dot(f1_ref[b], xs.astype(BF16), preferred_element_type=F32)
            a2[pl.ds(b, na, stride=ap), :] = r[:na]
            a2[pl.ds(nb + b, na, stride=ap), :] = r[na:]
            return c

        def stage2(kp, c):
            rows = [pl.ds(pl.multiple_of((2 * kp + e) * ap, 8), 2 * nb) for e in range(2)]
            slab = jnp.concatenate([a2[r, :] for r in rows], axis=1).astype(BF16)
            xf = jnp.dot(f2_ref[...], slab, preferred_element_type=F32)
            xr, xi = xf[:nb], xf[nb:]
            kr = jnp.concatenate([kf_ref[0, 2 * kp + e, 0] for e in range(2)], axis=1).astype(F32)
            ki = jnp.concatenate([kf_ref[0, 2 * kp + e, 1] for e in range(2)], axis=1).astype(F32)
            z = jnp.concatenate([xr * kr - xi * ki, xr * ki + xi * kr], axis=0).astype(BF16)
            back = jnp.dot(f2i_ref[...], z, preferred_element_type=F32)
            for e in range(2):
                a2[rows[e], :] = back[:, e * ct:(e + 1) * ct]
            return c

        def stage3(b, c):
            s = jnp.concatenate([a2[pl.ds(b, na, stride=ap), :], a2[pl.ds(nb + b, na, stride=ap), :]], axis=0)
            y = jnp.dot(g_ref[b], s.astype(BF16), preferred_element_type=F32)
            yp[0, pl.ds(b, ha, stride=xp), :] = y[:ha]
            yp[1, pl.ds(b, ha, stride=xp), :] = y[ha:]
            return c

        lax.fori_loop(0, nb, stage1, 0, unroll=HY_UNROLL_OUTER)
        lax.fori_loop(0, na // 2, stage2, 0, unroll=HY_UNROLL_INNER)
        lax.fori_loop(0, nb, stage3, 0, unroll=HY_UNROLL_OUTER)

    def gated(bias_row, write):
        for e in range(2):
            def body(a, c):
                rows = pl.ds(pl.multiple_of(a * xp, 8), nb)
                write(e, a, rows, gp[e, rows, :] * (yp[e, rows, :] + bias_row * vp[e, rows, :]))
                return c

            lax.fori_loop(0, ha, body, 0)

    def write_z(e, a, rows, val):
        vp[e, rows, :] = val

    def write_out(e, a, rows, val):
        o_ref[e, pl.ds(pl.multiple_of(a * nb, nb), nb), :] = val.astype(o_ref.dtype)

    short_conv_into(v_ref, 2, vp)
    short_conv_into(x1_ref, 0, gp)
    long_conv(kf1_ref)
    gated(bias_ref[0:1, :], write_z)
    short_conv_into(x2_ref, 1, gp)
    long_conv(kf2_ref)
    gated(bias_ref[1:2, :], write_out)


def _hyena_latent(hy_all, n_seq, conv_w, bias, kf):
    bn, t_all = hy_all.shape[:2]
    ch = hy_all.shape[-1] // 3
    ct = HY_CT
    nct = ch // ct
    nb = HY_NB
    na = 2 * n_seq // nb
    assert bn % 2 == 0 and ch % ct == 0 and n_seq % (8 * nb) == 0
    f1, f2, f2i, g, _ = _dft_tables(n_seq)
    col = lambda which: pl.BlockSpec((2, n_seq, ct), lambda j, p: (p, 0, which * nct + j), pipeline_mode=pl.Buffered(1))
    kf_spec = lambda f: pl.BlockSpec((1, na, 2, nb, ct), lambda j, p: (f, 0, 0, 0, j), pipeline_mode=pl.Buffered(1))
    pad_rows = (na // 2) * HY_XPITCH
    return pl.pallas_call(
        functools.partial(_hyena_kernel, na),
        grid=(nct, bn // 2),
        in_specs=[col(0), col(1), col(2),
                  pl.BlockSpec((HY_SHORT, 3, ct), lambda j, p: (0, 0, j)),
                  pl.BlockSpec((2, ct), lambda j, p: (0, j)),
                  kf_spec(0), kf_spec(1),
                  _const_spec(f1.shape), _const_spec(f2.shape), _const_spec(f2i.shape), _const_spec(g.shape)],
        out_specs=pl.BlockSpec((2, n_seq, ct), lambda j, p: (p, 0, j)),
        out_shape=jax.ShapeDtypeStruct((bn, t_all, ch), BF16),
        scratch_shapes=[pltpu.VMEM((2, n_seq + 2 * SUBLANES, ct), F32), pltpu.VMEM((2, pad_rows, ct), F32),
                        pltpu.VMEM((2, pad_rows, ct), F32), pltpu.VMEM((2, pad_rows, ct), F32),
                        pltpu.VMEM((na * HY_APITCH, ct), F32)],
        compiler_params=_params(2),
        name="hyena_latent",
    )(hy_all, hy_all, hy_all, conv_w.reshape(HY_SHORT, 3, ch), bias, kf, kf, f1, f2, f2i, g)


@functools.lru_cache(maxsize=None)
def _dense_dft_tables(n_seq):
    n = 2 * n_seq
    ang = 2.0 * np.pi * np.arange(n)[:, None] * np.arange(n_seq)[None, :] / n
    fd = np.concatenate([np.cos(ang), -np.sin(ang)], axis=0)
    gd = np.concatenate([np.cos(ang).T, -np.sin(ang).T], axis=1) / n
    return jnp.asarray(fd, dtype=BF16), jnp.asarray(gd, dtype=BF16)


def _hyena_ctx_kernel(n_seq, x1_ref, x2_ref, v_ref, cw_ref, bias_ref, kf1_ref, kf2_ref, fd_ref, gd_ref, buf_ref,
                      o_ref):
    del buf_ref
    n = 2 * n_seq

    def short_conv(raw_ref, which):
        x = raw_ref[0].astype(F32)
        zero = jnp.zeros((1, x.shape[-1]), F32)
        prev = jnp.concatenate([zero, x[:-1]], axis=0)
        nxt = jnp.concatenate([x[1:], zero], axis=0)
        return cw_ref[0, which:which + 1, :] * prev + cw_ref[1, which:which + 1, :] * x + cw_ref[2, which:which + 1, :] * nxt

    def long_conv(u, kf_ref, bias_row):
        xf = jnp.dot(fd_ref[...], u.astype(BF16), preferred_element_type=F32)
        xr, xi = xf[:n], xf[n:]
        kr, ki = kf_ref[0], kf_ref[1]
        z = jnp.concatenate([xr * kr - xi * ki, xr * ki + xi * kr], axis=0).astype(BF16)
        return jnp.dot(gd_ref[...], z, preferred_element_type=F32) + bias_row * u

    x1, x2, v = short_conv(x1_ref, 0), short_conv(x2_ref, 1), short_conv(v_ref, 2)
    z = x1 * long_conv(v, kf1_ref, bias_ref[0:1, :])
    o_ref[0] = (x2 * long_conv(z, kf2_ref, bias_ref[1:2, :])).astype(o_ref.dtype)


def _hyena_context(hy_all, y_buf, row_block, n_seq, conv_w, bias, kf1, kf2):
    bn = hy_all.shape[0]
    ch = hy_all.shape[-1] // 3
    fd, gd = _dense_dft_tables(n_seq)
    col = lambda which: pl.BlockSpec((1, n_seq, ch), lambda i: (i, row_block, which))
    split = lambda kf: jnp.stack([kf.real, kf.imag], axis=0)
    return pl.pallas_call(
        functools.partial(_hyena_ctx_kernel, n_seq),
        grid=(bn,),
        in_specs=[col(0), col(1), col(2), _const_spec((HY_SHORT, 3, ch)), _const_spec((2, ch)),
                  _const_spec((2, 2 * n_seq, ch)), _const_spec((2, 2 * n_seq, ch)),
                  _const_spec(fd.shape), _const_spec(gd.shape), pl.BlockSpec(memory_space=pl.ANY)],
        out_specs=pl.BlockSpec((1, n_seq, ch), lambda i: (i, row_block, 0)),
        out_shape=jax.ShapeDtypeStruct(y_buf.shape, y_buf.dtype),
        input_output_aliases={9: 0},
        compiler_params=_params(1),
        name="hyena_context",
    )(hy_all, hy_all, hy_all, conv_w.reshape(HY_SHORT, 3, ch), bias, split(kf1), split(kf2), fd, gd, y_buf)


def kernel(x, c, ctx, c_ctx, w_mod, b_mod, g_norm1, g_norm2, w_in, hy_conv, hy_w1, hy_b1, hy_w2, hy_b2, hy_w3,
           hy_freq1, hy_freq2, hy_decay, hy_bias, gla_wa2, gla_ba, gla_gnorm, s5_a_re, s5_a_im, s5_log_dt, s5_b_re,
           s5_b_im, s5_c_re, s5_c_im, s5_d, s5_w_glu, s5_b_glu, w_br_hy, w_br_gla, w_br_s5, w_out, w_ffn_in,
           w_ffn_out, g_final):
    bn, n_lat, d = x.shape
    n_ctx = ctx.shape[1]
    depth = w_in.shape[0]
    d_hy = w_br_hy.shape[1]
    s5_ch = w_br_s5.shape[1]
    d_ff = w_ffn_out.shape[1]
    assert n_ctx == ROW_TILE and n_lat % ROW_TILE == 0
    n_lat_tiles = n_lat // ROW_TILE
    n_low = 2 * GLA_LOWRANK
    o_a = GLA_QK + GLA_V
    o_u = o_a + n_low
    o_q = o_u + s5_ch
    o_gate = o_q + GLA_QK + GLA_V + 3 * d_hy
    col_sizes = (GLA_QK, GLA_V, A_PAD, GLA_QK, GLA_V, 3 * d_hy)

    cond = jnp.concatenate([c_ctx[None], c], axis=0)
    cond = jnp.pad(cond, ((0, (-cond.shape[0]) % SUBLANES), (0, 0)))
    mod_all = _modulation_all(cond, w_mod, b_mod)

    xc = jnp.concatenate([x, ctx], axis=1)
    for l in range(depth):
        last = l == depth - 1
        mods = jnp.stack([jnp.broadcast_to(mod_all[l, 0], (bn, 6 * d)), mod_all[l, 1:1 + bn]], axis=1)
        mods = mods[:, :, None, :]
        wl = w_in[l]
        w_proj = jnp.concatenate(
            [wl[:, :o_u], jnp.zeros((d, A_PAD - n_low), F32), wl[:, o_q:o_gate]], axis=1).astype(BF16)
        g1 = g_norm1[l][None]
        w_u_t = wl[:, o_u:o_q].T.astype(BF16)
        k_a, v_a, a_a, q_a, r_a, hy_a, u_t = _project(xc, mods, g1, w_proj, col_sizes, w_u_t, n_lat_tiles)

        wa = jnp.zeros((A_PAD, 2 * GLA_QK), F32)
        wa = wa.at[:GLA_LOWRANK, :GLA_QK].set(gla_wa2[l, 0]).at[GLA_LOWRANK:n_low, GLA_QK:].set(gla_wa2[l, 1])
        y_gla = _gla_mix(k_a, v_a, a_a, q_a, r_a, wa.astype(BF16), gla_ba[l], gla_gnorm[l][None], n_ctx)

        s5w = _s5_weights(s5_a_re[l], s5_a_im[l], s5_log_dt[l], s5_b_re[l], s5_b_im[l], s5_c_re[l], s5_c_im[l],
                          S5_CHUNK)
        ys = _s5_mix(u_t, n_ctx, s5w)

        hy_p = (hy_w1[l], hy_b1[l], hy_w2[l], hy_b2[l], hy_w3[l], hy_freq1[l], hy_freq2[l], hy_decay[l])
        y_hy = _hyena_latent(hy_a, n_lat, hy_conv[l], hy_bias[l], _hyena_spectra(_hyena_kernels(n_lat, hy_p)))
        if not last:
            kf_ctx = jnp.fft.fft(_hyena_kernels(n_ctx, hy_p), axis=1)
            y_hy = _hyena_context(hy_a, y_hy, n_lat // n_ctx, n_ctx, hy_conv[l], hy_bias[l], kf_ctx[0], kf_ctx[1])

        wts = (wl[:, o_gate:].astype(BF16), w_br_hy[l].astype(BF16), w_br_gla[l].astype(BF16),
               w_br_s5[l].astype(BF16), w_out[l].astype(BF16), s5_d[l][:, None], s5_w_glu[l].astype(BF16),
               s5_b_glu[l][None])
        n_tiles = n_lat_tiles if last else n_lat_tiles + n_ctx // ROW_TILE
        x_mid = _merge(xc, mods, g1, y_hy, y_gla, ys, u_t, wts, n_tiles, n_lat_tiles)
        wf = w_ffn_in[l]
        xc = _ffn(x_mid, mods, g_norm2[l][None], wf[:, :d_ff].astype(BF16), wf[:, d_ff:].astype(BF16),
                  w_ffn_out[l].astype(BF16), g_final[None], n_lat_tiles, last)
    return xc
```

```python
import functools
import math

import jax
import jax.numpy as jnp
import numpy as np
from jax import lax
from jax.experimental import pallas as pl
from jax.experimental.pallas import tpu as pltpu

F32 = jnp.float32
BF16 = jnp.bfloat16
EPS = 1e-6

HY_SHORT = 3
HY_POS_BANDS = 16
HY_N_FILT = 4
GLA_HEADS = 4
GLA_DK = 64
GLA_DV = 128
GLA_QK = GLA_HEADS * GLA_DK
GLA_V = GLA_HEADS * GLA_DV
GLA_LOWRANK = 16
GLA_TAU = 16.0
GLA_CHUNK = 64
S5_STATE = 64

LANES = 128
SUBLANES = 8
VMEM_LIMIT = 56 * 1024 * 1024

GLA_BLOCK = 4
A_PAD = LANES
ROW_TILE = 256


def _const_spec(shape):
    nd = len(shape)
    return pl.BlockSpec(shape, lambda *_: (0,) * nd, pipeline_mode=pl.Buffered(1))


def _params(n_axes):
    return pltpu.CompilerParams(dimension_semantics=("parallel",) * n_axes, vmem_limit_bytes=VMEM_LIMIT)


def _mod_kernel(c_ref, w_ref, b_ref, o_ref):
    c = c_ref[...]
    s = c * jax.nn.sigmoid(c)
    o_ref[0] = jnp.dot(s.astype(BF16), w_ref[0], preferred_element_type=F32) + b_ref[0]


def _modulation_all(cond, w_mod, b_mod):
    depth, d, n = w_mod.shape
    r = cond.shape[0]
    tn = 1536
    return pl.pallas_call(
        _mod_kernel,
        grid=(depth, n // tn),
        in_specs=[
            pl.BlockSpec((r, d), lambda l, j: (0, 0)),
            pl.BlockSpec((1, d, tn), lambda l, j: (l, 0, j)),
            pl.BlockSpec((1, 1, tn), lambda l, j: (l, 0, j)),
        ],
        out_specs=pl.BlockSpec((1, r, tn), lambda l, j: (l, 0, j)),
        out_shape=jax.ShapeDtypeStruct((depth, r, n), F32),
        compiler_params=_params(2),
        name="modulation",
    )(cond, w_mod.astype(BF16), b_mod.reshape(depth, 1, n))


def _norm_mod(x, g, shift, scale):
    y = x * lax.rsqrt(jnp.mean(x * x, axis=-1, keepdims=True) + EPS)
    return (y * g) * (1.0 + scale) + shift


_NT = (((1,), (1,)), ((), ()))


def _proj_kernel(col_sizes, d, x_ref, mod_ref, g_ref, w_ref, wt_ref, *o_refs):
    m = mod_ref[0, 0]
    h = _norm_mod(x_ref[0], g_ref[...], m[:, 0:d], m[:, d:2 * d]).astype(BF16)
    off = 0
    for o_ref, n in zip(o_refs[:-1], col_sizes):
        o_ref[0] = jnp.dot(h, w_ref[:, off:off + n], preferred_element_type=F32).astype(o_ref.dtype)
        off += n
    o_refs[-1][0] = lax.dot_general(wt_ref[...], h, _NT, preferred_element_type=F32)


def _mod_spec(d, n_lat_tiles):
    return pl.BlockSpec((1, 1, 1, 6 * d), lambda i, j: (i, (j < n_lat_tiles).astype(jnp.int32), 0, 0))


def _project(xc, mods, g, w, col_sizes, w_t, n_lat_tiles):
    b, t, d = xc.shape
    tm = ROW_TILE
    n_tot = sum(col_sizes)
    n_t = w_t.shape[0]
    out_shape = [jax.ShapeDtypeStruct((b, t, n), BF16) for n in col_sizes] + [jax.ShapeDtypeStruct((b, n_t, t), F32)]
    out_specs = [pl.BlockSpec((1, tm, n), lambda i, j: (i, j, 0)) for n in col_sizes]
    out_specs.append(pl.BlockSpec((1, n_t, tm), lambda i, j: (i, 0, j)))
    return pl.pallas_call(
        functools.partial(_proj_kernel, tuple(col_sizes), d),
        grid=(b, t // tm),
        in_specs=[
            pl.BlockSpec((1, tm, d), lambda i, j: (i, j, 0)),
            _mod_spec(d, n_lat_tiles),
            _const_spec((1, d)),
            _const_spec((d, n_tot)),
            _const_spec((n_t, d)),
        ],
        out_specs=out_specs,
        out_shape=out_shape,
        compiler_params=_params(2),
        name="in_proj",
    )(xc, mods, g, w, w_t)


def _gelu_tanh(x):
    return 0.5 * x * (1.0 + jnp.tanh(math.sqrt(2.0 / math.pi) * (x + 0.044715 * (x * x * x))))


def _merge_kernel(d, x_ref, mod_ref, g_ref, yhy_ref, ygla_ref, ys5_ref, u_ref, wg_ref, whb_ref, wgb_ref, wsb_ref,
                  wo_ref, s5d_ref, wglu_ref, bglu_ref, o_ref):
    x = x_ref[0]
    m = mod_ref[0, 0]
    h = _norm_mod(x, g_ref[...], m[:, 0:d], m[:, d:2 * d]).astype(BF16)
    y5 = (ys5_ref[0] + s5d_ref[...] * u_ref[0]).T
    g5 = _gelu_tanh(y5)
    y_s5 = g5 * jax.nn.sigmoid(jnp.dot(g5.astype(BF16), wglu_ref[...], preferred_element_type=F32) + bglu_ref[...])

    def branch(k, y, wb_ref):
        gate = jnp.dot(h, wg_ref[:, k * d:(k + 1) * d], preferred_element_type=F32)
        return jax.nn.sigmoid(gate) * jnp.dot(y, wb_ref[...], preferred_element_type=F32)

    mix = branch(0, yhy_ref[0], whb_ref) + branch(1, ygla_ref[0], wgb_ref) + branch(2, y_s5.astype(BF16), wsb_ref)
    out = jnp.dot(mix.astype(BF16), wo_ref[...], preferred_element_type=F32)
    o_ref[0] = x + m[:, 2 * d:3 * d] * out


def _merge(xc, mods, g, y_hy, y_gla, ys_s5, u_s5, wts, nj, n_lat_tiles):
    b, t, d = xc.shape
    tm = ROW_TILE
    ch = y_hy.shape[-1]
    row = lambda n: pl.BlockSpec((1, tm, n), lambda i, j: (i, j, 0))
    col = lambda n: pl.BlockSpec((1, n, tm), lambda i, j: (i, 0, j))
    w_gate, w_hy, w_gla, w_s5, w_out, s5_d, w_glu, b_glu = wts
    return pl.pallas_call(
        functools.partial(_merge_kernel, d),
        grid=(b, nj),
        in_specs=[
            row(d),
            _mod_spec(d, n_lat_tiles),
            _const_spec((1, d)),
            row(ch), row(ch), col(ch), col(ch),
            _const_spec(w_gate.shape), _const_spec(w_hy.shape), _const_spec(w_gla.shape), _const_spec(w_s5.shape),
            _const_spec(w_out.shape), _const_spec(s5_d.shape), _const_spec(w_glu.shape), _const_spec(b_glu.shape),
        ],
        out_specs=pl.BlockSpec((1, tm, d), lambda i, j: (i, j, 0)),
        out_shape=jax.ShapeDtypeStruct((b, nj * tm, d), F32),
        compiler_params=_params(2),
        name="merge",
    )(xc, mods, g, y_hy, y_gla, ys_s5, u_s5, w_gate, w_hy, w_gla, w_s5, w_out, s5_d, w_glu, b_glu)


def _ffn_kernel(d, final, x_ref, mod_ref, g_ref, wa_ref, wb_ref, wo_ref, gf_ref, o_ref):
    x = x_ref[0]
    m = mod_ref[0, 0]
    h = _norm_mod(x, g_ref[...], m[:, 3 * d:4 * d], m[:, 4 * d:5 * d]).astype(BF16)
    a = jnp.dot(h, wa_ref[...], preferred_element_type=F32)
    bb = jnp.dot(h, wb_ref[...], preferred_element_type=F32)
    act = (a * jax.nn.sigmoid(a) * bb).astype(BF16)
    y = x + m[:, 5 * d:6 * d] * jnp.dot(act, wo_ref[...], preferred_element_type=F32)
    if final:
        y = y * lax.rsqrt(jnp.mean(y * y, axis=-1, keepdims=True) + EPS) * gf_ref[...]
    o_ref[0] = y


def _ffn(x, mods, g, wa, wb, wo, g_final, n_lat_tiles, final):
    b, t, d = x.shape
    tm = ROW_TILE
    return pl.pallas_call(
        functools.partial(_ffn_kernel, d, final),
        grid=(b, t // tm),
        in_specs=[
            pl.BlockSpec((1, tm, d), lambda i, j: (i, j, 0)),
            _mod_spec(d, n_lat_tiles),
            _const_spec((1, d)),
            _const_spec(wa.shape), _const_spec(wb.shape), _const_spec(wo.shape),
            _const_spec((1, d)),
        ],
        out_specs=pl.BlockSpec((1, tm, d), lambda i, j: (i, j, 0)),
        out_shape=jax.ShapeDtypeStruct((b, t, d), F32),
        compiler_params=_params(2),
        name="ffn",
    )(x, mods, g, wa, wb, wo, g_final)


S5_CHUNK = LANES


def _s5_weights(a_re, a_im, log_dt, b_re, b_im, c_re, c_im, tc):
    g, p = a_re.shape[1:]
    i = b_re.shape[-1]
    lam_c = lax.complex(jnp.minimum(a_re, -1e-4), a_im)
    lam_dt = lam_c * jnp.exp(log_dt)[..., None]
    b_bar = ((jnp.exp(lam_dt) - 1.0) / lam_c)[..., None] * lax.complex(b_re, b_im)[None]
    c_mat = lax.complex(c_re, c_im)
    tau = jnp.arange(tc, dtype=F32)
    pw = jnp.exp(lam_dt[:, :, None, :] * tau[None, None, :, None])
    pw1 = pw * jnp.exp(lam_dt)[:, :, None, :]
    kern = jnp.einsum('gip,dgtp,dgpj->dgtij', c_mat, pw, b_bar).real
    k_lag = jnp.concatenate([kern[1][:, :0:-1], kern[0][:, :1] + kern[1][:, :1], kern[0][:, 1:]], axis=1)
    kr = jnp.pad(k_lag.transpose(0, 3, 2, 1), ((0, 0), (0, 0), (0, 0), (0, 1)))
    kr = kr.reshape(g, i, i * 2 * tc)

    inc_f = pw[0][:, None, ::-1, :] * b_bar[0].transpose(0, 2, 1)[:, :, None]
    inc_b = pw[1][:, None, :, :] * b_bar[1].transpose(0, 2, 1)[:, :, None]
    bm = jnp.concatenate([inc_f.real, inc_b.real, inc_f.imag, inc_b.imag], axis=-1).reshape(g, tc * i, 4 * p)

    out_f = c_mat.transpose(0, 2, 1)[..., None] * pw1[0].transpose(0, 2, 1)[:, :, None, :]
    out_b = c_mat.transpose(0, 2, 1)[..., None] * pw1[1][:, ::-1].transpose(0, 2, 1)[:, :, None, :]
    z = jnp.zeros((g, p, tc * i), F32)
    fl = lambda t: t.reshape(g, p, tc * i)
    cm = jnp.concatenate([fl(out_f.real), z, fl(-out_f.imag), z, z, fl(out_b.real), z, fl(-out_b.imag)], axis=1)

    lam_t = jnp.exp(lam_dt * float(tc))
    lam = jnp.stack([jnp.concatenate([lam_t[0].real, lam_t[1].real], -1),
                     jnp.concatenate([lam_t[0].imag, lam_t[1].imag], -1)], axis=1)
    return kr, bm.astype(BF16), cm.astype(BF16), lam


def _s5_kernel(nc, nc_ctx, tc, u_ref, kr_ref, bm_ref, cm_ref, lam_ref, y_ref, up_ref, dx_ref, p_ref, m_ref):
    bn, i_sz = u_ref.shape[:2]
    nl = nc - nc_ctx
    width = kr_ref.shape[-1]
    seg = width // i_sz
    for j in range(i_sz):
        lag_rows = pltpu.roll(jnp.broadcast_to(kr_ref[0, j:j + 1, :], (tc, width)), width - (tc - 1), axis=1,
                              stride=1, stride_axis=0)
        m_ref[j * tc:(j + 1) * tc, :] = jnp.concatenate(
            [lag_rows[:, i * seg:i * seg + tc] for i in range(i_sz)], axis=1).astype(BF16)
    for n in range(nc):
        for i in range(i_sz):
            up_ref[n * bn:(n + 1) * bn, i * tc:(i + 1) * tc] = u_ref[:, i, n * tc:(n + 1) * tc]
    u = up_ref[...].astype(BF16)
    dx_ref[...] = jnp.dot(u, bm_ref[0], preferred_element_type=F32)
    lam = lam_ref[0]
    lr, li = lam[0:1], lam[1:2]
    w = lam.shape[-1]
    half = w // 2
    is_f = lax.broadcasted_iota(jnp.int32, (bn, w), 1) < half

    def step(s, carry):
        sr, si = carry
        nf = jnp.where(s < nc_ctx, nl + s, s - nc_ctx)
        nb = nc - 1 - s
        rf = pl.multiple_of(nf * bn, bn)
        rb = pl.multiple_of(nb * bn, bn)
        p_ref[pl.ds(rf, bn), 0:w] = sr
        p_ref[pl.ds(rf, bn), w:2 * w] = si
        p_ref[pl.ds(rb, bn), 2 * w:3 * w] = sr
        p_ref[pl.ds(rb, bn), 3 * w:4 * w] = si
        d_re = jnp.where(is_f, dx_ref[pl.ds(rf, bn), 0:w], dx_ref[pl.ds(rb, bn), 0:w])
        d_im = jnp.where(is_f, dx_ref[pl.ds(rf, bn), w:2 * w], dx_ref[pl.ds(rb, bn), w:2 * w])
        return lr * sr - li * si + d_re, lr * si + li * sr + d_im

    zero = jnp.zeros((bn, 2 * half), F32)
    lax.fori_loop(0, nc, step, (zero, zero))
    y = jnp.dot(u, m_ref[...], preferred_element_type=F32)
    y = y + jnp.dot(p_ref[...].astype(BF16), cm_ref[0], preferred_element_type=F32)
    for n in range(nc):
        for i in range(i_sz):
            y_ref[:, i, n * tc:(n + 1) * tc] = y[n * bn:(n + 1) * bn, i * tc:(i + 1) * tc]


def _s5_mix(u_t, n_ctx, wts, layer):
    kr, bm, cm, lam = wts
    bn, ch, t = u_t.shape
    g, i = kr.shape[1:3]
    tc = kr.shape[3] // (2 * i)
    assert 2 * S5_STATE == LANES and bn % SUBLANES == 0 and n_ctx % tc == 0 and t % tc == 0 and tc % LANES == 0
    nc, nc_ctx = t // tc, n_ctx // tc
    k = tc * i
    r = nc * bn
    return pl.pallas_call(
        functools.partial(_s5_kernel, nc, nc_ctx, tc),
        grid=(g,),
        in_specs=[
            pl.BlockSpec((bn, i, t), lambda j: (0, j, 0)),
            pl.BlockSpec((None, 1, i, 2 * k), lambda j: (layer, j, 0, 0)),
            pl.BlockSpec((None, 1, k, 4 * S5_STATE), lambda j: (layer, j, 0, 0)),
            pl.BlockSpec((None, 1, 8 * S5_STATE, k), lambda j: (layer, j, 0, 0)),
            pl.BlockSpec((None, 1, 2, 2 * S5_STATE), lambda j: (layer, j, 0, 0)),
        ],
        out_specs=pl.BlockSpec((bn, i, t), lambda j: (0, j, 0)),
        out_shape=jax.ShapeDtypeStruct((bn, ch, t), F32),
        scratch_shapes=[pltpu.VMEM((r, k), F32), pltpu.VMEM((r, 4 * S5_STATE), F32),
                        pltpu.VMEM((r, 8 * S5_STATE), F32), pltpu.VMEM((k, k), BF16)],
        compiler_params=_params(1),
        name="s5_mix",
    )(u_t, kr, bm, cm, lam)


def _gla_kernel(nc, nc_ctx, k_ref, v_ref, a_ref, q_ref, r_ref, wa_ref, ba_ref, gn_ref, y_ref, of_ref, ob_ref,
                s_ref):
    c, h_n, dk, dv = GLA_CHUNK, GLA_HEADS, GLA_DK, GLA_DV
    qk = h_n * dk
    q_scale = dk ** -0.5
    nbk = GLA_BLOCK
    rb = nbk * c
    row_i = lax.broadcasted_iota(jnp.int32, (rb, rb), 0)
    col_i = lax.broadcasted_iota(jnp.int32, (rb, rb), 1)
    same = (row_i // c) == (col_i // c)
    causal = (same & (row_i >= col_i), same & (row_i <= col_i))
    cum_ops = [m.astype(BF16) for m in causal]
    att_mask = [jnp.concatenate([m] * h_n, axis=0) for m in causal]
    lane_head = lax.broadcasted_iota(jnp.int32, (1, qk), 1) // dk
    head_lanes = [lane_head == h for h in range(h_n)]

    def by_head(x):
        zero = jnp.zeros_like(x)
        return jnp.concatenate([jnp.where(m, x, zero) for m in head_lanes], axis=0)

    def block(j, d):
        rows = pl.ds(pl.multiple_of(j * rb, rb), rb)
        k = k_ref[0, rows, :].astype(F32)
        q = q_ref[0, rows, :].astype(F32) * q_scale
        v = v_ref[0, rows, :]
        z = jnp.dot(a_ref[0, rows, :], wa_ref[:, d * qk:(d + 1) * qk], preferred_element_type=F32) + ba_ref[d:d + 1, :]
        log_a = (jnp.minimum(z, 0.0) - jnp.log(1.0 + jnp.exp(-jnp.abs(z)))) * (1.0 / GLA_TAU)
        hi = log_a.astype(BF16)
        lo = (log_a - hi.astype(F32)).astype(BF16)
        cs = jnp.dot(cum_ops[d], jnp.concatenate([hi, lo], axis=1), preferred_element_type=F32)
        b = cs[:, :qk] + cs[:, qk:]
        b_tot = jnp.concatenate(
            [jnp.broadcast_to(b[n * c + (c - 1 if d == 0 else 0)][None], (c, qk)) for n in range(nbk)], axis=0)
        qd = (q * jnp.exp(b)).astype(BF16)
        kd = (k * jnp.exp(-b)).astype(BF16)
        kl = (k * jnp.exp(b_tot - b)).astype(BF16)
        qm = by_head(qd)
        intra = []
        for h in range(h_n):
            att = lax.dot_general(qm[h * rb:(h + 1) * rb], kd, _NT, preferred_element_type=F32)
            att = jnp.where(causal[d], att, 0.0).astype(BF16)
            intra.append(jnp.dot(att, v[:, h * dv:(h + 1) * dv], preferred_element_type=F32))
        s_t = s_ref[d]
        inter = [None] * nbk
        for n in (range(nbk) if d == 0 else reversed(range(nbk))):
            r0 = n * c
            qm_n = jnp.concatenate([qm[h * rb + r0:h * rb + r0 + c] for h in range(h_n)], axis=0)
            inter[n] = lax.dot_general(qm_n, s_t.astype(BF16), _NT, preferred_element_type=F32)
            v_t = jnp.concatenate(
                [jnp.concatenate([v[r0:r0 + c, h * dv:(h + 1) * dv] for h in range(p, p + dv // c)], axis=0).T
                 for p in range(0, h_n, dv // c)], axis=1)
            s_t = s_t * jnp.exp(b_tot[r0:r0 + 1]) + jnp.dot(v_t, by_head(kl[r0:r0 + c]), preferred_element_type=F32)
        s_ref[d] = s_t
        o = jnp.concatenate(
            [intra[h] + jnp.concatenate([inter[n][h * c:(h + 1) * c] for n in range(nbk)], axis=0)
             for h in range(h_n)], axis=1)
        return rows, o

    def scan_step(i, carry, lo_block, hi_block):
        rows_f, o_f = block(lo_block + i, 0)
        of_ref[rows_f, :] = o_f
        rows_b, o_b = block(hi_block - 1 - i, 1)
        ob_ref[rows_b, :] = o_b
        return carry

    def readout(n, carry):
        rows = pl.ds(pl.multiple_of(n * c, c), c)
        o = of_ref[rows, :] + ob_ref[rows, :]
        r = r_ref[0, rows, :].astype(F32)
        gate = r * jax.nn.sigmoid(r)
        outs = []
        for h in range(h_n):
            oh = o[:, h * dv:(h + 1) * dv]
            oh = oh * lax.rsqrt(jnp.mean(oh * oh, axis=-1, keepdims=True) + EPS) * gn_ref[...]
            outs.append(oh * gate[:, h * dv:(h + 1) * dv])
        y_ref[0, rows, :] = jnp.concatenate(outs, axis=1).astype(y_ref.dtype)
        return carry

    nl, n_ctx = (nc - nc_ctx) // nbk, nc_ctx // nbk
    s_ref[...] = jnp.zeros(s_ref.shape, F32)
    lax.fori_loop(0, n_ctx, functools.partial(scan_step, lo_block=nl, hi_block=nl + n_ctx), 0)
    lax.fori_loop(0, nl, functools.partial(scan_step, lo_block=0, hi_block=nl), 0)
    lax.fori_loop(0, nc, readout, 0, unroll=2)


def _gla_mix(k, v, a, q, r, wa, ba, gnorm, n_ctx):
    bn, t, qk = k.shape
    vd = v.shape[-1]
    c = GLA_CHUNK
    assert t % (c * GLA_BLOCK) == 0 and n_ctx % (c * GLA_BLOCK) == 0 and GLA_DV % c == 0
    seq = lambda n: pl.BlockSpec((1, t, n), lambda i: (i, 0, 0), pipeline_mode=pl.Buffered(1))
    return pl.pallas_call(
        functools.partial(_gla_kernel, t // c, n_ctx // c),
        grid=(bn,),
        in_specs=[seq(qk), seq(vd), seq(a.shape[-1]), seq(qk), seq(vd),
                  _const_spec(wa.shape), _const_spec(ba.shape), _const_spec(gnorm.shape)],
        out_specs=pl.BlockSpec((1, t, vd), lambda i: (i, 0, 0)),
        out_shape=jax.ShapeDtypeStruct((bn, t, vd), BF16),
        scratch_shapes=[pltpu.VMEM((t, vd), F32), pltpu.VMEM((t, vd), F32), pltpu.VMEM((2, GLA_DV, qk), F32)],
        compiler_params=_params(1),
        name="gla_mix",
    )(k, v, a, q, r, wa, ba, gnorm)


HY_NB = LANES
HY_XPITCH = HY_NB + SUBLANES
HY_APITCH = 2 * HY_NB + SUBLANES
HY_CT = LANES
HY_UNROLL_OUTER = 16
HY_UNROLL_INNER = 8


def _hyena_filters(n, params, reverse):
    w1, b1, w2, b2, w3, freq1, freq2, decay = (p.astype(F32) for p in params)
    pos = jnp.arange(n, dtype=F32)
    t = jnp.linspace(0.0, 1.0, n, dtype=F32)
    if reverse:
        pos, t = (n - 1) - pos, jnp.linspace(1.0, 0.0, n, dtype=F32)
    t = t[:, None]
    bands = jnp.linspace(1e-4, HY_POS_BANDS - 1, HY_POS_BANDS, dtype=F32)
    ang = (2.0 * math.pi / n) * pos[:, None] * bands[None]
    z = jnp.concatenate([t, jnp.cos(ang), -jnp.sin(ang)], axis=-1)
    hid = jnp.sin(freq1 * (z @ w1 + b1))
    hid = jnp.sin(freq2 * (hid @ w2 + b2))
    w3 = w3.reshape(w3.shape[0], HY_N_FILT, -1)
    window = jnp.exp(-t[None] * jnp.abs(decay).reshape(HY_N_FILT, 1, -1))
    return jnp.einsum('nk,kfc->fnc', hid, w3) * window


def _hyena_kernels(n, params):
    h_fwd = _hyena_filters(n, params, False)[0::2]
    h_bwd_rev = _hyena_filters(n, params, True)[1::2]
    k = jnp.concatenate([h_fwd[:, :1] + h_bwd_rev[:, n - 1:], h_fwd[:, 1:], jnp.zeros_like(h_fwd[:, :1]),
                         h_bwd_rev[:, :n - 1]], axis=1)
    return k * lax.rsqrt(jnp.sum(k * k, axis=1, keepdims=True) + EPS)


def _spectrum_kernel(na, k_ref, f1_ref, f2_ref, o_ref, a2):
    nb, ap = HY_NB, HY_APITCH

    def stage1(b, c):
        xs = k_ref[0, pl.ds(b, na, stride=nb), :].astype(BF16)
        r = jnp.dot(f1_ref[b], xs, preferred_element_type=F32)
        a2[pl.ds(b, na, stride=ap), :] = r[:na]
        a2[pl.ds(nb + b, na, stride=ap), :] = r[na:]
        return c

    def stage2(ka, c):
        rows = pl.ds(pl.multiple_of(ka * ap, 8), 2 * nb)
        xf = jnp.dot(f2_ref[...], a2[rows, :].astype(BF16), preferred_element_type=F32)
        o_ref[0, ka, 0] = xf[:nb].astype(o_ref.dtype)
        o_ref[0, ka, 1] = xf[nb:].astype(o_ref.dtype)
        return c

    lax.fori_loop(0, nb, stage1, 0, unroll=HY_UNROLL_INNER)
    lax.fori_loop(0, na, stage2, 0, unroll=HY_UNROLL_INNER)


def _hyena_spectra(kernels):
    nf, n, ch = kernels.shape
    nb, ct = HY_NB, HY_CT
    na = n // nb
    _, f2, _, _, f1_full = _dft_tables(n // 2)
    return pl.pallas_call(
        functools.partial(_spectrum_kernel, na),
        grid=(nf, ch // ct),
        in_specs=[pl.BlockSpec((1, n, ct), lambda f, j: (f, 0, j)), _const_spec(f1_full.shape), _const_spec(f2.shape)],
        out_specs=pl.BlockSpec((1, na, 2, nb, ct), lambda f, j: (f, 0, 0, 0, j)),
        out_shape=jax.ShapeDtypeStruct((nf, na, 2, nb, ch), BF16),
        scratch_shapes=[pltpu.VMEM((na * HY_APITCH, ct), F32)],
        compiler_params=_params(2),
        name="hyena_spectrum",
    )(kernels, f1_full, f2)


def _cplx_block(m):
    return np.block([[m.real, -m.imag], [m.imag, m.real]])


@functools.lru_cache(maxsize=None)
def _dft_tables(n_seq):
    nb = HY_NB
    n = 2 * n_seq
    na = n // nb
    ha = na // 2
    ka = np.arange(na)[:, None]
    b = np.arange(nb)
    tw = np.exp(-2j * np.pi * ka * b[None, :] / n)
    w_a = np.exp(-2j * np.pi * ka * np.arange(ha)[None, :] / na)
    f1 = np.stack([_cplx_block(tw[:, i:i + 1] * w_a) for i in range(nb)])
    g = np.stack([_cplx_block((np.conj(tw[:, i:i + 1] * w_a)).T / n) for i in range(nb)])
    w_b = np.exp(-2j * np.pi * b[:, None] * b[None, :] / nb)
    f2 = _cplx_block(w_b)
    f2i = _cplx_block(np.conj(w_b))
    w_full = np.exp(-2j * np.pi * ka * np.arange(na)[None, :] / na)
    f1_full = np.stack([np.concatenate([(tw[:, i:i + 1] * w_full).real, (tw[:, i:i + 1] * w_full).imag], axis=0)
                        for i in range(nb)])
    return tuple(jnp.asarray(t, dtype=BF16) for t in (f1, f2, f2i, g, f1_full))


def _hyena_kernel(na, x1_ref, x2_ref, v_ref, cw_ref, bias_ref, kf1_ref, kf2_ref, f1_ref, f2_ref, f2i_ref, g_ref,
                  o_ref, rawf, vp, gp, yp, a2):
    nb, xp, ap = HY_NB, HY_XPITCH, HY_APITCH
    ha = na // 2
    n_seq = ha * nb
    ct = o_ref.shape[-1]
    halo = SUBLANES
    rawf[:, 0:halo, :] = jnp.zeros((2, halo, ct), F32)
    rawf[:, halo + n_seq:2 * halo + n_seq, :] = jnp.zeros((2, halo, ct), F32)

    def short_conv_into(raw_ref, which, dst):
        w0, w1, w2 = (cw_ref[j, which:which + 1, :] for j in range(HY_SHORT))
        for e in range(2):
            rawf[e, halo:halo + n_seq, :] = raw_ref[e].astype(F32)

            def body(a, c):
                base = pl.multiple_of(a * nb, nb) + halo
                u = (w0 * rawf[e, pl.ds(base - 1, nb), :] + w1 * rawf[e, pl.ds(base, nb), :]
                     + w2 * rawf[e, pl.ds(base + 1, nb), :])
                dst[e, pl.ds(pl.multiple_of(a * xp, 8), nb), :] = u
                return c

            lax.fori_loop(0, ha, body, 0)

    def long_conv(kf_ref):
        def stage1(b, c):
            xs = jnp.concatenate([vp[0, pl.ds(b, ha, stride=xp), :], vp[1, pl.ds(b, ha, stride=xp), :]], axis=0)
            r = jnp.dot(f1_ref[b], xs.astype(BF16), preferred_element_type=F32)
            a2[pl.ds(b, na, stride=ap), :] = r[:na]
            a2[pl.ds(nb + b, na, stride=ap), :] = r[na:]
            return c

        def stage2(kp, c):
            rows = [pl.ds(pl.multiple_of((2 * kp + e) * ap, 8), 2 * nb) for e in range(2)]
            slab = jnp.concatenate([a2[r, :] for r in rows], axis=1).astype(BF16)
            xf = jnp.dot(f2_ref[...], slab, preferred_element_type=F32)
            xr, xi = xf[:nb], xf[nb:]
            kr = jnp.concatenate([kf_ref[0, 2 * kp + e, 0] for e in range(2)], axis=1).astype(F32)
            ki = jnp.concatenate([kf_ref[0, 2 * kp + e, 1] for e in range(2)], axis=1).astype(F32)
            z = jnp.concatenate([xr * kr - xi * ki, xr * ki + xi * kr], axis=0).astype(BF16)
            back = jnp.dot(f2i_ref[...], z, preferred_element_type=F32)
            for e in range(2):
                a2[rows[e], :] = back[:, e * ct:(e + 1) * ct]
            return c

        def stage3(b, c):
            s = jnp.concatenate([a2[pl.ds(b, na, stride=ap), :], a2[pl.ds(nb + b, na, stride=ap), :]], axis=0)
            y = jnp.dot(g_ref[b], s.astype(BF16), preferred_element_type=F32)
            yp[0, pl.ds(b, ha, stride=xp), :] = y[:ha]
            yp[1, pl.ds(b, ha, stride=xp), :] = y[ha:]
            return c

        lax.fori_loop(0, nb, stage1, 0, unroll=HY_UNROLL_OUTER)
        lax.fori_loop(0, na // 2, stage2, 0, unroll=HY_UNROLL_INNER)
        lax.fori_loop(0, nb, stage3, 0, unroll=HY_UNROLL_OUTER)

    def gated(bias_row, write):
        for e in range(2):
            def body(a, c):
                rows = pl.ds(pl.multiple_of(a * xp, 8), nb)
                write(e, a, rows, gp[e, rows, :] * (yp[e, rows, :] + bias_row * vp[e, rows, :]))
                return c

            lax.fori_loop(0, ha, body, 0)

    def write_z(e, a, rows, val):
        vp[e, rows, :] = val

    def write_out(e, a, rows, val):
        o_ref[e, pl.ds(pl.multiple_of(a * nb, nb), nb), :] = val.astype(o_ref.dtype)

    short_conv_into(v_ref, 2, vp)
    short_conv_into(x1_ref, 0, gp)
    long_conv(kf1_ref)
    gated(bias_ref[0:1, :], write_z)
    short_conv_into(x2_ref, 1, gp)
    long_conv(kf2_ref)
    gated(bias_ref[1:2, :], write_out)


def _hyena_latent(hy_all, n_seq, conv_w, bias, kf):
    bn, t_all = hy_all.shape[:2]
    ch = hy_all.shape[-1] // 3
    ct = HY_CT
    nct = ch // ct
    nb = HY_NB
    na = 2 * n_seq // nb
    assert bn % 2 == 0 and ch % ct == 0 and n_seq % (8 * nb) == 0
    f1, f2, f2i, g, _ = _dft_tables(n_seq)
    col = lambda which: pl.BlockSpec((2, n_seq, ct), lambda j, p: (p, 0, which * nct + j), pipeline_mode=pl.Buffered(1))
    kf_spec = lambda f: pl.BlockSpec((1, na, 2, nb, ct), lambda j, p: (f, 0, 0, 0, j), pipeline_mode=pl.Buffered(1))
    pad_rows = (na // 2) * HY_XPITCH
    return pl.pallas_call(
        functools.partial(_hyena_kernel, na),
        grid=(nct, bn // 2),
        in_specs=[col(0), col(1), col(2),
                  pl.BlockSpec((HY_SHORT, 3, ct), lambda j, p: (0, 0, j)),
                  pl.BlockSpec((2, ct), lambda j, p: (0, j)),
                  kf_spec(0), kf_spec(1),
                  _const_spec(f1.shape), _const_spec(f2.shape), _const_spec(f2i.shape), _const_spec(g.shape)],
        out_specs=pl.BlockSpec((2, n_seq, ct), lambda j, p: (p, 0, j)),
        out_shape=jax.ShapeDtypeStruct((bn, t_all, ch), BF16),
        scratch_shapes=[pltpu.VMEM((2, n_seq + 2 * SUBLANES, ct), F32), pltpu.VMEM((2, pad_rows, ct), F32),
                        pltpu.VMEM((2, pad_rows, ct), F32), pltpu.VMEM((2, pad_rows, ct), F32),
                        pltpu.VMEM((na * HY_APITCH, ct), F32)],
        compiler_params=_params(2),
        name="hyena_latent",
    )(hy_all, hy_all, hy_all, conv_w.reshape(HY_SHORT, 3, ch), bias, kf, kf, f1, f2, f2i, g)


@functools.lru_cache(maxsize=None)
def _dense_dft_tables(n_seq):
    n = 2 * n_seq
    ang = 2.0 * np.pi * np.arange(n)[:, None] * np.arange(n_seq)[None, :] / n
    fd = np.concatenate([np.cos(ang), -np.sin(ang)], axis=0)
    gd = np.concatenate([np.cos(ang).T, -np.sin(ang).T], axis=1) / n
    return jnp.asarray(fd, dtype=BF16), jnp.asarray(gd, dtype=BF16)


def _hyena_ctx_kernel(n_seq, x1_ref, x2_ref, v_ref, cw_ref, bias_ref, kf1_ref, kf2_ref, fd_ref, gd_ref, buf_ref,
                      o_ref):
    del buf_ref
    n = 2 * n_seq

    def short_conv(raw_ref, which):
        x = raw_ref[0].astype(F32)
        zero = jnp.zeros((1, x.shape[-1]), F32)
        prev = jnp.concatenate([zero, x[:-1]], axis=0)
        nxt = jnp.concatenate([x[1:], zero], axis=0)
        return cw_ref[0, which:which + 1, :] * prev + cw_ref[1, which:which + 1, :] * x + cw_ref[2, which:which + 1, :] * nxt

    def long_conv(u, kf_ref, bias_row):
        xf = jnp.dot(fd_ref[...], u.astype(BF16), preferred_element_type=F32)
        xr, xi = xf[:n], xf[n:]
        kr, ki = kf_ref[0], kf_ref[1]
        z = jnp.concatenate([xr * kr - xi * ki, xr * ki + xi * kr], axis=0).astype(BF16)
        return jnp.dot(gd_ref[...], z, preferred_element_type=F32) + bias_row * u

    x1, x2, v = short_conv(x1_ref, 0), short_conv(x2_ref, 1), short_conv(v_ref, 2)
    z = x1 * long_conv(v, kf1_ref, bias_ref[0:1, :])
    o_ref[0] = (x2 * long_conv(z, kf2_ref, bias_ref[1:2, :])).astype(o_ref.dtype)


def _hyena_context(hy_all, y_buf, row_block, n_seq, conv_w, bias, kf1, kf2):
    bn = hy_all.shape[0]
    ch = hy_all.shape[-1] // 3
    fd, gd = _dense_dft_tables(n_seq)
    col = lambda which: pl.BlockSpec((1, n_seq, ch), lambda i: (i, row_block, which))
    split = lambda kf: jnp.stack([kf.real, kf.imag], axis=0)
    return pl.pallas_call(
        functools.partial(_hyena_ctx_kernel, n_seq),
        grid=(bn,),
        in_specs=[col(0), col(1), col(2), _const_spec((HY_SHORT, 3, ch)), _const_spec((2, ch)),
                  _const_spec((2, 2 * n_seq, ch)), _const_spec((2, 2 * n_seq, ch)),
                  _const_spec(fd.shape), _const_spec(gd.shape), pl.BlockSpec(memory_space=pl.ANY)],
        out_specs=pl.BlockSpec((1, n_seq, ch), lambda i: (i, row_block, 0)),
        out_shape=jax.ShapeDtypeStruct(y_buf.shape, y_buf.dtype),
        input_output_aliases={9: 0},
        compiler_params=_params(1),
        name="hyena_context",
    )(hy_all, hy_all, hy_all, conv_w.reshape(HY_SHORT, 3, ch), bias, split(kf1), split(kf2), fd, gd, y_buf)


def kernel(x, c, ctx, c_ctx, w_mod, b_mod, g_norm1, g_norm2, w_in, hy_conv, hy_w1, hy_b1, hy_w2, hy_b2, hy_w3,
           hy_freq1, hy_freq2, hy_decay, hy_bias, gla_wa2, gla_ba, gla_gnorm, s5_a_re, s5_a_im, s5_log_dt, s5_b_re,
           s5_b_im, s5_c_re, s5_c_im, s5_d, s5_w_glu, s5_b_glu, w_br_hy, w_br_gla, w_br_s5, w_out, w_ffn_in,
           w_ffn_out, g_final):
    bn, n_lat, d = x.shape
    n_ctx = ctx.shape[1]
    depth = w_in.shape[0]
    d_hy = w_br_hy.shape[1]
    s5_ch = w_br_s5.shape[1]
    d_ff = w_ffn_out.shape[1]
    assert n_ctx == ROW_TILE and n_lat % ROW_TILE == 0
    n_lat_tiles = n_lat // ROW_TILE
    n_low = 2 * GLA_LOWRANK
    o_a = GLA_QK + GLA_V
    o_u = o_a + n_low
    o_q = o_u + s5_ch
    o_gate = o_q + GLA_QK + GLA_V + 3 * d_hy
    col_sizes = (GLA_QK, GLA_V, A_PAD, GLA_QK, GLA_V, 3 * d_hy)

    cond = jnp.concatenate([c_ctx[None], c], axis=0)
    cond = jnp.pad(cond, ((0, (-cond.shape[0]) % SUBLANES), (0, 0)))
    mod_all = _modulation_all(cond, w_mod, b_mod)

    s5w_all = jax.vmap(functools.partial(_s5_weights, tc=S5_CHUNK))(
        s5_a_re, s5_a_im, s5_log_dt, s5_b_re, s5_b_im, s5_c_re, s5_c_im)
    wa_all = jnp.zeros((depth, A_PAD, 2 * GLA_QK), F32)
    wa_all = wa_all.at[:, :GLA_LOWRANK, :GLA_QK].set(gla_wa2[:, 0]).at[:, GLA_LOWRANK:n_low, GLA_QK:].set(gla_wa2[:, 1])
    wa_all = wa_all.astype(BF16)

    xc = jnp.concatenate([x, ctx], axis=1)
    for l in range(depth):
        last = l == depth - 1
        mods = jnp.stack([jnp.broadcast_to(mod_all[l, 0], (bn, 6 * d)), mod_all[l, 1:1 + bn]], axis=1)
        mods = mods[:, :, None, :]
        wl = w_in[l]
        w_proj = jnp.concatenate(
            [wl[:, :o_u], jnp.zeros((d, A_PAD - n_low), F32), wl[:, o_q:o_gate]], axis=1).astype(BF16)
        g1 = g_norm1[l][None]
        w_u_t = wl[:, o_u:o_q].T.astype(BF16)
        k_a, v_a, a_a, q_a, r_a, hy_a, u_t = _project(xc, mods, g1, w_proj, col_sizes, w_u_t, n_lat_tiles)

        y_gla = _gla_mix(k_a, v_a, a_a, q_a, r_a, wa_all[l], gla_ba[l], gla_gnorm[l][None], n_ctx)

        ys = _s5_mix(u_t, n_ctx, s5w_all, l)

        hy_p = (hy_w1[l], hy_b1[l], hy_w2[l], hy_b2[l], hy_w3[l], hy_freq1[l], hy_freq2[l], hy_decay[l])
        y_hy = _hyena_latent(hy_a, n_lat, hy_conv[l], hy_bias[l], _hyena_spectra(_hyena_kernels(n_lat, hy_p)))
        if not last:
            kf_ctx = jnp.fft.fft(_hyena_kernels(n_ctx, hy_p), axis=1)
            y_hy = _hyena_context(hy_a, y_hy, n_lat // n_ctx, n_ctx, hy_conv[l], hy_bias[l], kf_ctx[0], kf_ctx[1])

        wts = (wl[:, o_gate:].astype(BF16), w_br_hy[l].astype(BF16), w_br_gla[l].astype(BF16),
               w_br_s5[l].astype(BF16), w_out[l].astype(BF16), s5_d[l][:, None], s5_w_glu[l].astype(BF16),
               s5_b_glu[l][None])
        n_tiles = n_lat_tiles if last else n_lat_tiles + n_ctx // ROW_TILE
        x_mid = _merge(xc, mods, g1, y_hy, y_gla, ys, u_t, wts, n_tiles, n_lat_tiles)
        wf = w_ffn_in[l]
        xc = _ffn(x_mid, mods, g_norm2[l][None], wf[:, :d_ff].astype(BF16), wf[:, d_ff:].astype(BF16),
                  w_ffn_out[l].astype(BF16), g_final[None], n_lat_tiles, last)
    return xc
```

```python
import functools
import math

import jax
import jax.numpy as jnp
import numpy as np
from jax import lax
from jax.experimental import pallas as pl
from jax.experimental.pallas import tpu as pltpu

F32 = jnp.float32
BF16 = jnp.bfloat16
EPS = 1e-6

HY_SHORT = 3
HY_POS_BANDS = 16
HY_N_FILT = 4
GLA_HEADS = 4
GLA_DK = 64
GLA_DV = 128
GLA_QK = GLA_HEADS * GLA_DK
GLA_V = GLA_HEADS * GLA_DV
GLA_LOWRANK = 16
GLA_TAU = 16.0
GLA_CHUNK = 64
S5_STATE = 64

LANES = 128
SUBLANES = 8
VMEM_LIMIT = 56 * 1024 * 1024

GLA_BLOCK = 4
A_PAD = LANES
ROW_TILE = 256


def _const_spec(shape):
    nd = len(shape)
    return pl.BlockSpec(shape, lambda *_: (0,) * nd, pipeline_mode=pl.Buffered(1))


def _params(n_axes):
    return pltpu.CompilerParams(dimension_semantics=("parallel",) * n_axes, vmem_limit_bytes=VMEM_LIMIT)


def _mod_kernel(c_ref, w_ref, b_ref, o_ref):
    c = c_ref[...]
    s = c * jax.nn.sigmoid(c)
    o_ref[0] = jnp.dot(s.astype(BF16), w_ref[0], preferred_element_type=F32) + b_ref[0]


def _modulation_all(cond, w_mod, b_mod):
    depth, d, n = w_mod.shape
    r = cond.shape[0]
    tn = 1536
    return pl.pallas_call(
        _mod_kernel,
        grid=(depth, n // tn),
        in_specs=[
            pl.BlockSpec((r, d), lambda l, j: (0, 0)),
            pl.BlockSpec((1, d, tn), lambda l, j: (l, 0, j)),
            pl.BlockSpec((1, 1, tn), lambda l, j: (l, 0, j)),
        ],
        out_specs=pl.BlockSpec((1, r, tn), lambda l, j: (l, 0, j)),
        out_shape=jax.ShapeDtypeStruct((depth, r, n), F32),
        compiler_params=_params(2),
        name="modulation",
    )(cond, w_mod.astype(BF16), b_mod.reshape(depth, 1, n))


def _norm_mod(x, g, shift, scale):
    y = x * lax.rsqrt(jnp.mean(x * x, axis=-1, keepdims=True) + EPS)
    return (y * g) * (1.0 + scale) + shift


_NT = (((1,), (1,)), ((), ()))


def _proj_kernel(col_sizes, d, x_ref, mod_ref, g_ref, w_ref, wt_ref, *o_refs):
    m = mod_ref[0, 0]
    h = _norm_mod(x_ref[0], g_ref[...], m[:, 0:d], m[:, d:2 * d]).astype(BF16)
    off = 0
    for o_ref, n in zip(o_refs[:-1], col_sizes):
        o_ref[0] = jnp.dot(h, w_ref[:, off:off + n], preferred_element_type=F32).astype(o_ref.dtype)
        off += n
    o_refs[-1][0] = lax.dot_general(wt_ref[...], h, _NT, preferred_element_type=F32)


def _mod_spec(d, n_lat_tiles):
    return pl.BlockSpec((1, 1, 1, 6 * d), lambda i, j: (i, (j < n_lat_tiles).astype(jnp.int32), 0, 0))


def _project(xc, mods, g, w, col_sizes, w_t, n_lat_tiles):
    b, t, d = xc.shape
    tm = ROW_TILE
    n_tot = sum(col_sizes)
    n_t = w_t.shape[0]
    out_shape = [jax.ShapeDtypeStruct((b, t, n), BF16) for n in col_sizes] + [jax.ShapeDtypeStruct((b, n_t, t), F32)]
    out_specs = [pl.BlockSpec((1, tm, n), lambda i, j: (i, j, 0)) for n in col_sizes]
    out_specs.append(pl.BlockSpec((1, n_t, tm), lambda i, j: (i, 0, j)))
    return pl.pallas_call(
        functools.partial(_proj_kernel, tuple(col_sizes), d),
        grid=(b, t // tm),
        in_specs=[
            pl.BlockSpec((1, tm, d), lambda i, j: (i, j, 0)),
            _mod_spec(d, n_lat_tiles),
            _const_spec((1, d)),
            _const_spec((d, n_tot)),
            _const_spec((n_t, d)),
        ],
        out_specs=out_specs,
        out_shape=out_shape,
        compiler_params=_params(2),
        name="in_proj",
    )(xc, mods, g, w, w_t)


def _gelu_tanh(x):
    return 0.5 * x * (1.0 + jnp.tanh(math.sqrt(2.0 / math.pi) * (x + 0.044715 * (x * x * x))))


def _merge_kernel(d, x_ref, mod_ref, g_ref, yhy_ref, ygla_ref, ys5_ref, u_ref, wg_ref, whb_ref, wgb_ref, wsb_ref,
                  wo_ref, s5d_ref, wglu_ref, bglu_ref, o_ref):
    x = x_ref[0]
    m = mod_ref[0, 0]
    h = _norm_mod(x, g_ref[...], m[:, 0:d], m[:, d:2 * d]).astype(BF16)
    y5 = (ys5_ref[0] + s5d_ref[...] * u_ref[0]).T
    g5 = _gelu_tanh(y5)
    y_s5 = g5 * jax.nn.sigmoid(jnp.dot(g5.astype(BF16), wglu_ref[...], preferred_element_type=F32) + bglu_ref[...])

    def branch(k, y, wb_ref):
        gate = jnp.dot(h, wg_ref[:, k * d:(k + 1) * d], preferred_element_type=F32)
        return jax.nn.sigmoid(gate) * jnp.dot(y, wb_ref[...], preferred_element_type=F32)

    mix = branch(0, yhy_ref[0], whb_ref) + branch(1, ygla_ref[0], wgb_ref) + branch(2, y_s5.astype(BF16), wsb_ref)
    out = jnp.dot(mix.astype(BF16), wo_ref[...], preferred_element_type=F32)
    o_ref[0] = x + m[:, 2 * d:3 * d] * out


def _merge(xc, mods, g, y_hy, y_gla, ys_s5, u_s5, wts, nj, n_lat_tiles):
    b, t, d = xc.shape
    tm = ROW_TILE
    ch = y_hy.shape[-1]
    row = lambda n: pl.BlockSpec((1, tm, n), lambda i, j: (i, j, 0))
    col = lambda n: pl.BlockSpec((1, n, tm), lambda i, j: (i, 0, j))
    w_gate, w_hy, w_gla, w_s5, w_out, s5_d, w_glu, b_glu = wts
    return pl.pallas_call(
        functools.partial(_merge_kernel, d),
        grid=(b, nj),
        in_specs=[
            row(d),
            _mod_spec(d, n_lat_tiles),
            _const_spec((1, d)),
            row(ch), row(ch), col(ch), col(ch),
            _const_spec(w_gate.shape), _const_spec(w_hy.shape), _const_spec(w_gla.shape), _const_spec(w_s5.shape),
            _const_spec(w_out.shape), _const_spec(s5_d.shape), _const_spec(w_glu.shape), _const_spec(b_glu.shape),
        ],
        out_specs=pl.BlockSpec((1, tm, d), lambda i, j: (i, j, 0)),
        out_shape=jax.ShapeDtypeStruct((b, nj * tm, d), F32),
        compiler_params=_params(2),
        name="merge",
    )(xc, mods, g, y_hy, y_gla, ys_s5, u_s5, w_gate, w_hy, w_gla, w_s5, w_out, s5_d, w_glu, b_glu)


def _ffn_kernel(d, final, x_ref, mod_ref, g_ref, wa_ref, wb_ref, wo_ref, gf_ref, o_ref):
    x = x_ref[0]
    m = mod_ref[0, 0]
    h = _norm_mod(x, g_ref[...], m[:, 3 * d:4 * d], m[:, 4 * d:5 * d]).astype(BF16)
    a = jnp.dot(h, wa_ref[...], preferred_element_type=F32)
    bb = jnp.dot(h, wb_ref[...], preferred_element_type=F32)
    act = (a * jax.nn.sigmoid(a) * bb).astype(BF16)
    y = x + m[:, 5 * d:6 * d] * jnp.dot(act, wo_ref[...], preferred_element_type=F32)
    if final:
        y = y * lax.rsqrt(jnp.mean(y * y, axis=-1, keepdims=True) + EPS) * gf_ref[...]
    o_ref[0] = y


def _ffn(x, mods, g, wa, wb, wo, g_final, n_lat_tiles, final):
    b, t, d = x.shape
    tm = ROW_TILE
    return pl.pallas_call(
        functools.partial(_ffn_kernel, d, final),
        grid=(b, t // tm),
        in_specs=[
            pl.BlockSpec((1, tm, d), lambda i, j: (i, j, 0)),
            _mod_spec(d, n_lat_tiles),
            _const_spec((1, d)),
            _const_spec(wa.shape), _const_spec(wb.shape), _const_spec(wo.shape),
            _const_spec((1, d)),
        ],
        out_specs=pl.BlockSpec((1, tm, d), lambda i, j: (i, j, 0)),
        out_shape=jax.ShapeDtypeStruct((b, t, d), F32),
        compiler_params=_params(2),
        name="ffn",
    )(x, mods, g, wa, wb, wo, g_final)


S5_CHUNK = LANES


def _s5_weights(a_re, a_im, log_dt, b_re, b_im, c_re, c_im, tc):
    g, p = a_re.shape[1:]
    i = b_re.shape[-1]
    lam_c = lax.complex(jnp.minimum(a_re, -1e-4), a_im)
    lam_dt = lam_c * jnp.exp(log_dt)[..., None]
    b_bar = ((jnp.exp(lam_dt) - 1.0) / lam_c)[..., None] * lax.complex(b_re, b_im)[None]
    c_mat = lax.complex(c_re, c_im)
    tau = jnp.arange(tc, dtype=F32)
    pw = jnp.exp(lam_dt[:, :, None, :] * tau[None, None, :, None])
    pw1 = pw * jnp.exp(lam_dt)[:, :, None, :]
    kern = jnp.einsum('gip,dgtp,dgpj->dgtij', c_mat, pw, b_bar).real
    k_lag = jnp.concatenate([kern[1][:, :0:-1], kern[0][:, :1] + kern[1][:, :1], kern[0][:, 1:]], axis=1)
    kr = jnp.pad(k_lag.transpose(0, 3, 2, 1), ((0, 0), (0, 0), (0, 0), (0, 1)))
    kr = kr.reshape(g, i, i * 2 * tc)

    inc_f = pw[0][:, None, ::-1, :] * b_bar[0].transpose(0, 2, 1)[:, :, None]
    inc_b = pw[1][:, None, :, :] * b_bar[1].transpose(0, 2, 1)[:, :, None]
    bm = jnp.concatenate([inc_f.real, inc_b.real, inc_f.imag, inc_b.imag], axis=-1).reshape(g, tc * i, 4 * p)

    out_f = c_mat.transpose(0, 2, 1)[..., None] * pw1[0].transpose(0, 2, 1)[:, :, None, :]
    out_b = c_mat.transpose(0, 2, 1)[..., None] * pw1[1][:, ::-1].transpose(0, 2, 1)[:, :, None, :]
    z = jnp.zeros((g, p, tc * i), F32)
    fl = lambda t: t.reshape(g, p, tc * i)
    cm = jnp.concatenate([fl(out_f.real), z, fl(-out_f.imag), z, z, fl(out_b.real), z, fl(-out_b.imag)], axis=1)

    lam_t = jnp.exp(lam_dt * float(tc))
    lam = jnp.stack([jnp.concatenate([lam_t[0].real, lam_t[1].real], -1),
                     jnp.concatenate([lam_t[0].imag, lam_t[1].imag], -1)], axis=1)
    return kr, bm.astype(BF16), cm.astype(BF16), lam


def _s5_kernel(nc, nc_ctx, tc, u_ref, kr_ref, bm_ref, cm_ref, lam_ref, y_ref, up_ref, dx_ref, p_ref, m_ref):
    bn, i_sz = u_ref.shape[:2]
    nl = nc - nc_ctx
    width = kr_ref.shape[-1]
    seg = width // i_sz
    for j in range(i_sz):
        lag_rows = pltpu.roll(jnp.broadcast_to(kr_ref[0, j:j + 1, :], (tc, width)), width - (tc - 1), axis=1,
                              stride=1, stride_axis=0)
        m_ref[j * tc:(j + 1) * tc, :] = jnp.concatenate(
            [lag_rows[:, i * seg:i * seg + tc] for i in range(i_sz)], axis=1).astype(BF16)
    for n in range(nc):
        for i in range(i_sz):
            up_ref[n * bn:(n + 1) * bn, i * tc:(i + 1) * tc] = u_ref[:, i, n * tc:(n + 1) * tc]
    u = up_ref[...].astype(BF16)
    dx_ref[...] = jnp.dot(u, bm_ref[0], preferred_element_type=F32)
    lam = lam_ref[0]
    lr, li = lam[0:1], lam[1:2]
    w = lam.shape[-1]
    half = w // 2
    is_f = lax.broadcasted_iota(jnp.int32, (bn, w), 1) < half

    def step(s, carry):
        sr, si = carry
        nf = jnp.where(s < nc_ctx, nl + s, s - nc_ctx)
        nb = nc - 1 - s
        rf = pl.multiple_of(nf * bn, bn)
        rb = pl.multiple_of(nb * bn, bn)
        p_ref[pl.ds(rf, bn), 0:w] = sr
        p_ref[pl.ds(rf, bn), w:2 * w] = si
        p_ref[pl.ds(rb, bn), 2 * w:3 * w] = sr
        p_ref[pl.ds(rb, bn), 3 * w:4 * w] = si
        d_re = jnp.where(is_f, dx_ref[pl.ds(rf, bn), 0:w], dx_ref[pl.ds(rb, bn), 0:w])
        d_im = jnp.where(is_f, dx_ref[pl.ds(rf, bn), w:2 * w], dx_ref[pl.ds(rb, bn), w:2 * w])
        return lr * sr - li * si + d_re, lr * si + li * sr + d_im

    zero = jnp.zeros((bn, 2 * half), F32)
    lax.fori_loop(0, nc, step, (zero, zero))
    y = jnp.dot(u, m_ref[...], preferred_element_type=F32)
    y = y + jnp.dot(p_ref[...].astype(BF16), cm_ref[0], preferred_element_type=F32)
    for n in range(nc):
        for i in range(i_sz):
            y_ref[:, i, n * tc:(n + 1) * tc] = y[n * bn:(n + 1) * bn, i * tc:(i + 1) * tc]


def _s5_mix(u_t, n_ctx, wts, layer):
    kr, bm, cm, lam = wts
    bn, ch, t = u_t.shape
    g, i = kr.shape[1:3]
    tc = kr.shape[3] // (2 * i)
    assert 2 * S5_STATE == LANES and bn % SUBLANES == 0 and n_ctx % tc == 0 and t % tc == 0 and tc % LANES == 0
    nc, nc_ctx = t // tc, n_ctx // tc
    k = tc * i
    r = nc * bn
    return pl.pallas_call(
        functools.partial(_s5_kernel, nc, nc_ctx, tc),
        grid=(g,),
        in_specs=[
            pl.BlockSpec((bn, i, t), lambda j: (0, j, 0)),
            pl.BlockSpec((None, 1, i, 2 * k), lambda j: (layer, j, 0, 0)),
            pl.BlockSpec((None, 1, k, 4 * S5_STATE), lambda j: (layer, j, 0, 0)),
            pl.BlockSpec((None, 1, 8 * S5_STATE, k), lambda j: (layer, j, 0, 0)),
            pl.BlockSpec((None, 1, 2, 2 * S5_STATE), lambda j: (layer, j, 0, 0)),
        ],
        out_specs=pl.BlockSpec((bn, i, t), lambda j: (0, j, 0)),
        out_shape=jax.ShapeDtypeStruct((bn, ch, t), F32),
        scratch_shapes=[pltpu.VMEM((r, k), F32), pltpu.VMEM((r, 4 * S5_STATE), F32),
                        pltpu.VMEM((r, 8 * S5_STATE), F32), pltpu.VMEM((k, k), BF16)],
        compiler_params=_params(1),
        name="s5_mix",
    )(u_t, kr, bm, cm, lam)


def _gla_kernel(nc, nc_ctx, k_ref, v_ref, a_ref, q_ref, r_ref, wa_ref, ba_ref, gn_ref, y_ref, of_ref, ob_ref,
                s_ref):
    c, h_n, dk, dv = GLA_CHUNK, GLA_HEADS, GLA_DK, GLA_DV
    qk = h_n * dk
    q_scale = dk ** -0.5
    nbk = GLA_BLOCK
    rb = nbk * c
    row_i = lax.broadcasted_iota(jnp.int32, (rb, rb), 0)
    col_i = lax.broadcasted_iota(jnp.int32, (rb, rb), 1)
    same = (row_i // c) == (col_i // c)
    causal = (same & (row_i >= col_i), same & (row_i <= col_i))
    cum_ops = [m.astype(BF16) for m in causal]
    att_mask = [jnp.concatenate([m] * h_n, axis=0) for m in causal]
    lane_head = lax.broadcasted_iota(jnp.int32, (1, qk), 1) // dk
    head_lanes = [lane_head == h for h in range(h_n)]

    def by_head(x):
        zero = jnp.zeros_like(x)
        return jnp.concatenate([jnp.where(m, x, zero) for m in head_lanes], axis=0)

    def block(j, d):
        rows = pl.ds(pl.multiple_of(j * rb, rb), rb)
        k = k_ref[0, rows, :].astype(F32)
        q = q_ref[0, rows, :].astype(F32) * q_scale
        v = v_ref[0, rows, :]
        z = jnp.dot(a_ref[0, rows, :], wa_ref[:, d * qk:(d + 1) * qk], preferred_element_type=F32) + ba_ref[d:d + 1, :]
        log_a = (jnp.minimum(z, 0.0) - jnp.log(1.0 + jnp.exp(-jnp.abs(z)))) * (1.0 / GLA_TAU)
        hi = log_a.astype(BF16)
        lo = (log_a - hi.astype(F32)).astype(BF16)
        cs = jnp.dot(cum_ops[d], jnp.concatenate([hi, lo], axis=1), preferred_element_type=F32)
        b = cs[:, :qk] + cs[:, qk:]
        b_tot = jnp.concatenate(
            [jnp.broadcast_to(b[n * c + (c - 1 if d == 0 else 0)][None], (c, qk)) for n in range(nbk)], axis=0)
        qd = (q * jnp.exp(b)).astype(BF16)
        kd = (k * jnp.exp(-b)).astype(BF16)
        kl = (k * jnp.exp(b_tot - b)).astype(BF16)
        qm = by_head(qd)
        intra = []
        for h in range(h_n):
            att = lax.dot_general(qm[h * rb:(h + 1) * rb], kd, _NT, preferred_element_type=F32)
            att = jnp.where(causal[d], att, 0.0).astype(BF16)
            intra.append(jnp.dot(att, v[:, h * dv:(h + 1) * dv], preferred_element_type=F32))
        s_t = s_ref[d]
        inter = [None] * nbk
        for n in (range(nbk) if d == 0 else reversed(range(nbk))):
            r0 = n * c
            qm_n = jnp.concatenate([qm[h * rb + r0:h * rb + r0 + c] for h in range(h_n)], axis=0)
            inter[n] = lax.dot_general(qm_n, s_t.astype(BF16), _NT, preferred_element_type=F32)
            v_t = jnp.concatenate(
                [jnp.concatenate([v[r0:r0 + c, h * dv:(h + 1) * dv] for h in range(p, p + dv // c)], axis=0).T
                 for p in range(0, h_n, dv // c)], axis=1)
            s_t = s_t * jnp.exp(b_tot[r0:r0 + 1]) + jnp.dot(v_t, by_head(kl[r0:r0 + c]), preferred_element_type=F32)
        s_ref[d] = s_t
        o = jnp.concatenate(
            [intra[h] + jnp.concatenate([inter[n][h * c:(h + 1) * c] for n in range(nbk)], axis=0)
             for h in range(h_n)], axis=1)
        return rows, o

    def scan_step(i, carry, lo_block, hi_block):
        rows_f, o_f = block(lo_block + i, 0)
        of_ref[rows_f, :] = o_f
        rows_b, o_b = block(hi_block - 1 - i, 1)
        ob_ref[rows_b, :] = o_b
        return carry

    def readout(n, carry):
        rows = pl.ds(pl.multiple_of(n * c, c), c)
        o = of_ref[rows, :] + ob_ref[rows, :]
        r = r_ref[0, rows, :].astype(F32)
        gate = r * jax.nn.sigmoid(r)
        outs = []
        for h in range(h_n):
            oh = o[:, h * dv:(h + 1) * dv]
            oh = oh * lax.rsqrt(jnp.mean(oh * oh, axis=-1, keepdims=True) + EPS) * gn_ref[...]
            outs.append(oh * gate[:, h * dv:(h + 1) * dv])
        y_ref[0, rows, :] = jnp.concatenate(outs, axis=1).astype(y_ref.dtype)
        return carry

    nl, n_ctx = (nc - nc_ctx) // nbk, nc_ctx // nbk
    s_ref[...] = jnp.zeros(s_ref.shape, F32)
    lax.fori_loop(0, n_ctx, functools.partial(scan_step, lo_block=nl, hi_block=nl + n_ctx), 0)
    lax.fori_loop(0, nl, functools.partial(scan_step, lo_block=0, hi_block=nl), 0)
    lax.fori_loop(0, nc, readout, 0, unroll=2)


def _gla_mix(k, v, a, q, r, wa, ba, gnorm, n_ctx):
    bn, t, qk = k.shape
    vd = v.shape[-1]
    c = GLA_CHUNK
    assert t % (c * GLA_BLOCK) == 0 and n_ctx % (c * GLA_BLOCK) == 0 and GLA_DV % c == 0
    seq = lambda n, bufs=2: pl.BlockSpec((1, t, n), lambda i: (i, 0, 0), pipeline_mode=pl.Buffered(bufs))
    return pl.pallas_call(
        functools.partial(_gla_kernel, t // c, n_ctx // c),
        grid=(bn,),
        in_specs=[seq(qk), seq(vd), seq(a.shape[-1]), seq(qk), seq(vd, 1),
                  _const_spec(wa.shape), _const_spec(ba.shape), _const_spec(gnorm.shape)],
        out_specs=pl.BlockSpec((1, t, vd), lambda i: (i, 0, 0)),
        out_shape=jax.ShapeDtypeStruct((bn, t, vd), BF16),
        scratch_shapes=[pltpu.VMEM((t, vd), F32), pltpu.VMEM((t, vd), F32), pltpu.VMEM((2, GLA_DV, qk), F32)],
        compiler_params=_params(1),
        name="gla_mix",
    )(k, v, a, q, r, wa, ba, gnorm)


HY_NB = LANES
HY_XPITCH = HY_NB + SUBLANES
HY_APITCH = 2 * HY_NB + SUBLANES
HY_CT = LANES
HY_UNROLL_OUTER = 16
HY_UNROLL_INNER = 8


def _hyena_filters(n, params, reverse):
    w1, b1, w2, b2, w3, freq1, freq2, decay = (p.astype(F32) for p in params)
    pos = jnp.arange(n, dtype=F32)
    t = jnp.linspace(0.0, 1.0, n, dtype=F32)
    if reverse:
        pos, t = (n - 1) - pos, jnp.linspace(1.0, 0.0, n, dtype=F32)
    t = t[:, None]
    bands = jnp.linspace(1e-4, HY_POS_BANDS - 1, HY_POS_BANDS, dtype=F32)
    ang = (2.0 * math.pi / n) * pos[:, None] * bands[None]
    z = jnp.concatenate([t, jnp.cos(ang), -jnp.sin(ang)], axis=-1)
    hid = jnp.sin(freq1 * (z @ w1 + b1))
    hid = jnp.sin(freq2 * (hid @ w2 + b2))
    w3 = w3.reshape(w3.shape[0], HY_N_FILT, -1)
    window = jnp.exp(-t[None] * jnp.abs(decay).reshape(HY_N_FILT, 1, -1))
    return jnp.einsum('nk,kfc->fnc', hid, w3) * window


def _hyena_kernels(n, params):
    h_fwd = _hyena_filters(n, params, False)[0::2]
    h_bwd_rev = _hyena_filters(n, params, True)[1::2]
    k = jnp.concatenate([h_fwd[:, :1] + h_bwd_rev[:, n - 1:], h_fwd[:, 1:], jnp.zeros_like(h_fwd[:, :1]),
                         h_bwd_rev[:, :n - 1]], axis=1)
    return k * lax.rsqrt(jnp.sum(k * k, axis=1, keepdims=True) + EPS)


def _spectrum_kernel(na, k_ref, f1_ref, f2_ref, o_ref, a2):
    nb, ap = HY_NB, HY_APITCH

    def stage1(b, c):
        xs = k_ref[0, pl.ds(b, na, stride=nb), :].astype(BF16)
        r = jnp.dot(f1_ref[b], xs, preferred_element_type=F32)
        a2[pl.ds(b, na, stride=ap), :] = r[:na]
        a2[pl.ds(nb + b, na, stride=ap), :] = r[na:]
        return c

    def stage2(ka, c):
        rows = pl.ds(pl.multiple_of(ka * ap, 8), 2 * nb)
        xf = jnp.dot(f2_ref[...], a2[rows, :].astype(BF16), preferred_element_type=F32)
        o_ref[0, ka, 0] = xf[:nb].astype(o_ref.dtype)
        o_ref[0, ka, 1] = xf[nb:].astype(o_ref.dtype)
        return c

    lax.fori_loop(0, nb, stage1, 0, unroll=HY_UNROLL_INNER)
    lax.fori_loop(0, na, stage2, 0, unroll=HY_UNROLL_INNER)


def _hyena_spectra(kernels):
    nf, n, ch = kernels.shape
    nb, ct = HY_NB, HY_CT
    na = n // nb
    _, f2, _, _, f1_full = _dft_tables(n // 2)
    return pl.pallas_call(
        functools.partial(_spectrum_kernel, na),
        grid=(nf, ch // ct),
        in_specs=[pl.BlockSpec((1, n, ct), lambda f, j: (f, 0, j)), _const_spec(f1_full.shape), _const_spec(f2.shape)],
        out_specs=pl.BlockSpec((1, na, 2, nb, ct), lambda f, j: (f, 0, 0, 0, j)),
        out_shape=jax.ShapeDtypeStruct((nf, na, 2, nb, ch), BF16),
        scratch_shapes=[pltpu.VMEM((na * HY_APITCH, ct), F32)],
        compiler_params=_params(2),
        name="hyena_spectrum",
    )(kernels, f1_full, f2)


def _cplx_block(m):
    return np.block([[m.real, -m.imag], [m.imag, m.real]])


@functools.lru_cache(maxsize=None)
def _dft_tables(n_seq):
    nb = HY_NB
    n = 2 * n_seq
    na = n // nb
    ha = na // 2
    ka = np.arange(na)[:, None]
    b = np.arange(nb)
    tw = np.exp(-2j * np.pi * ka * b[None, :] / n)
    w_a = np.exp(-2j * np.pi * ka * np.arange(ha)[None, :] / na)
    f1 = np.stack([_cplx_block(tw[:, i:i + 1] * w_a) for i in range(nb)])
    g = np.stack([_cplx_block((np.conj(tw[:, i:i + 1] * w_a)).T / n) for i in range(nb)])
    w_b = np.exp(-2j * np.pi * b[:, None] * b[None, :] / nb)
    f2 = _cplx_block(w_b)
    f2i = _cplx_block(np.conj(w_b))
    w_full = np.exp(-2j * np.pi * ka * np.arange(na)[None, :] / na)
    f1_full = np.stack([np.concatenate([(tw[:, i:i + 1] * w_full).real, (tw[:, i:i + 1] * w_full).imag], axis=0)
                        for i in range(nb)])
    return tuple(jnp.asarray(t, dtype=BF16) for t in (f1, f2, f2i, g, f1_full))


def _hyena_kernel(na, x1_ref, x2_ref, v_ref, cw_ref, bias_ref, kf1_ref, kf2_ref, f1_ref, f2_ref, f2i_ref, g_ref,
                  o_ref, rawf, vp, gp, yp, a2):
    nb, xp, ap = HY_NB, HY_XPITCH, HY_APITCH
    ha = na // 2
    n_seq = ha * nb
    ct = o_ref.shape[-1]
    halo = SUBLANES
    rawf[:, 0:halo, :] = jnp.zeros((2, halo, ct), F32)
    rawf[:, halo + n_seq:2 * halo + n_seq, :] = jnp.zeros((2, halo, ct), F32)

    def short_conv_into(raw_ref, which, dst):
        w0, w1, w2 = (cw_ref[j, which:which + 1, :] for j in range(HY_SHORT))
        for e in range(2):
            rawf[e, halo:halo + n_seq, :] = raw_ref[e].astype(F32)

            def body(a, c):
                base = pl.multiple_of(a * nb, nb) + halo
                u = (w0 * rawf[e, pl.ds(base - 1, nb), :] + w1 * rawf[e, pl.ds(base, nb), :]
                     + w2 * rawf[e, pl.ds(base + 1, nb), :])
                dst[e, pl.ds(pl.multiple_of(a * xp, 8), nb), :] = u
                return c

            lax.fori_loop(0, ha, body, 0)

    def long_conv(kf_ref):
        def stage1(b, c):
            xs = jnp.concatenate([vp[0, pl.ds(b, ha, stride=xp), :], vp[1, pl.ds(b, ha, stride=xp), :]], axis=0)
            r = jnp.dot(f1_ref[b], xs.astype(BF16), preferred_element_type=F32)
            a2[pl.ds(b, na, stride=ap), :] = r[:na]
            a2[pl.ds(nb + b, na, stride=ap), :] = r[na:]
            return c

        def stage2(kp, c):
            rows = [pl.ds(pl.multiple_of((2 * kp + e) * ap, 8), 2 * nb) for e in range(2)]
            slab = jnp.concatenate([a2[r, :] for r in rows], axis=1).astype(BF16)
            xf = jnp.dot(f2_ref[...], slab, preferred_element_type=F32)
            xr, xi = xf[:nb], xf[nb:]
            kr = jnp.concatenate([kf_ref[0, 2 * kp + e, 0] for e in range(2)], axis=1).astype(F32)
            ki = jnp.concatenate([kf_ref[0, 2 * kp + e, 1] for e in range(2)], axis=1).astype(F32)
            z = jnp.concatenate([xr * kr - xi * ki, xr * ki + xi * kr], axis=0).astype(BF16)
            back = jnp.dot(f2i_ref[...], z, preferred_element_type=F32)
            for e in range(2):
                a2[rows[e], :] = back[:, e * ct:(e + 1) * ct]
            return c

        def stage3(b, c):
            s = jnp.concatenate([a2[pl.ds(b, na, stride=ap), :], a2[pl.ds(nb + b, na, stride=ap), :]], axis=0)
            y = jnp.dot(g_ref[b], s.astype(BF16), preferred_element_type=F32)
            yp[0, pl.ds(b, ha, stride=xp), :] = y[:ha]
            yp[1, pl.ds(b, ha, stride=xp), :] = y[ha:]
            return c

        lax.fori_loop(0, nb, stage1, 0, unroll=HY_UNROLL_OUTER)
        lax.fori_loop(0, na // 2, stage2, 0, unroll=HY_UNROLL_INNER)
        lax.fori_loop(0, nb, stage3, 0, unroll=HY_UNROLL_OUTER)

    def gated(bias_row, write):
        for e in range(2):
            def body(a, c):
                rows = pl.ds(pl.multiple_of(a * xp, 8), nb)
                write(e, a, rows, gp[e, rows, :] * (yp[e, rows, :] + bias_row * vp[e, rows, :]))
                return c

            lax.fori_loop(0, ha, body, 0)

    def write_z(e, a, rows, val):
        vp[e, rows, :] = val

    def write_out(e, a, rows, val):
        o_ref[e, pl.ds(pl.multiple_of(a * nb, nb), nb), :] = val.astype(o_ref.dtype)

    short_conv_into(v_ref, 2, vp)
    short_conv_into(x1_ref, 0, gp)
    long_conv(kf1_ref)
    gated(bias_ref[0:1, :], write_z)
    short_conv_into(x2_ref, 1, gp)
    long_conv(kf2_ref)
    gated(bias_ref[1:2, :], write_out)


def _hyena_latent(hy_all, n_seq, t_out, conv_w, bias, kf):
    bn = hy_all.shape[0]
    ch = hy_all.shape[-1] // 3
    ct = HY_CT
    nct = ch // ct
    nb = HY_NB
    na = 2 * n_seq // nb
    assert bn % 2 == 0 and ch % ct == 0 and n_seq % (8 * nb) == 0
    f1, f2, f2i, g, _ = _dft_tables(n_seq)
    col = lambda which: pl.BlockSpec((2, n_seq, ct), lambda j, p: (p, 0, which * nct + j))
    kf_spec = lambda f: pl.BlockSpec((1, na, 2, nb, ct), lambda j, p: (f, 0, 0, 0, j), pipeline_mode=pl.Buffered(1))
    pad_rows = (na // 2) * HY_XPITCH
    return pl.pallas_call(
        functools.partial(_hyena_kernel, na),
        grid=(nct, bn // 2),
        in_specs=[col(0), col(1), col(2),
                  pl.BlockSpec((HY_SHORT, 3, ct), lambda j, p: (0, 0, j)),
                  pl.BlockSpec((2, ct), lambda j, p: (0, j)),
                  kf_spec(0), kf_spec(1),
                  _const_spec(f1.shape), _const_spec(f2.shape), _const_spec(f2i.shape), _const_spec(g.shape)],
        out_specs=pl.BlockSpec((2, n_seq, ct), lambda j, p: (p, 0, j)),
        out_shape=jax.ShapeDtypeStruct((bn, t_out, ch), BF16),
        scratch_shapes=[pltpu.VMEM((2, n_seq + 2 * SUBLANES, ct), F32), pltpu.VMEM((2, pad_rows, ct), F32),
                        pltpu.VMEM((2, pad_rows, ct), F32), pltpu.VMEM((2, pad_rows, ct), F32),
                        pltpu.VMEM((na * HY_APITCH, ct), F32)],
        compiler_params=_params(2),
        name="hyena_latent",
    )(hy_all, hy_all, hy_all, conv_w.reshape(HY_SHORT, 3, ch), bias, kf, kf, f1, f2, f2i, g)


@functools.lru_cache(maxsize=None)
def _dense_dft_tables(n_seq):
    n = 2 * n_seq
    ang = 2.0 * np.pi * np.arange(n)[:, None] * np.arange(n_seq)[None, :] / n
    fd = np.concatenate([np.cos(ang), -np.sin(ang)], axis=0)
    gd = np.concatenate([np.cos(ang).T, -np.sin(ang).T], axis=1) / n
    return jnp.asarray(fd, dtype=BF16), jnp.asarray(gd, dtype=BF16)


def _hyena_ctx_kernel(n_seq, x1_ref, x2_ref, v_ref, cw_ref, bias_ref, kf1_ref, kf2_ref, fd_ref, gd_ref, buf_ref,
                      o_ref):
    del buf_ref
    n = 2 * n_seq

    def short_conv(raw_ref, which):
        x = raw_ref[0].astype(F32)
        zero = jnp.zeros((1, x.shape[-1]), F32)
        prev = jnp.concatenate([zero, x[:-1]], axis=0)
        nxt = jnp.concatenate([x[1:], zero], axis=0)
        return cw_ref[0, which:which + 1, :] * prev + cw_ref[1, which:which + 1, :] * x + cw_ref[2, which:which + 1, :] * nxt

    def long_conv(u, kf_ref, bias_row):
        xf = jnp.dot(fd_ref[...], u.astype(BF16), preferred_element_type=F32)
        xr, xi = xf[:n], xf[n:]
        kr, ki = kf_ref[0], kf_ref[1]
        z = jnp.concatenate([xr * kr - xi * ki, xr * ki + xi * kr], axis=0).astype(BF16)
        return jnp.dot(gd_ref[...], z, preferred_element_type=F32) + bias_row * u

    x1, x2, v = short_conv(x1_ref, 0), short_conv(x2_ref, 1), short_conv(v_ref, 2)
    z = x1 * long_conv(v, kf1_ref, bias_ref[0:1, :])
    o_ref[0] = (x2 * long_conv(z, kf2_ref, bias_ref[1:2, :])).astype(o_ref.dtype)


def _hyena_context(hy_all, y_buf, row_block, n_seq, conv_w, bias, kf1, kf2):
    bn = hy_all.shape[0]
    ch = hy_all.shape[-1] // 3
    fd, gd = _dense_dft_tables(n_seq)
    col = lambda which: pl.BlockSpec((1, n_seq, ch), lambda i: (i, row_block, which))
    split = lambda kf: jnp.stack([kf.real, kf.imag], axis=0)
    return pl.pallas_call(
        functools.partial(_hyena_ctx_kernel, n_seq),
        grid=(bn,),
        in_specs=[col(0), col(1), col(2), _const_spec((HY_SHORT, 3, ch)), _const_spec((2, ch)),
                  _const_spec((2, 2 * n_seq, ch)), _const_spec((2, 2 * n_seq, ch)),
                  _const_spec(fd.shape), _const_spec(gd.shape), pl.BlockSpec(memory_space=pl.ANY)],
        out_specs=pl.BlockSpec((1, n_seq, ch), lambda i: (i, row_block, 0)),
        out_shape=jax.ShapeDtypeStruct(y_buf.shape, y_buf.dtype),
        input_output_aliases={9: 0},
        compiler_params=_params(1),
        name="hyena_context",
    )(hy_all, hy_all, hy_all, conv_w.reshape(HY_SHORT, 3, ch), bias, split(kf1), split(kf2), fd, gd, y_buf)


def kernel(x, c, ctx, c_ctx, w_mod, b_mod, g_norm1, g_norm2, w_in, hy_conv, hy_w1, hy_b1, hy_w2, hy_b2, hy_w3,
           hy_freq1, hy_freq2, hy_decay, hy_bias, gla_wa2, gla_ba, gla_gnorm, s5_a_re, s5_a_im, s5_log_dt, s5_b_re,
           s5_b_im, s5_c_re, s5_c_im, s5_d, s5_w_glu, s5_b_glu, w_br_hy, w_br_gla, w_br_s5, w_out, w_ffn_in,
           w_ffn_out, g_final):
    bn, n_lat, d = x.shape
    n_ctx = ctx.shape[1]
    depth = w_in.shape[0]
    d_hy = w_br_hy.shape[1]
    s5_ch = w_br_s5.shape[1]
    d_ff = w_ffn_out.shape[1]
    assert n_ctx == ROW_TILE and n_lat % ROW_TILE == 0
    n_lat_tiles = n_lat // ROW_TILE
    n_low = 2 * GLA_LOWRANK
    o_a = GLA_QK + GLA_V
    o_u = o_a + n_low
    o_q = o_u + s5_ch
    o_gate = o_q + GLA_QK + GLA_V + 3 * d_hy
    col_sizes = (GLA_QK, GLA_V, A_PAD, GLA_QK, GLA_V, 3 * d_hy)

    cond = jnp.concatenate([c_ctx[None], c], axis=0)
    cond = jnp.pad(cond, ((0, (-cond.shape[0]) % SUBLANES), (0, 0)))
    mod_all = _modulation_all(cond, w_mod, b_mod)

    s5w_all = jax.vmap(functools.partial(_s5_weights, tc=S5_CHUNK))(
        s5_a_re, s5_a_im, s5_log_dt, s5_b_re, s5_b_im, s5_c_re, s5_c_im)
    wa_all = jnp.zeros((depth, A_PAD, 2 * GLA_QK), F32)
    wa_all = wa_all.at[:, :GLA_LOWRANK, :GLA_QK].set(gla_wa2[:, 0]).at[:, GLA_LOWRANK:n_low, GLA_QK:].set(gla_wa2[:, 1])
    wa_all = wa_all.astype(BF16)

    xc = jnp.concatenate([x, ctx], axis=1)
    for l in range(depth):
        last = l == depth - 1
        mods = jnp.stack([jnp.broadcast_to(mod_all[l, 0], (bn, 6 * d)), mod_all[l, 1:1 + bn]], axis=1)
        mods = mods[:, :, None, :]
        wl = w_in[l]
        w_proj = jnp.concatenate(
            [wl[:, :o_u], jnp.zeros((d, A_PAD - n_low), F32), wl[:, o_q:o_gate]], axis=1).astype(BF16)
        g1 = g_norm1[l][None]
        w_u_t = wl[:, o_u:o_q].T.astype(BF16)
        k_a, v_a, a_a, q_a, r_a, hy_a, u_t = _project(xc, mods, g1, w_proj, col_sizes, w_u_t, n_lat_tiles)

        y_gla = _gla_mix(k_a, v_a, a_a, q_a, r_a, wa_all[l], gla_ba[l], gla_gnorm[l][None], n_ctx)

        ys = _s5_mix(u_t, n_ctx, s5w_all, l)

        hy_p = (hy_w1[l], hy_b1[l], hy_w2[l], hy_b2[l], hy_w3[l], hy_freq1[l], hy_freq2[l], hy_decay[l])
        y_hy = _hyena_latent(hy_a, n_lat, n_lat if last else n_lat + n_ctx, hy_conv[l], hy_bias[l],
                             _hyena_spectra(_hyena_kernels(n_lat, hy_p)))
        if not last:
            kf_ctx = jnp.fft.fft(_hyena_kernels(n_ctx, hy_p), axis=1)
            y_hy = _hyena_context(hy_a, y_hy, n_lat // n_ctx, n_ctx, hy_conv[l], hy_bias[l], kf_ctx[0], kf_ctx[1])

        wts = (wl[:, o_gate:].astype(BF16), w_br_hy[l].astype(BF16), w_br_gla[l].astype(BF16),
               w_br_s5[l].astype(BF16), w_out[l].astype(BF16), s5_d[l][:, None], s5_w_glu[l].astype(BF16),
               s5_b_glu[l][None])
        n_tiles = n_lat_tiles if last else n_lat_tiles + n_ctx // ROW_TILE
        x_mid = _merge(xc, mods, g1, y_hy, y_gla, ys, u_t, wts, n_tiles, n_lat_tiles)
        wf = w_ffn_in[l]
        xc = _ffn(x_mid, mods, g_norm2[l][None], wf[:, :d_ff].astype(BF16), wf[:, d_ff:].astype(BF16),
                  w_ffn_out[l].astype(BF16), g_final[None], n_lat_tiles, last)
    return xc
```

```python
import functools
import math

import jax
import jax.numpy as jnp
import numpy as np
from jax import lax
from jax.experimental import pallas as pl
from jax.experimental.pallas import tpu as pltpu

F32 = jnp.float32
BF16 = jnp.bfloat16
EPS = 1e-6

HY_SHORT = 3
HY_POS_BANDS = 16
HY_N_FILT = 4
GLA_HEADS = 4
GLA_DK = 64
GLA_DV = 128
GLA_QK = GLA_HEADS * GLA_DK
GLA_V = GLA_HEADS * GLA_DV
GLA_LOWRANK = 16
GLA_TAU = 16.0
GLA_CHUNK = 64
S5_STATE = 64

LANES = 128
SUBLANES = 8
VMEM_LIMIT = 56 * 1024 * 1024

GLA_BLOCK = 4
A_PAD = LANES
ROW_TILE = 256


def _const_spec(shape):
    nd = len(shape)
    return pl.BlockSpec(shape, lambda *_: (0,) * nd, pipeline_mode=pl.Buffered(1))


def _params(n_axes):
    return pltpu.CompilerParams(dimension_semantics=("parallel",) * n_axes, vmem_limit_bytes=VMEM_LIMIT)


def _mod_kernel(c_ref, w_ref, b_ref, o_ref):
    c = c_ref[...]
    s = c * jax.nn.sigmoid(c)
    o_ref[0] = jnp.dot(s.astype(BF16), w_ref[0], preferred_element_type=F32) + b_ref[0]


def _modulation_all(cond, w_mod, b_mod):
    depth, d, n = w_mod.shape
    r = cond.shape[0]
    tn = 1536
    return pl.pallas_call(
        _mod_kernel,
        grid=(depth, n // tn),
        in_specs=[
            pl.BlockSpec((r, d), lambda l, j: (0, 0)),
            pl.BlockSpec((1, d, tn), lambda l, j: (l, 0, j)),
            pl.BlockSpec((1, 1, tn), lambda l, j: (l, 0, j)),
        ],
        out_specs=pl.BlockSpec((1, r, tn), lambda l, j: (l, 0, j)),
        out_shape=jax.ShapeDtypeStruct((depth, r, n), F32),
        compiler_params=_params(2),
        name="modulation",
    )(cond, w_mod.astype(BF16), b_mod.reshape(depth, 1, n))


def _norm_mod(x, g, shift, scale):
    y = x * lax.rsqrt(jnp.mean(x * x, axis=-1, keepdims=True) + EPS)
    return (y * g) * (1.0 + scale) + shift


_NT = (((1,), (1,)), ((), ()))


def _proj_kernel(col_sizes, d, x_ref, mod_ref, g_ref, w_ref, wt_ref, *o_refs):
    m = mod_ref[0, 0]
    h = _norm_mod(x_ref[0], g_ref[...], m[:, 0:d], m[:, d:2 * d]).astype(BF16)
    off = 0
    for o_ref, n in zip(o_refs[:-1], col_sizes):
        o_ref[0] = jnp.dot(h, w_ref[:, off:off + n], preferred_element_type=F32).astype(o_ref.dtype)
        off += n
    o_refs[-1][0] = lax.dot_general(wt_ref[...], h, _NT, preferred_element_type=F32)


def _mod_spec(d, n_lat_tiles):
    return pl.BlockSpec((1, 1, 1, 6 * d), lambda i, j: (i, (j < n_lat_tiles).astype(jnp.int32), 0, 0))


def _project(xc, mods, g, w, col_sizes, w_t, n_lat_tiles):
    b, t, d = xc.shape
    tm = ROW_TILE
    n_tot = sum(col_sizes)
    n_t = w_t.shape[0]
    out_shape = [jax.ShapeDtypeStruct((b, t, n), BF16) for n in col_sizes] + [jax.ShapeDtypeStruct((b, n_t, t), F32)]
    out_specs = [pl.BlockSpec((1, tm, n), lambda i, j: (i, j, 0)) for n in col_sizes]
    out_specs.append(pl.BlockSpec((1, n_t, tm), lambda i, j: (i, 0, j)))
    return pl.pallas_call(
        functools.partial(_proj_kernel, tuple(col_sizes), d),
        grid=(b, t // tm),
        in_specs=[
            pl.BlockSpec((1, tm, d), lambda i, j: (i, j, 0)),
            _mod_spec(d, n_lat_tiles),
            _const_spec((1, d)),
            _const_spec((d, n_tot)),
            _const_spec((n_t, d)),
        ],
        out_specs=out_specs,
        out_shape=out_shape,
        compiler_params=_params(2),
        name="in_proj",
    )(xc, mods, g, w, w_t)


def _gelu_tanh(x):
    return 0.5 * x * (1.0 + jnp.tanh(math.sqrt(2.0 / math.pi) * (x + 0.044715 * (x * x * x))))


def _merge_kernel(d, x_ref, mod_ref, g_ref, yhy_ref, ygla_ref, ys5_ref, u_ref, wg_ref, whb_ref, wgb_ref, wsb_ref,
                  wo_ref, s5d_ref, wglu_ref, bglu_ref, o_ref):
    x = x_ref[0]
    m = mod_ref[0, 0]
    h = _norm_mod(x, g_ref[...], m[:, 0:d], m[:, d:2 * d]).astype(BF16)
    y5 = (ys5_ref[0] + s5d_ref[...] * u_ref[0]).T
    g5 = _gelu_tanh(y5)
    y_s5 = g5 * jax.nn.sigmoid(jnp.dot(g5.astype(BF16), wglu_ref[...], preferred_element_type=F32) + bglu_ref[...])

    def branch(k, y, wb_ref):
        gate = jnp.dot(h, wg_ref[:, k * d:(k + 1) * d], preferred_element_type=F32)
        return jax.nn.sigmoid(gate) * jnp.dot(y, wb_ref[...], preferred_element_type=F32)

    mix = branch(0, yhy_ref[0], whb_ref) + branch(1, ygla_ref[0], wgb_ref) + branch(2, y_s5.astype(BF16), wsb_ref)
    out = jnp.dot(mix.astype(BF16), wo_ref[...], preferred_element_type=F32)
    o_ref[0] = x + m[:, 2 * d:3 * d] * out


def _merge(xc, mods, g, y_hy, y_gla, ys_s5, u_s5, wts, nj, n_lat_tiles):
    b, t, d = xc.shape
    tm = ROW_TILE
    ch = y_hy.shape[-1]
    row = lambda n: pl.BlockSpec((1, tm, n), lambda i, j: (i, j, 0))
    col = lambda n: pl.BlockSpec((1, n, tm), lambda i, j: (i, 0, j))
    w_gate, w_hy, w_gla, w_s5, w_out, s5_d, w_glu, b_glu = wts
    return pl.pallas_call(
        functools.partial(_merge_kernel, d),
        grid=(b, nj),
        in_specs=[
            row(d),
            _mod_spec(d, n_lat_tiles),
            _const_spec((1, d)),
            row(ch), row(ch), col(ch), col(ch),
            _const_spec(w_gate.shape), _const_spec(w_hy.shape), _const_spec(w_gla.shape), _const_spec(w_s5.shape),
            _const_spec(w_out.shape), _const_spec(s5_d.shape), _const_spec(w_glu.shape), _const_spec(b_glu.shape),
        ],
        out_specs=pl.BlockSpec((1, tm, d), lambda i, j: (i, j, 0)),
        out_shape=jax.ShapeDtypeStruct((b, nj * tm, d), F32),
        compiler_params=_params(2),
        name="merge",
    )(xc, mods, g, y_hy, y_gla, ys_s5, u_s5, w_gate, w_hy, w_gla, w_s5, w_out, s5_d, w_glu, b_glu)


def _ffn_kernel(d, final, x_ref, mod_ref, g_ref, wa_ref, wb_ref, wo_ref, gf_ref, o_ref):
    x = x_ref[0]
    m = mod_ref[0, 0]
    h = _norm_mod(x, g_ref[...], m[:, 3 * d:4 * d], m[:, 4 * d:5 * d]).astype(BF16)
    a = jnp.dot(h, wa_ref[...], preferred_element_type=F32)
    bb = jnp.dot(h, wb_ref[...], preferred_element_type=F32)
    act = (a * jax.nn.sigmoid(a) * bb).astype(BF16)
    y = x + m[:, 5 * d:6 * d] * jnp.dot(act, wo_ref[...], preferred_element_type=F32)
    if final:
        y = y * lax.rsqrt(jnp.mean(y * y, axis=-1, keepdims=True) + EPS) * gf_ref[...]
    o_ref[0] = y


def _ffn(x, mods, g, wa, wb, wo, g_final, n_lat_tiles, final):
    b, t, d = x.shape
    tm = ROW_TILE
    return pl.pallas_call(
        functools.partial(_ffn_kernel, d, final),
        grid=(b, t // tm),
        in_specs=[
            pl.BlockSpec((1, tm, d), lambda i, j: (i, j, 0)),
            _mod_spec(d, n_lat_tiles),
            _const_spec((1, d)),
            _const_spec(wa.shape), _const_spec(wb.shape), _const_spec(wo.shape),
            _const_spec((1, d)),
        ],
        out_specs=pl.BlockSpec((1, tm, d), lambda i, j: (i, j, 0)),
        out_shape=jax.ShapeDtypeStruct((b, t, d), F32),
        compiler_params=_params(2),
        name="ffn",
    )(x, mods, g, wa, wb, wo, g_final)


S5_CHUNK = LANES


def _s5_weights(a_re, a_im, log_dt, b_re, b_im, c_re, c_im, tc):
    g, p = a_re.shape[1:]
    i = b_re.shape[-1]
    lam_c = lax.complex(jnp.minimum(a_re, -1e-4), a_im)
    lam_dt = lam_c * jnp.exp(log_dt)[..., None]
    b_bar = ((jnp.exp(lam_dt) - 1.0) / lam_c)[..., None] * lax.complex(b_re, b_im)[None]
    c_mat = lax.complex(c_re, c_im)
    tau = jnp.arange(tc, dtype=F32)
    pw = jnp.exp(lam_dt[:, :, None, :] * tau[None, None, :, None])
    pw1 = pw * jnp.exp(lam_dt)[:, :, None, :]
    kern = jnp.einsum('gip,dgtp,dgpj->dgtij', c_mat, pw, b_bar).real
    k_lag = jnp.concatenate([kern[1][:, :0:-1], kern[0][:, :1] + kern[1][:, :1], kern[0][:, 1:]], axis=1)
    kr = jnp.pad(k_lag.transpose(0, 3, 2, 1), ((0, 0), (0, 0), (0, 0), (0, 1)))
    kr = kr.reshape(g, i, i * 2 * tc)

    inc_f = pw[0][:, None, ::-1, :] * b_bar[0].transpose(0, 2, 1)[:, :, None]
    inc_b = pw[1][:, None, :, :] * b_bar[1].transpose(0, 2, 1)[:, :, None]
    bm = jnp.concatenate([inc_f.real, inc_b.real, inc_f.imag, inc_b.imag], axis=-1).reshape(g, tc * i, 4 * p)

    out_f = c_mat.transpose(0, 2, 1)[..., None] * pw1[0].transpose(0, 2, 1)[:, :, None, :]
    out_b = c_mat.transpose(0, 2, 1)[..., None] * pw1[1][:, ::-1].transpose(0, 2, 1)[:, :, None, :]
    z = jnp.zeros((g, p, tc * i), F32)
    fl = lambda t: t.reshape(g, p, tc * i)
    cm = jnp.concatenate([fl(out_f.real), z, fl(-out_f.imag), z, z, fl(out_b.real), z, fl(-out_b.imag)], axis=1)

    lam_t = jnp.exp(lam_dt * float(tc))
    lam = jnp.stack([jnp.concatenate([lam_t[0].real, lam_t[1].real], -1),
                     jnp.concatenate([lam_t[0].imag, lam_t[1].imag], -1)], axis=1)
    return kr, bm.astype(BF16), cm.astype(BF16), lam


def _s5_kernel(nc, nc_ctx, tc, u_ref, kr_ref, bm_ref, cm_ref, lam_ref, y_ref, up_ref, dx_ref, p_ref, m_ref):
    bn, i_sz = u_ref.shape[:2]
    nl = nc - nc_ctx
    width = kr_ref.shape[-1]
    seg = width // i_sz
    for j in range(i_sz):
        lag_rows = pltpu.roll(jnp.broadcast_to(kr_ref[0, j:j + 1, :], (tc, width)), width - (tc - 1), axis=1,
                              stride=1, stride_axis=0)
        m_ref[j * tc:(j + 1) * tc, :] = jnp.concatenate(
            [lag_rows[:, i * seg:i * seg + tc] for i in range(i_sz)], axis=1).astype(BF16)
    for n in range(nc):
        for i in range(i_sz):
            up_ref[n * bn:(n + 1) * bn, i * tc:(i + 1) * tc] = u_ref[:, i, n * tc:(n + 1) * tc]
    u = up_ref[...].astype(BF16)
    dx_ref[...] = jnp.dot(u, bm_ref[0], preferred_element_type=F32)
    lam = lam_ref[0]
    lr, li = lam[0:1], lam[1:2]
    w = lam.shape[-1]
    half = w // 2
    is_f = lax.broadcasted_iota(jnp.int32, (bn, w), 1) < half

    def step(s, carry):
        sr, si = carry
        nf = jnp.where(s < nc_ctx, nl + s, s - nc_ctx)
        nb = nc - 1 - s
        rf = pl.multiple_of(nf * bn, bn)
        rb = pl.multiple_of(nb * bn, bn)
        p_ref[pl.ds(rf, bn), 0:w] = sr
        p_ref[pl.ds(rf, bn), w:2 * w] = si
        p_ref[pl.ds(rb, bn), 2 * w:3 * w] = sr
        p_ref[pl.ds(rb, bn), 3 * w:4 * w] = si
        d_re = jnp.where(is_f, dx_ref[pl.ds(rf, bn), 0:w], dx_ref[pl.ds(rb, bn), 0:w])
        d_im = jnp.where(is_f, dx_ref[pl.ds(rf, bn), w:2 * w], dx_ref[pl.ds(rb, bn), w:2 * w])
        return lr * sr - li * si + d_re, lr * si + li * sr + d_im

    zero = jnp.zeros((bn, 2 * half), F32)
    lax.fori_loop(0, nc, step, (zero, zero))
    y = jnp.dot(u, m_ref[...], preferred_element_type=F32)
    y = y + jnp.dot(p_ref[...].astype(BF16), cm_ref[0], preferred_element_type=F32)
    for n in range(nc):
        for i in range(i_sz):
            y_ref[:, i, n * tc:(n + 1) * tc] = y[n * bn:(n + 1) * bn, i * tc:(i + 1) * tc]


def _s5_mix(u_t, n_ctx, wts, layer):
    kr, bm, cm, lam = wts
    bn, ch, t = u_t.shape
    g, i = kr.shape[1:3]
    tc = kr.shape[3] // (2 * i)
    assert 2 * S5_STATE == LANES and bn % SUBLANES == 0 and n_ctx % tc == 0 and t % tc == 0 and tc % LANES == 0
    nc, nc_ctx = t // tc, n_ctx // tc
    k = tc * i
    r = nc * bn
    return pl.pallas_call(
        functools.partial(_s5_kernel, nc, nc_ctx, tc),
        grid=(g,),
        in_specs=[
            pl.BlockSpec((bn, i, t), lambda j: (0, j, 0)),
            pl.BlockSpec((None, 1, i, 2 * k), lambda j: (layer, j, 0, 0)),
            pl.BlockSpec((None, 1, k, 4 * S5_STATE), lambda j: (layer, j, 0, 0)),
            pl.BlockSpec((None, 1, 8 * S5_STATE, k), lambda j: (layer, j, 0, 0)),
            pl.BlockSpec((None, 1, 2, 2 * S5_STATE), lambda j: (layer, j, 0, 0)),
        ],
        out_specs=pl.BlockSpec((bn, i, t), lambda j: (0, j, 0)),
        out_shape=jax.ShapeDtypeStruct((bn, ch, t), F32),
        scratch_shapes=[pltpu.VMEM((r, k), F32), pltpu.VMEM((r, 4 * S5_STATE), F32),
                        pltpu.VMEM((r, 8 * S5_STATE), F32), pltpu.VMEM((k, k), BF16)],
        compiler_params=_params(1),
        name="s5_mix",
    )(u_t, kr, bm, cm, lam)


def _gla_kernel(nc, nc_ctx, k_ref, v_ref, a_ref, q_ref, r_ref, wa_ref, ba_ref, gn_ref, y_ref, of_ref, ob_ref,
                s_ref):
    c, h_n, dk, dv = GLA_CHUNK, GLA_HEADS, GLA_DK, GLA_DV
    qk = h_n * dk
    q_scale = dk ** -0.5
    nbk = GLA_BLOCK
    rb = nbk * c
    row_i = lax.broadcasted_iota(jnp.int32, (rb, rb), 0)
    col_i = lax.broadcasted_iota(jnp.int32, (rb, rb), 1)
    same = (row_i // c) == (col_i // c)
    causal = (same & (row_i >= col_i), same & (row_i <= col_i))
    cum_ops = [m.astype(BF16) for m in causal]
    att_mask = [jnp.concatenate([m] * h_n, axis=0) for m in causal]
    lane_head = lax.broadcasted_iota(jnp.int32, (1, qk), 1) // dk
    head_lanes = [lane_head == h for h in range(h_n)]

    def by_head(x):
        zero = jnp.zeros_like(x)
        return jnp.concatenate([jnp.where(m, x, zero) for m in head_lanes], axis=0)

    def block(j, d):
        rows = pl.ds(pl.multiple_of(j * rb, rb), rb)
        k = k_ref[0, rows, :].astype(F32)
        q = q_ref[0, rows, :].astype(F32) * q_scale
        v = v_ref[0, rows, :]
        z = jnp.dot(a_ref[0, rows, :], wa_ref[:, d * qk:(d + 1) * qk], preferred_element_type=F32) + ba_ref[d:d + 1, :]
        log_a = (jnp.minimum(z, 0.0) - jnp.log(1.0 + jnp.exp(-jnp.abs(z)))) * (1.0 / GLA_TAU)
        hi = log_a.astype(BF16)
        lo = (log_a - hi.astype(F32)).astype(BF16)
        cs = jnp.dot(cum_ops[d], jnp.concatenate([hi, lo], axis=1), preferred_element_type=F32)
        b = cs[:, :qk] + cs[:, qk:]
        b_tot = jnp.concatenate(
            [jnp.broadcast_to(b[n * c + (c - 1 if d == 0 else 0)][None], (c, qk)) for n in range(nbk)], axis=0)
        qd = (q * jnp.exp(b)).astype(BF16)
        kd = (k * jnp.exp(-b)).astype(BF16)
        kl = (k * jnp.exp(b_tot - b)).astype(BF16)
        qm = by_head(qd)
        intra = []
        for h in range(h_n):
            att = lax.dot_general(qm[h * rb:(h + 1) * rb], kd, _NT, preferred_element_type=F32)
            att = jnp.where(causal[d], att, 0.0).astype(BF16)
            intra.append(jnp.dot(att, v[:, h * dv:(h + 1) * dv], preferred_element_type=F32))
        s_t = s_ref[d]
        inter = [None] * nbk
        for n in (range(nbk) if d == 0 else reversed(range(nbk))):
            r0 = n * c
            qm_n = jnp.concatenate([qm[h * rb + r0:h * rb + r0 + c] for h in range(h_n)], axis=0)
            inter[n] = lax.dot_general(qm_n, s_t.astype(BF16), _NT, preferred_element_type=F32)
            v_t = jnp.concatenate(
                [jnp.concatenate([v[r0:r0 + c, h * dv:(h + 1) * dv] for h in range(p, p + dv // c)], axis=0).T
                 for p in range(0, h_n, dv // c)], axis=1)
            s_t = s_t * jnp.exp(b_tot[r0:r0 + 1]) + jnp.dot(v_t, by_head(kl[r0:r0 + c]), preferred_element_type=F32)
        s_ref[d] = s_t
        o = jnp.concatenate(
            [intra[h] + jnp.concatenate([inter[n][h * c:(h + 1) * c] for n in range(nbk)], axis=0)
             for h in range(h_n)], axis=1)
        return rows, o

    def scan_step(i, carry, lo_block, hi_block):
        rows_f, o_f = block(lo_block + i, 0)
        of_ref[rows_f, :] = o_f
        rows_b, o_b = block(hi_block - 1 - i, 1)
        ob_ref[rows_b, :] = o_b
        return carry

    def readout(n, carry):
        rows = pl.ds(pl.multiple_of(n * c, c), c)
        o = of_ref[rows, :] + ob_ref[rows, :]
        r = r_ref[0, rows, :].astype(F32)
        gate = r * jax.nn.sigmoid(r)
        outs = []
        for h in range(h_n):
            oh = o[:, h * dv:(h + 1) * dv]
            oh = oh * lax.rsqrt(jnp.mean(oh * oh, axis=-1, keepdims=True) + EPS) * gn_ref[...]
            outs.append(oh * gate[:, h * dv:(h + 1) * dv])
        y_ref[0, rows, :] = jnp.concatenate(outs, axis=1).astype(y_ref.dtype)
        return carry

    nl, n_ctx = (nc - nc_ctx) // nbk, nc_ctx // nbk
    s_ref[...] = jnp.zeros(s_ref.shape, F32)
    lax.fori_loop(0, n_ctx, functools.partial(scan_step, lo_block=nl, hi_block=nl + n_ctx), 0)
    lax.fori_loop(0, nl, functools.partial(scan_step, lo_block=0, hi_block=nl), 0)
    lax.fori_loop(0, nc, readout, 0, unroll=2)


def _gla_mix(k, v, a, q, r, wa, ba, gnorm, n_ctx):
    bn, t, qk = k.shape
    vd = v.shape[-1]
    c = GLA_CHUNK
    assert t % (c * GLA_BLOCK) == 0 and n_ctx % (c * GLA_BLOCK) == 0 and GLA_DV % c == 0
    seq = lambda n, bufs=2: pl.BlockSpec((1, t, n), lambda i: (i, 0, 0), pipeline_mode=pl.Buffered(bufs))
    return pl.pallas_call(
        functools.partial(_gla_kernel, t // c, n_ctx // c),
        grid=(bn,),
        in_specs=[seq(qk), seq(vd), seq(a.shape[-1]), seq(qk), seq(vd, 1),
                  _const_spec(wa.shape), _const_spec(ba.shape), _const_spec(gnorm.shape)],
        out_specs=pl.BlockSpec((1, t, vd), lambda i: (i, 0, 0)),
        out_shape=jax.ShapeDtypeStruct((bn, t, vd), BF16),
        scratch_shapes=[pltpu.VMEM((t, vd), F32), pltpu.VMEM((t, vd), F32), pltpu.VMEM((2, GLA_DV, qk), F32)],
        compiler_params=_params(1),
        name="gla_mix",
    )(k, v, a, q, r, wa, ba, gnorm)


HY_NB = LANES
HY_XPITCH = HY_NB + SUBLANES
HY_APITCH = 2 * HY_NB + SUBLANES
HY_CT = LANES
HY_UNROLL_OUTER = 16
HY_UNROLL_INNER = 8


def _hyena_filters(n, pos, params):
    w1, b1, w2, b2, w3, freq1, freq2, decay = (p.astype(F32) for p in params)
    t = (pos / (n - 1))[:, None]
    bands = jnp.linspace(1e-4, HY_POS_BANDS - 1, HY_POS_BANDS, dtype=F32)
    ang = (2.0 * math.pi / n) * pos[:, None] * bands[None]
    z = jnp.concatenate([t, jnp.cos(ang), -jnp.sin(ang)], axis=-1)
    hid = jnp.sin(freq1 * (z @ w1 + b1))
    hid = jnp.sin(freq2 * (hid @ w2 + b2))
    w3 = w3.reshape(w3.shape[0], HY_N_FILT, -1)
    window = jnp.exp(-t[None] * jnp.abs(decay).reshape(HY_N_FILT, 1, -1))
    return jnp.einsum('nk,kfc->fnc', hid, w3) * window


def _hyena_kernels(n, params):
    slot = jnp.arange(2 * n)
    h = _hyena_filters(n, jnp.where(slot < n, slot, 2 * n - slot).astype(F32), params)
    is_fwd, is_bwd = (slot < n)[None, :, None], (slot > n)[None, :, None]
    k = jnp.where(is_fwd, h[0::2], 0.0) + jnp.where(is_bwd, h[1::2], 0.0)
    lag0_bwd = _hyena_filters(n, jnp.zeros((1,), F32), params)[1::2]
    k = k + jnp.where(slot[None, :, None] == 0, lag0_bwd, 0.0)
    return k * lax.rsqrt(jnp.sum(k * k, axis=1, keepdims=True) + EPS)


def _spectrum_kernel(na, k_ref, f1_ref, f2_ref, o_ref, a2):
    nb, ap = HY_NB, HY_APITCH

    def stage1(b, c):
        xs = k_ref[0, pl.ds(b, na, stride=nb), :].astype(BF16)
        r = jnp.dot(f1_ref[b], xs, preferred_element_type=F32)
        a2[pl.ds(b, na, stride=ap), :] = r[:na]
        a2[pl.ds(nb + b, na, stride=ap), :] = r[na:]
        return c

    def stage2(ka, c):
        rows = pl.ds(pl.multiple_of(ka * ap, 8), 2 * nb)
        xf = jnp.dot(f2_ref[...], a2[rows, :].astype(BF16), preferred_element_type=F32)
        o_ref[0, ka, 0] = xf[:nb].astype(o_ref.dtype)
        o_ref[0, ka, 1] = xf[nb:].astype(o_ref.dtype)
        return c

    lax.fori_loop(0, nb, stage1, 0, unroll=HY_UNROLL_INNER)
    lax.fori_loop(0, na, stage2, 0, unroll=HY_UNROLL_INNER)


def _hyena_spectra(kernels):
    nf, n, ch = kernels.shape
    nb, ct = HY_NB, HY_CT
    na = n // nb
    _, f2, _, _, f1_full = _dft_tables(n // 2)
    return pl.pallas_call(
        functools.partial(_spectrum_kernel, na),
        grid=(nf, ch // ct),
        in_specs=[pl.BlockSpec((1, n, ct), lambda f, j: (f, 0, j)), _const_spec(f1_full.shape), _const_spec(f2.shape)],
        out_specs=pl.BlockSpec((1, na, 2, nb, ct), lambda f, j: (f, 0, 0, 0, j)),
        out_shape=jax.ShapeDtypeStruct((nf, na, 2, nb, ch), BF16),
        scratch_shapes=[pltpu.VMEM((na * HY_APITCH, ct), F32)],
        compiler_params=_params(2),
        name="hyena_spectrum",
    )(kernels, f1_full, f2)


def _cplx_block(m):
    return np.block([[m.real, -m.imag], [m.imag, m.real]])


@functools.lru_cache(maxsize=None)
def _dft_tables(n_seq):
    nb = HY_NB
    n = 2 * n_seq
    na = n // nb
    ha = na // 2
    ka = np.arange(na)[:, None]
    b = np.arange(nb)
    tw = np.exp(-2j * np.pi * ka * b[None, :] / n)
    w_a = np.exp(-2j * np.pi * ka * np.arange(ha)[None, :] / na)
    f1 = np.stack([_cplx_block(tw[:, i:i + 1] * w_a) for i in range(nb)])
    g = np.stack([_cplx_block((np.conj(tw[:, i:i + 1] * w_a)).T / n) for i in range(nb)])
    w_b = np.exp(-2j * np.pi * b[:, None] * b[None, :] / nb)
    f2 = _cplx_block(w_b)
    f2i = _cplx_block(np.conj(w_b))
    w_full = np.exp(-2j * np.pi * ka * np.arange(na)[None, :] / na)
    f1_full = np.stack([np.concatenate([(tw[:, i:i + 1] * w_full).real, (tw[:, i:i + 1] * w_full).imag], axis=0)
                        for i in range(nb)])
    return tuple(jnp.asarray(t, dtype=BF16) for t in (f1, f2, f2i, g, f1_full))


def _hyena_kernel(na, x1_ref, x2_ref, v_ref, cw_ref, bias_ref, kf1_ref, kf2_ref, f1_ref, f2_ref, f2i_ref, g_ref,
                  o_ref, rawf, vp, gp, yp, a2):
    nb, xp, ap = HY_NB, HY_XPITCH, HY_APITCH
    ha = na // 2
    n_seq = ha * nb
    ct = o_ref.shape[-1]
    halo = SUBLANES
    rawf[:, 0:halo, :] = jnp.zeros((2, halo, ct), F32)
    rawf[:, halo + n_seq:2 * halo + n_seq, :] = jnp.zeros((2, halo, ct), F32)

    def short_conv_into(raw_ref, which, dst):
        w0, w1, w2 = (cw_ref[j, which:which + 1, :] for j in range(HY_SHORT))
        for e in range(2):
            rawf[e, halo:halo + n_seq, :] = raw_ref[e].astype(F32)

            def body(a, c):
                base = pl.multiple_of(a * nb, nb) + halo
                u = (w0 * rawf[e, pl.ds(base - 1, nb), :] + w1 * rawf[e, pl.ds(base, nb), :]
                     + w2 * rawf[e, pl.ds(base + 1, nb), :])
                dst[e, pl.ds(pl.multiple_of(a * xp, 8), nb), :] = u
                return c

            lax.fori_loop(0, ha, body, 0)

    def long_conv(kf_ref):
        def stage1(b, c):
            xs = jnp.concatenate([vp[0, pl.ds(b, ha, stride=xp), :], vp[1, pl.ds(b, ha, stride=xp), :]], axis=0)
            r = jnp.dot(f1_ref[b], xs.astype(BF16), preferred_element_type=F32)
            a2[pl.ds(b, na, stride=ap), :] = r[:na]
            a2[pl.ds(nb + b, na, stride=ap), :] = r[na:]
            return c

        def stage2(kp, c):
            rows = [pl.ds(pl.multiple_of((2 * kp + e) * ap, 8), 2 * nb) for e in range(2)]
            slab = jnp.concatenate([a2[r, :] for r in rows], axis=1).astype(BF16)
            xf = jnp.dot(f2_ref[...], slab, preferred_element_type=F32)
            xr, xi = xf[:nb], xf[nb:]
            kr = jnp.concatenate([kf_ref[0, 2 * kp + e, 0] for e in range(2)], axis=1).astype(F32)
            ki = jnp.concatenate([kf_ref[0, 2 * kp + e, 1] for e in range(2)], axis=1).astype(F32)
            z = jnp.concatenate([xr * kr - xi * ki, xr * ki + xi * kr], axis=0).astype(BF16)
            back = jnp.dot(f2i_ref[...], z, preferred_element_type=F32)
            for e in range(2):
                a2[rows[e], :] = back[:, e * ct:(e + 1) * ct]
            return c

        def stage3(b, c):
            s = jnp.concatenate([a2[pl.ds(b, na, stride=ap), :], a2[pl.ds(nb + b, na, stride=ap), :]], axis=0)
            y = jnp.dot(g_ref[b], s.astype(BF16), preferred_element_type=F32)
            yp[0, pl.ds(b, ha, stride=xp), :] = y[:ha]
            yp[1, pl.ds(b, ha, stride=xp), :] = y[ha:]
            return c

        lax.fori_loop(0, nb, stage1, 0, unroll=HY_UNROLL_OUTER)
        lax.fori_loop(0, na // 2, stage2, 0, unroll=HY_UNROLL_INNER)
        lax.fori_loop(0, nb, stage3, 0, unroll=HY_UNROLL_OUTER)

    def gated(bias_row, write):
        for e in range(2):
            def body(a, c):
                rows = pl.ds(pl.multiple_of(a * xp, 8), nb)
                write(e, a, rows, gp[e, rows, :] * (yp[e, rows, :] + bias_row * vp[e, rows, :]))
                return c

            lax.fori_loop(0, ha, body, 0)

    def write_z(e, a, rows, val):
        vp[e, rows, :] = val

    def write_out(e, a, rows, val):
        o_ref[e, pl.ds(pl.multiple_of(a * nb, nb), nb), :] = val.astype(o_ref.dtype)

    short_conv_into(v_ref, 2, vp)
    short_conv_into(x1_ref, 0, gp)
    long_conv(kf1_ref)
    gated(bias_ref[0:1, :], write_z)
    short_conv_into(x2_ref, 1, gp)
    long_conv(kf2_ref)
    gated(bias_ref[1:2, :], write_out)


def _hyena_latent(hy_all, n_seq, t_out, conv_w, bias, kf):
    bn = hy_all.shape[0]
    ch = hy_all.shape[-1] // 3
    ct = HY_CT
    nct = ch // ct
    nb = HY_NB
    na = 2 * n_seq // nb
    assert bn % 2 == 0 and ch % ct == 0 and n_seq % (8 * nb) == 0
    f1, f2, f2i, g, _ = _dft_tables(n_seq)
    col = lambda which: pl.BlockSpec((2, n_seq, ct), lambda j, p: (p, 0, which * nct + j))
    kf_spec = lambda f: pl.BlockSpec((1, na, 2, nb, ct), lambda j, p: (f, 0, 0, 0, j), pipeline_mode=pl.Buffered(1))
    pad_rows = (na // 2) * HY_XPITCH
    return pl.pallas_call(
        functools.partial(_hyena_kernel, na),
        grid=(nct, bn // 2),
        in_specs=[col(0), col(1), col(2),
                  pl.BlockSpec((HY_SHORT, 3, ct), lambda j, p: (0, 0, j)),
                  pl.BlockSpec((2, ct), lambda j, p: (0, j)),
                  kf_spec(0), kf_spec(1),
                  _const_spec(f1.shape), _const_spec(f2.shape), _const_spec(f2i.shape), _const_spec(g.shape)],
        out_specs=pl.BlockSpec((2, n_seq, ct), lambda j, p: (p, 0, j)),
        out_shape=jax.ShapeDtypeStruct((bn, t_out, ch), BF16),
        scratch_shapes=[pltpu.VMEM((2, n_seq + 2 * SUBLANES, ct), F32), pltpu.VMEM((2, pad_rows, ct), F32),
                        pltpu.VMEM((2, pad_rows, ct), F32), pltpu.VMEM((2, pad_rows, ct), F32),
                        pltpu.VMEM((na * HY_APITCH, ct), F32)],
        compiler_params=_params(2),
        name="hyena_latent",
    )(hy_all, hy_all, hy_all, conv_w.reshape(HY_SHORT, 3, ch), bias, kf, kf, f1, f2, f2i, g)


@functools.lru_cache(maxsize=None)
def _dense_dft_tables(n_seq):
    n = 2 * n_seq
    ang = 2.0 * np.pi * np.arange(n)[:, None] * np.arange(n_seq)[None, :] / n
    fd = np.concatenate([np.cos(ang), -np.sin(ang)], axis=0)
    gd = np.concatenate([np.cos(ang).T, -np.sin(ang).T], axis=1) / n
    return jnp.asarray(fd, dtype=BF16), jnp.asarray(gd, dtype=BF16)


def _hyena_ctx_kernel(n_seq, x1_ref, x2_ref, v_ref, cw_ref, bias_ref, kf1_ref, kf2_ref, fd_ref, gd_ref, buf_ref,
                      o_ref):
    del buf_ref
    n = 2 * n_seq

    def short_conv(raw_ref, which):
        x = raw_ref[0].astype(F32)
        zero = jnp.zeros((1, x.shape[-1]), F32)
        prev = jnp.concatenate([zero, x[:-1]], axis=0)
        nxt = jnp.concatenate([x[1:], zero], axis=0)
        return cw_ref[0, which:which + 1, :] * prev + cw_ref[1, which:which + 1, :] * x + cw_ref[2, which:which + 1, :] * nxt

    def long_conv(u, kf_ref, bias_row):
        xf = jnp.dot(fd_ref[...], u.astype(BF16), preferred_element_type=F32)
        xr, xi = xf[:n], xf[n:]
        kr, ki = kf_ref[0], kf_ref[1]
        z = jnp.concatenate([xr * kr - xi * ki, xr * ki + xi * kr], axis=0).astype(BF16)
        return jnp.dot(gd_ref[...], z, preferred_element_type=F32) + bias_row * u

    x1, x2, v = short_conv(x1_ref, 0), short_conv(x2_ref, 1), short_conv(v_ref, 2)
    z = x1 * long_conv(v, kf1_ref, bias_ref[0:1, :])
    o_ref[0] = (x2 * long_conv(z, kf2_ref, bias_ref[1:2, :])).astype(o_ref.dtype)


def _hyena_context(hy_all, y_buf, row_block, n_seq, conv_w, bias, kf1, kf2):
    bn = hy_all.shape[0]
    ch = hy_all.shape[-1] // 3
    fd, gd = _dense_dft_tables(n_seq)
    col = lambda which: pl.BlockSpec((1, n_seq, ch), lambda i: (i, row_block, which))
    split = lambda kf: jnp.stack([kf.real, kf.imag], axis=0)
    return pl.pallas_call(
        functools.partial(_hyena_ctx_kernel, n_seq),
        grid=(bn,),
        in_specs=[col(0), col(1), col(2), _const_spec((HY_SHORT, 3, ch)), _const_spec((2, ch)),
                  _const_spec((2, 2 * n_seq, ch)), _const_spec((2, 2 * n_seq, ch)),
                  _const_spec(fd.shape), _const_spec(gd.shape), pl.BlockSpec(memory_space=pl.ANY)],
        out_specs=pl.BlockSpec((1, n_seq, ch), lambda i: (i, row_block, 0)),
        out_shape=jax.ShapeDtypeStruct(y_buf.shape, y_buf.dtype),
        input_output_aliases={9: 0},
        compiler_params=_params(1),
        name="hyena_context",
    )(hy_all, hy_all, hy_all, conv_w.reshape(HY_SHORT, 3, ch), bias, split(kf1), split(kf2), fd, gd, y_buf)


def kernel(x, c, ctx, c_ctx, w_mod, b_mod, g_norm1, g_norm2, w_in, hy_conv, hy_w1, hy_b1, hy_w2, hy_b2, hy_w3,
           hy_freq1, hy_freq2, hy_decay, hy_bias, gla_wa2, gla_ba, gla_gnorm, s5_a_re, s5_a_im, s5_log_dt, s5_b_re,
           s5_b_im, s5_c_re, s5_c_im, s5_d, s5_w_glu, s5_b_glu, w_br_hy, w_br_gla, w_br_s5, w_out, w_ffn_in,
           w_ffn_out, g_final):
    bn, n_lat, d = x.shape
    n_ctx = ctx.shape[1]
    depth = w_in.shape[0]
    d_hy = w_br_hy.shape[1]
    s5_ch = w_br_s5.shape[1]
    d_ff = w_ffn_out.shape[1]
    assert n_ctx == ROW_TILE and n_lat % ROW_TILE == 0
    n_lat_tiles = n_lat // ROW_TILE
    n_low = 2 * GLA_LOWRANK
    o_a = GLA_QK + GLA_V
    o_u = o_a + n_low
    o_q = o_u + s5_ch
    o_gate = o_q + GLA_QK + GLA_V + 3 * d_hy
    col_sizes = (GLA_QK, GLA_V, A_PAD, GLA_QK, GLA_V, 3 * d_hy)

    cond = jnp.concatenate([c_ctx[None], c], axis=0)
    cond = jnp.pad(cond, ((0, (-cond.shape[0]) % SUBLANES), (0, 0)))
    mod_all = _modulation_all(cond, w_mod, b_mod)

    s5w_all = jax.vmap(functools.partial(_s5_weights, tc=S5_CHUNK))(
        s5_a_re, s5_a_im, s5_log_dt, s5_b_re, s5_b_im, s5_c_re, s5_c_im)
    wa_all = jnp.zeros((depth, A_PAD, 2 * GLA_QK), F32)
    wa_all = wa_all.at[:, :GLA_LOWRANK, :GLA_QK].set(gla_wa2[:, 0]).at[:, GLA_LOWRANK:n_low, GLA_QK:].set(gla_wa2[:, 1])
    wa_all = wa_all.astype(BF16)

    xc = jnp.concatenate([x, ctx], axis=1)
    for l in range(depth):
        last = l == depth - 1
        mods = jnp.stack([jnp.broadcast_to(mod_all[l, 0], (bn, 6 * d)), mod_all[l, 1:1 + bn]], axis=1)
        mods = mods[:, :, None, :]
        wl = w_in[l]
        w_proj = jnp.concatenate(
            [wl[:, :o_u], jnp.zeros((d, A_PAD - n_low), F32), wl[:, o_q:o_gate]], axis=1).astype(BF16)
        g1 = g_norm1[l][None]
        w_u_t = wl[:, o_u:o_q].T.astype(BF16)
        k_a, v_a, a_a, q_a, r_a, hy_a, u_t = _project(xc, mods, g1, w_proj, col_sizes, w_u_t, n_lat_tiles)

        y_gla = _gla_mix(k_a, v_a, a_a, q_a, r_a, wa_all[l], gla_ba[l], gla_gnorm[l][None], n_ctx)

        ys = _s5_mix(u_t, n_ctx, s5w_all, l)

        hy_p = (hy_w1[l], hy_b1[l], hy_w2[l], hy_b2[l], hy_w3[l], hy_freq1[l], hy_freq2[l], hy_decay[l])
        y_hy = _hyena_latent(hy_a, n_lat, n_lat if last else n_lat + n_ctx, hy_conv[l], hy_bias[l],
                             _hyena_spectra(_hyena_kernels(n_lat, hy_p)))
        if not last:
            kf_ctx = jnp.fft.fft(_hyena_kernels(n_ctx, hy_p), axis=1)
            y_hy = _hyena_context(hy_a, y_hy, n_lat // n_ctx, n_ctx, hy_conv[l], hy_bias[l], kf_ctx[0], kf_ctx[1])

        wts = (wl[:, o_gate:].astype(BF16), w_br_hy[l].astype(BF16), w_br_gla[l].astype(BF16),
               w_br_s5[l].astype(BF16), w_out[l].astype(BF16), s5_d[l][:, None], s5_w_glu[l].astype(BF16),
               s5_b_glu[l][None])
        n_tiles = n_lat_tiles if last else n_lat_tiles + n_ctx // ROW_TILE
        x_mid = _merge(xc, mods, g1, y_hy, y_gla, ys, u_t, wts, n_tiles, n_lat_tiles)
        wf = w_ffn_in[l]
        xc = _ffn(x_mid, mods, g_norm2[l][None], wf[:, :d_ff].astype(BF16), wf[:, d_ff:].astype(BF16),
                  w_ffn_out[l].astype(BF16), g_final[None], n_lat_tiles, last)
    return xc
```

```python
import functools
import math

import jax
import jax.numpy as jnp
import numpy as np
from jax import lax
from jax.experimental import pallas as pl
from jax.experimental.pallas import tpu as pltpu

F32 = jnp.float32
BF16 = jnp.bfloat16
EPS = 1e-6

HY_SHORT = 3
HY_POS_BANDS = 16
HY_N_FILT = 4
GLA_HEADS = 4
GLA_DK = 64
GLA_DV = 128
GLA_QK = GLA_HEADS * GLA_DK
GLA_V = GLA_HEADS * GLA_DV
GLA_LOWRANK = 16
GLA_TAU = 16.0
GLA_CHUNK = 64
S5_STATE = 64

LANES = 128
SUBLANES = 8
VMEM_LIMIT = 56 * 1024 * 1024

GLA_BLOCK = 4
A_PAD = LANES
ROW_TILE = 256


def _const_spec(shape):
    nd = len(shape)
    return pl.BlockSpec(shape, lambda *_: (0,) * nd, pipeline_mode=pl.Buffered(1))


def _params(n_axes):
    return pltpu.CompilerParams(dimension_semantics=("parallel",) * n_axes, vmem_limit_bytes=VMEM_LIMIT)


def _mod_kernel(c_ref, w_ref, b_ref, o_ref):
    c = c_ref[...]
    s = c * jax.nn.sigmoid(c)
    o_ref[0] = jnp.dot(s.astype(BF16), w_ref[0], preferred_element_type=F32) + b_ref[0]


def _modulation_all(cond, w_mod, b_mod):
    depth, d, n = w_mod.shape
    r = cond.shape[0]
    tn = 1536
    return pl.pallas_call(
        _mod_kernel,
        grid=(depth, n // tn),
        in_specs=[
            pl.BlockSpec((r, d), lambda l, j: (0, 0)),
            pl.BlockSpec((1, d, tn), lambda l, j: (l, 0, j)),
            pl.BlockSpec((1, 1, tn), lambda l, j: (l, 0, j)),
        ],
        out_specs=pl.BlockSpec((1, r, tn), lambda l, j: (l, 0, j)),
        out_shape=jax.ShapeDtypeStruct((depth, r, n), F32),
        compiler_params=_params(2),
        name="modulation",
    )(cond, w_mod.astype(BF16), b_mod.reshape(depth, 1, n))


def _norm_mod(x, g, shift, scale):
    y = x * lax.rsqrt(jnp.mean(x * x, axis=-1, keepdims=True) + EPS)
    return (y * g) * (1.0 + scale) + shift


_NT = (((1,), (1,)), ((), ()))


def _proj_kernel(col_sizes, d, x_ref, mod_ref, g_ref, w_ref, wt_ref, *o_refs):
    m = mod_ref[0, 0]
    h = _norm_mod(x_ref[0], g_ref[...], m[:, 0:d], m[:, d:2 * d]).astype(BF16)
    off = 0
    for o_ref, n in zip(o_refs[:-1], col_sizes):
        o_ref[0] = jnp.dot(h, w_ref[:, off:off + n], preferred_element_type=F32).astype(o_ref.dtype)
        off += n
    o_refs[-1][0] = lax.dot_general(wt_ref[...], h, _NT, preferred_element_type=F32)


def _mod_spec(d, n_lat_tiles):
    return pl.BlockSpec((1, 1, 1, 6 * d), lambda i, j: (i, (j < n_lat_tiles).astype(jnp.int32), 0, 0))


def _project(xc, mods, g, w, col_sizes, w_t, n_lat_tiles):
    b, t, d = xc.shape
    tm = ROW_TILE
    n_tot = sum(col_sizes)
    n_t = w_t.shape[0]
    out_shape = [jax.ShapeDtypeStruct((b, t, n), BF16) for n in col_sizes] + [jax.ShapeDtypeStruct((b, n_t, t), F32)]
    out_specs = [pl.BlockSpec((1, tm, n), lambda i, j: (i, j, 0)) for n in col_sizes]
    out_specs.append(pl.BlockSpec((1, n_t, tm), lambda i, j: (i, 0, j)))
    return pl.pallas_call(
        functools.partial(_proj_kernel, tuple(col_sizes), d),
        grid=(b, t // tm),
        in_specs=[
            pl.BlockSpec((1, tm, d), lambda i, j: (i, j, 0)),
            _mod_spec(d, n_lat_tiles),
            _const_spec((1, d)),
            _const_spec((d, n_tot)),
            _const_spec((n_t, d)),
        ],
        out_specs=out_specs,
        out_shape=out_shape,
        compiler_params=_params(2),
        name="in_proj",
    )(xc, mods, g, w, w_t)


def _gelu_tanh(x):
    return 0.5 * x * (1.0 + jnp.tanh(math.sqrt(2.0 / math.pi) * (x + 0.044715 * (x * x * x))))


def _merge_kernel(d, x_ref, mod_ref, g_ref, yhy_ref, ygla_ref, ys5_ref, u_ref, wg_ref, whb_ref, wgb_ref, wsb_ref,
                  wo_ref, s5d_ref, wglu_ref, bglu_ref, o_ref):
    x = x_ref[0]
    m = mod_ref[0, 0]
    h = _norm_mod(x, g_ref[...], m[:, 0:d], m[:, d:2 * d]).astype(BF16)
    y5 = (ys5_ref[0] + s5d_ref[...] * u_ref[0]).T
    g5 = _gelu_tanh(y5)
    y_s5 = g5 * jax.nn.sigmoid(jnp.dot(g5.astype(BF16), wglu_ref[...], preferred_element_type=F32) + bglu_ref[...])

    def branch(k, y, wb_ref):
        gate = jnp.dot(h, wg_ref[:, k * d:(k + 1) * d], preferred_element_type=F32)
        return jax.nn.sigmoid(gate) * jnp.dot(y, wb_ref[...], preferred_element_type=F32)

    mix = branch(0, yhy_ref[0], whb_ref) + branch(1, ygla_ref[0], wgb_ref) + branch(2, y_s5.astype(BF16), wsb_ref)
    out = jnp.dot(mix.astype(BF16), wo_ref[...], preferred_element_type=F32)
    o_ref[0] = x + m[:, 2 * d:3 * d] * out


def _merge(xc, mods, g, y_hy, y_gla, ys_s5, u_s5, wts, nj, n_lat_tiles):
    b, t, d = xc.shape
    tm = ROW_TILE
    ch = y_hy.shape[-1]
    row = lambda n: pl.BlockSpec((1, tm, n), lambda i, j: (i, j, 0))
    col = lambda n: pl.BlockSpec((1, n, tm), lambda i, j: (i, 0, j))
    w_gate, w_hy, w_gla, w_s5, w_out, s5_d, w_glu, b_glu = wts
    return pl.pallas_call(
        functools.partial(_merge_kernel, d),
        grid=(b, nj),
        in_specs=[
            row(d),
            _mod_spec(d, n_lat_tiles),
            _const_spec((1, d)),
            row(ch), row(ch), col(ch), col(ch),
            _const_spec(w_gate.shape), _const_spec(w_hy.shape), _const_spec(w_gla.shape), _const_spec(w_s5.shape),
            _const_spec(w_out.shape), _const_spec(s5_d.shape), _const_spec(w_glu.shape), _const_spec(b_glu.shape),
        ],
        out_specs=pl.BlockSpec((1, tm, d), lambda i, j: (i, j, 0)),
        out_shape=jax.ShapeDtypeStruct((b, nj * tm, d), F32),
        compiler_params=_params(2),
        name="merge",
    )(xc, mods, g, y_hy, y_gla, ys_s5, u_s5, w_gate, w_hy, w_gla, w_s5, w_out, s5_d, w_glu, b_glu)


def _ffn_kernel(d, final, x_ref, mod_ref, g_ref, wa_ref, wb_ref, wo_ref, gf_ref, o_ref):
    x = x_ref[0]
    m = mod_ref[0, 0]
    h = _norm_mod(x, g_ref[...], m[:, 3 * d:4 * d], m[:, 4 * d:5 * d]).astype(BF16)
    a = jnp.dot(h, wa_ref[...], preferred_element_type=F32)
    bb = jnp.dot(h, wb_ref[...], preferred_element_type=F32)
    act = (a * jax.nn.sigmoid(a) * bb).astype(BF16)
    y = x + m[:, 5 * d:6 * d] * jnp.dot(act, wo_ref[...], preferred_element_type=F32)
    if final:
        y = y * lax.rsqrt(jnp.mean(y * y, axis=-1, keepdims=True) + EPS) * gf_ref[...]
    o_ref[0] = y


def _ffn(x, mods, g, wa, wb, wo, g_final, n_lat_tiles, final):
    b, t, d = x.shape
    tm = ROW_TILE
    return pl.pallas_call(
        functools.partial(_ffn_kernel, d, final),
        grid=(b, t // tm),
        in_specs=[
            pl.BlockSpec((1, tm, d), lambda i, j: (i, j, 0)),
            _mod_spec(d, n_lat_tiles),
            _const_spec((1, d)),
            _const_spec(wa.shape), _const_spec(wb.shape), _const_spec(wo.shape),
            _const_spec((1, d)),
        ],
        out_specs=pl.BlockSpec((1, tm, d), lambda i, j: (i, j, 0)),
        out_shape=jax.ShapeDtypeStruct((b, t, d), F32),
        compiler_params=_params(2),
        name="ffn",
    )(x, mods, g, wa, wb, wo, g_final)


S5_CHUNK = LANES


def _s5_weights(a_re, a_im, log_dt, b_re, b_im, c_re, c_im, tc):
    g, p = a_re.shape[1:]
    i = b_re.shape[-1]
    lam_c = lax.complex(jnp.minimum(a_re, -1e-4), a_im)
    lam_dt = lam_c * jnp.exp(log_dt)[..., None]
    b_bar = ((jnp.exp(lam_dt) - 1.0) / lam_c)[..., None] * lax.complex(b_re, b_im)[None]
    c_mat = lax.complex(c_re, c_im)
    tau = jnp.arange(tc, dtype=F32)
    pw = jnp.exp(lam_dt[:, :, None, :] * tau[None, None, :, None])
    pw1 = pw * jnp.exp(lam_dt)[:, :, None, :]
    kern = jnp.einsum('gip,dgtp,dgpj->dgtij', c_mat, pw, b_bar).real
    k_lag = jnp.concatenate([kern[1][:, :0:-1], kern[0][:, :1] + kern[1][:, :1], kern[0][:, 1:]], axis=1)
    kr = jnp.pad(k_lag.transpose(0, 3, 2, 1), ((0, 0), (0, 0), (0, 0), (0, 1)))
    kr = kr.reshape(g, i, i * 2 * tc)

    inc_f = pw[0][:, None, ::-1, :] * b_bar[0].transpose(0, 2, 1)[:, :, None]
    inc_b = pw[1][:, None, :, :] * b_bar[1].transpose(0, 2, 1)[:, :, None]
    bm = jnp.concatenate([inc_f.real, inc_b.real, inc_f.imag, inc_b.imag], axis=-1).reshape(g, tc * i, 4 * p)

    out_f = c_mat.transpose(0, 2, 1)[..., None] * pw1[0].transpose(0, 2, 1)[:, :, None, :]
    out_b = c_mat.transpose(0, 2, 1)[..., None] * pw1[1][:, ::-1].transpose(0, 2, 1)[:, :, None, :]
    z = jnp.zeros((g, p, tc * i), F32)
    fl = lambda t: t.reshape(g, p, tc * i)
    cm = jnp.concatenate([fl(out_f.real), z, fl(-out_f.imag), z, z, fl(out_b.real), z, fl(-out_b.imag)], axis=1)

    lam_t = jnp.exp(lam_dt * float(tc))
    lam = jnp.stack([jnp.concatenate([lam_t[0].real, lam_t[1].real], -1),
                     jnp.concatenate([lam_t[0].imag, lam_t[1].imag], -1)], axis=1)
    return kr, bm.astype(BF16), cm.astype(BF16), lam


def _s5_kernel(nc, nc_ctx, tc, u_ref, kr_ref, bm_ref, cm_ref, lam_ref, y_ref, up_ref, dx_ref, p_ref, m_ref):
    bn, i_sz = u_ref.shape[:2]
    nl = nc - nc_ctx
    width = kr_ref.shape[-1]
    seg = width // i_sz
    for j in range(i_sz):
        lag_rows = pltpu.roll(jnp.broadcast_to(kr_ref[0, j:j + 1, :], (tc, width)), width - (tc - 1), axis=1,
                              stride=1, stride_axis=0)
        m_ref[j * tc:(j + 1) * tc, :] = jnp.concatenate(
            [lag_rows[:, i * seg:i * seg + tc] for i in range(i_sz)], axis=1).astype(BF16)
    for n in range(nc):
        for i in range(i_sz):
            up_ref[n * bn:(n + 1) * bn, i * tc:(i + 1) * tc] = u_ref[:, i, n * tc:(n + 1) * tc]
    u = up_ref[...].astype(BF16)
    dx_ref[...] = jnp.dot(u, bm_ref[0], preferred_element_type=F32)
    lam = lam_ref[0]
    lr, li = lam[0:1], lam[1:2]
    w = lam.shape[-1]
    half = w // 2
    is_f = lax.broadcasted_iota(jnp.int32, (bn, w), 1) < half

    def step(s, carry):
        sr, si = carry
        nf = jnp.where(s < nc_ctx, nl + s, s - nc_ctx)
        nb = nc - 1 - s
        rf = pl.multiple_of(nf * bn, bn)
        rb = pl.multiple_of(nb * bn, bn)
        p_ref[pl.ds(rf, bn), 0:w] = sr
        p_ref[pl.ds(rf, bn), w:2 * w] = si
        p_ref[pl.ds(rb, bn), 2 * w:3 * w] = sr
        p_ref[pl.ds(rb, bn), 3 * w:4 * w] = si
        d_re = jnp.where(is_f, dx_ref[pl.ds(rf, bn), 0:w], dx_ref[pl.ds(rb, bn), 0:w])
        d_im = jnp.where(is_f, dx_ref[pl.ds(rf, bn), w:2 * w], dx_ref[pl.ds(rb, bn), w:2 * w])
        return lr * sr - li * si + d_re, lr * si + li * sr + d_im

    zero = jnp.zeros((bn, 2 * half), F32)
    lax.fori_loop(0, nc, step, (zero, zero))
    y = jnp.dot(u, m_ref[...], preferred_element_type=F32)
    y = y + jnp.dot(p_ref[...].astype(BF16), cm_ref[0], preferred_element_type=F32)
    for n in range(nc):
        for i in range(i_sz):
            y_ref[:, i, n * tc:(n + 1) * tc] = y[n * bn:(n + 1) * bn, i * tc:(i + 1) * tc]


def _s5_mix(u_t, n_ctx, wts, layer):
    kr, bm, cm, lam = wts
    bn, ch, t = u_t.shape
    g, i = kr.shape[1:3]
    tc = kr.shape[3] // (2 * i)
    assert 2 * S5_STATE == LANES and bn % SUBLANES == 0 and n_ctx % tc == 0 and t % tc == 0 and tc % LANES == 0
    nc, nc_ctx = t // tc, n_ctx // tc
    k = tc * i
    r = nc * bn
    return pl.pallas_call(
        functools.partial(_s5_kernel, nc, nc_ctx, tc),
        grid=(g,),
        in_specs=[
            pl.BlockSpec((bn, i, t), lambda j: (0, j, 0)),
            pl.BlockSpec((None, 1, i, 2 * k), lambda j: (layer, j, 0, 0)),
            pl.BlockSpec((None, 1, k, 4 * S5_STATE), lambda j: (layer, j, 0, 0)),
            pl.BlockSpec((None, 1, 8 * S5_STATE, k), lambda j: (layer, j, 0, 0)),
            pl.BlockSpec((None, 1, 2, 2 * S5_STATE), lambda j: (layer, j, 0, 0)),
        ],
        out_specs=pl.BlockSpec((bn, i, t), lambda j: (0, j, 0)),
        out_shape=jax.ShapeDtypeStruct((bn, ch, t), F32),
        scratch_shapes=[pltpu.VMEM((r, k), F32), pltpu.VMEM((r, 4 * S5_STATE), F32),
                        pltpu.VMEM((r, 8 * S5_STATE), F32), pltpu.VMEM((k, k), BF16)],
        compiler_params=_params(1),
        name="s5_mix",
    )(u_t, kr, bm, cm, lam)


def _gla_kernel(nc, nc_ctx, k_ref, v_ref, a_ref, q_ref, r_ref, wa_ref, ba_ref, gn_ref, y_ref, of_ref, ob_ref,
                s_ref):
    c, h_n, dk, dv = GLA_CHUNK, GLA_HEADS, GLA_DK, GLA_DV
    qk = h_n * dk
    q_scale = dk ** -0.5
    nbk = GLA_BLOCK
    rb = nbk * c
    row_i = lax.broadcasted_iota(jnp.int32, (rb, rb), 0)
    col_i = lax.broadcasted_iota(jnp.int32, (rb, rb), 1)
    same = (row_i // c) == (col_i // c)
    causal = (same & (row_i >= col_i), same & (row_i <= col_i))
    cum_ops = [m.astype(BF16) for m in causal]
    att_mask = [jnp.concatenate([m] * h_n, axis=0) for m in causal]
    lane_head = lax.broadcasted_iota(jnp.int32, (1, qk), 1) // dk
    head_lanes = [lane_head == h for h in range(h_n)]

    def by_head(x):
        zero = jnp.zeros_like(x)
        return jnp.concatenate([jnp.where(m, x, zero) for m in head_lanes], axis=0)

    def block(j, d):
        rows = pl.ds(pl.multiple_of(j * rb, rb), rb)
        k = k_ref[0, rows, :].astype(F32)
        q = q_ref[0, rows, :].astype(F32) * q_scale
        v = v_ref[0, rows, :]
        z = jnp.dot(a_ref[0, rows, :], wa_ref[:, d * qk:(d + 1) * qk], preferred_element_type=F32) + ba_ref[d:d + 1, :]
        log_a = (jnp.minimum(z, 0.0) - jnp.log(1.0 + jnp.exp(-jnp.abs(z)))) * (1.0 / GLA_TAU)
        hi = log_a.astype(BF16)
        lo = (log_a - hi.astype(F32)).astype(BF16)
        cs = jnp.dot(cum_ops[d], jnp.concatenate([hi, lo], axis=1), preferred_element_type=F32)
        b = cs[:, :qk] + cs[:, qk:]
        b_tot = jnp.concatenate(
            [jnp.broadcast_to(b[n * c + (c - 1 if d == 0 else 0)][None], (c, qk)) for n in range(nbk)], axis=0)
        qd = (q * jnp.exp(b)).astype(BF16)
        kd = (k * jnp.exp(-b)).astype(BF16)
        kl = (k * jnp.exp(b_tot - b)).astype(BF16)
        qm = by_head(qd)
        intra = []
        for h in range(h_n):
            att = lax.dot_general(qm[h * rb:(h + 1) * rb], kd, _NT, preferred_element_type=F32)
            att = jnp.where(causal[d], att, 0.0).astype(BF16)
            intra.append(jnp.dot(att, v[:, h * dv:(h + 1) * dv], preferred_element_type=F32))
        s_t = s_ref[d]
        inter = [None] * nbk
        for n in (range(nbk) if d == 0 else reversed(range(nbk))):
            r0 = n * c
            qm_n = jnp.concatenate([qm[h * rb + r0:h * rb + r0 + c] for h in range(h_n)], axis=0)
            inter[n] = lax.dot_general(qm_n, s_t.astype(BF16), _NT, preferred_element_type=F32)
            v_t = jnp.concatenate(
                [jnp.concatenate([v[r0:r0 + c, h * dv:(h + 1) * dv] for h in range(p, p + dv // c)], axis=0).T
                 for p in range(0, h_n, dv // c)], axis=1)
            s_t = s_t * jnp.exp(b_tot[r0:r0 + 1]) + jnp.dot(v_t, by_head(kl[r0:r0 + c]), preferred_element_type=F32)
        s_ref[d] = s_t
        o = jnp.concatenate(
            [intra[h] + jnp.concatenate([inter[n][h * c:(h + 1) * c] for n in range(nbk)], axis=0)
             for h in range(h_n)], axis=1)
        return rows, o

    def readout(rows, o):
        r = r_ref[0, rows, :].astype(F32)
        gate = r * jax.nn.sigmoid(r)
        outs = []
        for h in range(h_n):
            oh = o[:, h * dv:(h + 1) * dv]
            oh = oh * lax.rsqrt(jnp.mean(oh * oh, axis=-1, keepdims=True) + EPS) * gn_ref[...]
            outs.append(oh * gate[:, h * dv:(h + 1) * dv])
        y_ref[0, rows, :] = jnp.concatenate(outs, axis=1).astype(y_ref.dtype)

    def scan_step(i, carry, lo_block, hi_block, phase):
        rows_f, o_f = block(lo_block + i, 0)
        if phase == "park":
            of_ref[rows_f, :] = o_f
        rows_b, o_b = block(hi_block - 1 - i, 1)
        if phase == "park":
            ob_ref[rows_b, :] = o_b
        elif phase == "meet":
            readout(rows_f, o_f + o_b)
        else:
            readout(rows_f, o_f + ob_ref[rows_f, :])
            readout(rows_b, o_b + of_ref[rows_b, :])
        return carry

    def scan(lo_block, n_blocks):
        step = functools.partial(scan_step, lo_block=lo_block, hi_block=lo_block + n_blocks)
        half = n_blocks // 2
        lax.fori_loop(0, half, functools.partial(step, phase="park"), 0)
        if n_blocks % 2:
            step(jnp.int32(half), 0, phase="meet")
        lax.fori_loop(n_blocks - half, n_blocks, functools.partial(step, phase="finish"), 0)

    nl, n_ctx = (nc - nc_ctx) // nbk, nc_ctx // nbk
    s_ref[...] = jnp.zeros(s_ref.shape, F32)
    scan(nl, n_ctx)
    scan(0, nl)


def _gla_mix(k, v, a, q, r, wa, ba, gnorm, n_ctx):
    bn, t, qk = k.shape
    vd = v.shape[-1]
    c = GLA_CHUNK
    assert t % (c * GLA_BLOCK) == 0 and n_ctx % (c * GLA_BLOCK) == 0 and GLA_DV % c == 0
    seq = lambda n, bufs=2: pl.BlockSpec((1, t, n), lambda i: (i, 0, 0), pipeline_mode=pl.Buffered(bufs))
    return pl.pallas_call(
        functools.partial(_gla_kernel, t // c, n_ctx // c),
        grid=(bn,),
        in_specs=[seq(qk), seq(vd), seq(a.shape[-1]), seq(qk), seq(vd, 1),
                  _const_spec(wa.shape), _const_spec(ba.shape), _const_spec(gnorm.shape)],
        out_specs=pl.BlockSpec((1, t, vd), lambda i: (i, 0, 0)),
        out_shape=jax.ShapeDtypeStruct((bn, t, vd), BF16),
        scratch_shapes=[pltpu.VMEM((t, vd), F32), pltpu.VMEM((t, vd), F32), pltpu.VMEM((2, GLA_DV, qk), F32)],
        compiler_params=_params(1),
        name="gla_mix",
    )(k, v, a, q, r, wa, ba, gnorm)


HY_NB = LANES
HY_XPITCH = HY_NB + SUBLANES
HY_APITCH = 2 * HY_NB + SUBLANES
HY_CT = LANES
HY_UNROLL_OUTER = 16
HY_UNROLL_INNER = 8


def _hyena_filters(n, pos, params):
    w1, b1, w2, b2, w3, freq1, freq2, decay = (p.astype(F32) for p in params)
    t = (pos / (n - 1))[:, None]
    bands = jnp.linspace(1e-4, HY_POS_BANDS - 1, HY_POS_BANDS, dtype=F32)
    ang = (2.0 * math.pi / n) * pos[:, None] * bands[None]
    z = jnp.concatenate([t, jnp.cos(ang), -jnp.sin(ang)], axis=-1)
    hid = jnp.sin(freq1 * (z @ w1 + b1))
    hid = jnp.sin(freq2 * (hid @ w2 + b2))
    w3 = w3.reshape(w3.shape[0], HY_N_FILT, -1)
    window = jnp.exp(-t[None] * jnp.abs(decay).reshape(HY_N_FILT, 1, -1))
    return jnp.einsum('nk,kfc->fnc', hid, w3) * window


def _hyena_kernels(n, params):
    slot = jnp.arange(2 * n)
    h = _hyena_filters(n, jnp.where(slot < n, slot, 2 * n - slot).astype(F32), params)
    is_fwd, is_bwd = (slot < n)[None, :, None], (slot > n)[None, :, None]
    k = jnp.where(is_fwd, h[0::2], 0.0) + jnp.where(is_bwd, h[1::2], 0.0)
    lag0_bwd = _hyena_filters(n, jnp.zeros((1,), F32), params)[1::2]
    k = k + jnp.where(slot[None, :, None] == 0, lag0_bwd, 0.0)
    return k * lax.rsqrt(jnp.sum(k * k, axis=1, keepdims=True) + EPS)


def _spectrum_kernel(na, k_ref, f1_ref, f2_ref, o_ref, a2):
    nb, ap = HY_NB, HY_APITCH

    def stage1(b, c):
        xs = k_ref[0, pl.ds(b, na, stride=nb), :].astype(BF16)
        r = jnp.dot(f1_ref[b], xs, preferred_element_type=F32)
        a2[pl.ds(b, na, stride=ap), :] = r[:na]
        a2[pl.ds(nb + b, na, stride=ap), :] = r[na:]
        return c

    def stage2(ka, c):
        rows = pl.ds(pl.multiple_of(ka * ap, 8), 2 * nb)
        xf = jnp.dot(f2_ref[...], a2[rows, :].astype(BF16), preferred_element_type=F32)
        o_ref[0, ka, 0] = xf[:nb].astype(o_ref.dtype)
        o_ref[0, ka, 1] = xf[nb:].astype(o_ref.dtype)
        return c

    lax.fori_loop(0, nb, stage1, 0, unroll=HY_UNROLL_INNER)
    lax.fori_loop(0, na, stage2, 0, unroll=HY_UNROLL_INNER)


def _hyena_spectra(kernels):
    nf, n, ch = kernels.shape
    nb, ct = HY_NB, HY_CT
    na = n // nb
    _, f2, _, _, f1_full = _dft_tables(n // 2)
    return pl.pallas_call(
        functools.partial(_spectrum_kernel, na),
        grid=(nf, ch // ct),
        in_specs=[pl.BlockSpec((1, n, ct), lambda f, j: (f, 0, j)), _const_spec(f1_full.shape), _const_spec(f2.shape)],
        out_specs=pl.BlockSpec((1, na, 2, nb, ct), lambda f, j: (f, 0, 0, 0, j)),
        out_shape=jax.ShapeDtypeStruct((nf, na, 2, nb, ch), BF16),
        scratch_shapes=[pltpu.VMEM((na * HY_APITCH, ct), F32)],
        compiler_params=_params(2),
        name="hyena_spectrum",
    )(kernels, f1_full, f2)


def _cplx_block(m):
    return np.block([[m.real, -m.imag], [m.imag, m.real]])


@functools.lru_cache(maxsize=None)
def _dft_tables(n_seq):
    nb = HY_NB
    n = 2 * n_seq
    na = n // nb
    ha = na // 2
    ka = np.arange(na)[:, None]
    b = np.arange(nb)
    tw = np.exp(-2j * np.pi * ka * b[None, :] / n)
    w_a = np.exp(-2j * np.pi * ka * np.arange(ha)[None, :] / na)
    f1 = np.stack([_cplx_block(tw[:, i:i + 1] * w_a) for i in range(nb)])
    g = np.stack([_cplx_block((np.conj(tw[:, i:i + 1] * w_a)).T / n) for i in range(nb)])
    w_b = np.exp(-2j * np.pi * b[:, None] * b[None, :] / nb)
    f2 = _cplx_block(w_b)
    f2i = _cplx_block(np.conj(w_b))
    w_full = np.exp(-2j * np.pi * ka * np.arange(na)[None, :] / na)
    f1_full = np.stack([np.concatenate([(tw[:, i:i + 1] * w_full).real, (tw[:, i:i + 1] * w_full).imag], axis=0)
                        for i in range(nb)])
    return tuple(jnp.asarray(t, dtype=BF16) for t in (f1, f2, f2i, g, f1_full))


def _hyena_kernel(na, x1_ref, x2_ref, v_ref, cw_ref, bias_ref, kf1_ref, kf2_ref, f1_ref, f2_ref, f2i_ref, g_ref,
                  o_ref, rawf, vp, gp, yp, a2):
    nb, xp, ap = HY_NB, HY_XPITCH, HY_APITCH
    ha = na // 2
    n_seq = ha * nb
    ct = o_ref.shape[-1]
    halo = SUBLANES
    rawf[:, 0:halo, :] = jnp.zeros((2, halo, ct), F32)
    rawf[:, halo + n_seq:2 * halo + n_seq, :] = jnp.zeros((2, halo, ct), F32)

    def short_conv_into(raw_ref, which, dst):
        w0, w1, w2 = (cw_ref[j, which:which + 1, :] for j in range(HY_SHORT))
        for e in range(2):
            rawf[e, halo:halo + n_seq, :] = raw_ref[e].astype(F32)

            def body(a, c):
                base = pl.multiple_of(a * nb, nb) + halo
                u = (w0 * rawf[e, pl.ds(base - 1, nb), :] + w1 * rawf[e, pl.ds(base, nb), :]
                     + w2 * rawf[e, pl.ds(base + 1, nb), :])
                dst[e, pl.ds(pl.multiple_of(a * xp, 8), nb), :] = u
                return c

            lax.fori_loop(0, ha, body, 0)

    def long_conv(kf_ref):
        def stage1(b, c):
            xs = jnp.concatenate([vp[0, pl.ds(b, ha, stride=xp), :], vp[1, pl.ds(b, ha, stride=xp), :]], axis=0)
            r = jnp.dot(f1_ref[b], xs.astype(BF16), preferred_element_type=F32)
            a2[pl.ds(b, na, stride=ap), :] = r[:na]
            a2[pl.ds(nb + b, na, stride=ap), :] = r[na:]
            return c

        def stage2(kp, c):
            rows = [pl.ds(pl.multiple_of((2 * kp + e) * ap, 8), 2 * nb) for e in range(2)]
            slab = jnp.concatenate([a2[r, :] for r in rows], axis=1).astype(BF16)
            xf = jnp.dot(f2_ref[...], slab, preferred_element_type=F32)
            xr, xi = xf[:nb], xf[nb:]
            kr = jnp.concatenate([kf_ref[0, 2 * kp + e, 0] for e in range(2)], axis=1).astype(F32)
            ki = jnp.concatenate([kf_ref[0, 2 * kp + e, 1] for e in range(2)], axis=1).astype(F32)
            z = jnp.concatenate([xr * kr - xi * ki, xr * ki + xi * kr], axis=0).astype(BF16)
            back = jnp.dot(f2i_ref[...], z, preferred_element_type=F32)
            for e in range(2):
                a2[rows[e], :] = back[:, e * ct:(e + 1) * ct]
            return c

        def stage3(b, c):
            s = jnp.concatenate([a2[pl.ds(b, na, stride=ap), :], a2[pl.ds(nb + b, na, stride=ap), :]], axis=0)
            y = jnp.dot(g_ref[b], s.astype(BF16), preferred_element_type=F32)
            yp[0, pl.ds(b, ha, stride=xp), :] = y[:ha]
            yp[1, pl.ds(b, ha, stride=xp), :] = y[ha:]
            return c

        lax.fori_loop(0, nb, stage1, 0, unroll=HY_UNROLL_OUTER)
        lax.fori_loop(0, na // 2, stage2, 0, unroll=HY_UNROLL_INNER)
        lax.fori_loop(0, nb, stage3, 0, unroll=HY_UNROLL_OUTER)

    def gated(bias_row, write):
        for e in range(2):
            def body(a, c):
                rows = pl.ds(pl.multiple_of(a * xp, 8), nb)
                write(e, a, rows, gp[e, rows, :] * (yp[e, rows, :] + bias_row * vp[e, rows, :]))
                return c

            lax.fori_loop(0, ha, body, 0)

    def write_z(e, a, rows, val):
        vp[e, rows, :] = val

    def write_out(e, a, rows, val):
        o_ref[e, pl.ds(pl.multiple_of(a * nb, nb), nb), :] = val.astype(o_ref.dtype)

    short_conv_into(v_ref, 2, vp)
    short_conv_into(x1_ref, 0, gp)
    long_conv(kf1_ref)
    gated(bias_ref[0:1, :], write_z)
    short_conv_into(x2_ref, 1, gp)
    long_conv(kf2_ref)
    gated(bias_ref[1:2, :], write_out)


def _hyena_latent(hy_all, n_seq, t_out, conv_w, bias, kf):
    bn = hy_all.shape[0]
    ch = hy_all.shape[-1] // 3
    ct = HY_CT
    nct = ch // ct
    nb = HY_NB
    na = 2 * n_seq // nb
    assert bn % 2 == 0 and ch % ct == 0 and n_seq % (8 * nb) == 0
    f1, f2, f2i, g, _ = _dft_tables(n_seq)
    col = lambda which: pl.BlockSpec((2, n_seq, ct), lambda j, p: (p, 0, which * nct + j))
    kf_spec = lambda f: pl.BlockSpec((1, na, 2, nb, ct), lambda j, p: (f, 0, 0, 0, j), pipeline_mode=pl.Buffered(1))
    pad_rows = (na // 2) * HY_XPITCH
    return pl.pallas_call(
        functools.partial(_hyena_kernel, na),
        grid=(nct, bn // 2),
        in_specs=[col(0), col(1), col(2),
                  pl.BlockSpec((HY_SHORT, 3, ct), lambda j, p: (0, 0, j)),
                  pl.BlockSpec((2, ct), lambda j, p: (0, j)),
                  kf_spec(0), kf_spec(1),
                  _const_spec(f1.shape), _const_spec(f2.shape), _const_spec(f2i.shape), _const_spec(g.shape)],
        out_specs=pl.BlockSpec((2, n_seq, ct), lambda j, p: (p, 0, j)),
        out_shape=jax.ShapeDtypeStruct((bn, t_out, ch), BF16),
        scratch_shapes=[pltpu.VMEM((2, n_seq + 2 * SUBLANES, ct), F32), pltpu.VMEM((2, pad_rows, ct), F32),
                        pltpu.VMEM((2, pad_rows, ct), F32), pltpu.VMEM((2, pad_rows, ct), F32),
                        pltpu.VMEM((na * HY_APITCH, ct), F32)],
        compiler_params=_params(2),
        name="hyena_latent",
    )(hy_all, hy_all, hy_all, conv_w.reshape(HY_SHORT, 3, ch), bias, kf, kf, f1, f2, f2i, g)


@functools.lru_cache(maxsize=None)
def _dense_dft_tables(n_seq):
    n = 2 * n_seq
    ang = 2.0 * np.pi * np.arange(n)[:, None] * np.arange(n_seq)[None, :] / n
    fd = np.concatenate([np.cos(ang), -np.sin(ang)], axis=0)
    gd = np.concatenate([np.cos(ang).T, -np.sin(ang).T], axis=1) / n
    return jnp.asarray(fd, dtype=BF16), jnp.asarray(gd, dtype=BF16)


def _hyena_ctx_kernel(n_seq, x1_ref, x2_ref, v_ref, cw_ref, bias_ref, kf1_ref, kf2_ref, fd_ref, gd_ref, buf_ref,
                      o_ref):
    del buf_ref
    n = 2 * n_seq

    def short_conv(raw_ref, which):
        x = raw_ref[0].astype(F32)
        zero = jnp.zeros((1, x.shape[-1]), F32)
        prev = jnp.concatenate([zero, x[:-1]], axis=0)
        nxt = jnp.concatenate([x[1:], zero], axis=0)
        return cw_ref[0, which:which + 1, :] * prev + cw_ref[1, which:which + 1, :] * x + cw_ref[2, which:which + 1, :] * nxt

    def long_conv(u, kf_ref, bias_row):
        xf = jnp.dot(fd_ref[...], u.astype(BF16), preferred_element_type=F32)
        xr, xi = xf[:n], xf[n:]
        kr, ki = kf_ref[0], kf_ref[1]
        z = jnp.concatenate([xr * kr - xi * ki, xr * ki + xi * kr], axis=0).astype(BF16)
        return jnp.dot(gd_ref[...], z, preferred_element_type=F32) + bias_row * u

    x1, x2, v = short_conv(x1_ref, 0), short_conv(x2_ref, 1), short_conv(v_ref, 2)
    z = x1 * long_conv(v, kf1_ref, bias_ref[0:1, :])
    o_ref[0] = (x2 * long_conv(z, kf2_ref, bias_ref[1:2, :])).astype(o_ref.dtype)


def _hyena_context(hy_all, y_buf, row_block, n_seq, conv_w, bias, kf1, kf2):
    bn = hy_all.shape[0]
    ch = hy_all.shape[-1] // 3
    fd, gd = _dense_dft_tables(n_seq)
    col = lambda which: pl.BlockSpec((1, n_seq, ch), lambda i: (i, row_block, which))
    split = lambda kf: jnp.stack([kf.real, kf.imag], axis=0)
    return pl.pallas_call(
        functools.partial(_hyena_ctx_kernel, n_seq),
        grid=(bn,),
        in_specs=[col(0), col(1), col(2), _const_spec((HY_SHORT, 3, ch)), _const_spec((2, ch)),
                  _const_spec((2, 2 * n_seq, ch)), _const_spec((2, 2 * n_seq, ch)),
                  _const_spec(fd.shape), _const_spec(gd.shape), pl.BlockSpec(memory_space=pl.ANY)],
        out_specs=pl.BlockSpec((1, n_seq, ch), lambda i: (i, row_block, 0)),
        out_shape=jax.ShapeDtypeStruct(y_buf.shape, y_buf.dtype),
        input_output_aliases={9: 0},
        compiler_params=_params(1),
        name="hyena_context",
    )(hy_all, hy_all, hy_all, conv_w.reshape(HY_SHORT, 3, ch), bias, split(kf1), split(kf2), fd, gd, y_buf)


def kernel(x, c, ctx, c_ctx, w_mod, b_mod, g_norm1, g_norm2, w_in, hy_conv, hy_w1, hy_b1, hy_w2, hy_b2, hy_w3,
           hy_freq1, hy_freq2, hy_decay, hy_bias, gla_wa2, gla_ba, gla_gnorm, s5_a_re, s5_a_im, s5_log_dt, s5_b_re,
           s5_b_im, s5_c_re, s5_c_im, s5_d, s5_w_glu, s5_b_glu, w_br_hy, w_br_gla, w_br_s5, w_out, w_ffn_in,
           w_ffn_out, g_final):
    bn, n_lat, d = x.shape
    n_ctx = ctx.shape[1]
    depth = w_in.shape[0]
    d_hy = w_br_hy.shape[1]
    s5_ch = w_br_s5.shape[1]
    d_ff = w_ffn_out.shape[1]
    assert n_ctx == ROW_TILE and n_lat % ROW_TILE == 0
    n_lat_tiles = n_lat // ROW_TILE
    n_low = 2 * GLA_LOWRANK
    o_a = GLA_QK + GLA_V
    o_u = o_a + n_low
    o_q = o_u + s5_ch
    o_gate = o_q + GLA_QK + GLA_V + 3 * d_hy
    col_sizes = (GLA_QK, GLA_V, A_PAD, GLA_QK, GLA_V, 3 * d_hy)

    cond = jnp.concatenate([c_ctx[None], c], axis=0)
    cond = jnp.pad(cond, ((0, (-cond.shape[0]) % SUBLANES), (0, 0)))
    mod_all = _modulation_all(cond, w_mod, b_mod)

    s5w_all = jax.vmap(functools.partial(_s5_weights, tc=S5_CHUNK))(
        s5_a_re, s5_a_im, s5_log_dt, s5_b_re, s5_b_im, s5_c_re, s5_c_im)
    wa_all = jnp.zeros((depth, A_PAD, 2 * GLA_QK), F32)
    wa_all = wa_all.at[:, :GLA_LOWRANK, :GLA_QK].set(gla_wa2[:, 0]).at[:, GLA_LOWRANK:n_low, GLA_QK:].set(gla_wa2[:, 1])
    wa_all = wa_all.astype(BF16)

    xc = jnp.concatenate([x, ctx], axis=1)
    for l in range(depth):
        last = l == depth - 1
        mods = jnp.stack([jnp.broadcast_to(mod_all[l, 0], (bn, 6 * d)), mod_all[l, 1:1 + bn]], axis=1)
        mods = mods[:, :, None, :]
        wl = w_in[l]
        w_proj = jnp.concatenate(
            [wl[:, :o_u], jnp.zeros((d, A_PAD - n_low), F32), wl[:, o_q:o_gate]], axis=1).astype(BF16)
        g1 = g_norm1[l][None]
        w_u_t = wl[:, o_u:o_q].T.astype(BF16)
        k_a, v_a, a_a, q_a, r_a, hy_a, u_t = _project(xc, mods, g1, w_proj, col_sizes, w_u_t, n_lat_tiles)

        y_gla = _gla_mix(k_a, v_a, a_a, q_a, r_a, wa_all[l], gla_ba[l], gla_gnorm[l][None], n_ctx)

        ys = _s5_mix(u_t, n_ctx, s5w_all, l)

        hy_p = (hy_w1[l], hy_b1[l], hy_w2[l], hy_b2[l], hy_w3[l], hy_freq1[l], hy_freq2[l], hy_decay[l])
        y_hy = _hyena_latent(hy_a, n_lat, n_lat if last else n_lat + n_ctx, hy_conv[l], hy_bias[l],
                             _hyena_spectra(_hyena_kernels(n_lat, hy_p)))
        if not last:
            kf_ctx = jnp.fft.fft(_hyena_kernels(n_ctx, hy_p), axis=1)
            y_hy = _hyena_context(hy_a, y_hy, n_lat // n_ctx, n_ctx, hy_conv[l], hy_bias[l], kf_ctx[0], kf_ctx[1])

        wts = (wl[:, o_gate:].astype(BF16), w_br_hy[l].astype(BF16), w_br_gla[l].astype(BF16),
               w_br_s5[l].astype(BF16), w_out[l].astype(BF16), s5_d[l][:, None], s5_w_glu[l].astype(BF16),
               s5_b_glu[l][None])
        n_tiles = n_lat_tiles if last else n_lat_tiles + n_ctx // ROW_TILE
        x_mid = _merge(xc, mods, g1, y_hy, y_gla, ys, u_t, wts, n_tiles, n_lat_tiles)
        wf = w_ffn_in[l]
        xc = _ffn(x_mid, mods, g_norm2[l][None], wf[:, :d_ff].astype(BF16), wf[:, d_ff:].astype(BF16),
                  w_ffn_out[l].astype(BF16), g_final[None], n_lat_tiles, last)
    return xc
```

```python
import functools
import math

import jax
import jax.numpy as jnp
import numpy as np
from jax import lax
from jax.experimental import pallas as pl
from jax.experimental.pallas import tpu as pltpu

F32 = jnp.float32
BF16 = jnp.bfloat16
EPS = 1e-6

HY_SHORT = 3
HY_POS_BANDS = 16
HY_N_FILT = 4
GLA_HEADS = 4
GLA_DK = 64
GLA_DV = 128
GLA_QK = GLA_HEADS * GLA_DK
GLA_V = GLA_HEADS * GLA_DV
GLA_LOWRANK = 16
GLA_TAU = 16.0
GLA_CHUNK = 64
S5_STATE = 64

LANES = 128
SUBLANES = 8
VMEM_LIMIT = 56 * 1024 * 1024

GLA_BLOCK = 4
A_PAD = LANES
ROW_TILE = 256


def _const_spec(shape):
    nd = len(shape)
    return pl.BlockSpec(shape, lambda *_: (0,) * nd, pipeline_mode=pl.Buffered(1))


def _params(n_axes):
    return pltpu.CompilerParams(dimension_semantics=("parallel",) * n_axes, vmem_limit_bytes=VMEM_LIMIT)


def _mod_kernel(c_ref, w_ref, b_ref, o_ref):
    c = c_ref[...]
    s = c * jax.nn.sigmoid(c)
    o_ref[0] = jnp.dot(s.astype(BF16), w_ref[0], preferred_element_type=F32) + b_ref[0]


def _modulation_all(cond, w_mod, b_mod):
    depth, d, n = w_mod.shape
    r = cond.shape[0]
    tn = 1536
    return pl.pallas_call(
        _mod_kernel,
        grid=(depth, n // tn),
        in_specs=[
            pl.BlockSpec((r, d), lambda l, j: (0, 0)),
            pl.BlockSpec((1, d, tn), lambda l, j: (l, 0, j)),
            pl.BlockSpec((1, 1, tn), lambda l, j: (l, 0, j)),
        ],
        out_specs=pl.BlockSpec((1, r, tn), lambda l, j: (l, 0, j)),
        out_shape=jax.ShapeDtypeStruct((depth, r, n), F32),
        compiler_params=_params(2),
        name="modulation",
    )(cond, w_mod.astype(BF16), b_mod.reshape(depth, 1, n))


def _norm_mod(x, g, shift, scale):
    y = x * lax.rsqrt(jnp.mean(x * x, axis=-1, keepdims=True) + EPS)
    return (y * g) * (1.0 + scale) + shift


_NT = (((1,), (1,)), ((), ()))


def _proj_kernel(col_sizes, d, x_ref, mod_ref, g_ref, w_ref, wt_ref, *o_refs):
    m = mod_ref[0, 0]
    h = _norm_mod(x_ref[0], g_ref[...], m[:, 0:d], m[:, d:2 * d]).astype(BF16)
    off = 0
    for o_ref, n in zip(o_refs[:-1], col_sizes):
        o_ref[0] = jnp.dot(h, w_ref[:, off:off + n], preferred_element_type=F32).astype(o_ref.dtype)
        off += n
    o_refs[-1][0] = lax.dot_general(wt_ref[...], h, _NT, preferred_element_type=F32)


def _mod_spec(d, n_lat_tiles):
    return pl.BlockSpec((1, 1, 1, 6 * d), lambda i, j: (i, (j < n_lat_tiles).astype(jnp.int32), 0, 0))


def _project(xc, mods, g, w, col_sizes, w_t, n_lat_tiles):
    b, t, d = xc.shape
    tm = ROW_TILE
    n_tot = sum(col_sizes)
    n_t = w_t.shape[0]
    out_shape = [jax.ShapeDtypeStruct((b, t, n), BF16) for n in col_sizes] + [jax.ShapeDtypeStruct((b, n_t, t), F32)]
    out_specs = [pl.BlockSpec((1, tm, n), lambda i, j: (i, j, 0)) for n in col_sizes]
    out_specs.append(pl.BlockSpec((1, n_t, tm), lambda i, j: (i, 0, j)))
    return pl.pallas_call(
        functools.partial(_proj_kernel, tuple(col_sizes), d),
        grid=(b, t // tm),
        in_specs=[
            pl.BlockSpec((1, tm, d), lambda i, j: (i, j, 0)),
            _mod_spec(d, n_lat_tiles),
            _const_spec((1, d)),
            _const_spec((d, n_tot)),
            _const_spec((n_t, d)),
        ],
        out_specs=out_specs,
        out_shape=out_shape,
        compiler_params=_params(2),
        name="in_proj",
    )(xc, mods, g, w, w_t)


def _gelu_tanh(x):
    return 0.5 * x * (1.0 + jnp.tanh(math.sqrt(2.0 / math.pi) * (x + 0.044715 * (x * x * x))))


def _merge_kernel(d, n_lat_tiles, x_ref, mod_ref, g_ref, yhy_ref, yhyc_ref, ygla_ref, ys5_ref, u_ref, wg_ref, whb_ref,
                  wgb_ref, wsb_ref, wo_ref, s5d_ref, wglu_ref, bglu_ref, o_ref):
    x = x_ref[0]
    m = mod_ref[0, 0]
    h = _norm_mod(x, g_ref[...], m[:, 0:d], m[:, d:2 * d]).astype(BF16)
    y_hy = jnp.where(pl.program_id(1) < n_lat_tiles, yhy_ref[0], yhyc_ref[0])
    y5 = (ys5_ref[0] + s5d_ref[...] * u_ref[0]).T
    g5 = _gelu_tanh(y5)
    y_s5 = g5 * jax.nn.sigmoid(jnp.dot(g5.astype(BF16), wglu_ref[...], preferred_element_type=F32) + bglu_ref[...])

    def branch(k, y, wb_ref):
        gate = jnp.dot(h, wg_ref[:, k * d:(k + 1) * d], preferred_element_type=F32)
        return jax.nn.sigmoid(gate) * jnp.dot(y, wb_ref[...], preferred_element_type=F32)

    mix = branch(0, y_hy, whb_ref) + branch(1, ygla_ref[0], wgb_ref) + branch(2, y_s5.astype(BF16), wsb_ref)
    out = jnp.dot(mix.astype(BF16), wo_ref[...], preferred_element_type=F32)
    o_ref[0] = x + m[:, 2 * d:3 * d] * out


def _merge(xc, mods, g, y_hy, y_hy_ctx, y_gla, ys_s5, u_s5, wts, nj, n_lat_tiles):
    b, t, d = xc.shape
    tm = ROW_TILE
    ch = y_hy.shape[-1]
    row = lambda n: pl.BlockSpec((1, tm, n), lambda i, j: (i, j, 0))
    lat_row = pl.BlockSpec((1, tm, ch), lambda i, j: (i, jnp.minimum(j, n_lat_tiles - 1), 0))
    ctx_row = pl.BlockSpec((1, tm, ch), lambda i, j: (i, 0, 0))
    col = lambda n: pl.BlockSpec((1, n, tm), lambda i, j: (i, 0, j))
    w_gate, w_hy, w_gla, w_s5, w_out, s5_d, w_glu, b_glu = wts
    return pl.pallas_call(
        functools.partial(_merge_kernel, d, n_lat_tiles),
        grid=(b, nj),
        in_specs=[
            row(d),
            _mod_spec(d, n_lat_tiles),
            _const_spec((1, d)),
            lat_row, ctx_row, row(ch), col(ch), col(ch),
            _const_spec(w_gate.shape), _const_spec(w_hy.shape), _const_spec(w_gla.shape), _const_spec(w_s5.shape),
            _const_spec(w_out.shape), _const_spec(s5_d.shape), _const_spec(w_glu.shape), _const_spec(b_glu.shape),
        ],
        out_specs=pl.BlockSpec((1, tm, d), lambda i, j: (i, j, 0)),
        out_shape=jax.ShapeDtypeStruct((b, nj * tm, d), F32),
        compiler_params=_params(2),
        name="merge",
    )(xc, mods, g, y_hy, y_hy_ctx, y_gla, ys_s5, u_s5, w_gate, w_hy, w_gla, w_s5, w_out, s5_d, w_glu, b_glu)


def _ffn_kernel(d, final, x_ref, mod_ref, g_ref, wa_ref, wb_ref, wo_ref, gf_ref, o_ref):
    x = x_ref[0]
    m = mod_ref[0, 0]
    h = _norm_mod(x, g_ref[...], m[:, 3 * d:4 * d], m[:, 4 * d:5 * d]).astype(BF16)
    a = jnp.dot(h, wa_ref[...], preferred_element_type=F32)
    bb = jnp.dot(h, wb_ref[...], preferred_element_type=F32)
    act = (a * jax.nn.sigmoid(a) * bb).astype(BF16)
    y = x + m[:, 5 * d:6 * d] * jnp.dot(act, wo_ref[...], preferred_element_type=F32)
    if final:
        y = y * lax.rsqrt(jnp.mean(y * y, axis=-1, keepdims=True) + EPS) * gf_ref[...]
    o_ref[0] = y


def _ffn(x, mods, g, wa, wb, wo, g_final, n_lat_tiles, final):
    b, t, d = x.shape
    tm = ROW_TILE
    return pl.pallas_call(
        functools.partial(_ffn_kernel, d, final),
        grid=(b, t // tm),
        in_specs=[
            pl.BlockSpec((1, tm, d), lambda i, j: (i, j, 0)),
            _mod_spec(d, n_lat_tiles),
            _const_spec((1, d)),
            _const_spec(wa.shape), _const_spec(wb.shape), _const_spec(wo.shape),
            _const_spec((1, d)),
        ],
        out_specs=pl.BlockSpec((1, tm, d), lambda i, j: (i, j, 0)),
        out_shape=jax.ShapeDtypeStruct((b, t, d), F32),
        compiler_params=_params(2),
        name="ffn",
    )(x, mods, g, wa, wb, wo, g_final)


S5_CHUNK = LANES


def _s5_weights(a_re, a_im, log_dt, b_re, b_im, c_re, c_im, tc):
    g, p = a_re.shape[1:]
    i = b_re.shape[-1]
    lam_c = lax.complex(jnp.minimum(a_re, -1e-4), a_im)
    lam_dt = lam_c * jnp.exp(log_dt)[..., None]
    b_bar = ((jnp.exp(lam_dt) - 1.0) / lam_c)[..., None] * lax.complex(b_re, b_im)[None]
    c_mat = lax.complex(c_re, c_im)
    tau = jnp.arange(tc, dtype=F32)
    pw = jnp.exp(lam_dt[:, :, None, :] * tau[None, None, :, None])
    pw1 = pw * jnp.exp(lam_dt)[:, :, None, :]
    kern = jnp.einsum('gip,dgtp,dgpj->dgtij', c_mat, pw, b_bar).real
    k_lag = jnp.concatenate([kern[1][:, :0:-1], kern[0][:, :1] + kern[1][:, :1], kern[0][:, 1:]], axis=1)
    kr = jnp.pad(k_lag.transpose(0, 3, 2, 1), ((0, 0), (0, 0), (0, 0), (0, 1)))
    kr = kr.reshape(g, i, i * 2 * tc)

    inc_f = pw[0][:, None, ::-1, :] * b_bar[0].transpose(0, 2, 1)[:, :, None]
    inc_b = pw[1][:, None, :, :] * b_bar[1].transpose(0, 2, 1)[:, :, None]
    bm = jnp.concatenate([inc_f.real, inc_b.real, inc_f.imag, inc_b.imag], axis=-1).reshape(g, tc * i, 4 * p)

    out_f = c_mat.transpose(0, 2, 1)[..., None] * pw1[0].transpose(0, 2, 1)[:, :, None, :]
    out_b = c_mat.transpose(0, 2, 1)[..., None] * pw1[1][:, ::-1].transpose(0, 2, 1)[:, :, None, :]
    z = jnp.zeros((g, p, tc * i), F32)
    fl = lambda t: t.reshape(g, p, tc * i)
    cm = jnp.concatenate([fl(out_f.real), z, fl(-out_f.imag), z, z, fl(out_b.real), z, fl(-out_b.imag)], axis=1)

    lam_t = jnp.exp(lam_dt * float(tc))
    lam = jnp.stack([jnp.concatenate([lam_t[0].real, lam_t[1].real], -1),
                     jnp.concatenate([lam_t[0].imag, lam_t[1].imag], -1)], axis=1)
    return kr, bm.astype(BF16), cm.astype(BF16), lam


def _s5_kernel(nc, nc_ctx, tc, u_ref, kr_ref, bm_ref, cm_ref, lam_ref, y_ref, up_ref, dx_ref, p_ref, m_ref):
    bn, i_sz = u_ref.shape[:2]
    nl = nc - nc_ctx
    width = kr_ref.shape[-1]
    seg = width // i_sz
    for j in range(i_sz):
        lag_rows = pltpu.roll(jnp.broadcast_to(kr_ref[0, j:j + 1, :], (tc, width)), width - (tc - 1), axis=1,
                              stride=1, stride_axis=0)
        m_ref[j * tc:(j + 1) * tc, :] = jnp.concatenate(
            [lag_rows[:, i * seg:i * seg + tc] for i in range(i_sz)], axis=1).astype(BF16)
    for n in range(nc):
        for i in range(i_sz):
            up_ref[n * bn:(n + 1) * bn, i * tc:(i + 1) * tc] = u_ref[:, i, n * tc:(n + 1) * tc]
    u = up_ref[...].astype(BF16)
    dx_ref[...] = jnp.dot(u, bm_ref[0], preferred_element_type=F32)
    lam = lam_ref[0]
    lr, li = lam[0:1], lam[1:2]
    w = lam.shape[-1]
    half = w // 2
    is_f = lax.broadcasted_iota(jnp.int32, (bn, w), 1) < half

    def step(s, carry):
        sr, si = carry
        nf = jnp.where(s < nc_ctx, nl + s, s - nc_ctx)
        nb = nc - 1 - s
        rf = pl.multiple_of(nf * bn, bn)
        rb = pl.multiple_of(nb * bn, bn)
        p_ref[pl.ds(rf, bn), 0:w] = sr
        p_ref[pl.ds(rf, bn), w:2 * w] = si
        p_ref[pl.ds(rb, bn), 2 * w:3 * w] = sr
        p_ref[pl.ds(rb, bn), 3 * w:4 * w] = si
        d_re = jnp.where(is_f, dx_ref[pl.ds(rf, bn), 0:w], dx_ref[pl.ds(rb, bn), 0:w])
        d_im = jnp.where(is_f, dx_ref[pl.ds(rf, bn), w:2 * w], dx_ref[pl.ds(rb, bn), w:2 * w])
        return lr * sr - li * si + d_re, lr * si + li * sr + d_im

    zero = jnp.zeros((bn, 2 * half), F32)
    lax.fori_loop(0, nc, step, (zero, zero))
    y = jnp.dot(u, m_ref[...], preferred_element_type=F32)
    y = y + jnp.dot(p_ref[...].astype(BF16), cm_ref[0], preferred_element_type=F32)
    for n in range(nc):
        for i in range(i_sz):
            y_ref[:, i, n * tc:(n + 1) * tc] = y[n * bn:(n + 1) * bn, i * tc:(i + 1) * tc]


def _s5_mix(u_t, n_ctx, wts, layer):
    kr, bm, cm, lam = wts
    bn, ch, t = u_t.shape
    g, i = kr.shape[1:3]
    tc = kr.shape[3] // (2 * i)
    assert 2 * S5_STATE == LANES and bn % SUBLANES == 0 and n_ctx % tc == 0 and t % tc == 0 and tc % LANES == 0
    nc, nc_ctx = t // tc, n_ctx // tc
    k = tc * i
    r = nc * bn
    return pl.pallas_call(
        functools.partial(_s5_kernel, nc, nc_ctx, tc),
        grid=(g,),
        in_specs=[
            pl.BlockSpec((bn, i, t), lambda j: (0, j, 0)),
            pl.BlockSpec((None, 1, i, 2 * k), lambda j: (layer, j, 0, 0)),
            pl.BlockSpec((None, 1, k, 4 * S5_STATE), lambda j: (layer, j, 0, 0)),
            pl.BlockSpec((None, 1, 8 * S5_STATE, k), lambda j: (layer, j, 0, 0)),
            pl.BlockSpec((None, 1, 2, 2 * S5_STATE), lambda j: (layer, j, 0, 0)),
        ],
        out_specs=pl.BlockSpec((bn, i, t), lambda j: (0, j, 0)),
        out_shape=jax.ShapeDtypeStruct((bn, ch, t), F32),
        scratch_shapes=[pltpu.VMEM((r, k), F32), pltpu.VMEM((r, 4 * S5_STATE), F32),
                        pltpu.VMEM((r, 8 * S5_STATE), F32), pltpu.VMEM((k, k), BF16)],
        compiler_params=_params(1),
        name="s5_mix",
    )(u_t, kr, bm, cm, lam)


def _gla_kernel(nc, nc_ctx, k_ref, v_ref, a_ref, q_ref, r_ref, wa_ref, ba_ref, gn_ref, y_ref, of_ref, ob_ref,
                s_ref):
    c, h_n, dk, dv = GLA_CHUNK, GLA_HEADS, GLA_DK, GLA_DV
    qk = h_n * dk
    q_scale = dk ** -0.5
    nbk = GLA_BLOCK
    rb = nbk * c
    row_i = lax.broadcasted_iota(jnp.int32, (rb, rb), 0)
    col_i = lax.broadcasted_iota(jnp.int32, (rb, rb), 1)
    same = (row_i // c) == (col_i // c)
    causal = (same & (row_i >= col_i), same & (row_i <= col_i))
    cum_ops = [m.astype(BF16) for m in causal]
    att_mask = [jnp.concatenate([m] * h_n, axis=0) for m in causal]
    lane_head = lax.broadcasted_iota(jnp.int32, (1, qk), 1) // dk
    head_lanes = [lane_head == h for h in range(h_n)]

    def by_head(x):
        zero = jnp.zeros_like(x)
        return jnp.concatenate([jnp.where(m, x, zero) for m in head_lanes], axis=0)

    def block(j, d):
        rows = pl.ds(pl.multiple_of(j * rb, rb), rb)
        k = k_ref[0, rows, :].astype(F32)
        q = q_ref[0, rows, :].astype(F32) * q_scale
        v = v_ref[0, rows, :]
        z = jnp.dot(a_ref[0, rows, :], wa_ref[:, d * qk:(d + 1) * qk], preferred_element_type=F32) + ba_ref[d:d + 1, :]
        log_a = (jnp.minimum(z, 0.0) - jnp.log(1.0 + jnp.exp(-jnp.abs(z)))) * (1.0 / GLA_TAU)
        hi = log_a.astype(BF16)
        lo = (log_a - hi.astype(F32)).astype(BF16)
        cs = jnp.dot(cum_ops[d], jnp.concatenate([hi, lo], axis=1), preferred_element_type=F32)
        b = cs[:, :qk] + cs[:, qk:]
        b_tot = jnp.concatenate(
            [jnp.broadcast_to(b[n * c + (c - 1 if d == 0 else 0)][None], (c, qk)) for n in range(nbk)], axis=0)
        qd = (q * jnp.exp(b)).astype(BF16)
        kd = (k * jnp.exp(-b)).astype(BF16)
        kl = (k * jnp.exp(b_tot - b)).astype(BF16)
        qm = by_head(qd)
        intra = []
        for h in range(h_n):
            att = lax.dot_general(qm[h * rb:(h + 1) * rb], kd, _NT, preferred_element_type=F32)
            att = jnp.where(causal[d], att, 0.0).astype(BF16)
            intra.append(jnp.dot(att, v[:, h * dv:(h + 1) * dv], preferred_element_type=F32))
        s_t = s_ref[d]
        inter = [None] * nbk
        for n in (range(nbk) if d == 0 else reversed(range(nbk))):
            r0 = n * c
            qm_n = jnp.concatenate([qm[h * rb + r0:h * rb + r0 + c] for h in range(h_n)], axis=0)
            inter[n] = lax.dot_general(qm_n, s_t.astype(BF16), _NT, preferred_element_type=F32)
            v_t = jnp.concatenate(
                [jnp.concatenate([v[r0:r0 + c, h * dv:(h + 1) * dv] for h in range(p, p + dv // c)], axis=0).T
                 for p in range(0, h_n, dv // c)], axis=1)
            s_t = s_t * jnp.exp(b_tot[r0:r0 + 1]) + jnp.dot(v_t, by_head(kl[r0:r0 + c]), preferred_element_type=F32)
        s_ref[d] = s_t
        o = jnp.concatenate(
            [intra[h] + jnp.concatenate([inter[n][h * c:(h + 1) * c] for n in range(nbk)], axis=0)
             for h in range(h_n)], axis=1)
        return rows, o

    def readout(rows, o):
        r = r_ref[0, rows, :].astype(F32)
        gate = r * jax.nn.sigmoid(r)
        outs = []
        for h in range(h_n):
            oh = o[:, h * dv:(h + 1) * dv]
            oh = oh * lax.rsqrt(jnp.mean(oh * oh, axis=-1, keepdims=True) + EPS) * gn_ref[...]
            outs.append(oh * gate[:, h * dv:(h + 1) * dv])
        y_ref[0, rows, :] = jnp.concatenate(outs, axis=1).astype(y_ref.dtype)

    def scan_step(i, carry, lo_block, hi_block, phase):
        rows_f, o_f = block(lo_block + i, 0)
        if phase == "park":
            of_ref[rows_f, :] = o_f
        rows_b, o_b = block(hi_block - 1 - i, 1)
        if phase == "park":
            ob_ref[rows_b, :] = o_b
        elif phase == "meet":
            readout(rows_f, o_f + o_b)
        else:
            readout(rows_f, o_f + ob_ref[rows_f, :])
            readout(rows_b, o_b + of_ref[rows_b, :])
        return carry

    def scan(lo_block, n_blocks):
        step = functools.partial(scan_step, lo_block=lo_block, hi_block=lo_block + n_blocks)
        half = n_blocks // 2
        lax.fori_loop(0, half, functools.partial(step, phase="park"), 0)
        if n_blocks % 2:
            step(jnp.int32(half), 0, phase="meet")
        lax.fori_loop(n_blocks - half, n_blocks, functools.partial(step, phase="finish"), 0)

    nl, n_ctx = (nc - nc_ctx) // nbk, nc_ctx // nbk
    s_ref[...] = jnp.zeros(s_ref.shape, F32)
    scan(nl, n_ctx)
    scan(0, nl)


def _gla_mix(k, v, a, q, r, wa, ba, gnorm, n_ctx):
    bn, t, qk = k.shape
    vd = v.shape[-1]
    c = GLA_CHUNK
    assert t % (c * GLA_BLOCK) == 0 and n_ctx % (c * GLA_BLOCK) == 0 and GLA_DV % c == 0
    seq = lambda n, bufs=2: pl.BlockSpec((1, t, n), lambda i: (i, 0, 0), pipeline_mode=pl.Buffered(bufs))
    return pl.pallas_call(
        functools.partial(_gla_kernel, t // c, n_ctx // c),
        grid=(bn,),
        in_specs=[seq(qk), seq(vd), seq(a.shape[-1]), seq(qk), seq(vd, 1),
                  _const_spec(wa.shape), _const_spec(ba.shape), _const_spec(gnorm.shape)],
        out_specs=pl.BlockSpec((1, t, vd), lambda i: (i, 0, 0)),
        out_shape=jax.ShapeDtypeStruct((bn, t, vd), BF16),
        scratch_shapes=[pltpu.VMEM((t, vd), F32), pltpu.VMEM((t, vd), F32), pltpu.VMEM((2, GLA_DV, qk), F32)],
        compiler_params=_params(1),
        name="gla_mix",
    )(k, v, a, q, r, wa, ba, gnorm)


HY_NB = LANES
HY_XPITCH = HY_NB + SUBLANES
HY_APITCH = 2 * HY_NB + SUBLANES
HY_CT = LANES
HY_UNROLL_OUTER = 16
HY_UNROLL_INNER = 8


def _hyena_filters(n, pos, params):
    w1, b1, w2, b2, w3, freq1, freq2, decay = (p.astype(F32) for p in params)
    t = (pos / (n - 1))[:, None]
    bands = jnp.linspace(1e-4, HY_POS_BANDS - 1, HY_POS_BANDS, dtype=F32)
    ang = (2.0 * math.pi / n) * pos[:, None] * bands[None]
    z = jnp.concatenate([t, jnp.cos(ang), -jnp.sin(ang)], axis=-1)
    hid = jnp.sin(freq1 * (z @ w1 + b1))
    hid = jnp.sin(freq2 * (hid @ w2 + b2))
    w3 = w3.reshape(w3.shape[0], HY_N_FILT, -1)
    window = jnp.exp(-t[None] * jnp.abs(decay).reshape(HY_N_FILT, 1, -1))
    return jnp.einsum('nk,kfc->fnc', hid, w3) * window


def _hyena_kernels(n, params):
    slot = jnp.arange(2 * n)
    h = _hyena_filters(n, jnp.where(slot < n, slot, 2 * n - slot).astype(F32), params)
    is_fwd, is_bwd = (slot < n)[None, :, None], (slot > n)[None, :, None]
    k = jnp.where(is_fwd, h[0::2], 0.0) + jnp.where(is_bwd, h[1::2], 0.0)
    lag0_bwd = _hyena_filters(n, jnp.zeros((1,), F32), params)[1::2]
    k = k + jnp.where(slot[None, :, None] == 0, lag0_bwd, 0.0)
    return k * lax.rsqrt(jnp.sum(k * k, axis=1, keepdims=True) + EPS)


def _spectrum_kernel(na, k_ref, f1_ref, f2_ref, o_ref, a2):
    nb, ap = HY_NB, HY_APITCH

    def stage1(b, c):
        xs = k_ref[0, pl.ds(b, na, stride=nb), :].astype(BF16)
        r = jnp.dot(f1_ref[b], xs, preferred_element_type=F32)
        a2[pl.ds(b, na, stride=ap), :] = r[:na]
        a2[pl.ds(nb + b, na, stride=ap), :] = r[na:]
        return c

    def stage2(ka, c):
        rows = pl.ds(pl.multiple_of(ka * ap, 8), 2 * nb)
        xf = jnp.dot(f2_ref[...], a2[rows, :].astype(BF16), preferred_element_type=F32)
        o_ref[0, ka, 0] = xf[:nb].astype(o_ref.dtype)
        o_ref[0, ka, 1] = xf[nb:].astype(o_ref.dtype)
        return c

    lax.fori_loop(0, nb, stage1, 0, unroll=HY_UNROLL_INNER)
    lax.fori_loop(0, na, stage2, 0, unroll=HY_UNROLL_INNER)


def _hyena_spectra(kernels):
    nf, n, ch = kernels.shape
    nb, ct = HY_NB, HY_CT
    na = n // nb
    _, f2, _, _, f1_full = _dft_tables(n // 2)
    return pl.pallas_call(
        functools.partial(_spectrum_kernel, na),
        grid=(nf, ch // ct),
        in_specs=[pl.BlockSpec((1, n, ct), lambda f, j: (f, 0, j)), _const_spec(f1_full.shape), _const_spec(f2.shape)],
        out_specs=pl.BlockSpec((1, na, 2, nb, ct), lambda f, j: (f, 0, 0, 0, j)),
        out_shape=jax.ShapeDtypeStruct((nf, na, 2, nb, ch), BF16),
        scratch_shapes=[pltpu.VMEM((na * HY_APITCH, ct), F32)],
        compiler_params=_params(2),
        name="hyena_spectrum",
    )(kernels, f1_full, f2)


def _cplx_block(m):
    return np.block([[m.real, -m.imag], [m.imag, m.real]])


@functools.lru_cache(maxsize=None)
def _dft_tables(n_seq):
    nb = HY_NB
    n = 2 * n_seq
    na = n // nb
    ha = na // 2
    ka = np.arange(na)[:, None]
    b = np.arange(nb)
    tw = np.exp(-2j * np.pi * ka * b[None, :] / n)
    w_a = np.exp(-2j * np.pi * ka * np.arange(ha)[None, :] / na)
    f1 = np.stack([_cplx_block(tw[:, i:i + 1] * w_a) for i in range(nb)])
    g = np.stack([_cplx_block((np.conj(tw[:, i:i + 1] * w_a)).T / n) for i in range(nb)])
    w_b = np.exp(-2j * np.pi * b[:, None] * b[None, :] / nb)
    f2 = _cplx_block(w_b)
    f2i = _cplx_block(np.conj(w_b))
    w_full = np.exp(-2j * np.pi * ka * np.arange(na)[None, :] / na)
    f1_full = np.stack([np.concatenate([(tw[:, i:i + 1] * w_full).real, (tw[:, i:i + 1] * w_full).imag], axis=0)
                        for i in range(nb)])
    return tuple(jnp.asarray(t, dtype=BF16) for t in (f1, f2, f2i, g, f1_full))


def _hyena_kernel(na, x1_ref, x2_ref, v_ref, cw_ref, bias_ref, kf1_ref, kf2_ref, f1_ref, f2_ref, f2i_ref, g_ref,
                  o_ref, rawf, vp, gp, yp, a2):
    nb, xp, ap = HY_NB, HY_XPITCH, HY_APITCH
    ha = na // 2
    n_seq = ha * nb
    ct = o_ref.shape[-1]
    halo = SUBLANES
    rawf[:, 0:halo, :] = jnp.zeros((2, halo, ct), F32)
    rawf[:, halo + n_seq:2 * halo + n_seq, :] = jnp.zeros((2, halo, ct), F32)

    def short_conv_into(raw_ref, which, dst):
        w0, w1, w2 = (cw_ref[j, which:which + 1, :] for j in range(HY_SHORT))
        for e in range(2):
            rawf[e, halo:halo + n_seq, :] = raw_ref[e].astype(F32)

            def body(a, c):
                base = pl.multiple_of(a * nb, nb) + halo
                u = (w0 * rawf[e, pl.ds(base - 1, nb), :] + w1 * rawf[e, pl.ds(base, nb), :]
                     + w2 * rawf[e, pl.ds(base + 1, nb), :])
                dst[e, pl.ds(pl.multiple_of(a * xp, 8), nb), :] = u
                return c

            lax.fori_loop(0, ha, body, 0)

    def long_conv(kf_ref):
        def stage1(b, c):
            xs = jnp.concatenate([vp[0, pl.ds(b, ha, stride=xp), :], vp[1, pl.ds(b, ha, stride=xp), :]], axis=0)
            r = jnp.dot(f1_ref[b], xs.astype(BF16), preferred_element_type=F32)
            a2[pl.ds(b, na, stride=ap), :] = r[:na]
            a2[pl.ds(nb + b, na, stride=ap), :] = r[na:]
            return c

        def stage2(kp, c):
            rows = [pl.ds(pl.multiple_of((2 * kp + e) * ap, 8), 2 * nb) for e in range(2)]
            slab = jnp.concatenate([a2[r, :] for r in rows], axis=1).astype(BF16)
            xf = jnp.dot(f2_ref[...], slab, preferred_element_type=F32)
            xr, xi = xf[:nb], xf[nb:]
            kr = jnp.concatenate([kf_ref[0, 2 * kp + e, 0] for e in range(2)], axis=1).astype(F32)
            ki = jnp.concatenate([kf_ref[0, 2 * kp + e, 1] for e in range(2)], axis=1).astype(F32)
            z = jnp.concatenate([xr * kr - xi * ki, xr * ki + xi * kr], axis=0).astype(BF16)
            back = jnp.dot(f2i_ref[...], z, preferred_element_type=F32)
            for e in range(2):
                a2[rows[e], :] = back[:, e * ct:(e + 1) * ct]
            return c

        def stage3(b, c):
            s = jnp.concatenate([a2[pl.ds(b, na, stride=ap), :], a2[pl.ds(nb + b, na, stride=ap), :]], axis=0)
            y = jnp.dot(g_ref[b], s.astype(BF16), preferred_element_type=F32)
            yp[0, pl.ds(b, ha, stride=xp), :] = y[:ha]
            yp[1, pl.ds(b, ha, stride=xp), :] = y[ha:]
            return c

        lax.fori_loop(0, nb, stage1, 0, unroll=HY_UNROLL_OUTER)
        lax.fori_loop(0, na // 2, stage2, 0, unroll=HY_UNROLL_INNER)
        lax.fori_loop(0, nb, stage3, 0, unroll=HY_UNROLL_OUTER)

    def gated(bias_row, write):
        for e in range(2):
            def body(a, c):
                rows = pl.ds(pl.multiple_of(a * xp, 8), nb)
                write(e, a, rows, gp[e, rows, :] * (yp[e, rows, :] + bias_row * vp[e, rows, :]))
                return c

            lax.fori_loop(0, ha, body, 0)

    def write_z(e, a, rows, val):
        vp[e, rows, :] = val

    def write_out(e, a, rows, val):
        o_ref[e, pl.ds(pl.multiple_of(a * nb, nb), nb), :] = val.astype(o_ref.dtype)

    short_conv_into(v_ref, 2, vp)
    short_conv_into(x1_ref, 0, gp)
    long_conv(kf1_ref)
    gated(bias_ref[0:1, :], write_z)
    short_conv_into(x2_ref, 1, gp)
    long_conv(kf2_ref)
    gated(bias_ref[1:2, :], write_out)


def _hyena_latent(hy_all, n_seq, conv_w, bias, kf):
    bn = hy_all.shape[0]
    ch = hy_all.shape[-1] // 3
    ct = HY_CT
    nct = ch // ct
    nb = HY_NB
    na = 2 * n_seq // nb
    assert bn % 2 == 0 and ch % ct == 0 and n_seq % (8 * nb) == 0
    f1, f2, f2i, g, _ = _dft_tables(n_seq)
    col = lambda which: pl.BlockSpec((2, n_seq, ct), lambda j, p: (p, 0, which * nct + j))
    kf_spec = lambda f: pl.BlockSpec((1, na, 2, nb, ct), lambda j, p: (f, 0, 0, 0, j), pipeline_mode=pl.Buffered(1))
    pad_rows = (na // 2) * HY_XPITCH
    return pl.pallas_call(
        functools.partial(_hyena_kernel, na),
        grid=(nct, bn // 2),
        in_specs=[col(0), col(1), col(2),
                  pl.BlockSpec((HY_SHORT, 3, ct), lambda j, p: (0, 0, j)),
                  pl.BlockSpec((2, ct), lambda j, p: (0, j)),
                  kf_spec(0), kf_spec(1),
                  _const_spec(f1.shape), _const_spec(f2.shape), _const_spec(f2i.shape), _const_spec(g.shape)],
        out_specs=pl.BlockSpec((2, n_seq, ct), lambda j, p: (p, 0, j)),
        out_shape=jax.ShapeDtypeStruct((bn, n_seq, ch), BF16),
        scratch_shapes=[pltpu.VMEM((2, n_seq + 2 * SUBLANES, ct), F32), pltpu.VMEM((2, pad_rows, ct), F32),
                        pltpu.VMEM((2, pad_rows, ct), F32), pltpu.VMEM((2, pad_rows, ct), F32),
                        pltpu.VMEM((na * HY_APITCH, ct), F32)],
        compiler_params=_params(2),
        name="hyena_latent",
    )(hy_all, hy_all, hy_all, conv_w.reshape(HY_SHORT, 3, ch), bias, kf, kf, f1, f2, f2i, g)


@functools.lru_cache(maxsize=None)
def _dense_dft_tables(n_seq):
    n = 2 * n_seq
    ang = 2.0 * np.pi * np.arange(n)[:, None] * np.arange(n_seq)[None, :] / n
    fd = np.concatenate([np.cos(ang), -np.sin(ang)], axis=0)
    gd = np.concatenate([np.cos(ang).T, -np.sin(ang).T], axis=1) / n
    return jnp.asarray(fd, dtype=BF16), jnp.asarray(gd, dtype=BF16)


def _hyena_ctx_kernel(n_seq, x1_ref, x2_ref, v_ref, cw_ref, bias_ref, kf1_ref, kf2_ref, fd_ref, gd_ref, o_ref):
    n = 2 * n_seq

    def short_conv(raw_ref, which):
        x = raw_ref[0].astype(F32)
        zero = jnp.zeros((1, x.shape[-1]), F32)
        prev = jnp.concatenate([zero, x[:-1]], axis=0)
        nxt = jnp.concatenate([x[1:], zero], axis=0)
        return cw_ref[0, which:which + 1, :] * prev + cw_ref[1, which:which + 1, :] * x + cw_ref[2, which:which + 1, :] * nxt

    def long_conv(u, kf_ref, bias_row):
        xf = jnp.dot(fd_ref[...], u.astype(BF16), preferred_element_type=F32)
        xr, xi = xf[:n], xf[n:]
        kr, ki = kf_ref[0], kf_ref[1]
        z = jnp.concatenate([xr * kr - xi * ki, xr * ki + xi * kr], axis=0).astype(BF16)
        return jnp.dot(gd_ref[...], z, preferred_element_type=F32) + bias_row * u

    x1, x2, v = short_conv(x1_ref, 0), short_conv(x2_ref, 1), short_conv(v_ref, 2)
    z = x1 * long_conv(v, kf1_ref, bias_ref[0:1, :])
    o_ref[0] = (x2 * long_conv(z, kf2_ref, bias_ref[1:2, :])).astype(o_ref.dtype)


def _hyena_context(hy_all, row_block, n_seq, conv_w, bias, kf1, kf2):
    bn = hy_all.shape[0]
    ch = hy_all.shape[-1] // 3
    fd, gd = _dense_dft_tables(n_seq)
    col = lambda which: pl.BlockSpec((1, n_seq, ch), lambda i: (i, row_block, which))
    split = lambda kf: jnp.stack([kf.real, kf.imag], axis=0)
    return pl.pallas_call(
        functools.partial(_hyena_ctx_kernel, n_seq),
        grid=(bn,),
        in_specs=[col(0), col(1), col(2), _const_spec((HY_SHORT, 3, ch)), _const_spec((2, ch)),
                  _const_spec((2, 2 * n_seq, ch)), _const_spec((2, 2 * n_seq, ch)),
                  _const_spec(fd.shape), _const_spec(gd.shape)],
        out_specs=pl.BlockSpec((1, n_seq, ch), lambda i: (i, 0, 0)),
        out_shape=jax.ShapeDtypeStruct((bn, n_seq, ch), BF16),
        compiler_params=_params(1),
        name="hyena_context",
    )(hy_all, hy_all, hy_all, conv_w.reshape(HY_SHORT, 3, ch), bias, split(kf1), split(kf2), fd, gd)


def kernel(x, c, ctx, c_ctx, w_mod, b_mod, g_norm1, g_norm2, w_in, hy_conv, hy_w1, hy_b1, hy_w2, hy_b2, hy_w3,
           hy_freq1, hy_freq2, hy_decay, hy_bias, gla_wa2, gla_ba, gla_gnorm, s5_a_re, s5_a_im, s5_log_dt, s5_b_re,
           s5_b_im, s5_c_re, s5_c_im, s5_d, s5_w_glu, s5_b_glu, w_br_hy, w_br_gla, w_br_s5, w_out, w_ffn_in,
           w_ffn_out, g_final):
    bn, n_lat, d = x.shape
    n_ctx = ctx.shape[1]
    depth = w_in.shape[0]
    d_hy = w_br_hy.shape[1]
    s5_ch = w_br_s5.shape[1]
    d_ff = w_ffn_out.shape[1]
    assert n_ctx == ROW_TILE and n_lat % ROW_TILE == 0
    n_lat_tiles = n_lat // ROW_TILE
    n_low = 2 * GLA_LOWRANK
    o_a = GLA_QK + GLA_V
    o_u = o_a + n_low
    o_q = o_u + s5_ch
    o_gate = o_q + GLA_QK + GLA_V + 3 * d_hy
    col_sizes = (GLA_QK, GLA_V, A_PAD, GLA_QK, GLA_V, 3 * d_hy)

    cond = jnp.concatenate([c_ctx[None], c], axis=0)
    cond = jnp.pad(cond, ((0, (-cond.shape[0]) % SUBLANES), (0, 0)))
    mod_all = _modulation_all(cond, w_mod, b_mod)

    s5w_all = jax.vmap(functools.partial(_s5_weights, tc=S5_CHUNK))(
        s5_a_re, s5_a_im, s5_log_dt, s5_b_re, s5_b_im, s5_c_re, s5_c_im)
    wa_all = jnp.zeros((depth, A_PAD, 2 * GLA_QK), F32)
    wa_all = wa_all.at[:, :GLA_LOWRANK, :GLA_QK].set(gla_wa2[:, 0]).at[:, GLA_LOWRANK:n_low, GLA_QK:].set(gla_wa2[:, 1])
    wa_all = wa_all.astype(BF16)

    xc = jnp.concatenate([x, ctx], axis=1)
    for l in range(depth):
        last = l == depth - 1
        mods = jnp.stack([jnp.broadcast_to(mod_all[l, 0], (bn, 6 * d)), mod_all[l, 1:1 + bn]], axis=1)
        mods = mods[:, :, None, :]
        wl = w_in[l]
        w_proj = jnp.concatenate(
            [wl[:, :o_u], jnp.zeros((d, A_PAD - n_low), F32), wl[:, o_q:o_gate]], axis=1).astype(BF16)
        g1 = g_norm1[l][None]
        w_u_t = wl[:, o_u:o_q].T.astype(BF16)
        k_a, v_a, a_a, q_a, r_a, hy_a, u_t = _project(xc, mods, g1, w_proj, col_sizes, w_u_t, n_lat_tiles)

        y_gla = _gla_mix(k_a, v_a, a_a, q_a, r_a, wa_all[l], gla_ba[l], gla_gnorm[l][None], n_ctx)

        ys = _s5_mix(u_t, n_ctx, s5w_all, l)

        hy_p = (hy_w1[l], hy_b1[l], hy_w2[l], hy_b2[l], hy_w3[l], hy_freq1[l], hy_freq2[l], hy_decay[l])
        y_hy = _hyena_latent(hy_a, n_lat, hy_conv[l], hy_bias[l], _hyena_spectra(_hyena_kernels(n_lat, hy_p)))
        if last:
            y_hy_ctx = y_hy
        else:
            kf_ctx = jnp.fft.fft(_hyena_kernels(n_ctx, hy_p), axis=1)
            y_hy_ctx = _hyena_context(hy_a, n_lat // n_ctx, n_ctx, hy_conv[l], hy_bias[l], kf_ctx[0], kf_ctx[1])

        wts = (wl[:, o_gate:].astype(BF16), w_br_hy[l].astype(BF16), w_br_gla[l].astype(BF16),
               w_br_s5[l].astype(BF16), w_out[l].astype(BF16), s5_d[l][:, None], s5_w_glu[l].astype(BF16),
               s5_b_glu[l][None])
        n_tiles = n_lat_tiles if last else n_lat_tiles + n_ctx // ROW_TILE
        x_mid = _merge(xc, mods, g1, y_hy, y_hy_ctx, y_gla, ys, u_t, wts, n_tiles, n_lat_tiles)
        wf = w_ffn_in[l]
        xc = _ffn(x_mid, mods, g_norm2[l][None], wf[:, :d_ff].astype(BF16), wf[:, d_ff:].astype(BF16),
                  w_ffn_out[l].astype(BF16), g_final[None], n_lat_tiles, last)
    return xc
```

```python
import functools
import math

import jax
import jax.numpy as jnp
import numpy as np
from jax import lax
from jax.experimental import pallas as pl
from jax.experimental.pallas import tpu as pltpu

F32 = jnp.float32
BF16 = jnp.bfloat16
EPS = 1e-6

HY_SHORT = 3
HY_POS_BANDS = 16
HY_N_FILT = 4
GLA_HEADS = 4
GLA_DK = 64
GLA_DV = 128
GLA_QK = GLA_HEADS * GLA_DK
GLA_V = GLA_HEADS * GLA_DV
GLA_LOWRANK = 16
GLA_TAU = 16.0
GLA_CHUNK = 64
S5_STATE = 64

LANES = 128
SUBLANES = 8
VMEM_LIMIT = 56 * 1024 * 1024

GLA_BLOCK = 4
GLA_UNROLL = 2
A_PAD = LANES
ROW_TILE = 256


def _const_spec(shape):
    nd = len(shape)
    return pl.BlockSpec(shape, lambda *_: (0,) * nd, pipeline_mode=pl.Buffered(1))


def _params(n_axes):
    return pltpu.CompilerParams(dimension_semantics=("parallel",) * n_axes, vmem_limit_bytes=VMEM_LIMIT)


def _mod_kernel(c_ref, w_ref, b_ref, o_ref):
    c = c_ref[...]
    s = c * jax.nn.sigmoid(c)
    o_ref[0] = jnp.dot(s.astype(BF16), w_ref[0], preferred_element_type=F32) + b_ref[0]


def _modulation_all(cond, w_mod, b_mod):
    depth, d, n = w_mod.shape
    r = cond.shape[0]
    tn = 1536
    return pl.pallas_call(
        _mod_kernel,
        grid=(depth, n // tn),
        in_specs=[
            pl.BlockSpec((r, d), lambda l, j: (0, 0)),
            pl.BlockSpec((1, d, tn), lambda l, j: (l, 0, j)),
            pl.BlockSpec((1, 1, tn), lambda l, j: (l, 0, j)),
        ],
        out_specs=pl.BlockSpec((1, r, tn), lambda l, j: (l, 0, j)),
        out_shape=jax.ShapeDtypeStruct((depth, r, n), F32),
        compiler_params=_params(2),
        name="modulation",
    )(cond, w_mod.astype(BF16), b_mod.reshape(depth, 1, n))


def _norm_mod(x, g, shift, scale):
    y = x * lax.rsqrt(jnp.mean(x * x, axis=-1, keepdims=True) + EPS)
    return (y * g) * (1.0 + scale) + shift


_NT = (((1,), (1,)), ((), ()))


def _proj_kernel(col_sizes, d, x_ref, mod_ref, g_ref, w_ref, wt_ref, *o_refs):
    m = mod_ref[0, 0]
    h = _norm_mod(x_ref[0], g_ref[...], m[:, 0:d], m[:, d:2 * d]).astype(BF16)
    off = 0
    for o_ref, n in zip(o_refs[:-1], col_sizes):
        o_ref[0] = jnp.dot(h, w_ref[:, off:off + n], preferred_element_type=F32).astype(o_ref.dtype)
        off += n
    o_refs[-1][0] = lax.dot_general(wt_ref[...], h, _NT, preferred_element_type=F32)


def _mod_spec(d, n_lat_tiles):
    return pl.BlockSpec((1, 1, 1, 6 * d), lambda i, j: (i, (j < n_lat_tiles).astype(jnp.int32), 0, 0))


def _project(xc, mods, g, w, col_sizes, w_t, n_lat_tiles):
    b, t, d = xc.shape
    tm = ROW_TILE
    n_tot = sum(col_sizes)
    n_t = w_t.shape[0]
    out_shape = [jax.ShapeDtypeStruct((b, t, n), BF16) for n in col_sizes] + [jax.ShapeDtypeStruct((b, n_t, t), F32)]
    out_specs = [pl.BlockSpec((1, tm, n), lambda i, j: (i, j, 0)) for n in col_sizes]
    out_specs.append(pl.BlockSpec((1, n_t, tm), lambda i, j: (i, 0, j)))
    return pl.pallas_call(
        functools.partial(_proj_kernel, tuple(col_sizes), d),
        grid=(b, t // tm),
        in_specs=[
            pl.BlockSpec((1, tm, d), lambda i, j: (i, j, 0)),
            _mod_spec(d, n_lat_tiles),
            _const_spec((1, d)),
            _const_spec((d, n_tot)),
            _const_spec((n_t, d)),
        ],
        out_specs=out_specs,
        out_shape=out_shape,
        compiler_params=_params(2),
        name="in_proj",
    )(xc, mods, g, w, w_t)


def _gelu_tanh(x):
    return 0.5 * x * (1.0 + jnp.tanh(math.sqrt(2.0 / math.pi) * (x + 0.044715 * (x * x * x))))


def _merge_kernel(d, n_lat_tiles, x_ref, mod_ref, g_ref, yhy_ref, yhyc_ref, ygla_ref, ys5_ref, u_ref, wg_ref, whb_ref,
                  wgb_ref, wsb_ref, wo_ref, s5d_ref, wglu_ref, bglu_ref, o_ref):
    x = x_ref[0]
    m = mod_ref[0, 0]
    h = _norm_mod(x, g_ref[...], m[:, 0:d], m[:, d:2 * d]).astype(BF16)
    y_hy = jnp.where(pl.program_id(1) < n_lat_tiles, yhy_ref[0], yhyc_ref[0])
    y5 = (ys5_ref[0] + s5d_ref[...] * u_ref[0]).T
    g5 = _gelu_tanh(y5)
    y_s5 = g5 * jax.nn.sigmoid(jnp.dot(g5.astype(BF16), wglu_ref[...], preferred_element_type=F32) + bglu_ref[...])

    def branch(k, y, wb_ref):
        gate = jnp.dot(h, wg_ref[:, k * d:(k + 1) * d], preferred_element_type=F32)
        return jax.nn.sigmoid(gate) * jnp.dot(y, wb_ref[...], preferred_element_type=F32)

    mix = branch(0, y_hy, whb_ref) + branch(1, ygla_ref[0], wgb_ref) + branch(2, y_s5.astype(BF16), wsb_ref)
    out = jnp.dot(mix.astype(BF16), wo_ref[...], preferred_element_type=F32)
    o_ref[0] = x + m[:, 2 * d:3 * d] * out


def _merge(xc, mods, g, y_hy, y_hy_ctx, y_gla, ys_s5, u_s5, wts, nj, n_lat_tiles):
    b, t, d = xc.shape
    tm = ROW_TILE
    ch = y_hy.shape[-1]
    row = lambda n: pl.BlockSpec((1, tm, n), lambda i, j: (i, j, 0))
    lat_row = pl.BlockSpec((1, tm, ch), lambda i, j: (i, jnp.minimum(j, n_lat_tiles - 1), 0))
    ctx_row = pl.BlockSpec((1, tm, ch), lambda i, j: (i, 0, 0))
    col = lambda n: pl.BlockSpec((1, n, tm), lambda i, j: (i, 0, j))
    w_gate, w_hy, w_gla, w_s5, w_out, s5_d, w_glu, b_glu = wts
    return pl.pallas_call(
        functools.partial(_merge_kernel, d, n_lat_tiles),
        grid=(b, nj),
        in_specs=[
            row(d),
            _mod_spec(d, n_lat_tiles),
            _const_spec((1, d)),
            lat_row, ctx_row, row(ch), col(ch), col(ch),
            _const_spec(w_gate.shape), _const_spec(w_hy.shape), _const_spec(w_gla.shape), _const_spec(w_s5.shape),
            _const_spec(w_out.shape), _const_spec(s5_d.shape), _const_spec(w_glu.shape), _const_spec(b_glu.shape),
        ],
        out_specs=pl.BlockSpec((1, tm, d), lambda i, j: (i, j, 0)),
        out_shape=jax.ShapeDtypeStruct((b, nj * tm, d), F32),
        compiler_params=_params(2),
        name="merge",
    )(xc, mods, g, y_hy, y_hy_ctx, y_gla, ys_s5, u_s5, w_gate, w_hy, w_gla, w_s5, w_out, s5_d, w_glu, b_glu)


def _ffn_kernel(d, final, x_ref, mod_ref, g_ref, wa_ref, wb_ref, wo_ref, gf_ref, o_ref):
    x = x_ref[0]
    m = mod_ref[0, 0]
    h = _norm_mod(x, g_ref[...], m[:, 3 * d:4 * d], m[:, 4 * d:5 * d]).astype(BF16)
    a = jnp.dot(h, wa_ref[...], preferred_element_type=F32)
    bb = jnp.dot(h, wb_ref[...], preferred_element_type=F32)
    act = (a * jax.nn.sigmoid(a) * bb).astype(BF16)
    y = x + m[:, 5 * d:6 * d] * jnp.dot(act, wo_ref[...], preferred_element_type=F32)
    if final:
        y = y * lax.rsqrt(jnp.mean(y * y, axis=-1, keepdims=True) + EPS) * gf_ref[...]
    o_ref[0] = y


def _ffn(x, mods, g, wa, wb, wo, g_final, n_lat_tiles, final):
    b, t, d = x.shape
    tm = ROW_TILE
    return pl.pallas_call(
        functools.partial(_ffn_kernel, d, final),
        grid=(b, t // tm),
        in_specs=[
            pl.BlockSpec((1, tm, d), lambda i, j: (i, j, 0)),
            _mod_spec(d, n_lat_tiles),
            _const_spec((1, d)),
            _const_spec(wa.shape), _const_spec(wb.shape), _const_spec(wo.shape),
            _const_spec((1, d)),
        ],
        out_specs=pl.BlockSpec((1, tm, d), lambda i, j: (i, j, 0)),
        out_shape=jax.ShapeDtypeStruct((b, t, d), F32),
        compiler_params=_params(2),
        name="ffn",
    )(x, mods, g, wa, wb, wo, g_final)


S5_CHUNK = LANES


def _s5_weights(a_re, a_im, log_dt, b_re, b_im, c_re, c_im, tc):
    g, p = a_re.shape[1:]
    i = b_re.shape[-1]
    lam_c = lax.complex(jnp.minimum(a_re, -1e-4), a_im)
    lam_dt = lam_c * jnp.exp(log_dt)[..., None]
    b_bar = ((jnp.exp(lam_dt) - 1.0) / lam_c)[..., None] * lax.complex(b_re, b_im)[None]
    c_mat = lax.complex(c_re, c_im)
    tau = jnp.arange(tc, dtype=F32)
    pw = jnp.exp(lam_dt[:, :, None, :] * tau[None, None, :, None])
    pw1 = pw * jnp.exp(lam_dt)[:, :, None, :]
    kern = jnp.einsum('gip,dgtp,dgpj->dgtij', c_mat, pw, b_bar).real
    k_lag = jnp.concatenate([kern[1][:, :0:-1], kern[0][:, :1] + kern[1][:, :1], kern[0][:, 1:]], axis=1)
    kr = jnp.pad(k_lag.transpose(0, 3, 2, 1), ((0, 0), (0, 0), (0, 0), (0, 1)))
    kr = kr.reshape(g, i, i * 2 * tc)

    inc_f = pw[0][:, None, ::-1, :] * b_bar[0].transpose(0, 2, 1)[:, :, None]
    inc_b = pw[1][:, None, :, :] * b_bar[1].transpose(0, 2, 1)[:, :, None]
    bm = jnp.concatenate([inc_f.real, inc_b.real, inc_f.imag, inc_b.imag], axis=-1).reshape(g, tc * i, 4 * p)

    out_f = c_mat.transpose(0, 2, 1)[..., None] * pw1[0].transpose(0, 2, 1)[:, :, None, :]
    out_b = c_mat.transpose(0, 2, 1)[..., None] * pw1[1][:, ::-1].transpose(0, 2, 1)[:, :, None, :]
    z = jnp.zeros((g, p, tc * i), F32)
    fl = lambda t: t.reshape(g, p, tc * i)
    cm = jnp.concatenate([fl(out_f.real), z, fl(-out_f.imag), z, z, fl(out_b.real), z, fl(-out_b.imag)], axis=1)

    lam_t = jnp.exp(lam_dt * float(tc))
    lam = jnp.stack([jnp.concatenate([lam_t[0].real, lam_t[1].real], -1),
                     jnp.concatenate([lam_t[0].imag, lam_t[1].imag], -1)], axis=1)
    return kr, bm.astype(BF16), cm.astype(BF16), lam


def _s5_kernel(nc, nc_ctx, tc, u_ref, kr_ref, bm_ref, cm_ref, lam_ref, y_ref, up_ref, dx_ref, p_ref, m_ref):
    bn, i_sz = u_ref.shape[:2]
    nl = nc - nc_ctx
    width = kr_ref.shape[-1]
    seg = width // i_sz
    for j in range(i_sz):
        lag_rows = pltpu.roll(jnp.broadcast_to(kr_ref[0, j:j + 1, :], (tc, width)), width - (tc - 1), axis=1,
                              stride=1, stride_axis=0)
        m_ref[j * tc:(j + 1) * tc, :] = jnp.concatenate(
            [lag_rows[:, i * seg:i * seg + tc] for i in range(i_sz)], axis=1).astype(BF16)
    for n in range(nc):
        for i in range(i_sz):
            up_ref[n * bn:(n + 1) * bn, i * tc:(i + 1) * tc] = u_ref[:, i, n * tc:(n + 1) * tc]
    u = up_ref[...].astype(BF16)
    dx_ref[...] = jnp.dot(u, bm_ref[0], preferred_element_type=F32)
    lam = lam_ref[0]
    lr, li = lam[0:1], lam[1:2]
    w = lam.shape[-1]
    half = w // 2
    is_f = lax.broadcasted_iota(jnp.int32, (bn, w), 1) < half

    def step(s, carry):
        sr, si = carry
        nf = jnp.where(s < nc_ctx, nl + s, s - nc_ctx)
        nb = nc - 1 - s
        rf = pl.multiple_of(nf * bn, bn)
        rb = pl.multiple_of(nb * bn, bn)
        p_ref[pl.ds(rf, bn), 0:w] = sr
        p_ref[pl.ds(rf, bn), w:2 * w] = si
        p_ref[pl.ds(rb, bn), 2 * w:3 * w] = sr
        p_ref[pl.ds(rb, bn), 3 * w:4 * w] = si
        d_re = jnp.where(is_f, dx_ref[pl.ds(rf, bn), 0:w], dx_ref[pl.ds(rb, bn), 0:w])
        d_im = jnp.where(is_f, dx_ref[pl.ds(rf, bn), w:2 * w], dx_ref[pl.ds(rb, bn), w:2 * w])
        return lr * sr - li * si + d_re, lr * si + li * sr + d_im

    zero = jnp.zeros((bn, 2 * half), F32)
    lax.fori_loop(0, nc, step, (zero, zero))
    y = jnp.dot(u, m_ref[...], preferred_element_type=F32)
    y = y + jnp.dot(p_ref[...].astype(BF16), cm_ref[0], preferred_element_type=F32)
    for n in range(nc):
        for i in range(i_sz):
            y_ref[:, i, n * tc:(n + 1) * tc] = y[n * bn:(n + 1) * bn, i * tc:(i + 1) * tc]


def _s5_mix(u_t, n_ctx, wts, layer):
    kr, bm, cm, lam = wts
    bn, ch, t = u_t.shape
    g, i = kr.shape[1:3]
    tc = kr.shape[3] // (2 * i)
    assert 2 * S5_STATE == LANES and bn % SUBLANES == 0 and n_ctx % tc == 0 and t % tc == 0 and tc % LANES == 0
    nc, nc_ctx = t // tc, n_ctx // tc
    k = tc * i
    r = nc * bn
    return pl.pallas_call(
        functools.partial(_s5_kernel, nc, nc_ctx, tc),
        grid=(g,),
        in_specs=[
            pl.BlockSpec((bn, i, t), lambda j: (0, j, 0)),
            pl.BlockSpec((None, 1, i, 2 * k), lambda j: (layer, j, 0, 0)),
            pl.BlockSpec((None, 1, k, 4 * S5_STATE), lambda j: (layer, j, 0, 0)),
            pl.BlockSpec((None, 1, 8 * S5_STATE, k), lambda j: (layer, j, 0, 0)),
            pl.BlockSpec((None, 1, 2, 2 * S5_STATE), lambda j: (layer, j, 0, 0)),
        ],
        out_specs=pl.BlockSpec((bn, i, t), lambda j: (0, j, 0)),
        out_shape=jax.ShapeDtypeStruct((bn, ch, t), F32),
        scratch_shapes=[pltpu.VMEM((r, k), F32), pltpu.VMEM((r, 4 * S5_STATE), F32),
                        pltpu.VMEM((r, 8 * S5_STATE), F32), pltpu.VMEM((k, k), BF16)],
        compiler_params=_params(1),
        name="s5_mix",
    )(u_t, kr, bm, cm, lam)


def _gla_kernel(nc, nc_ctx, k_ref, v_ref, a_ref, q_ref, r_ref, wa_ref, ba_ref, gn_ref, y_ref, of_ref, ob_ref,
                s_ref):
    c, h_n, dk, dv = GLA_CHUNK, GLA_HEADS, GLA_DK, GLA_DV
    qk = h_n * dk
    q_scale = dk ** -0.5
    nbk = GLA_BLOCK
    rb = nbk * c
    row_i = lax.broadcasted_iota(jnp.int32, (rb, rb), 0)
    col_i = lax.broadcasted_iota(jnp.int32, (rb, rb), 1)
    same = (row_i // c) == (col_i // c)
    causal = (same & (row_i >= col_i), same & (row_i <= col_i))
    cum_ops = [m.astype(BF16) for m in causal]
    att_mask = [jnp.concatenate([m] * h_n, axis=0) for m in causal]
    lane_head = lax.broadcasted_iota(jnp.int32, (1, qk), 1) // dk
    head_lanes = [lane_head == h for h in range(h_n)]

    def by_head(x):
        zero = jnp.zeros_like(x)
        return jnp.concatenate([jnp.where(m, x, zero) for m in head_lanes], axis=0)

    def block(j, d):
        rows = pl.ds(pl.multiple_of(j * rb, rb), rb)
        k = k_ref[0, rows, :].astype(F32)
        q = q_ref[0, rows, :].astype(F32) * q_scale
        v = v_ref[0, rows, :]
        z = jnp.dot(a_ref[0, rows, :], wa_ref[:, d * qk:(d + 1) * qk], preferred_element_type=F32) + ba_ref[d:d + 1, :]
        log_a = (jnp.minimum(z, 0.0) - jnp.log(1.0 + jnp.exp(-jnp.abs(z)))) * (1.0 / GLA_TAU)
        hi = log_a.astype(BF16)
        lo = (log_a - hi.astype(F32)).astype(BF16)
        cs = jnp.dot(cum_ops[d], jnp.concatenate([hi, lo], axis=1), preferred_element_type=F32)
        b = cs[:, :qk] + cs[:, qk:]
        b_tot = jnp.concatenate(
            [jnp.broadcast_to(b[n * c + (c - 1 if d == 0 else 0)][None], (c, qk)) for n in range(nbk)], axis=0)
        qd = (q * jnp.exp(b)).astype(BF16)
        kd = (k * jnp.exp(-b)).astype(BF16)
        kl = (k * jnp.exp(b_tot - b)).astype(BF16)
        qm = by_head(qd)
        intra = []
        for h in range(h_n):
            att = lax.dot_general(qm[h * rb:(h + 1) * rb], kd, _NT, preferred_element_type=F32)
            att = jnp.where(causal[d], att, 0.0).astype(BF16)
            intra.append(jnp.dot(att, v[:, h * dv:(h + 1) * dv], preferred_element_type=F32))
        s_t = s_ref[d]
        inter = [None] * nbk
        for n in (range(nbk) if d == 0 else reversed(range(nbk))):
            r0 = n * c
            qm_n = jnp.concatenate([qm[h * rb + r0:h * rb + r0 + c] for h in range(h_n)], axis=0)
            inter[n] = lax.dot_general(qm_n, s_t.astype(BF16), _NT, preferred_element_type=F32)
            v_t = jnp.concatenate(
                [jnp.concatenate([v[r0:r0 + c, h * dv:(h + 1) * dv] for h in range(p, p + dv // c)], axis=0).T
                 for p in range(0, h_n, dv // c)], axis=1)
            s_t = s_t * jnp.exp(b_tot[r0:r0 + 1]) + jnp.dot(v_t, by_head(kl[r0:r0 + c]), preferred_element_type=F32)
        s_ref[d] = s_t
        o = jnp.concatenate(
            [intra[h] + jnp.concatenate([inter[n][h * c:(h + 1) * c] for n in range(nbk)], axis=0)
             for h in range(h_n)], axis=1)
        return rows, o

    def readout(rows, o):
        r = r_ref[0, rows, :].astype(F32)
        gate = r * jax.nn.sigmoid(r)
        outs = []
        for h in range(h_n):
            oh = o[:, h * dv:(h + 1) * dv]
            oh = oh * lax.rsqrt(jnp.mean(oh * oh, axis=-1, keepdims=True) + EPS) * gn_ref[...]
            outs.append(oh * gate[:, h * dv:(h + 1) * dv])
        y_ref[0, rows, :] = jnp.concatenate(outs, axis=1).astype(y_ref.dtype)

    def scan_step(i, carry, lo_block, hi_block, phase):
        rows_f, o_f = block(lo_block + i, 0)
        if phase == "park":
            of_ref[rows_f, :] = o_f
        rows_b, o_b = block(hi_block - 1 - i, 1)
        if phase == "park":
            ob_ref[rows_b, :] = o_b
        elif phase == "meet":
            readout(rows_f, o_f + o_b)
        else:
            readout(rows_f, o_f + ob_ref[rows_f, :])
            readout(rows_b, o_b + of_ref[rows_b, :])
        return carry

    def scan(lo_block, n_blocks):
        step = functools.partial(scan_step, lo_block=lo_block, hi_block=lo_block + n_blocks)
        half = n_blocks // 2
        lax.fori_loop(0, half, functools.partial(step, phase="park"), 0, unroll=GLA_UNROLL)
        if n_blocks % 2:
            step(jnp.int32(half), 0, phase="meet")
        lax.fori_loop(n_blocks - half, n_blocks, functools.partial(step, phase="finish"), 0, unroll=GLA_UNROLL)

    nl, n_ctx = (nc - nc_ctx) // nbk, nc_ctx // nbk
    s_ref[...] = jnp.zeros(s_ref.shape, F32)
    scan(nl, n_ctx)
    scan(0, nl)


def _gla_mix(k, v, a, q, r, wa, ba, gnorm, n_ctx):
    bn, t, qk = k.shape
    vd = v.shape[-1]
    c = GLA_CHUNK
    assert t % (c * GLA_BLOCK) == 0 and n_ctx % (c * GLA_BLOCK) == 0 and GLA_DV % c == 0
    seq = lambda n, bufs=2: pl.BlockSpec((1, t, n), lambda i: (i, 0, 0), pipeline_mode=pl.Buffered(bufs))
    return pl.pallas_call(
        functools.partial(_gla_kernel, t // c, n_ctx // c),
        grid=(bn,),
        in_specs=[seq(qk), seq(vd), seq(a.shape[-1]), seq(qk), seq(vd, 1),
                  _const_spec(wa.shape), _const_spec(ba.shape), _const_spec(gnorm.shape)],
        out_specs=pl.BlockSpec((1, t, vd), lambda i: (i, 0, 0)),
        out_shape=jax.ShapeDtypeStruct((bn, t, vd), BF16),
        scratch_shapes=[pltpu.VMEM((t, vd), F32), pltpu.VMEM((t, vd), F32), pltpu.VMEM((2, GLA_DV, qk), F32)],
        compiler_params=_params(1),
        name="gla_mix",
    )(k, v, a, q, r, wa, ba, gnorm)


HY_NB = LANES
HY_XPITCH = HY_NB + SUBLANES
HY_APITCH = 2 * HY_NB + SUBLANES
HY_CT = LANES
HY_UNROLL_OUTER = 32
HY_UNROLL_INNER = 16


def _hyena_filters(n, pos, params):
    w1, b1, w2, b2, w3, freq1, freq2, decay = (p.astype(F32) for p in params)
    t = (pos / (n - 1))[:, None]
    bands = jnp.linspace(1e-4, HY_POS_BANDS - 1, HY_POS_BANDS, dtype=F32)
    ang = (2.0 * math.pi / n) * pos[:, None] * bands[None]
    z = jnp.concatenate([t, jnp.cos(ang), -jnp.sin(ang)], axis=-1)
    hid = jnp.sin(freq1 * (z @ w1 + b1))
    hid = jnp.sin(freq2 * (hid @ w2 + b2))
    w3 = w3.reshape(w3.shape[0], HY_N_FILT, -1)
    window = jnp.exp(-t[None] * jnp.abs(decay).reshape(HY_N_FILT, 1, -1))
    return jnp.einsum('nk,kfc->fnc', hid, w3) * window


def _hyena_kernels(n, params):
    slot = jnp.arange(2 * n)
    h = _hyena_filters(n, jnp.where(slot < n, slot, 2 * n - slot).astype(F32), params)
    is_fwd, is_bwd = (slot < n)[None, :, None], (slot > n)[None, :, None]
    k = jnp.where(is_fwd, h[0::2], 0.0) + jnp.where(is_bwd, h[1::2], 0.0)
    lag0_bwd = _hyena_filters(n, jnp.zeros((1,), F32), params)[1::2]
    k = k + jnp.where(slot[None, :, None] == 0, lag0_bwd, 0.0)
    return k * lax.rsqrt(jnp.sum(k * k, axis=1, keepdims=True) + EPS)


def _spectrum_kernel(na, k_ref, f1_ref, f2_ref, o_ref, a2):
    nb, ap = HY_NB, HY_APITCH

    def stage1(b, c):
        xs = k_ref[0, pl.ds(b, na, stride=nb), :].astype(BF16)
        r = jnp.dot(f1_ref[b], xs, preferred_element_type=F32)
        a2[pl.ds(b, na, stride=ap), :] = r[:na]
        a2[pl.ds(nb + b, na, stride=ap), :] = r[na:]
        return c

    def stage2(ka, c):
        rows = pl.ds(pl.multiple_of(ka * ap, 8), 2 * nb)
        xf = jnp.dot(f2_ref[...], a2[rows, :].astype(BF16), preferred_element_type=F32)
        o_ref[0, ka, 0] = xf[:nb].astype(o_ref.dtype)
        o_ref[0, ka, 1] = xf[nb:].astype(o_ref.dtype)
        return c

    lax.fori_loop(0, nb, stage1, 0, unroll=HY_UNROLL_INNER)
    lax.fori_loop(0, na, stage2, 0, unroll=HY_UNROLL_INNER)


def _hyena_spectra(kernels):
    nf, n, ch = kernels.shape
    nb, ct = HY_NB, HY_CT
    na = n // nb
    _, f2, _, _, f1_full = _dft_tables(n // 2)
    return pl.pallas_call(
        functools.partial(_spectrum_kernel, na),
        grid=(nf, ch // ct),
        in_specs=[pl.BlockSpec((1, n, ct), lambda f, j: (f, 0, j)), _const_spec(f1_full.shape), _const_spec(f2.shape)],
        out_specs=pl.BlockSpec((1, na, 2, nb, ct), lambda f, j: (f, 0, 0, 0, j)),
        out_shape=jax.ShapeDtypeStruct((nf, na, 2, nb, ch), BF16),
        scratch_shapes=[pltpu.VMEM((na * HY_APITCH, ct), F32)],
        compiler_params=_params(2),
        name="hyena_spectrum",
    )(kernels, f1_full, f2)


def _cplx_block(m):
    return np.block([[m.real, -m.imag], [m.imag, m.real]])


@functools.lru_cache(maxsize=None)
def _dft_tables(n_seq):
    nb = HY_NB
    n = 2 * n_seq
    na = n // nb
    ha = na // 2
    ka = np.arange(na)[:, None]
    b = np.arange(nb)
    tw = np.exp(-2j * np.pi * ka * b[None, :] / n)
    w_a = np.exp(-2j * np.pi * ka * np.arange(ha)[None, :] / na)
    f1 = np.stack([_cplx_block(tw[:, i:i + 1] * w_a) for i in range(nb)])
    g = np.stack([_cplx_block((np.conj(tw[:, i:i + 1] * w_a)).T / n) for i in range(nb)])
    w_b = np.exp(-2j * np.pi * b[:, None] * b[None, :] / nb)
    f2 = _cplx_block(w_b)
    f2i = _cplx_block(np.conj(w_b))
    w_full = np.exp(-2j * np.pi * ka * np.arange(na)[None, :] / na)
    f1_full = np.stack([np.concatenate([(tw[:, i:i + 1] * w_full).real, (tw[:, i:i + 1] * w_full).imag], axis=0)
                        for i in range(nb)])
    return tuple(jnp.asarray(t, dtype=BF16) for t in (f1, f2, f2i, g, f1_full))


def _hyena_kernel(na, x1_ref, x2_ref, v_ref, cw_ref, bias_ref, kf1_ref, kf2_ref, f1_ref, f2_ref, f2i_ref, g_ref,
                  o_ref, rawf, vp, gp, yp, a2):
    nb, xp, ap = HY_NB, HY_XPITCH, HY_APITCH
    ha = na // 2
    n_seq = ha * nb
    ct = o_ref.shape[-1]
    halo = SUBLANES
    rawf[:, 0:halo, :] = jnp.zeros((2, halo, ct), F32)
    rawf[:, halo + n_seq:2 * halo + n_seq, :] = jnp.zeros((2, halo, ct), F32)

    def short_conv_into(raw_ref, which, dst):
        w0, w1, w2 = (cw_ref[j, which:which + 1, :] for j in range(HY_SHORT))
        for e in range(2):
            rawf[e, halo:halo + n_seq, :] = raw_ref[e].astype(F32)

            def body(a, c):
                base = pl.multiple_of(a * nb, nb) + halo
                u = (w0 * rawf[e, pl.ds(base - 1, nb), :] + w1 * rawf[e, pl.ds(base, nb), :]
                     + w2 * rawf[e, pl.ds(base + 1, nb), :])
                dst[e, pl.ds(pl.multiple_of(a * xp, 8), nb), :] = u
                return c

            lax.fori_loop(0, ha, body, 0)

    def long_conv(kf_ref):
        def stage1(b, c):
            xs = jnp.concatenate([vp[0, pl.ds(b, ha, stride=xp), :], vp[1, pl.ds(b, ha, stride=xp), :]], axis=0)
            r = jnp.dot(f1_ref[b], xs.astype(BF16), preferred_element_type=F32)
            a2[pl.ds(b, na, stride=ap), :] = r[:na]
            a2[pl.ds(nb + b, na, stride=ap), :] = r[na:]
            return c

        def stage2(kp, c):
            rows = [pl.ds(pl.multiple_of((2 * kp + e) * ap, 8), 2 * nb) for e in range(2)]
            slab = jnp.concatenate([a2[r, :] for r in rows], axis=1).astype(BF16)
            xf = jnp.dot(f2_ref[...], slab, preferred_element_type=F32)
            xr, xi = xf[:nb], xf[nb:]
            kr = jnp.concatenate([kf_ref[0, 2 * kp + e, 0] for e in range(2)], axis=1).astype(F32)
            ki = jnp.concatenate([kf_ref[0, 2 * kp + e, 1] for e in range(2)], axis=1).astype(F32)
            z = jnp.concatenate([xr * kr - xi * ki, xr * ki + xi * kr], axis=0).astype(BF16)
            back = jnp.dot(f2i_ref[...], z, preferred_element_type=F32)
            for e in range(2):
                a2[rows[e], :] = back[:, e * ct:(e + 1) * ct]
            return c

        def stage3(b, c):
            s = jnp.concatenate([a2[pl.ds(b, na, stride=ap), :], a2[pl.ds(nb + b, na, stride=ap), :]], axis=0)
            y = jnp.dot(g_ref[b], s.astype(BF16), preferred_element_type=F32)
            yp[0, pl.ds(b, ha, stride=xp), :] = y[:ha]
            yp[1, pl.ds(b, ha, stride=xp), :] = y[ha:]
            return c

        lax.fori_loop(0, nb, stage1, 0, unroll=HY_UNROLL_OUTER)
        lax.fori_loop(0, na // 2, stage2, 0, unroll=HY_UNROLL_INNER)
        lax.fori_loop(0, nb, stage3, 0, unroll=HY_UNROLL_OUTER)

    def gated(bias_row, write):
        for e in range(2):
            def body(a, c):
                rows = pl.ds(pl.multiple_of(a * xp, 8), nb)
                write(e, a, rows, gp[e, rows, :] * (yp[e, rows, :] + bias_row * vp[e, rows, :]))
                return c

            lax.fori_loop(0, ha, body, 0)

    def write_z(e, a, rows, val):
        vp[e, rows, :] = val

    def write_out(e, a, rows, val):
        o_ref[e, pl.ds(pl.multiple_of(a * nb, nb), nb), :] = val.astype(o_ref.dtype)

    short_conv_into(v_ref, 2, vp)
    short_conv_into(x1_ref, 0, gp)
    long_conv(kf1_ref)
    gated(bias_ref[0:1, :], write_z)
    short_conv_into(x2_ref, 1, gp)
    long_conv(kf2_ref)
    gated(bias_ref[1:2, :], write_out)


def _hyena_latent(hy_all, n_seq, conv_w, bias, kf):
    bn = hy_all.shape[0]
    ch = hy_all.shape[-1] // 3
    ct = HY_CT
    nct = ch // ct
    nb = HY_NB
    na = 2 * n_seq // nb
    assert bn % 2 == 0 and ch % ct == 0 and n_seq % (8 * nb) == 0
    f1, f2, f2i, g, _ = _dft_tables(n_seq)
    col = lambda which: pl.BlockSpec((2, n_seq, ct), lambda j, p: (p, 0, which * nct + j))
    kf_spec = lambda f: pl.BlockSpec((1, na, 2, nb, ct), lambda j, p: (f, 0, 0, 0, j), pipeline_mode=pl.Buffered(1))
    pad_rows = (na // 2) * HY_XPITCH
    return pl.pallas_call(
        functools.partial(_hyena_kernel, na),
        grid=(nct, bn // 2),
        in_specs=[col(0), col(1), col(2),
                  pl.BlockSpec((HY_SHORT, 3, ct), lambda j, p: (0, 0, j)),
                  pl.BlockSpec((2, ct), lambda j, p: (0, j)),
                  kf_spec(0), kf_spec(1),
                  _const_spec(f1.shape), _const_spec(f2.shape), _const_spec(f2i.shape), _const_spec(g.shape)],
        out_specs=pl.BlockSpec((2, n_seq, ct), lambda j, p: (p, 0, j)),
        out_shape=jax.ShapeDtypeStruct((bn, n_seq, ch), BF16),
        scratch_shapes=[pltpu.VMEM((2, n_seq + 2 * SUBLANES, ct), F32), pltpu.VMEM((2, pad_rows, ct), F32),
                        pltpu.VMEM((2, pad_rows, ct), F32), pltpu.VMEM((2, pad_rows, ct), F32),
                        pltpu.VMEM((na * HY_APITCH, ct), F32)],
        compiler_params=_params(2),
        name="hyena_latent",
    )(hy_all, hy_all, hy_all, conv_w.reshape(HY_SHORT, 3, ch), bias, kf, kf, f1, f2, f2i, g)


@functools.lru_cache(maxsize=None)
def _dense_dft_tables(n_seq):
    n = 2 * n_seq
    ang = 2.0 * np.pi * np.arange(n)[:, None] * np.arange(n_seq)[None, :] / n
    fd = np.concatenate([np.cos(ang), -np.sin(ang)], axis=0)
    gd = np.concatenate([np.cos(ang).T, -np.sin(ang).T], axis=1) / n
    return jnp.asarray(fd, dtype=BF16), jnp.asarray(gd, dtype=BF16)


def _hyena_ctx_kernel(n_seq, x1_ref, x2_ref, v_ref, cw_ref, bias_ref, kf1_ref, kf2_ref, fd_ref, gd_ref, o_ref):
    n = 2 * n_seq

    def short_conv(raw_ref, which):
        x = raw_ref[0].astype(F32)
        zero = jnp.zeros((1, x.shape[-1]), F32)
        prev = jnp.concatenate([zero, x[:-1]], axis=0)
        nxt = jnp.concatenate([x[1:], zero], axis=0)
        return cw_ref[0, which:which + 1, :] * prev + cw_ref[1, which:which + 1, :] * x + cw_ref[2, which:which + 1, :] * nxt

    def long_conv(u, kf_ref, bias_row):
        xf = jnp.dot(fd_ref[...], u.astype(BF16), preferred_element_type=F32)
        xr, xi = xf[:n], xf[n:]
        kr, ki = kf_ref[0], kf_ref[1]
        z = jnp.concatenate([xr * kr - xi * ki, xr * ki + xi * kr], axis=0).astype(BF16)
        return jnp.dot(gd_ref[...], z, preferred_element_type=F32) + bias_row * u

    x1, x2, v = short_conv(x1_ref, 0), short_conv(x2_ref, 1), short_conv(v_ref, 2)
    z = x1 * long_conv(v, kf1_ref, bias_ref[0:1, :])
    o_ref[0] = (x2 * long_conv(z, kf2_ref, bias_ref[1:2, :])).astype(o_ref.dtype)


def _hyena_context(hy_all, row_block, n_seq, conv_w, bias, kf1, kf2):
    bn = hy_all.shape[0]
    ch = hy_all.shape[-1] // 3
    fd, gd = _dense_dft_tables(n_seq)
    col = lambda which: pl.BlockSpec((1, n_seq, ch), lambda i: (i, row_block, which))
    split = lambda kf: jnp.stack([kf.real, kf.imag], axis=0)
    return pl.pallas_call(
        functools.partial(_hyena_ctx_kernel, n_seq),
        grid=(bn,),
        in_specs=[col(0), col(1), col(2), _const_spec((HY_SHORT, 3, ch)), _const_spec((2, ch)),
                  _const_spec((2, 2 * n_seq, ch)), _const_spec((2, 2 * n_seq, ch)),
                  _const_spec(fd.shape), _const_spec(gd.shape)],
        out_specs=pl.BlockSpec((1, n_seq, ch), lambda i: (i, 0, 0)),
        out_shape=jax.ShapeDtypeStruct((bn, n_seq, ch), BF16),
        compiler_params=_params(1),
        name="hyena_context",
    )(hy_all, hy_all, hy_all, conv_w.reshape(HY_SHORT, 3, ch), bias, split(kf1), split(kf2), fd, gd)


def kernel(x, c, ctx, c_ctx, w_mod, b_mod, g_norm1, g_norm2, w_in, hy_conv, hy_w1, hy_b1, hy_w2, hy_b2, hy_w3,
           hy_freq1, hy_freq2, hy_decay, hy_bias, gla_wa2, gla_ba, gla_gnorm, s5_a_re, s5_a_im, s5_log_dt, s5_b_re,
           s5_b_im, s5_c_re, s5_c_im, s5_d, s5_w_glu, s5_b_glu, w_br_hy, w_br_gla, w_br_s5, w_out, w_ffn_in,
           w_ffn_out, g_final):
    bn, n_lat, d = x.shape
    n_ctx = ctx.shape[1]
    depth = w_in.shape[0]
    d_hy = w_br_hy.shape[1]
    s5_ch = w_br_s5.shape[1]
    d_ff = w_ffn_out.shape[1]
    assert n_ctx == ROW_TILE and n_lat % ROW_TILE == 0
    n_lat_tiles = n_lat // ROW_TILE
    n_low = 2 * GLA_LOWRANK
    o_a = GLA_QK + GLA_V
    o_u = o_a + n_low
    o_q = o_u + s5_ch
    o_gate = o_q + GLA_QK + GLA_V + 3 * d_hy
    col_sizes = (GLA_QK, GLA_V, A_PAD, GLA_QK, GLA_V, 3 * d_hy)

    cond = jnp.concatenate([c_ctx[None], c], axis=0)
    cond = jnp.pad(cond, ((0, (-cond.shape[0]) % SUBLANES), (0, 0)))
    mod_all = _modulation_all(cond, w_mod, b_mod)

    s5w_all = jax.vmap(functools.partial(_s5_weights, tc=S5_CHUNK))(
        s5_a_re, s5_a_im, s5_log_dt, s5_b_re, s5_b_im, s5_c_re, s5_c_im)
    wa_all = jnp.zeros((depth, A_PAD, 2 * GLA_QK), F32)
    wa_all = wa_all.at[:, :GLA_LOWRANK, :GLA_QK].set(gla_wa2[:, 0]).at[:, GLA_LOWRANK:n_low, GLA_QK:].set(gla_wa2[:, 1])
    wa_all = wa_all.astype(BF16)

    xc = jnp.concatenate([x, ctx], axis=1)
    for l in range(depth):
        last = l == depth - 1
        mods = jnp.stack([jnp.broadcast_to(mod_all[l, 0], (bn, 6 * d)), mod_all[l, 1:1 + bn]], axis=1)
        mods = mods[:, :, None, :]
        wl = w_in[l]
        w_proj = jnp.concatenate(
            [wl[:, :o_u], jnp.zeros((d, A_PAD - n_low), F32), wl[:, o_q:o_gate]], axis=1).astype(BF16)
        g1 = g_norm1[l][None]
        w_u_t = wl[:, o_u:o_q].T.astype(BF16)
        k_a, v_a, a_a, q_a, r_a, hy_a, u_t = _project(xc, mods, g1, w_proj, col_sizes, w_u_t, n_lat_tiles)

        y_gla = _gla_mix(k_a, v_a, a_a, q_a, r_a, wa_all[l], gla_ba[l], gla_gnorm[l][None], n_ctx)

        ys = _s5_mix(u_t, n_ctx, s5w_all, l)

        hy_p = (hy_w1[l], hy_b1[l], hy_w2[l], hy_b2[l], hy_w3[l], hy_freq1[l], hy_freq2[l], hy_decay[l])
        y_hy = _hyena_latent(hy_a, n_lat, hy_conv[l], hy_bias[l], _hyena_spectra(_hyena_kernels(n_lat, hy_p)))
        if last:
            y_hy_ctx = y_hy
        else:
            kf_ctx = jnp.fft.fft(_hyena_kernels(n_ctx, hy_p), axis=1)
            y_hy_ctx = _hyena_context(hy_a, n_lat // n_ctx, n_ctx, hy_conv[l], hy_bias[l], kf_ctx[0], kf_ctx[1])

        wts = (wl[:, o_gate:].astype(BF16), w_br_hy[l].astype(BF16), w_br_gla[l].astype(BF16),
               w_br_s5[l].astype(BF16), w_out[l].astype(BF16), s5_d[l][:, None], s5_w_glu[l].astype(BF16),
               s5_b_glu[l][None])
        n_tiles = n_lat_tiles if last else n_lat_tiles + n_ctx // ROW_TILE
        x_mid = _merge(xc, mods, g1, y_hy, y_hy_ctx, y_gla, ys, u_t, wts, n_tiles, n_lat_tiles)
        wf = w_ffn_in[l]
        xc = _ffn(x_mid, mods, g_norm2[l][None], wf[:, :d_ff].astype(BF16), wf[:, d_ff:].astype(BF16),
                  w_ffn_out[l].astype(BF16), g_final[None], n_lat_tiles, last)
    return xc
```

```python
import functools
import math

import jax
import jax.numpy as jnp
import numpy as np
from jax import lax
from jax.experimental import pallas as pl
from jax.experimental.pallas import tpu as pltpu

F32 = jnp.float32
BF16 = jnp.bfloat16
EPS = 1e-6

HY_SHORT = 3
HY_POS_BANDS = 16
HY_N_FILT = 4
GLA_HEADS = 4
GLA_DK = 64
GLA_DV = 128
GLA_QK = GLA_HEADS * GLA_DK
GLA_V = GLA_HEADS * GLA_DV
GLA_LOWRANK = 16
GLA_TAU = 16.0
GLA_CHUNK = 64
S5_STATE = 64

LANES = 128
SUBLANES = 8
VMEM_LIMIT = 56 * 1024 * 1024

GLA_BLOCK = 4
GLA_UNROLL = 2
A_PAD = LANES
ROW_TILE = 256


def _const_spec(shape):
    nd = len(shape)
    return pl.BlockSpec(shape, lambda *_: (0,) * nd, pipeline_mode=pl.Buffered(1))


def _params(n_axes):
    return pltpu.CompilerParams(dimension_semantics=("parallel",) * n_axes, vmem_limit_bytes=VMEM_LIMIT)


def _mod_kernel(c_ref, w_ref, b_ref, o_ref):
    c = c_ref[...]
    s = c * jax.nn.sigmoid(c)
    o_ref[0] = jnp.dot(s.astype(BF16), w_ref[0], preferred_element_type=F32) + b_ref[0]


def _modulation_all(cond, w_mod, b_mod):
    depth, d, n = w_mod.shape
    r = cond.shape[0]
    tn = 1536
    return pl.pallas_call(
        _mod_kernel,
        grid=(depth, n // tn),
        in_specs=[
            pl.BlockSpec((r, d), lambda l, j: (0, 0)),
            pl.BlockSpec((1, d, tn), lambda l, j: (l, 0, j)),
            pl.BlockSpec((1, 1, tn), lambda l, j: (l, 0, j)),
        ],
        out_specs=pl.BlockSpec((1, r, tn), lambda l, j: (l, 0, j)),
        out_shape=jax.ShapeDtypeStruct((depth, r, n), F32),
        compiler_params=_params(2),
        name="modulation",
    )(cond, w_mod.astype(BF16), b_mod.reshape(depth, 1, n))


def _norm_mod(x, g, shift, scale):
    y = x * lax.rsqrt(jnp.mean(x * x, axis=-1, keepdims=True) + EPS)
    return (y * g) * (1.0 + scale) + shift


_NT = (((1,), (1,)), ((), ()))


def _proj_kernel(col_sizes, d, x_ref, mod_ref, g_ref, w_ref, wt_ref, *o_refs):
    m = mod_ref[0, 0]
    h = _norm_mod(x_ref[0], g_ref[...], m[:, 0:d], m[:, d:2 * d]).astype(BF16)
    off = 0
    for o_ref, n in zip(o_refs[:-1], col_sizes):
        o_ref[0] = jnp.dot(h, w_ref[:, off:off + n], preferred_element_type=F32).astype(o_ref.dtype)
        off += n
    o_refs[-1][0] = lax.dot_general(wt_ref[...], h, _NT, preferred_element_type=F32)


def _mod_spec(d, n_lat_tiles):
    return pl.BlockSpec((1, 1, 1, 6 * d), lambda i, j: (i, (j < n_lat_tiles).astype(jnp.int32), 0, 0))


def _project(xc, mods, g, w, col_sizes, w_t, n_lat_tiles):
    b, t, d = xc.shape
    tm = ROW_TILE
    n_tot = sum(col_sizes)
    n_t = w_t.shape[0]
    out_shape = [jax.ShapeDtypeStruct((b, t, n), BF16) for n in col_sizes] + [jax.ShapeDtypeStruct((b, n_t, t), F32)]
    out_specs = [pl.BlockSpec((1, tm, n), lambda i, j: (i, j, 0)) for n in col_sizes]
    out_specs.append(pl.BlockSpec((1, n_t, tm), lambda i, j: (i, 0, j)))
    return pl.pallas_call(
        functools.partial(_proj_kernel, tuple(col_sizes), d),
        grid=(b, t // tm),
        in_specs=[
            pl.BlockSpec((1, tm, d), lambda i, j: (i, j, 0)),
            _mod_spec(d, n_lat_tiles),
            _const_spec((1, d)),
            _const_spec((d, n_tot)),
            _const_spec((n_t, d)),
        ],
        out_specs=out_specs,
        out_shape=out_shape,
        compiler_params=_params(2),
        name="in_proj",
    )(xc, mods, g, w, w_t)


def _gelu_tanh(x):
    return 0.5 * x * (1.0 + jnp.tanh(math.sqrt(2.0 / math.pi) * (x + 0.044715 * (x * x * x))))


def _merge_kernel(d, n_lat_tiles, x_ref, mod_ref, g_ref, yhy_ref, yhyc_ref, ygla_ref, ys5_ref, u_ref, wg_ref, whb_ref,
                  wgb_ref, wsb_ref, wo_ref, s5d_ref, wglu_ref, bglu_ref, o_ref):
    x = x_ref[0]
    m = mod_ref[0, 0]
    h = _norm_mod(x, g_ref[...], m[:, 0:d], m[:, d:2 * d]).astype(BF16)
    y_hy = jnp.where(pl.program_id(1) < n_lat_tiles, yhy_ref[0], yhyc_ref[0])
    y5 = (ys5_ref[0] + s5d_ref[...] * u_ref[0]).T
    g5 = _gelu_tanh(y5)
    y_s5 = g5 * jax.nn.sigmoid(jnp.dot(g5.astype(BF16), wglu_ref[...], preferred_element_type=F32) + bglu_ref[...])

    def branch(k, y, wb_ref):
        gate = jnp.dot(h, wg_ref[:, k * d:(k + 1) * d], preferred_element_type=F32)
        return jax.nn.sigmoid(gate) * jnp.dot(y, wb_ref[...], preferred_element_type=F32)

    mix = branch(0, y_hy, whb_ref) + branch(1, ygla_ref[0], wgb_ref) + branch(2, y_s5.astype(BF16), wsb_ref)
    out = jnp.dot(mix.astype(BF16), wo_ref[...], preferred_element_type=F32)
    o_ref[0] = x + m[:, 2 * d:3 * d] * out


def _merge(xc, mods, g, y_hy, y_hy_ctx, y_gla, ys_s5, u_s5, wts, nj, n_lat_tiles):
    b, t, d = xc.shape
    tm = ROW_TILE
    ch = y_hy.shape[-1]
    row = lambda n: pl.BlockSpec((1, tm, n), lambda i, j: (i, j, 0))
    lat_row = pl.BlockSpec((1, tm, ch), lambda i, j: (i, jnp.minimum(j, n_lat_tiles - 1), 0))
    ctx_row = pl.BlockSpec((1, tm, ch), lambda i, j: (i, 0, 0))
    col = lambda n: pl.BlockSpec((1, n, tm), lambda i, j: (i, 0, j))
    w_gate, w_hy, w_gla, w_s5, w_out, s5_d, w_glu, b_glu = wts
    return pl.pallas_call(
        functools.partial(_merge_kernel, d, n_lat_tiles),
        grid=(b, nj),
        in_specs=[
            row(d),
            _mod_spec(d, n_lat_tiles),
            _const_spec((1, d)),
            lat_row, ctx_row, row(ch), col(ch), col(ch),
            _const_spec(w_gate.shape), _const_spec(w_hy.shape), _const_spec(w_gla.shape), _const_spec(w_s5.shape),
            _const_spec(w_out.shape), _const_spec(s5_d.shape), _const_spec(w_glu.shape), _const_spec(b_glu.shape),
        ],
        out_specs=pl.BlockSpec((1, tm, d), lambda i, j: (i, j, 0)),
        out_shape=jax.ShapeDtypeStruct((b, nj * tm, d), F32),
        compiler_params=_params(2),
        name="merge",
    )(xc, mods, g, y_hy, y_hy_ctx, y_gla, ys_s5, u_s5, w_gate, w_hy, w_gla, w_s5, w_out, s5_d, w_glu, b_glu)


def _ffn_kernel(d, final, x_ref, mod_ref, g_ref, wa_ref, wb_ref, wo_ref, gf_ref, o_ref):
    x = x_ref[0]
    m = mod_ref[0, 0]
    h = _norm_mod(x, g_ref[...], m[:, 3 * d:4 * d], m[:, 4 * d:5 * d]).astype(BF16)
    a = jnp.dot(h, wa_ref[...], preferred_element_type=F32)
    bb = jnp.dot(h, wb_ref[...], preferred_element_type=F32)
    act = (a * jax.nn.sigmoid(a) * bb).astype(BF16)
    y = x + m[:, 5 * d:6 * d] * jnp.dot(act, wo_ref[...], preferred_element_type=F32)
    if final:
        y = y * lax.rsqrt(jnp.mean(y * y, axis=-1, keepdims=True) + EPS) * gf_ref[...]
    o_ref[0] = y


def _ffn(x, mods, g, wa, wb, wo, g_final, n_lat_tiles, final):
    b, t, d = x.shape
    tm = ROW_TILE
    return pl.pallas_call(
        functools.partial(_ffn_kernel, d, final),
        grid=(b, t // tm),
        in_specs=[
            pl.BlockSpec((1, tm, d), lambda i, j: (i, j, 0)),
            _mod_spec(d, n_lat_tiles),
            _const_spec((1, d)),
            _const_spec(wa.shape), _const_spec(wb.shape), _const_spec(wo.shape),
            _const_spec((1, d)),
        ],
        out_specs=pl.BlockSpec((1, tm, d), lambda i, j: (i, j, 0)),
        out_shape=jax.ShapeDtypeStruct((b, t, d), F32),
        compiler_params=_params(2),
        name="ffn",
    )(x, mods, g, wa, wb, wo, g_final)


S5_CHUNK = LANES


def _s5_weights(a_re, a_im, log_dt, b_re, b_im, c_re, c_im, tc):
    g, p = a_re.shape[1:]
    i = b_re.shape[-1]
    lam_c = lax.complex(jnp.minimum(a_re, -1e-4), a_im)
    lam_dt = lam_c * jnp.exp(log_dt)[..., None]
    b_bar = ((jnp.exp(lam_dt) - 1.0) / lam_c)[..., None] * lax.complex(b_re, b_im)[None]
    c_mat = lax.complex(c_re, c_im)
    tau = jnp.arange(tc, dtype=F32)
    pw = jnp.exp(lam_dt[:, :, None, :] * tau[None, None, :, None])
    pw1 = pw * jnp.exp(lam_dt)[:, :, None, :]
    kern = jnp.einsum('gip,dgtp,dgpj->dgtij', c_mat, pw, b_bar).real
    k_lag = jnp.concatenate([kern[1][:, :0:-1], kern[0][:, :1] + kern[1][:, :1], kern[0][:, 1:]], axis=1)
    kr = jnp.pad(k_lag.transpose(0, 3, 2, 1), ((0, 0), (0, 0), (0, 0), (0, 1)))
    kr = kr.reshape(g, i, i * 2 * tc)

    inc_f = pw[0][:, None, ::-1, :] * b_bar[0].transpose(0, 2, 1)[:, :, None]
    inc_b = pw[1][:, None, :, :] * b_bar[1].transpose(0, 2, 1)[:, :, None]
    bm = jnp.concatenate([inc_f.real, inc_b.real, inc_f.imag, inc_b.imag], axis=-1).reshape(g, tc * i, 4 * p)

    out_f = c_mat.transpose(0, 2, 1)[..., None] * pw1[0].transpose(0, 2, 1)[:, :, None, :]
    out_b = c_mat.transpose(0, 2, 1)[..., None] * pw1[1][:, ::-1].transpose(0, 2, 1)[:, :, None, :]
    z = jnp.zeros((g, p, tc * i), F32)
    fl = lambda t: t.reshape(g, p, tc * i)
    cm = jnp.concatenate([fl(out_f.real), z, fl(-out_f.imag), z, z, fl(out_b.real), z, fl(-out_b.imag)], axis=1)

    lam_t = jnp.exp(lam_dt * float(tc))
    lam = jnp.stack([jnp.concatenate([lam_t[0].real, lam_t[1].real], -1),
                     jnp.concatenate([lam_t[0].imag, lam_t[1].imag], -1)], axis=1)
    return kr, bm.astype(BF16), cm.astype(BF16), lam


def _s5_kernel(nc, nc_ctx, tc, u_ref, kr_ref, bm_ref, cm_ref, lam_ref, y_ref, up_ref, dx_ref, p_ref, m_ref):
    bn, i_sz = u_ref.shape[:2]
    nl = nc - nc_ctx
    width = kr_ref.shape[-1]
    seg = width // i_sz
    for j in range(i_sz):
        lag_rows = pltpu.roll(jnp.broadcast_to(kr_ref[0, j:j + 1, :], (tc, width)), width - (tc - 1), axis=1,
                              stride=1, stride_axis=0)
        m_ref[j * tc:(j + 1) * tc, :] = jnp.concatenate(
            [lag_rows[:, i * seg:i * seg + tc] for i in range(i_sz)], axis=1).astype(BF16)
    for n in range(nc):
        for i in range(i_sz):
            up_ref[n * bn:(n + 1) * bn, i * tc:(i + 1) * tc] = u_ref[:, i, n * tc:(n + 1) * tc]
    u = up_ref[...].astype(BF16)
    dx_ref[...] = jnp.dot(u, bm_ref[0], preferred_element_type=F32)
    lam = lam_ref[0]
    lr, li = lam[0:1], lam[1:2]
    w = lam.shape[-1]
    half = w // 2
    is_f = lax.broadcasted_iota(jnp.int32, (bn, w), 1) < half

    def step(s, carry):
        sr, si = carry
        nf = jnp.where(s < nc_ctx, nl + s, s - nc_ctx)
        nb = nc - 1 - s
        rf = pl.multiple_of(nf * bn, bn)
        rb = pl.multiple_of(nb * bn, bn)
        p_ref[pl.ds(rf, bn), 0:w] = sr
        p_ref[pl.ds(rf, bn), w:2 * w] = si
        p_ref[pl.ds(rb, bn), 2 * w:3 * w] = sr
        p_ref[pl.ds(rb, bn), 3 * w:4 * w] = si
        d_re = jnp.where(is_f, dx_ref[pl.ds(rf, bn), 0:w], dx_ref[pl.ds(rb, bn), 0:w])
        d_im = jnp.where(is_f, dx_ref[pl.ds(rf, bn), w:2 * w], dx_ref[pl.ds(rb, bn), w:2 * w])
        return lr * sr - li * si + d_re, lr * si + li * sr + d_im

    zero = jnp.zeros((bn, 2 * half), F32)
    lax.fori_loop(0, nc, step, (zero, zero))
    y = jnp.dot(u, m_ref[...], preferred_element_type=F32)
    y = y + jnp.dot(p_ref[...].astype(BF16), cm_ref[0], preferred_element_type=F32)
    for n in range(nc):
        for i in range(i_sz):
            y_ref[:, i, n * tc:(n + 1) * tc] = y[n * bn:(n + 1) * bn, i * tc:(i + 1) * tc]


def _s5_mix(u_t, n_ctx, wts, layer):
    kr, bm, cm, lam = wts
    bn, ch, t = u_t.shape
    g, i = kr.shape[1:3]
    tc = kr.shape[3] // (2 * i)
    assert 2 * S5_STATE == LANES and bn % SUBLANES == 0 and n_ctx % tc == 0 and t % tc == 0 and tc % LANES == 0
    nc, nc_ctx = t // tc, n_ctx // tc
    k = tc * i
    r = nc * bn
    return pl.pallas_call(
        functools.partial(_s5_kernel, nc, nc_ctx, tc),
        grid=(g,),
        in_specs=[
            pl.BlockSpec((bn, i, t), lambda j: (0, j, 0)),
            pl.BlockSpec((None, 1, i, 2 * k), lambda j: (layer, j, 0, 0)),
            pl.BlockSpec((None, 1, k, 4 * S5_STATE), lambda j: (layer, j, 0, 0)),
            pl.BlockSpec((None, 1, 8 * S5_STATE, k), lambda j: (layer, j, 0, 0)),
            pl.BlockSpec((None, 1, 2, 2 * S5_STATE), lambda j: (layer, j, 0, 0)),
        ],
        out_specs=pl.BlockSpec((bn, i, t), lambda j: (0, j, 0)),
        out_shape=jax.ShapeDtypeStruct((bn, ch, t), F32),
        scratch_shapes=[pltpu.VMEM((r, k), F32), pltpu.VMEM((r, 4 * S5_STATE), F32),
                        pltpu.VMEM((r, 8 * S5_STATE), F32), pltpu.VMEM((k, k), BF16)],
        compiler_params=_params(1),
        name="s5_mix",
    )(u_t, kr, bm, cm, lam)


def _gla_kernel(nc, nc_ctx, k_ref, v_ref, a_ref, q_ref, r_ref, wa_ref, ba_ref, gn_ref, y_ref, of_ref, ob_ref,
                s_ref):
    c, h_n, dk, dv = GLA_CHUNK, GLA_HEADS, GLA_DK, GLA_DV
    qk = h_n * dk
    q_scale = dk ** -0.5
    nbk = GLA_BLOCK
    rb = nbk * c
    row_i = lax.broadcasted_iota(jnp.int32, (rb, rb), 0)
    col_i = lax.broadcasted_iota(jnp.int32, (rb, rb), 1)
    same = (row_i // c) == (col_i // c)
    causal = (same & (row_i >= col_i), same & (row_i <= col_i))
    cum_ops = [m.astype(BF16) for m in causal]
    att_mask = [jnp.concatenate([m] * h_n, axis=0) for m in causal]
    lane_head = lax.broadcasted_iota(jnp.int32, (1, qk), 1) // dk
    head_lanes = [lane_head == h for h in range(h_n)]

    def by_head(x):
        zero = jnp.zeros_like(x)
        return jnp.concatenate([jnp.where(m, x, zero) for m in head_lanes], axis=0)

    def block(j, d):
        rows = pl.ds(pl.multiple_of(j * rb, rb), rb)
        k = k_ref[0, rows, :].astype(F32)
        q = q_ref[0, rows, :].astype(F32) * q_scale
        v = v_ref[0, rows, :]
        z = jnp.dot(a_ref[0, rows, :], wa_ref[:, d * qk:(d + 1) * qk], preferred_element_type=F32) + ba_ref[d:d + 1, :]
        log_a = (jnp.minimum(z, 0.0) - jnp.log(1.0 + jnp.exp(-jnp.abs(z)))) * (1.0 / GLA_TAU)
        hi = log_a.astype(BF16)
        lo = (log_a - hi.astype(F32)).astype(BF16)
        cs = jnp.dot(cum_ops[d], jnp.concatenate([hi, lo], axis=1), preferred_element_type=F32)
        b = cs[:, :qk] + cs[:, qk:]
        b_tot = jnp.concatenate(
            [jnp.broadcast_to(b[n * c + (c - 1 if d == 0 else 0)][None], (c, qk)) for n in range(nbk)], axis=0)
        qd = (q * jnp.exp(b)).astype(BF16)
        kd = (k * jnp.exp(-b)).astype(BF16)
        kl = (k * jnp.exp(b_tot - b)).astype(BF16)
        qm = by_head(qd)
        intra = []
        for h in range(h_n):
            att = lax.dot_general(qm[h * rb:(h + 1) * rb], kd, _NT, preferred_element_type=F32)
            att = jnp.where(causal[d], att, 0.0).astype(BF16)
            intra.append(jnp.dot(att, v[:, h * dv:(h + 1) * dv], preferred_element_type=F32))
        s_t = s_ref[d]
        inter = [None] * nbk
        for n in (range(nbk) if d == 0 else reversed(range(nbk))):
            r0 = n * c
            qm_n = jnp.concatenate([qm[h * rb + r0:h * rb + r0 + c] for h in range(h_n)], axis=0)
            inter[n] = lax.dot_general(qm_n, s_t.astype(BF16), _NT, preferred_element_type=F32)
            v_t = jnp.concatenate(
                [jnp.concatenate([v[r0:r0 + c, h * dv:(h + 1) * dv] for h in range(p, p + dv // c)], axis=0).T
                 for p in range(0, h_n, dv // c)], axis=1)
            s_t = s_t * jnp.exp(b_tot[r0:r0 + 1]) + jnp.dot(v_t, by_head(kl[r0:r0 + c]), preferred_element_type=F32)
        s_ref[d] = s_t
        o = jnp.concatenate(
            [intra[h] + jnp.concatenate([inter[n][h * c:(h + 1) * c] for n in range(nbk)], axis=0)
             for h in range(h_n)], axis=1)
        return rows, o

    def readout(rows, o):
        r = r_ref[0, rows, :].astype(F32)
        gate = r * jax.nn.sigmoid(r)
        outs = []
        for h in range(h_n):
            oh = o[:, h * dv:(h + 1) * dv]
            oh = oh * lax.rsqrt(jnp.mean(oh * oh, axis=-1, keepdims=True) + EPS) * gn_ref[...]
            outs.append(oh * gate[:, h * dv:(h + 1) * dv])
        y_ref[0, rows, :] = jnp.concatenate(outs, axis=1).astype(y_ref.dtype)

    def scan_step(i, carry, lo_block, hi_block, phase):
        rows_f, o_f = block(lo_block + i, 0)
        if phase == "park":
            of_ref[rows_f, :] = o_f
        rows_b, o_b = block(hi_block - 1 - i, 1)
        if phase == "park":
            ob_ref[rows_b, :] = o_b
        elif phase == "meet":
            readout(rows_f, o_f + o_b)
        else:
            readout(rows_f, o_f + ob_ref[rows_f, :])
            readout(rows_b, o_b + of_ref[rows_b, :])
        return carry

    def scan(lo_block, n_blocks):
        step = functools.partial(scan_step, lo_block=lo_block, hi_block=lo_block + n_blocks)
        half = n_blocks // 2
        lax.fori_loop(0, half, functools.partial(step, phase="park"), 0, unroll=GLA_UNROLL)
        if n_blocks % 2:
            step(jnp.int32(half), 0, phase="meet")
        lax.fori_loop(n_blocks - half, n_blocks, functools.partial(step, phase="finish"), 0, unroll=GLA_UNROLL)

    nl, n_ctx = (nc - nc_ctx) // nbk, nc_ctx // nbk
    s_ref[...] = jnp.zeros(s_ref.shape, F32)
    scan(nl, n_ctx)
    scan(0, nl)


def _gla_mix(k, v, a, q, r, wa, ba, gnorm, n_ctx):
    bn, t, qk = k.shape
    vd = v.shape[-1]
    c = GLA_CHUNK
    assert t % (c * GLA_BLOCK) == 0 and n_ctx % (c * GLA_BLOCK) == 0 and GLA_DV % c == 0
    seq = lambda n, bufs=2: pl.BlockSpec((1, t, n), lambda i: (i, 0, 0), pipeline_mode=pl.Buffered(bufs))
    return pl.pallas_call(
        functools.partial(_gla_kernel, t // c, n_ctx // c),
        grid=(bn,),
        in_specs=[seq(qk), seq(vd), seq(a.shape[-1]), seq(qk), seq(vd, 1),
                  _const_spec(wa.shape), _const_spec(ba.shape), _const_spec(gnorm.shape)],
        out_specs=pl.BlockSpec((1, t, vd), lambda i: (i, 0, 0)),
        out_shape=jax.ShapeDtypeStruct((bn, t, vd), BF16),
        scratch_shapes=[pltpu.VMEM((t, vd), F32), pltpu.VMEM((t, vd), F32), pltpu.VMEM((2, GLA_DV, qk), F32)],
        compiler_params=_params(1),
        name="gla_mix",
    )(k, v, a, q, r, wa, ba, gnorm)


HY_NB = LANES
HY_XPITCH = HY_NB + SUBLANES
HY_APITCH = 2 * HY_NB + SUBLANES
HY_CT = LANES
HY_UNROLL_OUTER = 32
HY_UNROLL_INNER = 16
HY_UNROLL_ROWS = 4


def _hyena_filters(n, pos, params):
    w1, b1, w2, b2, w3, freq1, freq2, decay = (p.astype(F32) for p in params)
    t = (pos / (n - 1))[:, None]
    bands = jnp.linspace(1e-4, HY_POS_BANDS - 1, HY_POS_BANDS, dtype=F32)
    ang = (2.0 * math.pi / n) * pos[:, None] * bands[None]
    z = jnp.concatenate([t, jnp.cos(ang), -jnp.sin(ang)], axis=-1)
    hid = jnp.sin(freq1 * (z @ w1 + b1))
    hid = jnp.sin(freq2 * (hid @ w2 + b2))
    w3 = w3.reshape(w3.shape[0], HY_N_FILT, -1)
    window = jnp.exp(-t[None] * jnp.abs(decay).reshape(HY_N_FILT, 1, -1))
    return jnp.einsum('nk,kfc->fnc', hid, w3) * window


def _hyena_kernels(n, params):
    slot = jnp.arange(2 * n)
    h = _hyena_filters(n, jnp.where(slot < n, slot, 2 * n - slot).astype(F32), params)
    is_fwd, is_bwd = (slot < n)[None, :, None], (slot > n)[None, :, None]
    k = jnp.where(is_fwd, h[0::2], 0.0) + jnp.where(is_bwd, h[1::2], 0.0)
    lag0_bwd = _hyena_filters(n, jnp.zeros((1,), F32), params)[1::2]
    k = k + jnp.where(slot[None, :, None] == 0, lag0_bwd, 0.0)
    return k * lax.rsqrt(jnp.sum(k * k, axis=1, keepdims=True) + EPS)


def _spectrum_kernel(na, k_ref, f1_ref, f2_ref, o_ref, a2):
    nb, ap = HY_NB, HY_APITCH

    def stage1(b, c):
        xs = k_ref[0, pl.ds(b, na, stride=nb), :].astype(BF16)
        r = jnp.dot(f1_ref[b], xs, preferred_element_type=F32)
        a2[pl.ds(b, na, stride=ap), :] = r[:na]
        a2[pl.ds(nb + b, na, stride=ap), :] = r[na:]
        return c

    def stage2(ka, c):
        rows = pl.ds(pl.multiple_of(ka * ap, 8), 2 * nb)
        xf = jnp.dot(f2_ref[...], a2[rows, :].astype(BF16), preferred_element_type=F32)
        o_ref[0, ka, 0] = xf[:nb].astype(o_ref.dtype)
        o_ref[0, ka, 1] = xf[nb:].astype(o_ref.dtype)
        return c

    lax.fori_loop(0, nb, stage1, 0, unroll=HY_UNROLL_INNER)
    lax.fori_loop(0, na, stage2, 0, unroll=HY_UNROLL_INNER)


def _hyena_spectra(kernels):
    nf, n, ch = kernels.shape
    nb, ct = HY_NB, HY_CT
    na = n // nb
    _, f2, _, _, f1_full = _dft_tables(n // 2)
    return pl.pallas_call(
        functools.partial(_spectrum_kernel, na),
        grid=(nf, ch // ct),
        in_specs=[pl.BlockSpec((1, n, ct), lambda f, j: (f, 0, j)), _const_spec(f1_full.shape), _const_spec(f2.shape)],
        out_specs=pl.BlockSpec((1, na, 2, nb, ct), lambda f, j: (f, 0, 0, 0, j)),
        out_shape=jax.ShapeDtypeStruct((nf, na, 2, nb, ch), BF16),
        scratch_shapes=[pltpu.VMEM((na * HY_APITCH, ct), F32)],
        compiler_params=_params(2),
        name="hyena_spectrum",
    )(kernels, f1_full, f2)


def _cplx_block(m):
    return np.block([[m.real, -m.imag], [m.imag, m.real]])


@functools.lru_cache(maxsize=None)
def _dft_tables(n_seq):
    nb = HY_NB
    n = 2 * n_seq
    na = n // nb
    ha = na // 2
    ka = np.arange(na)[:, None]
    b = np.arange(nb)
    tw = np.exp(-2j * np.pi * ka * b[None, :] / n)
    w_a = np.exp(-2j * np.pi * ka * np.arange(ha)[None, :] / na)
    f1 = np.stack([_cplx_block(tw[:, i:i + 1] * w_a) for i in range(nb)])
    g = np.stack([_cplx_block((np.conj(tw[:, i:i + 1] * w_a)).T / n) for i in range(nb)])
    w_b = np.exp(-2j * np.pi * b[:, None] * b[None, :] / nb)
    f2 = _cplx_block(w_b)
    f2i = _cplx_block(np.conj(w_b))
    w_full = np.exp(-2j * np.pi * ka * np.arange(na)[None, :] / na)
    f1_full = np.stack([np.concatenate([(tw[:, i:i + 1] * w_full).real, (tw[:, i:i + 1] * w_full).imag], axis=0)
                        for i in range(nb)])
    return tuple(jnp.asarray(t, dtype=BF16) for t in (f1, f2, f2i, g, f1_full))


def _hyena_kernel(na, x1_ref, x2_ref, v_ref, cw_ref, bias_ref, kf1_ref, kf2_ref, f1_ref, f2_ref, f2i_ref, g_ref,
                  o_ref, rawf, vp, gp, yp, a2):
    nb, xp, ap = HY_NB, HY_XPITCH, HY_APITCH
    ha = na // 2
    n_seq = ha * nb
    ct = o_ref.shape[-1]
    halo = SUBLANES
    rawf[:, 0:halo, :] = jnp.zeros((2, halo, ct), F32)
    rawf[:, halo + n_seq:2 * halo + n_seq, :] = jnp.zeros((2, halo, ct), F32)

    def short_conv_into(raw_ref, which, dst):
        w0, w1, w2 = (cw_ref[j, which:which + 1, :] for j in range(HY_SHORT))
        for e in range(2):
            rawf[e, halo:halo + n_seq, :] = raw_ref[e].astype(F32)

            def body(a, c):
                base = pl.multiple_of(a * nb, nb) + halo
                u = (w0 * rawf[e, pl.ds(base - 1, nb), :] + w1 * rawf[e, pl.ds(base, nb), :]
                     + w2 * rawf[e, pl.ds(base + 1, nb), :])
                dst[e, pl.ds(pl.multiple_of(a * xp, 8), nb), :] = u
                return c

            lax.fori_loop(0, ha, body, 0, unroll=HY_UNROLL_ROWS)

    def long_conv(kf_ref):
        def stage1(b, c):
            xs = jnp.concatenate([vp[0, pl.ds(b, ha, stride=xp), :], vp[1, pl.ds(b, ha, stride=xp), :]], axis=0)
            r = jnp.dot(f1_ref[b], xs.astype(BF16), preferred_element_type=F32)
            a2[pl.ds(b, na, stride=ap), :] = r[:na]
            a2[pl.ds(nb + b, na, stride=ap), :] = r[na:]
            return c

        def stage2(kp, c):
            rows = [pl.ds(pl.multiple_of((2 * kp + e) * ap, 8), 2 * nb) for e in range(2)]
            slab = jnp.concatenate([a2[r, :] for r in rows], axis=1).astype(BF16)
            xf = jnp.dot(f2_ref[...], slab, preferred_element_type=F32)
            xr, xi = xf[:nb], xf[nb:]
            kr = jnp.concatenate([kf_ref[0, 2 * kp + e, 0] for e in range(2)], axis=1).astype(F32)
            ki = jnp.concatenate([kf_ref[0, 2 * kp + e, 1] for e in range(2)], axis=1).astype(F32)
            z = jnp.concatenate([xr * kr - xi * ki, xr * ki + xi * kr], axis=0).astype(BF16)
            back = jnp.dot(f2i_ref[...], z, preferred_element_type=F32)
            for e in range(2):
                a2[rows[e], :] = back[:, e * ct:(e + 1) * ct]
            return c

        def stage3(b, c):
            s = jnp.concatenate([a2[pl.ds(b, na, stride=ap), :], a2[pl.ds(nb + b, na, stride=ap), :]], axis=0)
            y = jnp.dot(g_ref[b], s.astype(BF16), preferred_element_type=F32)
            yp[0, pl.ds(b, ha, stride=xp), :] = y[:ha]
            yp[1, pl.ds(b, ha, stride=xp), :] = y[ha:]
            return c

        lax.fori_loop(0, nb, stage1, 0, unroll=HY_UNROLL_OUTER)
        lax.fori_loop(0, na // 2, stage2, 0, unroll=HY_UNROLL_INNER)
        lax.fori_loop(0, nb, stage3, 0, unroll=HY_UNROLL_OUTER)

    def gated(bias_row, write):
        for e in range(2):
            def body(a, c):
                rows = pl.ds(pl.multiple_of(a * xp, 8), nb)
                write(e, a, rows, gp[e, rows, :] * (yp[e, rows, :] + bias_row * vp[e, rows, :]))
                return c

            lax.fori_loop(0, ha, body, 0, unroll=HY_UNROLL_ROWS)

    def write_z(e, a, rows, val):
        vp[e, rows, :] = val

    def write_out(e, a, rows, val):
        o_ref[e, pl.ds(pl.multiple_of(a * nb, nb), nb), :] = val.astype(o_ref.dtype)

    short_conv_into(v_ref, 2, vp)
    short_conv_into(x1_ref, 0, gp)
    long_conv(kf1_ref)
    gated(bias_ref[0:1, :], write_z)
    short_conv_into(x2_ref, 1, gp)
    long_conv(kf2_ref)
    gated(bias_ref[1:2, :], write_out)


def _hyena_latent(hy_all, n_seq, conv_w, bias, kf):
    bn = hy_all.shape[0]
    ch = hy_all.shape[-1] // 3
    ct = HY_CT
    nct = ch // ct
    nb = HY_NB
    na = 2 * n_seq // nb
    assert bn % 2 == 0 and ch % ct == 0 and n_seq % (8 * nb) == 0
    f1, f2, f2i, g, _ = _dft_tables(n_seq)
    col = lambda which: pl.BlockSpec((2, n_seq, ct), lambda j, p: (p, 0, which * nct + j))
    kf_spec = lambda f: pl.BlockSpec((1, na, 2, nb, ct), lambda j, p: (f, 0, 0, 0, j), pipeline_mode=pl.Buffered(1))
    pad_rows = (na // 2) * HY_XPITCH
    return pl.pallas_call(
        functools.partial(_hyena_kernel, na),
        grid=(nct, bn // 2),
        in_specs=[col(0), col(1), col(2),
                  pl.BlockSpec((HY_SHORT, 3, ct), lambda j, p: (0, 0, j)),
                  pl.BlockSpec((2, ct), lambda j, p: (0, j)),
                  kf_spec(0), kf_spec(1),
                  _const_spec(f1.shape), _const_spec(f2.shape), _const_spec(f2i.shape), _const_spec(g.shape)],
        out_specs=pl.BlockSpec((2, n_seq, ct), lambda j, p: (p, 0, j)),
        out_shape=jax.ShapeDtypeStruct((bn, n_seq, ch), BF16),
        scratch_shapes=[pltpu.VMEM((2, n_seq + 2 * SUBLANES, ct), F32), pltpu.VMEM((2, pad_rows, ct), F32),
                        pltpu.VMEM((2, pad_rows, ct), F32), pltpu.VMEM((2, pad_rows, ct), F32),
                        pltpu.VMEM((na * HY_APITCH, ct), F32)],
        compiler_params=_params(2),
        name="hyena_latent",
    )(hy_all, hy_all, hy_all, conv_w.reshape(HY_SHORT, 3, ch), bias, kf, kf, f1, f2, f2i, g)


@functools.lru_cache(maxsize=None)
def _dense_dft_tables(n_seq):
    n = 2 * n_seq
    ang = 2.0 * np.pi * np.arange(n)[:, None] * np.arange(n_seq)[None, :] / n
    fd = np.concatenate([np.cos(ang), -np.sin(ang)], axis=0)
    gd = np.concatenate([np.cos(ang).T, -np.sin(ang).T], axis=1) / n
    return jnp.asarray(fd, dtype=BF16), jnp.asarray(gd, dtype=BF16)


def _hyena_ctx_kernel(n_seq, x1_ref, x2_ref, v_ref, cw_ref, bias_ref, kf1_ref, kf2_ref, fd_ref, gd_ref, o_ref):
    n = 2 * n_seq

    def short_conv(raw_ref, which):
        x = raw_ref[0].astype(F32)
        zero = jnp.zeros((1, x.shape[-1]), F32)
        prev = jnp.concatenate([zero, x[:-1]], axis=0)
        nxt = jnp.concatenate([x[1:], zero], axis=0)
        return cw_ref[0, which:which + 1, :] * prev + cw_ref[1, which:which + 1, :] * x + cw_ref[2, which:which + 1, :] * nxt

    def long_conv(u, kf_ref, bias_row):
        xf = jnp.dot(fd_ref[...], u.astype(BF16), preferred_element_type=F32)
        xr, xi = xf[:n], xf[n:]
        kr, ki = kf_ref[0], kf_ref[1]
        z = jnp.concatenate([xr * kr - xi * ki, xr * ki + xi * kr], axis=0).astype(BF16)
        return jnp.dot(gd_ref[...], z, preferred_element_type=F32) + bias_row * u

    x1, x2, v = short_conv(x1_ref, 0), short_conv(x2_ref, 1), short_conv(v_ref, 2)
    z = x1 * long_conv(v, kf1_ref, bias_ref[0:1, :])
    o_ref[0] = (x2 * long_conv(z, kf2_ref, bias_ref[1:2, :])).astype(o_ref.dtype)


def _hyena_context(hy_all, row_block, n_seq, conv_w, bias, kf1, kf2):
    bn = hy_all.shape[0]
    ch = hy_all.shape[-1] // 3
    fd, gd = _dense_dft_tables(n_seq)
    col = lambda which: pl.BlockSpec((1, n_seq, ch), lambda i: (i, row_block, which))
    split = lambda kf: jnp.stack([kf.real, kf.imag], axis=0)
    return pl.pallas_call(
        functools.partial(_hyena_ctx_kernel, n_seq),
        grid=(bn,),
        in_specs=[col(0), col(1), col(2), _const_spec((HY_SHORT, 3, ch)), _const_spec((2, ch)),
                  _const_spec((2, 2 * n_seq, ch)), _const_spec((2, 2 * n_seq, ch)),
                  _const_spec(fd.shape), _const_spec(gd.shape)],
        out_specs=pl.BlockSpec((1, n_seq, ch), lambda i: (i, 0, 0)),
        out_shape=jax.ShapeDtypeStruct((bn, n_seq, ch), BF16),
        compiler_params=_params(1),
        name="hyena_context",
    )(hy_all, hy_all, hy_all, conv_w.reshape(HY_SHORT, 3, ch), bias, split(kf1), split(kf2), fd, gd)


def kernel(x, c, ctx, c_ctx, w_mod, b_mod, g_norm1, g_norm2, w_in, hy_conv, hy_w1, hy_b1, hy_w2, hy_b2, hy_w3,
           hy_freq1, hy_freq2, hy_decay, hy_bias, gla_wa2, gla_ba, gla_gnorm, s5_a_re, s5_a_im, s5_log_dt, s5_b_re,
           s5_b_im, s5_c_re, s5_c_im, s5_d, s5_w_glu, s5_b_glu, w_br_hy, w_br_gla, w_br_s5, w_out, w_ffn_in,
           w_ffn_out, g_final):
    bn, n_lat, d = x.shape
    n_ctx = ctx.shape[1]
    depth = w_in.shape[0]
    d_hy = w_br_hy.shape[1]
    s5_ch = w_br_s5.shape[1]
    d_ff = w_ffn_out.shape[1]
    assert n_ctx == ROW_TILE and n_lat % ROW_TILE == 0
    n_lat_tiles = n_lat // ROW_TILE
    n_low = 2 * GLA_LOWRANK
    o_a = GLA_QK + GLA_V
    o_u = o_a + n_low
    o_q = o_u + s5_ch
    o_gate = o_q + GLA_QK + GLA_V + 3 * d_hy
    col_sizes = (GLA_QK, GLA_V, A_PAD, GLA_QK, GLA_V, 3 * d_hy)

    cond = jnp.concatenate([c_ctx[None], c], axis=0)
    cond = jnp.pad(cond, ((0, (-cond.shape[0]) % SUBLANES), (0, 0)))
    mod_all = _modulation_all(cond, w_mod, b_mod)

    s5w_all = jax.vmap(functools.partial(_s5_weights, tc=S5_CHUNK))(
        s5_a_re, s5_a_im, s5_log_dt, s5_b_re, s5_b_im, s5_c_re, s5_c_im)
    wa_all = jnp.zeros((depth, A_PAD, 2 * GLA_QK), F32)
    wa_all = wa_all.at[:, :GLA_LOWRANK, :GLA_QK].set(gla_wa2[:, 0]).at[:, GLA_LOWRANK:n_low, GLA_QK:].set(gla_wa2[:, 1])
    wa_all = wa_all.astype(BF16)

    xc = jnp.concatenate([x, ctx], axis=1)
    for l in range(depth):
        last = l == depth - 1
        mods = jnp.stack([jnp.broadcast_to(mod_all[l, 0], (bn, 6 * d)), mod_all[l, 1:1 + bn]], axis=1)
        mods = mods[:, :, None, :]
        wl = w_in[l]
        w_proj = jnp.concatenate(
            [wl[:, :o_u], jnp.zeros((d, A_PAD - n_low), F32), wl[:, o_q:o_gate]], axis=1).astype(BF16)
        g1 = g_norm1[l][None]
        w_u_t = wl[:, o_u:o_q].T.astype(BF16)
        k_a, v_a, a_a, q_a, r_a, hy_a, u_t = _project(xc, mods, g1, w_proj, col_sizes, w_u_t, n_lat_tiles)

        y_gla = _gla_mix(k_a, v_a, a_a, q_a, r_a, wa_all[l], gla_ba[l], gla_gnorm[l][None], n_ctx)

        ys = _s5_mix(u_t, n_ctx, s5w_all, l)

        hy_p = (hy_w1[l], hy_b1[l], hy_w2[l], hy_b2[l], hy_w3[l], hy_freq1[l], hy_freq2[l], hy_decay[l])
        y_hy = _hyena_latent(hy_a, n_lat, hy_conv[l], hy_bias[l], _hyena_spectra(_hyena_kernels(n_lat, hy_p)))
        if last:
            y_hy_ctx = y_hy
        else:
            kf_ctx = jnp.fft.fft(_hyena_kernels(n_ctx, hy_p), axis=1)
            y_hy_ctx = _hyena_context(hy_a, n_lat // n_ctx, n_ctx, hy_conv[l], hy_bias[l], kf_ctx[0], kf_ctx[1])

        wts = (wl[:, o_gate:].astype(BF16), w_br_hy[l].astype(BF16), w_br_gla[l].astype(BF16),
               w_br_s5[l].astype(BF16), w_out[l].astype(BF16), s5_d[l][:, None], s5_w_glu[l].astype(BF16),
               s5_b_glu[l][None])
        n_tiles = n_lat_tiles if last else n_lat_tiles + n_ctx // ROW_TILE
        x_mid = _merge(xc, mods, g1, y_hy, y_hy_ctx, y_gla, ys, u_t, wts, n_tiles, n_lat_tiles)
        wf = w_ffn_in[l]
        xc = _ffn(x_mid, mods, g_norm2[l][None], wf[:, :d_ff].astype(BF16), wf[:, d_ff:].astype(BF16),
                  w_ffn_out[l].astype(BF16), g_final[None], n_lat_tiles, last)
    return xc
```

```python
import functools
import math

import jax
import jax.numpy as jnp
import numpy as np
from jax import lax
from jax.experimental import pallas as pl
from jax.experimental.pallas import tpu as pltpu

F32 = jnp.float32
BF16 = jnp.bfloat16
EPS = 1e-6

HY_SHORT = 3
HY_POS_BANDS = 16
HY_N_FILT = 4
GLA_HEADS = 4
GLA_DK = 64
GLA_DV = 128
GLA_QK = GLA_HEADS * GLA_DK
GLA_V = GLA_HEADS * GLA_DV
GLA_LOWRANK = 16
GLA_TAU = 16.0
GLA_CHUNK = 64
S5_STATE = 64

LANES = 128
SUBLANES = 8
VMEM_LIMIT = 56 * 1024 * 1024

GLA_BLOCK = 4
GLA_UNROLL = 2
A_PAD = LANES
ROW_TILE = 256


def _const_spec(shape):
    nd = len(shape)
    return pl.BlockSpec(shape, lambda *_: (0,) * nd, pipeline_mode=pl.Buffered(1))


def _params(n_axes):
    return pltpu.CompilerParams(dimension_semantics=("parallel",) * n_axes, vmem_limit_bytes=VMEM_LIMIT)


def _mod_kernel(c_ref, w_ref, b_ref, o_ref):
    c = c_ref[...]
    s = c * jax.nn.sigmoid(c)
    o_ref[0] = jnp.dot(s.astype(BF16), w_ref[0], preferred_element_type=F32) + b_ref[0]


def _modulation_all(cond, w_mod, b_mod):
    depth, d, n = w_mod.shape
    r = cond.shape[0]
    tn = 1536
    return pl.pallas_call(
        _mod_kernel,
        grid=(depth, n // tn),
        in_specs=[
            pl.BlockSpec((r, d), lambda l, j: (0, 0)),
            pl.BlockSpec((1, d, tn), lambda l, j: (l, 0, j)),
            pl.BlockSpec((1, 1, tn), lambda l, j: (l, 0, j)),
        ],
        out_specs=pl.BlockSpec((1, r, tn), lambda l, j: (l, 0, j)),
        out_shape=jax.ShapeDtypeStruct((depth, r, n), F32),
        compiler_params=_params(2),
        name="modulation",
    )(cond, w_mod.astype(BF16), b_mod.reshape(depth, 1, n))


def _norm_mod(x, g, shift, scale):
    y = x * lax.rsqrt(jnp.mean(x * x, axis=-1, keepdims=True) + EPS)
    return (y * g) * (1.0 + scale) + shift


_NT = (((1,), (1,)), ((), ()))


def _proj_kernel(col_sizes, d, x_ref, mod_ref, g_ref, w_ref, wt_ref, *o_refs):
    m = mod_ref[0, 0]
    h = _norm_mod(x_ref[0], g_ref[...], m[:, 0:d], m[:, d:2 * d]).astype(BF16)
    off = 0
    for o_ref, n in zip(o_refs[:-1], col_sizes):
        o_ref[0] = jnp.dot(h, w_ref[:, off:off + n], preferred_element_type=F32).astype(o_ref.dtype)
        off += n
    o_refs[-1][0] = lax.dot_general(wt_ref[...], h, _NT, preferred_element_type=F32)


def _mod_spec(d, n_lat_tiles):
    return pl.BlockSpec((1, 1, 1, 6 * d), lambda i, j: (i, (j < n_lat_tiles).astype(jnp.int32), 0, 0))


def _project(xc, mods, g, w, col_sizes, w_t, n_lat_tiles):
    b, t, d = xc.shape
    tm = ROW_TILE
    n_tot = sum(col_sizes)
    n_t = w_t.shape[0]
    out_shape = [jax.ShapeDtypeStruct((b, t, n), BF16) for n in col_sizes] + [jax.ShapeDtypeStruct((b, n_t, t), F32)]
    out_specs = [pl.BlockSpec((1, tm, n), lambda i, j: (i, j, 0)) for n in col_sizes]
    out_specs.append(pl.BlockSpec((1, n_t, tm), lambda i, j: (i, 0, j)))
    return pl.pallas_call(
        functools.partial(_proj_kernel, tuple(col_sizes), d),
        grid=(b, t // tm),
        in_specs=[
            pl.BlockSpec((1, tm, d), lambda i, j: (i, j, 0)),
            _mod_spec(d, n_lat_tiles),
            _const_spec((1, d)),
            _const_spec((d, n_tot)),
            _const_spec((n_t, d)),
        ],
        out_specs=out_specs,
        out_shape=out_shape,
        compiler_params=_params(2),
        name="in_proj",
    )(xc, mods, g, w, w_t)


def _gelu_tanh(x):
    return 0.5 * x * (1.0 + jnp.tanh(math.sqrt(2.0 / math.pi) * (x + 0.044715 * (x * x * x))))


def _merge_kernel(d, n_lat_tiles, x_ref, mod_ref, g_ref, yhy_ref, yhyc_ref, ygla_ref, ys5_ref, u_ref, wg_ref, whb_ref,
                  wgb_ref, wsb_ref, wo_ref, s5d_ref, wglu_ref, bglu_ref, o_ref):
    x = x_ref[0]
    m = mod_ref[0, 0]
    h = _norm_mod(x, g_ref[...], m[:, 0:d], m[:, d:2 * d]).astype(BF16)
    y_hy = jnp.where(pl.program_id(1) < n_lat_tiles, yhy_ref[0], yhyc_ref[0])
    y5 = (ys5_ref[0] + s5d_ref[...] * u_ref[0]).T
    g5 = _gelu_tanh(y5)
    y_s5 = g5 * jax.nn.sigmoid(jnp.dot(g5.astype(BF16), wglu_ref[...], preferred_element_type=F32) + bglu_ref[...])

    def branch(k, y, wb_ref):
        gate = jnp.dot(h, wg_ref[:, k * d:(k + 1) * d], preferred_element_type=F32)
        return jax.nn.sigmoid(gate) * jnp.dot(y, wb_ref[...], preferred_element_type=F32)

    mix = branch(0, y_hy, whb_ref) + branch(1, ygla_ref[0], wgb_ref) + branch(2, y_s5.astype(BF16), wsb_ref)
    out = jnp.dot(mix.astype(BF16), wo_ref[...], preferred_element_type=F32)
    o_ref[0] = x + m[:, 2 * d:3 * d] * out


def _merge(xc, mods, g, y_hy, y_hy_ctx, y_gla, ys_s5, u_s5, wts, nj, n_lat_tiles):
    b, t, d = xc.shape
    tm = ROW_TILE
    ch = y_hy.shape[-1]
    row = lambda n: pl.BlockSpec((1, tm, n), lambda i, j: (i, j, 0))
    lat_row = pl.BlockSpec((1, tm, ch), lambda i, j: (i, jnp.minimum(j, n_lat_tiles - 1), 0))
    ctx_row = pl.BlockSpec((1, tm, ch), lambda i, j: (i, 0, 0))
    col = lambda n: pl.BlockSpec((1, n, tm), lambda i, j: (i, 0, j))
    w_gate, w_hy, w_gla, w_s5, w_out, s5_d, w_glu, b_glu = wts
    return pl.pallas_call(
        functools.partial(_merge_kernel, d, n_lat_tiles),
        grid=(b, nj),
        in_specs=[
            row(d),
            _mod_spec(d, n_lat_tiles),
            _const_spec((1, d)),
            lat_row, ctx_row, row(ch), col(ch), col(ch),
            _const_spec(w_gate.shape), _const_spec(w_hy.shape), _const_spec(w_gla.shape), _const_spec(w_s5.shape),
            _const_spec(w_out.shape), _const_spec(s5_d.shape), _const_spec(w_glu.shape), _const_spec(b_glu.shape),
        ],
        out_specs=pl.BlockSpec((1, tm, d), lambda i, j: (i, j, 0)),
        out_shape=jax.ShapeDtypeStruct((b, nj * tm, d), F32),
        compiler_params=_params(2),
        name="merge",
    )(xc, mods, g, y_hy, y_hy_ctx, y_gla, ys_s5, u_s5, w_gate, w_hy, w_gla, w_s5, w_out, s5_d, w_glu, b_glu)


def _ffn_kernel(d, final, x_ref, mod_ref, g_ref, wa_ref, wb_ref, wo_ref, gf_ref, o_ref):
    x = x_ref[0]
    m = mod_ref[0, 0]
    h = _norm_mod(x, g_ref[...], m[:, 3 * d:4 * d], m[:, 4 * d:5 * d]).astype(BF16)
    a = jnp.dot(h, wa_ref[...], preferred_element_type=F32)
    bb = jnp.dot(h, wb_ref[...], preferred_element_type=F32)
    act = (a * jax.nn.sigmoid(a) * bb).astype(BF16)
    y = x + m[:, 5 * d:6 * d] * jnp.dot(act, wo_ref[...], preferred_element_type=F32)
    if final:
        y = y * lax.rsqrt(jnp.mean(y * y, axis=-1, keepdims=True) + EPS) * gf_ref[...]
    o_ref[0] = y


def _ffn(x, mods, g, wa, wb, wo, g_final, n_lat_tiles, final):
    b, t, d = x.shape
    tm = ROW_TILE
    return pl.pallas_call(
        functools.partial(_ffn_kernel, d, final),
        grid=(b, t // tm),
        in_specs=[
            pl.BlockSpec((1, tm, d), lambda i, j: (i, j, 0)),
            _mod_spec(d, n_lat_tiles),
            _const_spec((1, d)),
            _const_spec(wa.shape), _const_spec(wb.shape), _const_spec(wo.shape),
            _const_spec((1, d)),
        ],
        out_specs=pl.BlockSpec((1, tm, d), lambda i, j: (i, j, 0)),
        out_shape=jax.ShapeDtypeStruct((b, t, d), F32),
        compiler_params=_params(2),
        name="ffn",
    )(x, mods, g, wa, wb, wo, g_final)


S5_CHUNK = LANES


def _s5_weights(a_re, a_im, log_dt, b_re, b_im, c_re, c_im, tc):
    g, p = a_re.shape[1:]
    i = b_re.shape[-1]
    lam_c = lax.complex(jnp.minimum(a_re, -1e-4), a_im)
    lam_dt = lam_c * jnp.exp(log_dt)[..., None]
    b_bar = ((jnp.exp(lam_dt) - 1.0) / lam_c)[..., None] * lax.complex(b_re, b_im)[None]
    c_mat = lax.complex(c_re, c_im)
    tau = jnp.arange(tc, dtype=F32)
    pw = jnp.exp(lam_dt[:, :, None, :] * tau[None, None, :, None])
    pw1 = pw * jnp.exp(lam_dt)[:, :, None, :]
    kern = jnp.einsum('gip,dgtp,dgpj->dgtij', c_mat, pw, b_bar).real
    k_lag = jnp.concatenate([kern[1][:, :0:-1], kern[0][:, :1] + kern[1][:, :1], kern[0][:, 1:]], axis=1)
    kr = jnp.pad(k_lag.transpose(0, 3, 2, 1), ((0, 0), (0, 0), (0, 0), (0, 1)))
    kr = kr.reshape(g, i, i * 2 * tc)

    inc_f = pw[0][:, None, ::-1, :] * b_bar[0].transpose(0, 2, 1)[:, :, None]
    inc_b = pw[1][:, None, :, :] * b_bar[1].transpose(0, 2, 1)[:, :, None]
    bm = jnp.concatenate([inc_f.real, inc_b.real, inc_f.imag, inc_b.imag], axis=-1).reshape(g, tc * i, 4 * p)

    out_f = c_mat.transpose(0, 2, 1)[..., None] * pw1[0].transpose(0, 2, 1)[:, :, None, :]
    out_b = c_mat.transpose(0, 2, 1)[..., None] * pw1[1][:, ::-1].transpose(0, 2, 1)[:, :, None, :]
    z = jnp.zeros((g, p, tc * i), F32)
    fl = lambda t: t.reshape(g, p, tc * i)
    cm = jnp.concatenate([fl(out_f.real), z, fl(-out_f.imag), z, z, fl(out_b.real), z, fl(-out_b.imag)], axis=1)

    lam_t = jnp.exp(lam_dt * float(tc))
    lam = jnp.stack([jnp.concatenate([lam_t[0].real, lam_t[1].real], -1),
                     jnp.concatenate([lam_t[0].imag, lam_t[1].imag], -1)], axis=1)
    return kr, bm.astype(BF16), cm.astype(BF16), lam


def _s5_kernel(nc, nc_ctx, tc, u_ref, kr_ref, bm_ref, cm_ref, lam_ref, y_ref, up_ref, dx_ref, p_ref):
    bn, i_sz = u_ref.shape[:2]
    nl = nc - nc_ctx
    width = kr_ref.shape[-1]
    seg = width // i_sz

    def operator_rows(j):
        lag_rows = pltpu.roll(jnp.broadcast_to(kr_ref[0, j:j + 1, :], (tc, width)), width - (tc - 1), axis=1,
                              stride=1, stride_axis=0)
        return jnp.concatenate([lag_rows[:, i * seg:i * seg + tc] for i in range(i_sz)], axis=1).astype(BF16)

    for n in range(nc):
        for i in range(i_sz):
            up_ref[n * bn:(n + 1) * bn, i * tc:(i + 1) * tc] = u_ref[:, i, n * tc:(n + 1) * tc]
    u = up_ref[...].astype(BF16)
    dx_ref[...] = jnp.dot(u, bm_ref[0], preferred_element_type=F32)
    lam = lam_ref[0]
    lr, li = lam[0:1], lam[1:2]
    w = lam.shape[-1]
    half = w // 2
    is_f = lax.broadcasted_iota(jnp.int32, (bn, w), 1) < half

    def step(s, carry):
        sr, si = carry
        nf = jnp.where(s < nc_ctx, nl + s, s - nc_ctx)
        nb = nc - 1 - s
        rf = pl.multiple_of(nf * bn, bn)
        rb = pl.multiple_of(nb * bn, bn)
        p_ref[pl.ds(rf, bn), 0:w] = sr
        p_ref[pl.ds(rf, bn), w:2 * w] = si
        p_ref[pl.ds(rb, bn), 2 * w:3 * w] = sr
        p_ref[pl.ds(rb, bn), 3 * w:4 * w] = si
        d_re = jnp.where(is_f, dx_ref[pl.ds(rf, bn), 0:w], dx_ref[pl.ds(rb, bn), 0:w])
        d_im = jnp.where(is_f, dx_ref[pl.ds(rf, bn), w:2 * w], dx_ref[pl.ds(rb, bn), w:2 * w])
        return lr * sr - li * si + d_re, lr * si + li * sr + d_im

    zero = jnp.zeros((bn, 2 * half), F32)
    lax.fori_loop(0, nc, step, (zero, zero))
    y = jnp.dot(p_ref[...].astype(BF16), cm_ref[0], preferred_element_type=F32)
    for j in range(0, i_sz, 2):
        m_j = jnp.concatenate([operator_rows(j), operator_rows(j + 1)], axis=0)
        y = y + jnp.dot(up_ref[:, j * tc:(j + 2) * tc].astype(BF16), m_j, preferred_element_type=F32)
    for n in range(nc):
        for i in range(i_sz):
            y_ref[:, i, n * tc:(n + 1) * tc] = y[n * bn:(n + 1) * bn, i * tc:(i + 1) * tc]


def _s5_mix(u_t, n_ctx, wts, layer):
    kr, bm, cm, lam = wts
    bn, ch, t = u_t.shape
    g, i = kr.shape[1:3]
    tc = kr.shape[3] // (2 * i)
    assert 2 * S5_STATE == LANES and bn % SUBLANES == 0 and n_ctx % tc == 0 and t % tc == 0 and tc % LANES == 0
    nc, nc_ctx = t // tc, n_ctx // tc
    k = tc * i
    r = nc * bn
    return pl.pallas_call(
        functools.partial(_s5_kernel, nc, nc_ctx, tc),
        grid=(g,),
        in_specs=[
            pl.BlockSpec((bn, i, t), lambda j: (0, j, 0)),
            pl.BlockSpec((None, 1, i, 2 * k), lambda j: (layer, j, 0, 0)),
            pl.BlockSpec((None, 1, k, 4 * S5_STATE), lambda j: (layer, j, 0, 0)),
            pl.BlockSpec((None, 1, 8 * S5_STATE, k), lambda j: (layer, j, 0, 0)),
            pl.BlockSpec((None, 1, 2, 2 * S5_STATE), lambda j: (layer, j, 0, 0)),
        ],
        out_specs=pl.BlockSpec((bn, i, t), lambda j: (0, j, 0)),
        out_shape=jax.ShapeDtypeStruct((bn, ch, t), F32),
        scratch_shapes=[pltpu.VMEM((r, k), F32), pltpu.VMEM((r, 4 * S5_STATE), F32),
                        pltpu.VMEM((r, 8 * S5_STATE), F32)],
        compiler_params=_params(1),
        name="s5_mix",
    )(u_t, kr, bm, cm, lam)


def _gla_kernel(nc, nc_ctx, k_ref, v_ref, a_ref, q_ref, r_ref, wa_ref, ba_ref, gn_ref, y_ref, of_ref, ob_ref,
                s_ref):
    c, h_n, dk, dv = GLA_CHUNK, GLA_HEADS, GLA_DK, GLA_DV
    qk = h_n * dk
    q_scale = dk ** -0.5
    nbk = GLA_BLOCK
    rb = nbk * c
    row_i = lax.broadcasted_iota(jnp.int32, (rb, rb), 0)
    col_i = lax.broadcasted_iota(jnp.int32, (rb, rb), 1)
    same = (row_i // c) == (col_i // c)
    causal = (same & (row_i >= col_i), same & (row_i <= col_i))
    cum_ops = [m.astype(BF16) for m in causal]
    att_mask = [jnp.concatenate([m] * h_n, axis=0) for m in causal]
    lane_head = lax.broadcasted_iota(jnp.int32, (1, qk), 1) // dk
    head_lanes = [lane_head == h for h in range(h_n)]

    def by_head(x):
        zero = jnp.zeros_like(x)
        return jnp.concatenate([jnp.where(m, x, zero) for m in head_lanes], axis=0)

    def block(j, d):
        rows = pl.ds(pl.multiple_of(j * rb, rb), rb)
        k = k_ref[0, rows, :].astype(F32)
        q = q_ref[0, rows, :].astype(F32) * q_scale
        v = v_ref[0, rows, :]
        z = jnp.dot(a_ref[0, rows, :], wa_ref[:, d * qk:(d + 1) * qk], preferred_element_type=F32) + ba_ref[d:d + 1, :]
        log_a = (jnp.minimum(z, 0.0) - jnp.log(1.0 + jnp.exp(-jnp.abs(z)))) * (1.0 / GLA_TAU)
        hi = log_a.astype(BF16)
        lo = (log_a - hi.astype(F32)).astype(BF16)
        cs = jnp.dot(cum_ops[d], jnp.concatenate([hi, lo], axis=1), preferred_element_type=F32)
        b = cs[:, :qk] + cs[:, qk:]
        b_tot = jnp.concatenate(
            [jnp.broadcast_to(b[n * c + (c - 1 if d == 0 else 0)][None], (c, qk)) for n in range(nbk)], axis=0)
        qd = (q * jnp.exp(b)).astype(BF16)
        kd = (k * jnp.exp(-b)).astype(BF16)
        kl = (k * jnp.exp(b_tot - b)).astype(BF16)
        qm = by_head(qd)
        intra = []
        for h in range(h_n):
            att = lax.dot_general(qm[h * rb:(h + 1) * rb], kd, _NT, preferred_element_type=F32)
            att = jnp.where(causal[d], att, 0.0).astype(BF16)
            intra.append(jnp.dot(att, v[:, h * dv:(h + 1) * dv], preferred_element_type=F32))
        s_t = s_ref[d]
        inter = [None] * nbk
        for n in (range(nbk) if d == 0 else reversed(range(nbk))):
            r0 = n * c
            qm_n = jnp.concatenate([qm[h * rb + r0:h * rb + r0 + c] for h in range(h_n)], axis=0)
            inter[n] = lax.dot_general(qm_n, s_t.astype(BF16), _NT, preferred_element_type=F32)
            v_t = jnp.concatenate(
                [jnp.concatenate([v[r0:r0 + c, h * dv:(h + 1) * dv] for h in range(p, p + dv // c)], axis=0).T
                 for p in range(0, h_n, dv // c)], axis=1)
            s_t = s_t * jnp.exp(b_tot[r0:r0 + 1]) + jnp.dot(v_t, by_head(kl[r0:r0 + c]), preferred_element_type=F32)
        s_ref[d] = s_t
        o = jnp.concatenate(
            [intra[h] + jnp.concatenate([inter[n][h * c:(h + 1) * c] for n in range(nbk)], axis=0)
             for h in range(h_n)], axis=1)
        return rows, o

    def readout(rows, o):
        r = r_ref[0, rows, :].astype(F32)
        gate = r * jax.nn.sigmoid(r)
        outs = []
        for h in range(h_n):
            oh = o[:, h * dv:(h + 1) * dv]
            oh = oh * lax.rsqrt(jnp.mean(oh * oh, axis=-1, keepdims=True) + EPS) * gn_ref[...]
            outs.append(oh * gate[:, h * dv:(h + 1) * dv])
        y_ref[0, rows, :] = jnp.concatenate(outs, axis=1).astype(y_ref.dtype)

    def scan_step(i, carry, lo_block, hi_block, phase):
        rows_f, o_f = block(lo_block + i, 0)
        if phase == "park":
            of_ref[rows_f, :] = o_f
        rows_b, o_b = block(hi_block - 1 - i, 1)
        if phase == "park":
            ob_ref[rows_b, :] = o_b
        elif phase == "meet":
            readout(rows_f, o_f + o_b)
        else:
            readout(rows_f, o_f + ob_ref[rows_f, :])
            readout(rows_b, o_b + of_ref[rows_b, :])
        return carry

    def scan(lo_block, n_blocks):
        step = functools.partial(scan_step, lo_block=lo_block, hi_block=lo_block + n_blocks)
        half = n_blocks // 2
        lax.fori_loop(0, half, functools.partial(step, phase="park"), 0, unroll=GLA_UNROLL)
        if n_blocks % 2:
            step(jnp.int32(half), 0, phase="meet")
        lax.fori_loop(n_blocks - half, n_blocks, functools.partial(step, phase="finish"), 0, unroll=GLA_UNROLL)

    nl, n_ctx = (nc - nc_ctx) // nbk, nc_ctx // nbk
    s_ref[...] = jnp.zeros(s_ref.shape, F32)
    scan(nl, n_ctx)
    scan(0, nl)


def _gla_mix(k, v, a, q, r, wa, ba, gnorm, n_ctx):
    bn, t, qk = k.shape
    vd = v.shape[-1]
    c = GLA_CHUNK
    assert t % (c * GLA_BLOCK) == 0 and n_ctx % (c * GLA_BLOCK) == 0 and GLA_DV % c == 0
    seq = lambda n, bufs=2: pl.BlockSpec((1, t, n), lambda i: (i, 0, 0), pipeline_mode=pl.Buffered(bufs))
    return pl.pallas_call(
        functools.partial(_gla_kernel, t // c, n_ctx // c),
        grid=(bn,),
        in_specs=[seq(qk), seq(vd), seq(a.shape[-1]), seq(qk), seq(vd, 1),
                  _const_spec(wa.shape), _const_spec(ba.shape), _const_spec(gnorm.shape)],
        out_specs=pl.BlockSpec((1, t, vd), lambda i: (i, 0, 0)),
        out_shape=jax.ShapeDtypeStruct((bn, t, vd), BF16),
        scratch_shapes=[pltpu.VMEM((t, vd), F32), pltpu.VMEM((t, vd), F32), pltpu.VMEM((2, GLA_DV, qk), F32)],
        compiler_params=_params(1),
        name="gla_mix",
    )(k, v, a, q, r, wa, ba, gnorm)


HY_NB = LANES
HY_XPITCH = HY_NB + SUBLANES
HY_APITCH = 2 * HY_NB + SUBLANES
HY_CT = LANES
HY_UNROLL_OUTER = 32
HY_UNROLL_INNER = 16
HY_UNROLL_ROWS = 4


def _hyena_filters(n, pos, params):
    w1, b1, w2, b2, w3, freq1, freq2, decay = (p.astype(F32) for p in params)
    t = (pos / (n - 1))[:, None]
    bands = jnp.linspace(1e-4, HY_POS_BANDS - 1, HY_POS_BANDS, dtype=F32)
    ang = (2.0 * math.pi / n) * pos[:, None] * bands[None]
    z = jnp.concatenate([t, jnp.cos(ang), -jnp.sin(ang)], axis=-1)
    hid = jnp.sin(freq1 * (z @ w1 + b1))
    hid = jnp.sin(freq2 * (hid @ w2 + b2))
    w3 = w3.reshape(w3.shape[0], HY_N_FILT, -1)
    window = jnp.exp(-t[None] * jnp.abs(decay).reshape(HY_N_FILT, 1, -1))
    return jnp.einsum('nk,kfc->fnc', hid, w3) * window


def _hyena_kernels(n, params):
    slot = jnp.arange(2 * n)
    h = _hyena_filters(n, jnp.where(slot < n, slot, 2 * n - slot).astype(F32), params)
    is_fwd, is_bwd = (slot < n)[None, :, None], (slot > n)[None, :, None]
    k = jnp.where(is_fwd, h[0::2], 0.0) + jnp.where(is_bwd, h[1::2], 0.0)
    lag0_bwd = _hyena_filters(n, jnp.zeros((1,), F32), params)[1::2]
    k = k + jnp.where(slot[None, :, None] == 0, lag0_bwd, 0.0)
    return k * lax.rsqrt(jnp.sum(k * k, axis=1, keepdims=True) + EPS)


def _spectrum_kernel(na, k_ref, f1_ref, f2_ref, o_ref, a2):
    nb, ap = HY_NB, HY_APITCH

    def stage1(b, c):
        xs = k_ref[0, pl.ds(b, na, stride=nb), :].astype(BF16)
        r = jnp.dot(f1_ref[b], xs, preferred_element_type=F32)
        a2[pl.ds(b, na, stride=ap), :] = r[:na]
        a2[pl.ds(nb + b, na, stride=ap), :] = r[na:]
        return c

    def stage2(ka, c):
        rows = pl.ds(pl.multiple_of(ka * ap, 8), 2 * nb)
        xf = jnp.dot(f2_ref[...], a2[rows, :].astype(BF16), preferred_element_type=F32)
        o_ref[0, ka, 0] = xf[:nb].astype(o_ref.dtype)
        o_ref[0, ka, 1] = xf[nb:].astype(o_ref.dtype)
        return c

    lax.fori_loop(0, nb, stage1, 0, unroll=HY_UNROLL_INNER)
    lax.fori_loop(0, na, stage2, 0, unroll=HY_UNROLL_INNER)


def _hyena_spectra(kernels):
    nf, n, ch = kernels.shape
    nb, ct = HY_NB, HY_CT
    na = n // nb
    _, f2, _, _, f1_full = _dft_tables(n // 2)
    return pl.pallas_call(
        functools.partial(_spectrum_kernel, na),
        grid=(nf, ch // ct),
        in_specs=[pl.BlockSpec((1, n, ct), lambda f, j: (f, 0, j)), _const_spec(f1_full.shape), _const_spec(f2.shape)],
        out_specs=pl.BlockSpec((1, na, 2, nb, ct), lambda f, j: (f, 0, 0, 0, j)),
        out_shape=jax.ShapeDtypeStruct((nf, na, 2, nb, ch), BF16),
        scratch_shapes=[pltpu.VMEM((na * HY_APITCH, ct), F32)],
        compiler_params=_params(2),
        name="hyena_spectrum",
    )(kernels, f1_full, f2)


def _cplx_block(m):
    return np.block([[m.real, -m.imag], [m.imag, m.real]])


@functools.lru_cache(maxsize=None)
def _dft_tables(n_seq):
    nb = HY_NB
    n = 2 * n_seq
    na = n // nb
    ha = na // 2
    ka = np.arange(na)[:, None]
    b = np.arange(nb)
    tw = np.exp(-2j * np.pi * ka * b[None, :] / n)
    w_a = np.exp(-2j * np.pi * ka * np.arange(ha)[None, :] / na)
    f1 = np.stack([_cplx_block(tw[:, i:i + 1] * w_a) for i in range(nb)])
    g = np.stack([_cplx_block((np.conj(tw[:, i:i + 1] * w_a)).T / n) for i in range(nb)])
    w_b = np.exp(-2j * np.pi * b[:, None] * b[None, :] / nb)
    f2 = _cplx_block(w_b)
    f2i = _cplx_block(np.conj(w_b))
    w_full = np.exp(-2j * np.pi * ka * np.arange(na)[None, :] / na)
    f1_full = np.stack([np.concatenate([(tw[:, i:i + 1] * w_full).real, (tw[:, i:i + 1] * w_full).imag], axis=0)
                        for i in range(nb)])
    return tuple(jnp.asarray(t, dtype=BF16) for t in (f1, f2, f2i, g, f1_full))


def _hyena_kernel(na, x1_ref, x2_ref, v_ref, cw_ref, bias_ref, kf1_ref, kf2_ref, f1_ref, f2_ref, f2i_ref, g_ref,
                  o_ref, rawf, vp, gp, yp, a2):
    nb, xp, ap = HY_NB, HY_XPITCH, HY_APITCH
    ha = na // 2
    n_seq = ha * nb
    ct = o_ref.shape[-1]
    halo = SUBLANES
    rawf[:, 0:halo, :] = jnp.zeros((2, halo, ct), F32)
    rawf[:, halo + n_seq:2 * halo + n_seq, :] = jnp.zeros((2, halo, ct), F32)

    def short_conv_into(raw_ref, which, dst):
        w0, w1, w2 = (cw_ref[j, which:which + 1, :] for j in range(HY_SHORT))
        for e in range(2):
            rawf[e, halo:halo + n_seq, :] = raw_ref[e].astype(F32)

            def body(a, c):
                base = pl.multiple_of(a * nb, nb) + halo
                u = (w0 * rawf[e, pl.ds(base - 1, nb), :] + w1 * rawf[e, pl.ds(base, nb), :]
                     + w2 * rawf[e, pl.ds(base + 1, nb), :])
                dst[e, pl.ds(pl.multiple_of(a * xp, 8), nb), :] = u
                return c

            lax.fori_loop(0, ha, body, 0, unroll=HY_UNROLL_ROWS)

    def long_conv(kf_ref):
        def stage1(b, c):
            xs = jnp.concatenate([vp[0, pl.ds(b, ha, stride=xp), :], vp[1, pl.ds(b, ha, stride=xp), :]], axis=0)
            r = jnp.dot(f1_ref[b], xs.astype(BF16), preferred_element_type=F32)
            a2[pl.ds(b, na, stride=ap), :] = r[:na]
            a2[pl.ds(nb + b, na, stride=ap), :] = r[na:]
            return c

        def stage2(kp, c):
            rows = [pl.ds(pl.multiple_of((2 * kp + e) * ap, 8), 2 * nb) for e in range(2)]
            slab = jnp.concatenate([a2[r, :] for r in rows], axis=1).astype(BF16)
            xf = jnp.dot(f2_ref[...], slab, preferred_element_type=F32)
            xr, xi = xf[:nb], xf[nb:]
            kr = jnp.concatenate([kf_ref[0, 2 * kp + e, 0] for e in range(2)], axis=1).astype(F32)
            ki = jnp.concatenate([kf_ref[0, 2 * kp + e, 1] for e in range(2)], axis=1).astype(F32)
            z = jnp.concatenate([xr * kr - xi * ki, xr * ki + xi * kr], axis=0).astype(BF16)
            back = jnp.dot(f2i_ref[...], z, preferred_element_type=F32)
            for e in range(2):
                a2[rows[e], :] = back[:, e * ct:(e + 1) * ct]
            return c

        def stage3(b, c):
            s = jnp.concatenate([a2[pl.ds(b, na, stride=ap), :], a2[pl.ds(nb + b, na, stride=ap), :]], axis=0)
            y = jnp.dot(g_ref[b], s.astype(BF16), preferred_element_type=F32)
            yp[0, pl.ds(b, ha, stride=xp), :] = y[:ha]
            yp[1, pl.ds(b, ha, stride=xp), :] = y[ha:]
            return c

        lax.fori_loop(0, nb, stage1, 0, unroll=HY_UNROLL_OUTER)
        lax.fori_loop(0, na // 2, stage2, 0, unroll=HY_UNROLL_INNER)
        lax.fori_loop(0, nb, stage3, 0, unroll=HY_UNROLL_OUTER)

    def gated(bias_row, write):
        for e in range(2):
            def body(a, c):
                rows = pl.ds(pl.multiple_of(a * xp, 8), nb)
                write(e, a, rows, gp[e, rows, :] * (yp[e, rows, :] + bias_row * vp[e, rows, :]))
                return c

            lax.fori_loop(0, ha, body, 0, unroll=HY_UNROLL_ROWS)

    def write_z(e, a, rows, val):
        vp[e, rows, :] = val

    def write_out(e, a, rows, val):
        o_ref[e, pl.ds(pl.multiple_of(a * nb, nb), nb), :] = val.astype(o_ref.dtype)

    short_conv_into(v_ref, 2, vp)
    short_conv_into(x1_ref, 0, gp)
    long_conv(kf1_ref)
    gated(bias_ref[0:1, :], write_z)
    short_conv_into(x2_ref, 1, gp)
    long_conv(kf2_ref)
    gated(bias_ref[1:2, :], write_out)


def _hyena_latent(hy_all, n_seq, conv_w, bias, kf):
    bn = hy_all.shape[0]
    ch = hy_all.shape[-1] // 3
    ct = HY_CT
    nct = ch // ct
    nb = HY_NB
    na = 2 * n_seq // nb
    assert bn % 2 == 0 and ch % ct == 0 and n_seq % (8 * nb) == 0
    f1, f2, f2i, g, _ = _dft_tables(n_seq)
    col = lambda which: pl.BlockSpec((2, n_seq, ct), lambda j, p: (p, 0, which * nct + j))
    kf_spec = lambda f: pl.BlockSpec((1, na, 2, nb, ct), lambda j, p: (f, 0, 0, 0, j), pipeline_mode=pl.Buffered(1))
    pad_rows = (na // 2) * HY_XPITCH
    return pl.pallas_call(
        functools.partial(_hyena_kernel, na),
        grid=(nct, bn // 2),
        in_specs=[col(0), col(1), col(2),
                  pl.BlockSpec((HY_SHORT, 3, ct), lambda j, p: (0, 0, j)),
                  pl.BlockSpec((2, ct), lambda j, p: (0, j)),
                  kf_spec(0), kf_spec(1),
                  _const_spec(f1.shape), _const_spec(f2.shape), _const_spec(f2i.shape), _const_spec(g.shape)],
        out_specs=pl.BlockSpec((2, n_seq, ct), lambda j, p: (p, 0, j)),
        out_shape=jax.ShapeDtypeStruct((bn, n_seq, ch), BF16),
        scratch_shapes=[pltpu.VMEM((2, n_seq + 2 * SUBLANES, ct), F32), pltpu.VMEM((2, pad_rows, ct), F32),
                        pltpu.VMEM((2, pad_rows, ct), F32), pltpu.VMEM((2, pad_rows, ct), F32),
                        pltpu.VMEM((na * HY_APITCH, ct), F32)],
        compiler_params=_params(2),
        name="hyena_latent",
    )(hy_all, hy_all, hy_all, conv_w.reshape(HY_SHORT, 3, ch), bias, kf, kf, f1, f2, f2i, g)


@functools.lru_cache(maxsize=None)
def _dense_dft_tables(n_seq):
    n = 2 * n_seq
    ang = 2.0 * np.pi * np.arange(n)[:, None] * np.arange(n_seq)[None, :] / n
    fd = np.concatenate([np.cos(ang), -np.sin(ang)], axis=0)
    gd = np.concatenate([np.cos(ang).T, -np.sin(ang).T], axis=1) / n
    return jnp.asarray(fd, dtype=BF16), jnp.asarray(gd, dtype=BF16)


def _hyena_ctx_kernel(n_seq, x1_ref, x2_ref, v_ref, cw_ref, bias_ref, kf1_ref, kf2_ref, fd_ref, gd_ref, o_ref):
    n = 2 * n_seq

    def short_conv(raw_ref, which):
        x = raw_ref[0].astype(F32)
        zero = jnp.zeros((1, x.shape[-1]), F32)
        prev = jnp.concatenate([zero, x[:-1]], axis=0)
        nxt = jnp.concatenate([x[1:], zero], axis=0)
        return cw_ref[0, which:which + 1, :] * prev + cw_ref[1, which:which + 1, :] * x + cw_ref[2, which:which + 1, :] * nxt

    def long_conv(u, kf_ref, bias_row):
        xf = jnp.dot(fd_ref[...], u.astype(BF16), preferred_element_type=F32)
        xr, xi = xf[:n], xf[n:]
        kr, ki = kf_ref[0], kf_ref[1]
        z = jnp.concatenate([xr * kr - xi * ki, xr * ki + xi * kr], axis=0).astype(BF16)
        return jnp.dot(gd_ref[...], z, preferred_element_type=F32) + bias_row * u

    x1, x2, v = short_conv(x1_ref, 0), short_conv(x2_ref, 1), short_conv(v_ref, 2)
    z = x1 * long_conv(v, kf1_ref, bias_ref[0:1, :])
    o_ref[0] = (x2 * long_conv(z, kf2_ref, bias_ref[1:2, :])).astype(o_ref.dtype)


def _hyena_context(hy_all, row_block, n_seq, conv_w, bias, kf1, kf2):
    bn = hy_all.shape[0]
    ch = hy_all.shape[-1] // 3
    fd, gd = _dense_dft_tables(n_seq)
    col = lambda which: pl.BlockSpec((1, n_seq, ch), lambda i: (i, row_block, which))
    split = lambda kf: jnp.stack([kf.real, kf.imag], axis=0)
    return pl.pallas_call(
        functools.partial(_hyena_ctx_kernel, n_seq),
        grid=(bn,),
        in_specs=[col(0), col(1), col(2), _const_spec((HY_SHORT, 3, ch)), _const_spec((2, ch)),
                  _const_spec((2, 2 * n_seq, ch)), _const_spec((2, 2 * n_seq, ch)),
                  _const_spec(fd.shape), _const_spec(gd.shape)],
        out_specs=pl.BlockSpec((1, n_seq, ch), lambda i: (i, 0, 0)),
        out_shape=jax.ShapeDtypeStruct((bn, n_seq, ch), BF16),
        compiler_params=_params(1),
        name="hyena_context",
    )(hy_all, hy_all, hy_all, conv_w.reshape(HY_SHORT, 3, ch), bias, split(kf1), split(kf2), fd, gd)


def kernel(x, c, ctx, c_ctx, w_mod, b_mod, g_norm1, g_norm2, w_in, hy_conv, hy_w1, hy_b1, hy_w2, hy_b2, hy_w3,
           hy_freq1, hy_freq2, hy_decay, hy_bias, gla_wa2, gla_ba, gla_gnorm, s5_a_re, s5_a_im, s5_log_dt, s5_b_re,
           s5_b_im, s5_c_re, s5_c_im, s5_d, s5_w_glu, s5_b_glu, w_br_hy, w_br_gla, w_br_s5, w_out, w_ffn_in,
           w_ffn_out, g_final):
    bn, n_lat, d = x.shape
    n_ctx = ctx.shape[1]
    depth = w_in.shape[0]
    d_hy = w_br_hy.shape[1]
    s5_ch = w_br_s5.shape[1]
    d_ff = w_ffn_out.shape[1]
    assert n_ctx == ROW_TILE and n_lat % ROW_TILE == 0
    n_lat_tiles = n_lat // ROW_TILE
    n_low = 2 * GLA_LOWRANK
    o_a = GLA_QK + GLA_V
    o_u = o_a + n_low
    o_q = o_u + s5_ch
    o_gate = o_q + GLA_QK + GLA_V + 3 * d_hy
    col_sizes = (GLA_QK, GLA_V, A_PAD, GLA_QK, GLA_V, 3 * d_hy)

    cond = jnp.concatenate([c_ctx[None], c], axis=0)
    cond = jnp.pad(cond, ((0, (-cond.shape[0]) % SUBLANES), (0, 0)))
    mod_all = _modulation_all(cond, w_mod, b_mod)

    s5w_all = jax.vmap(functools.partial(_s5_weights, tc=S5_CHUNK))(
        s5_a_re, s5_a_im, s5_log_dt, s5_b_re, s5_b_im, s5_c_re, s5_c_im)
    wa_all = jnp.zeros((depth, A_PAD, 2 * GLA_QK), F32)
    wa_all = wa_all.at[:, :GLA_LOWRANK, :GLA_QK].set(gla_wa2[:, 0]).at[:, GLA_LOWRANK:n_low, GLA_QK:].set(gla_wa2[:, 1])
    wa_all = wa_all.astype(BF16)

    xc = jnp.concatenate([x, ctx], axis=1)
    for l in range(depth):
        last = l == depth - 1
        mods = jnp.stack([jnp.broadcast_to(mod_all[l, 0], (bn, 6 * d)), mod_all[l, 1:1 + bn]], axis=1)
        mods = mods[:, :, None, :]
        wl = w_in[l]
        w_proj = jnp.concatenate(
            [wl[:, :o_u], jnp.zeros((d, A_PAD - n_low), F32), wl[:, o_q:o_gate]], axis=1).astype(BF16)
        g1 = g_norm1[l][None]
        w_u_t = wl[:, o_u:o_q].T.astype(BF16)
        k_a, v_a, a_a, q_a, r_a, hy_a, u_t = _project(xc, mods, g1, w_proj, col_sizes, w_u_t, n_lat_tiles)

        y_gla = _gla_mix(k_a, v_a, a_a, q_a, r_a, wa_all[l], gla_ba[l], gla_gnorm[l][None], n_ctx)

        ys = _s5_mix(u_t, n_ctx, s5w_all, l)

        hy_p = (hy_w1[l], hy_b1[l], hy_w2[l], hy_b2[l], hy_w3[l], hy_freq1[l], hy_freq2[l], hy_decay[l])
        y_hy = _hyena_latent(hy_a, n_lat, hy_conv[l], hy_bias[l], _hyena_spectra(_hyena_kernels(n_lat, hy_p)))
        if last:
            y_hy_ctx = y_hy
        else:
            kf_ctx = jnp.fft.fft(_hyena_kernels(n_ctx, hy_p), axis=1)
            y_hy_ctx = _hyena_context(hy_a, n_lat // n_ctx, n_ctx, hy_conv[l], hy_bias[l], kf_ctx[0], kf_ctx[1])

        wts = (wl[:, o_gate:].astype(BF16), w_br_hy[l].astype(BF16), w_br_gla[l].astype(BF16),
               w_br_s5[l].astype(BF16), w_out[l].astype(BF16), s5_d[l][:, None], s5_w_glu[l].astype(BF16),
               s5_b_glu[l][None])
        n_tiles = n_lat_tiles if last else n_lat_tiles + n_ctx // ROW_TILE
        x_mid = _merge(xc, mods, g1, y_hy, y_hy_ctx, y_gla, ys, u_t, wts, n_tiles, n_lat_tiles)
        wf = w_ffn_in[l]
        xc = _ffn(x_mid, mods, g_norm2[l][None], wf[:, :d_ff].astype(BF16), wf[:, d_ff:].astype(BF16),
                  w_ffn_out[l].astype(BF16), g_final[None], n_lat_tiles, last)
    return xc
```

```python
import functools
import math

import jax
import jax.numpy as jnp
import numpy as np
from jax import lax
from jax.experimental import pallas as pl
from jax.experimental.pallas import tpu as pltpu

F32 = jnp.float32
BF16 = jnp.bfloat16
EPS = 1e-6

HY_SHORT = 3
HY_POS_BANDS = 16
HY_N_FILT = 4
GLA_HEADS = 4
GLA_DK = 64
GLA_DV = 128
GLA_QK = GLA_HEADS * GLA_DK
GLA_V = GLA_HEADS * GLA_DV
GLA_LOWRANK = 16
GLA_TAU = 16.0
GLA_CHUNK = 64
S5_STATE = 64

LANES = 128
SUBLANES = 8
VMEM_LIMIT = 56 * 1024 * 1024

GLA_BLOCK = 4
GLA_UNROLL = 2
A_PAD = LANES
ROW_TILE = 256


def _const_spec(shape):
    nd = len(shape)
    return pl.BlockSpec(shape, lambda *_: (0,) * nd, pipeline_mode=pl.Buffered(1))


def _params(n_axes):
    return pltpu.CompilerParams(dimension_semantics=("parallel",) * n_axes, vmem_limit_bytes=VMEM_LIMIT)


def _mod_kernel(c_ref, w_ref, b_ref, o_ref):
    c = c_ref[...]
    s = c * jax.nn.sigmoid(c)
    o_ref[0] = jnp.dot(s.astype(BF16), w_ref[0], preferred_element_type=F32) + b_ref[0]


def _modulation_all(cond, w_mod, b_mod):
    depth, d, n = w_mod.shape
    r = cond.shape[0]
    tn = 1536
    return pl.pallas_call(
        _mod_kernel,
        grid=(depth, n // tn),
        in_specs=[
            pl.BlockSpec((r, d), lambda l, j: (0, 0)),
            pl.BlockSpec((1, d, tn), lambda l, j: (l, 0, j)),
            pl.BlockSpec((1, 1, tn), lambda l, j: (l, 0, j)),
        ],
        out_specs=pl.BlockSpec((1, r, tn), lambda l, j: (l, 0, j)),
        out_shape=jax.ShapeDtypeStruct((depth, r, n), F32),
        compiler_params=_params(2),
        name="modulation",
    )(cond, w_mod.astype(BF16), b_mod.reshape(depth, 1, n))


def _norm_mod(x, g, shift, scale):
    y = x * lax.rsqrt(jnp.mean(x * x, axis=-1, keepdims=True) + EPS)
    return (y * g) * (1.0 + scale) + shift


_NT = (((1,), (1,)), ((), ()))


def _row_specs(xs, n_lat_tiles):
    d = xs[0].shape[-1]
    return (pl.BlockSpec((1, ROW_TILE, d), lambda i, j: (i, jnp.minimum(j, n_lat_tiles - 1), 0)),
            pl.BlockSpec((1, ROW_TILE, d), lambda i, j: (i, xs[2], 0)))


def _proj_kernel(col_sizes, d, n_lat_tiles, x_ref, xc_ref, mod_ref, g_ref, w_ref, wt_ref, *o_refs):
    m = mod_ref[0, 0]
    x = jnp.where(pl.program_id(1) < n_lat_tiles, x_ref[0], xc_ref[0])
    h = _norm_mod(x, g_ref[...], m[:, 0:d], m[:, d:2 * d]).astype(BF16)
    off = 0
    for o_ref, n in zip(o_refs[:-1], col_sizes):
        o_ref[0] = jnp.dot(h, w_ref[:, off:off + n], preferred_element_type=F32).astype(o_ref.dtype)
        off += n
    o_refs[-1][0] = lax.dot_general(wt_ref[...], h, _NT, preferred_element_type=F32)


def _mod_spec(d, n_lat_tiles):
    return pl.BlockSpec((1, 1, 1, 6 * d), lambda i, j: (i, (j < n_lat_tiles).astype(jnp.int32), 0, 0))


def _project(xs, mods, g, w, col_sizes, w_t, n_lat_tiles):
    b, _, d = xs[0].shape
    tm = ROW_TILE
    t = (n_lat_tiles + 1) * tm
    n_tot = sum(col_sizes)
    n_t = w_t.shape[0]
    out_shape = [jax.ShapeDtypeStruct((b, t, n), BF16) for n in col_sizes] + [jax.ShapeDtypeStruct((b, n_t, t), F32)]
    out_specs = [pl.BlockSpec((1, tm, n), lambda i, j: (i, j, 0)) for n in col_sizes]
    out_specs.append(pl.BlockSpec((1, n_t, tm), lambda i, j: (i, 0, j)))
    return pl.pallas_call(
        functools.partial(_proj_kernel, tuple(col_sizes), d, n_lat_tiles),
        grid=(b, t // tm),
        in_specs=[
            *_row_specs(xs, n_lat_tiles),
            _mod_spec(d, n_lat_tiles),
            _const_spec((1, d)),
            _const_spec((d, n_tot)),
            _const_spec((n_t, d)),
        ],
        out_specs=out_specs,
        out_shape=out_shape,
        compiler_params=_params(2),
        name="in_proj",
    )(xs[0], xs[1], mods, g, w, w_t)


def _gelu_tanh(x):
    return 0.5 * x * (1.0 + jnp.tanh(math.sqrt(2.0 / math.pi) * (x + 0.044715 * (x * x * x))))


def _merge_kernel(d, n_lat_tiles, x_ref, xc_ref, mod_ref, g_ref, yhy_ref, yhyc_ref, ygla_ref, ys5_ref, u_ref, wg_ref,
                  whb_ref, wgb_ref, wsb_ref, wo_ref, s5d_ref, wglu_ref, bglu_ref, o_ref):
    x = jnp.where(pl.program_id(1) < n_lat_tiles, x_ref[0], xc_ref[0])
    m = mod_ref[0, 0]
    h = _norm_mod(x, g_ref[...], m[:, 0:d], m[:, d:2 * d]).astype(BF16)
    y_hy = jnp.where(pl.program_id(1) < n_lat_tiles, yhy_ref[0], yhyc_ref[0])
    y5 = (ys5_ref[0] + s5d_ref[...] * u_ref[0]).T
    g5 = _gelu_tanh(y5)
    y_s5 = g5 * jax.nn.sigmoid(jnp.dot(g5.astype(BF16), wglu_ref[...], preferred_element_type=F32) + bglu_ref[...])

    def branch(k, y, wb_ref):
        gate = jnp.dot(h, wg_ref[:, k * d:(k + 1) * d], preferred_element_type=F32)
        return jax.nn.sigmoid(gate) * jnp.dot(y, wb_ref[...], preferred_element_type=F32)

    mix = branch(0, y_hy, whb_ref) + branch(1, ygla_ref[0], wgb_ref) + branch(2, y_s5.astype(BF16), wsb_ref)
    out = jnp.dot(mix.astype(BF16), wo_ref[...], preferred_element_type=F32)
    o_ref[0] = x + m[:, 2 * d:3 * d] * out


def _merge(xs, mods, g, y_hy, y_hy_ctx, y_gla, ys_s5, u_s5, wts, nj, n_lat_tiles):
    b, _, d = xs[0].shape
    tm = ROW_TILE
    ch = y_hy.shape[-1]
    row = lambda n: pl.BlockSpec((1, tm, n), lambda i, j: (i, j, 0))
    lat_row = pl.BlockSpec((1, tm, ch), lambda i, j: (i, jnp.minimum(j, n_lat_tiles - 1), 0))
    ctx_row = pl.BlockSpec((1, tm, ch), lambda i, j: (i, 0, 0))
    col = lambda n: pl.BlockSpec((1, n, tm), lambda i, j: (i, 0, j))
    w_gate, w_hy, w_gla, w_s5, w_out, s5_d, w_glu, b_glu = wts
    return pl.pallas_call(
        functools.partial(_merge_kernel, d, n_lat_tiles),
        grid=(b, nj),
        in_specs=[
            *_row_specs(xs, n_lat_tiles),
            _mod_spec(d, n_lat_tiles),
            _const_spec((1, d)),
            lat_row, ctx_row, row(ch), col(ch), col(ch),
            _const_spec(w_gate.shape), _const_spec(w_hy.shape), _const_spec(w_gla.shape), _const_spec(w_s5.shape),
            _const_spec(w_out.shape), _const_spec(s5_d.shape), _const_spec(w_glu.shape), _const_spec(b_glu.shape),
        ],
        out_specs=pl.BlockSpec((1, tm, d), lambda i, j: (i, j, 0)),
        out_shape=jax.ShapeDtypeStruct((b, nj * tm, d), F32),
        compiler_params=_params(2),
        name="merge",
    )(xs[0], xs[1], mods, g, y_hy, y_hy_ctx, y_gla, ys_s5, u_s5, w_gate, w_hy, w_gla, w_s5, w_out, s5_d, w_glu, b_glu)


def _ffn_kernel(d, final, x_ref, mod_ref, g_ref, wa_ref, wb_ref, wo_ref, gf_ref, o_ref):
    x = x_ref[0]
    m = mod_ref[0, 0]
    h = _norm_mod(x, g_ref[...], m[:, 3 * d:4 * d], m[:, 4 * d:5 * d]).astype(BF16)
    a = jnp.dot(h, wa_ref[...], preferred_element_type=F32)
    bb = jnp.dot(h, wb_ref[...], preferred_element_type=F32)
    act = (a * jax.nn.sigmoid(a) * bb).astype(BF16)
    y = x + m[:, 5 * d:6 * d] * jnp.dot(act, wo_ref[...], preferred_element_type=F32)
    if final:
        y = y * lax.rsqrt(jnp.mean(y * y, axis=-1, keepdims=True) + EPS) * gf_ref[...]
    o_ref[0] = y


def _ffn(x, mods, g, wa, wb, wo, g_final, n_lat_tiles, final):
    b, t, d = x.shape
    tm = ROW_TILE
    return pl.pallas_call(
        functools.partial(_ffn_kernel, d, final),
        grid=(b, t // tm),
        in_specs=[
            pl.BlockSpec((1, tm, d), lambda i, j: (i, j, 0)),
            _mod_spec(d, n_lat_tiles),
            _const_spec((1, d)),
            _const_spec(wa.shape), _const_spec(wb.shape), _const_spec(wo.shape),
            _const_spec((1, d)),
        ],
        out_specs=pl.BlockSpec((1, tm, d), lambda i, j: (i, j, 0)),
        out_shape=jax.ShapeDtypeStruct((b, t, d), F32),
        compiler_params=_params(2),
        name="ffn",
    )(x, mods, g, wa, wb, wo, g_final)


S5_CHUNK = LANES


def _s5_weights(a_re, a_im, log_dt, b_re, b_im, c_re, c_im, tc):
    g, p = a_re.shape[1:]
    i = b_re.shape[-1]
    lam_c = lax.complex(jnp.minimum(a_re, -1e-4), a_im)
    lam_dt = lam_c * jnp.exp(log_dt)[..., None]
    b_bar = ((jnp.exp(lam_dt) - 1.0) / lam_c)[..., None] * lax.complex(b_re, b_im)[None]
    c_mat = lax.complex(c_re, c_im)
    tau = jnp.arange(tc, dtype=F32)
    pw = jnp.exp(lam_dt[:, :, None, :] * tau[None, None, :, None])
    pw1 = pw * jnp.exp(lam_dt)[:, :, None, :]
    kern = jnp.einsum('gip,dgtp,dgpj->dgtij', c_mat, pw, b_bar).real
    k_lag = jnp.concatenate([kern[1][:, :0:-1], kern[0][:, :1] + kern[1][:, :1], kern[0][:, 1:]], axis=1)
    kr = jnp.pad(k_lag.transpose(0, 3, 2, 1), ((0, 0), (0, 0), (0, 0), (0, 1)))
    kr = kr.reshape(g, i, i * 2 * tc)

    inc_f = pw[0][:, None, ::-1, :] * b_bar[0].transpose(0, 2, 1)[:, :, None]
    inc_b = pw[1][:, None, :, :] * b_bar[1].transpose(0, 2, 1)[:, :, None]
    bm = jnp.concatenate([inc_f.real, inc_b.real, inc_f.imag, inc_b.imag], axis=-1).reshape(g, tc * i, 4 * p)

    out_f = c_mat.transpose(0, 2, 1)[..., None] * pw1[0].transpose(0, 2, 1)[:, :, None, :]
    out_b = c_mat.transpose(0, 2, 1)[..., None] * pw1[1][:, ::-1].transpose(0, 2, 1)[:, :, None, :]
    z = jnp.zeros((g, p, tc * i), F32)
    fl = lambda t: t.reshape(g, p, tc * i)
    cm = jnp.concatenate([fl(out_f.real), z, fl(-out_f.imag), z, z, fl(out_b.real), z, fl(-out_b.imag)], axis=1)

    lam_t = jnp.exp(lam_dt * float(tc))
    lam = jnp.stack([jnp.concatenate([lam_t[0].real, lam_t[1].real], -1),
                     jnp.concatenate([lam_t[0].imag, lam_t[1].imag], -1)], axis=1)
    return kr, bm.astype(BF16), cm.astype(BF16), lam


def _s5_kernel(nc, nc_ctx, tc, u_ref, kr_ref, bm_ref, cm_ref, lam_ref, y_ref, up_ref, dx_ref, p_ref):
    bn, i_sz = u_ref.shape[:2]
    nl = nc - nc_ctx
    width = kr_ref.shape[-1]
    seg = width // i_sz

    def operator_rows(j):
        lag_rows = pltpu.roll(jnp.broadcast_to(kr_ref[0, j:j + 1, :], (tc, width)), width - (tc - 1), axis=1,
                              stride=1, stride_axis=0)
        return jnp.concatenate([lag_rows[:, i * seg:i * seg + tc] for i in range(i_sz)], axis=1).astype(BF16)

    for n in range(nc):
        for i in range(i_sz):
            up_ref[n * bn:(n + 1) * bn, i * tc:(i + 1) * tc] = u_ref[:, i, n * tc:(n + 1) * tc]
    u = up_ref[...].astype(BF16)
    dx_ref[...] = jnp.dot(u, bm_ref[0], preferred_element_type=F32)
    lam = lam_ref[0]
    lr, li = lam[0:1], lam[1:2]
    w = lam.shape[-1]
    half = w // 2
    is_f = lax.broadcasted_iota(jnp.int32, (bn, w), 1) < half

    def step(s, carry):
        sr, si = carry
        nf = jnp.where(s < nc_ctx, nl + s, s - nc_ctx)
        nb = nc - 1 - s
        rf = pl.multiple_of(nf * bn, bn)
        rb = pl.multiple_of(nb * bn, bn)
        p_ref[pl.ds(rf, bn), 0:w] = sr
        p_ref[pl.ds(rf, bn), w:2 * w] = si
        p_ref[pl.ds(rb, bn), 2 * w:3 * w] = sr
        p_ref[pl.ds(rb, bn), 3 * w:4 * w] = si
        d_re = jnp.where(is_f, dx_ref[pl.ds(rf, bn), 0:w], dx_ref[pl.ds(rb, bn), 0:w])
        d_im = jnp.where(is_f, dx_ref[pl.ds(rf, bn), w:2 * w], dx_ref[pl.ds(rb, bn), w:2 * w])
        return lr * sr - li * si + d_re, lr * si + li * sr + d_im

    zero = jnp.zeros((bn, 2 * half), F32)
    lax.fori_loop(0, nc, step, (zero, zero))
    y = jnp.dot(p_ref[...].astype(BF16), cm_ref[0], preferred_element_type=F32)
    for j in range(0, i_sz, 2):
        m_j = jnp.concatenate([operator_rows(j), operator_rows(j + 1)], axis=0)
        y = y + jnp.dot(up_ref[:, j * tc:(j + 2) * tc].astype(BF16), m_j, preferred_element_type=F32)
    for n in range(nc):
        for i in range(i_sz):
            y_ref[:, i, n * tc:(n + 1) * tc] = y[n * bn:(n + 1) * bn, i * tc:(i + 1) * tc]


def _s5_mix(u_t, n_ctx, wts, layer):
    kr, bm, cm, lam = wts
    bn, ch, t = u_t.shape
    g, i = kr.shape[1:3]
    tc = kr.shape[3] // (2 * i)
    assert 2 * S5_STATE == LANES and bn % SUBLANES == 0 and n_ctx % tc == 0 and t % tc == 0 and tc % LANES == 0
    nc, nc_ctx = t // tc, n_ctx // tc
    k = tc * i
    r = nc * bn
    return pl.pallas_call(
        functools.partial(_s5_kernel, nc, nc_ctx, tc),
        grid=(g,),
        in_specs=[
            pl.BlockSpec((bn, i, t), lambda j: (0, j, 0)),
            pl.BlockSpec((None, 1, i, 2 * k), lambda j: (layer, j, 0, 0)),
            pl.BlockSpec((None, 1, k, 4 * S5_STATE), lambda j: (layer, j, 0, 0)),
            pl.BlockSpec((None, 1, 8 * S5_STATE, k), lambda j: (layer, j, 0, 0)),
            pl.BlockSpec((None, 1, 2, 2 * S5_STATE), lambda j: (layer, j, 0, 0)),
        ],
        out_specs=pl.BlockSpec((bn, i, t), lambda j: (0, j, 0)),
        out_shape=jax.ShapeDtypeStruct((bn, ch, t), F32),
        scratch_shapes=[pltpu.VMEM((r, k), F32), pltpu.VMEM((r, 4 * S5_STATE), F32),
                        pltpu.VMEM((r, 8 * S5_STATE), F32)],
        compiler_params=_params(1),
        name="s5_mix",
    )(u_t, kr, bm, cm, lam)


def _gla_kernel(nc, nc_ctx, k_ref, v_ref, a_ref, q_ref, r_ref, wa_ref, ba_ref, gn_ref, y_ref, of_ref, ob_ref,
                s_ref):
    c, h_n, dk, dv = GLA_CHUNK, GLA_HEADS, GLA_DK, GLA_DV
    qk = h_n * dk
    q_scale = dk ** -0.5
    nbk = GLA_BLOCK
    rb = nbk * c
    row_i = lax.broadcasted_iota(jnp.int32, (rb, rb), 0)
    col_i = lax.broadcasted_iota(jnp.int32, (rb, rb), 1)
    same = (row_i // c) == (col_i // c)
    causal = (same & (row_i >= col_i), same & (row_i <= col_i))
    cum_ops = [m.astype(BF16) for m in causal]
    att_mask = [jnp.concatenate([m] * h_n, axis=0) for m in causal]
    lane_head = lax.broadcasted_iota(jnp.int32, (1, qk), 1) // dk
    head_lanes = [lane_head == h for h in range(h_n)]

    def by_head(x):
        zero = jnp.zeros_like(x)
        return jnp.concatenate([jnp.where(m, x, zero) for m in head_lanes], axis=0)

    def block(j, d):
        rows = pl.ds(pl.multiple_of(j * rb, rb), rb)
        k = k_ref[0, rows, :].astype(F32)
        q = q_ref[0, rows, :].astype(F32) * q_scale
        v = v_ref[0, rows, :]
        z = jnp.dot(a_ref[0, rows, :], wa_ref[:, d * qk:(d + 1) * qk], preferred_element_type=F32) + ba_ref[d:d + 1, :]
        log_a = (jnp.minimum(z, 0.0) - jnp.log(1.0 + jnp.exp(-jnp.abs(z)))) * (1.0 / GLA_TAU)
        hi = log_a.astype(BF16)
        lo = (log_a - hi.astype(F32)).astype(BF16)
        cs = jnp.dot(cum_ops[d], jnp.concatenate([hi, lo], axis=1), preferred_element_type=F32)
        b = cs[:, :qk] + cs[:, qk:]
        b_tot = jnp.concatenate(
            [jnp.broadcast_to(b[n * c + (c - 1 if d == 0 else 0)][None], (c, qk)) for n in range(nbk)], axis=0)
        qd = (q * jnp.exp(b)).astype(BF16)
        kd = (k * jnp.exp(-b)).astype(BF16)
        kl = (k * jnp.exp(b_tot - b)).astype(BF16)
        qm = by_head(qd)
        intra = []
        for h in range(h_n):
            att = lax.dot_general(qm[h * rb:(h + 1) * rb], kd, _NT, preferred_element_type=F32)
            att = jnp.where(causal[d], att, 0.0).astype(BF16)
            intra.append(jnp.dot(att, v[:, h * dv:(h + 1) * dv], preferred_element_type=F32))
        s_t = s_ref[d]
        inter = [None] * nbk
        for n in (range(nbk) if d == 0 else reversed(range(nbk))):
            r0 = n * c
            qm_n = jnp.concatenate([qm[h * rb + r0:h * rb + r0 + c] for h in range(h_n)], axis=0)
            inter[n] = lax.dot_general(qm_n, s_t.astype(BF16), _NT, preferred_element_type=F32)
            v_t = jnp.concatenate(
                [jnp.concatenate([v[r0:r0 + c, h * dv:(h + 1) * dv] for h in range(p, p + dv // c)], axis=0).T
                 for p in range(0, h_n, dv // c)], axis=1)
            s_t = s_t * jnp.exp(b_tot[r0:r0 + 1]) + jnp.dot(v_t, by_head(kl[r0:r0 + c]), preferred_element_type=F32)
        s_ref[d] = s_t
        o = jnp.concatenate(
            [intra[h] + jnp.concatenate([inter[n][h * c:(h + 1) * c] for n in range(nbk)], axis=0)
             for h in range(h_n)], axis=1)
        return rows, o

    def readout(rows, o):
        r = r_ref[0, rows, :].astype(F32)
        gate = r * jax.nn.sigmoid(r)
        outs = []
        for h in range(h_n):
            oh = o[:, h * dv:(h + 1) * dv]
            oh = oh * lax.rsqrt(jnp.mean(oh * oh, axis=-1, keepdims=True) + EPS) * gn_ref[...]
            outs.append(oh * gate[:, h * dv:(h + 1) * dv])
        y_ref[0, rows, :] = jnp.concatenate(outs, axis=1).astype(y_ref.dtype)

    def scan_step(i, carry, lo_block, hi_block, phase):
        rows_f, o_f = block(lo_block + i, 0)
        if phase == "park":
            of_ref[rows_f, :] = o_f
        rows_b, o_b = block(hi_block - 1 - i, 1)
        if phase == "park":
            ob_ref[rows_b, :] = o_b
        elif phase == "meet":
            readout(rows_f, o_f + o_b)
        else:
            readout(rows_f, o_f + ob_ref[rows_f, :])
            readout(rows_b, o_b + of_ref[rows_b, :])
        return carry

    def scan(lo_block, n_blocks):
        step = functools.partial(scan_step, lo_block=lo_block, hi_block=lo_block + n_blocks)
        half = n_blocks // 2
        lax.fori_loop(0, half, functools.partial(step, phase="park"), 0, unroll=GLA_UNROLL)
        if n_blocks % 2:
            step(jnp.int32(half), 0, phase="meet")
        lax.fori_loop(n_blocks - half, n_blocks, functools.partial(step, phase="finish"), 0, unroll=GLA_UNROLL)

    nl, n_ctx = (nc - nc_ctx) // nbk, nc_ctx // nbk
    s_ref[...] = jnp.zeros(s_ref.shape, F32)
    scan(nl, n_ctx)
    scan(0, nl)


def _gla_mix(k, v, a, q, r, wa, ba, gnorm, n_ctx):
    bn, t, qk = k.shape
    vd = v.shape[-1]
    c = GLA_CHUNK
    assert t % (c * GLA_BLOCK) == 0 and n_ctx % (c * GLA_BLOCK) == 0 and GLA_DV % c == 0
    seq = lambda n, bufs=2: pl.BlockSpec((1, t, n), lambda i: (i, 0, 0), pipeline_mode=pl.Buffered(bufs))
    return pl.pallas_call(
        functools.partial(_gla_kernel, t // c, n_ctx // c),
        grid=(bn,),
        in_specs=[seq(qk), seq(vd), seq(a.shape[-1]), seq(qk), seq(vd, 1),
                  _const_spec(wa.shape), _const_spec(ba.shape), _const_spec(gnorm.shape)],
        out_specs=pl.BlockSpec((1, t, vd), lambda i: (i, 0, 0)),
        out_shape=jax.ShapeDtypeStruct((bn, t, vd), BF16),
        scratch_shapes=[pltpu.VMEM((t, vd), F32), pltpu.VMEM((t, vd), F32), pltpu.VMEM((2, GLA_DV, qk), F32)],
        compiler_params=_params(1),
        name="gla_mix",
    )(k, v, a, q, r, wa, ba, gnorm)


HY_NB = LANES
HY_XPITCH = HY_NB + SUBLANES
HY_APITCH = 2 * HY_NB + SUBLANES
HY_CT = LANES
HY_UNROLL_OUTER = 32
HY_UNROLL_INNER = 16
HY_UNROLL_ROWS = 4


def _hyena_filters(n, pos, params):
    w1, b1, w2, b2, w3, freq1, freq2, decay = (p.astype(F32) for p in params)
    t = (pos / (n - 1))[:, None]
    bands = jnp.linspace(1e-4, HY_POS_BANDS - 1, HY_POS_BANDS, dtype=F32)
    ang = (2.0 * math.pi / n) * pos[:, None] * bands[None]
    z = jnp.concatenate([t, jnp.cos(ang), -jnp.sin(ang)], axis=-1)
    hid = jnp.sin(freq1 * (z @ w1 + b1))
    hid = jnp.sin(freq2 * (hid @ w2 + b2))
    w3 = w3.reshape(w3.shape[0], HY_N_FILT, -1)
    window = jnp.exp(-t[None] * jnp.abs(decay).reshape(HY_N_FILT, 1, -1))
    return jnp.einsum('nk,kfc->fnc', hid, w3) * window


def _hyena_kernels(n, params):
    slot = jnp.arange(2 * n)
    h = _hyena_filters(n, jnp.where(slot < n, slot, 2 * n - slot).astype(F32), params)
    is_fwd, is_bwd = (slot < n)[None, :, None], (slot > n)[None, :, None]
    k = jnp.where(is_fwd, h[0::2], 0.0) + jnp.where(is_bwd, h[1::2], 0.0)
    lag0_bwd = _hyena_filters(n, jnp.zeros((1,), F32), params)[1::2]
    k = k + jnp.where(slot[None, :, None] == 0, lag0_bwd, 0.0)
    return k * lax.rsqrt(jnp.sum(k * k, axis=1, keepdims=True) + EPS)


def _spectrum_kernel(na, k_ref, f1_ref, f2_ref, o_ref, a2):
    nb, ap = HY_NB, HY_APITCH

    def stage1(b, c):
        xs = k_ref[0, pl.ds(b, na, stride=nb), :].astype(BF16)
        r = jnp.dot(f1_ref[b], xs, preferred_element_type=F32)
        a2[pl.ds(b, na, stride=ap), :] = r[:na]
        a2[pl.ds(nb + b, na, stride=ap), :] = r[na:]
        return c

    def stage2(ka, c):
        rows = pl.ds(pl.multiple_of(ka * ap, 8), 2 * nb)
        xf = jnp.dot(f2_ref[...], a2[rows, :].astype(BF16), preferred_element_type=F32)
        o_ref[0, ka, 0] = xf[:nb].astype(o_ref.dtype)
        o_ref[0, ka, 1] = xf[nb:].astype(o_ref.dtype)
        return c

    lax.fori_loop(0, nb, stage1, 0, unroll=HY_UNROLL_INNER)
    lax.fori_loop(0, na, stage2, 0, unroll=HY_UNROLL_INNER)


def _hyena_spectra(kernels):
    nf, n, ch = kernels.shape
    nb, ct = HY_NB, HY_CT
    na = n // nb
    _, f2, _, _, f1_full = _dft_tables(n // 2)
    return pl.pallas_call(
        functools.partial(_spectrum_kernel, na),
        grid=(nf, ch // ct),
        in_specs=[pl.BlockSpec((1, n, ct), lambda f, j: (f, 0, j)), _const_spec(f1_full.shape), _const_spec(f2.shape)],
        out_specs=pl.BlockSpec((1, na, 2, nb, ct), lambda f, j: (f, 0, 0, 0, j)),
        out_shape=jax.ShapeDtypeStruct((nf, na, 2, nb, ch), BF16),
        scratch_shapes=[pltpu.VMEM((na * HY_APITCH, ct), F32)],
        compiler_params=_params(2),
        name="hyena_spectrum",
    )(kernels, f1_full, f2)


def _cplx_block(m):
    return np.block([[m.real, -m.imag], [m.imag, m.real]])


@functools.lru_cache(maxsize=None)
def _dft_tables(n_seq):
    nb = HY_NB
    n = 2 * n_seq
    na = n // nb
    ha = na // 2
    ka = np.arange(na)[:, None]
    b = np.arange(nb)
    tw = np.exp(-2j * np.pi * ka * b[None, :] / n)
    w_a = np.exp(-2j * np.pi * ka * np.arange(ha)[None, :] / na)
    f1 = np.stack([_cplx_block(tw[:, i:i + 1] * w_a) for i in range(nb)])
    g = np.stack([_cplx_block((np.conj(tw[:, i:i + 1] * w_a)).T / n) for i in range(nb)])
    w_b = np.exp(-2j * np.pi * b[:, None] * b[None, :] / nb)
    f2 = _cplx_block(w_b)
    f2i = _cplx_block(np.conj(w_b))
    w_full = np.exp(-2j * np.pi * ka * np.arange(na)[None, :] / na)
    f1_full = np.stack([np.concatenate([(tw[:, i:i + 1] * w_full).real, (tw[:, i:i + 1] * w_full).imag], axis=0)
                        for i in range(nb)])
    return tuple(jnp.asarray(t, dtype=BF16) for t in (f1, f2, f2i, g, f1_full))


def _hyena_kernel(na, x1_ref, x2_ref, v_ref, cw_ref, bias_ref, kf1_ref, kf2_ref, f1_ref, f2_ref, f2i_ref, g_ref,
                  o_ref, rawf, vp, gp, yp, a2):
    nb, xp, ap = HY_NB, HY_XPITCH, HY_APITCH
    ha = na // 2
    n_seq = ha * nb
    ct = o_ref.shape[-1]
    halo = SUBLANES
    rawf[:, 0:halo, :] = jnp.zeros((2, halo, ct), F32)
    rawf[:, halo + n_seq:2 * halo + n_seq, :] = jnp.zeros((2, halo, ct), F32)

    def short_conv_into(raw_ref, which, dst):
        w0, w1, w2 = (cw_ref[j, which:which + 1, :] for j in range(HY_SHORT))
        for e in range(2):
            rawf[e, halo:halo + n_seq, :] = raw_ref[e].astype(F32)

            def body(a, c):
                base = pl.multiple_of(a * nb, nb) + halo
                u = (w0 * rawf[e, pl.ds(base - 1, nb), :] + w1 * rawf[e, pl.ds(base, nb), :]
                     + w2 * rawf[e, pl.ds(base + 1, nb), :])
                dst[e, pl.ds(pl.multiple_of(a * xp, 8), nb), :] = u
                return c

            lax.fori_loop(0, ha, body, 0, unroll=HY_UNROLL_ROWS)

    def long_conv(kf_ref):
        def stage1(b, c):
            xs = jnp.concatenate([vp[0, pl.ds(b, ha, stride=xp), :], vp[1, pl.ds(b, ha, stride=xp), :]], axis=0)
            r = jnp.dot(f1_ref[b], xs.astype(BF16), preferred_element_type=F32)
            a2[pl.ds(b, na, stride=ap), :] = r[:na]
            a2[pl.ds(nb + b, na, stride=ap), :] = r[na:]
            return c

        def stage2(kp, c):
            rows = [pl.ds(pl.multiple_of((2 * kp + e) * ap, 8), 2 * nb) for e in range(2)]
            slab = jnp.concatenate([a2[r, :] for r in rows], axis=1).astype(BF16)
            xf = jnp.dot(f2_ref[...], slab, preferred_element_type=F32)
            xr, xi = xf[:nb], xf[nb:]
            kr = jnp.concatenate([kf_ref[0, 2 * kp + e, 0] for e in range(2)], axis=1).astype(F32)
            ki = jnp.concatenate([kf_ref[0, 2 * kp + e, 1] for e in range(2)], axis=1).astype(F32)
            z = jnp.concatenate([xr * kr - xi * ki, xr * ki + xi * kr], axis=0).astype(BF16)
            back = jnp.dot(f2i_ref[...], z, preferred_element_type=F32)
            for e in range(2):
                a2[rows[e], :] = back[:, e * ct:(e + 1) * ct]
            return c

        def stage3(b, c):
            s = jnp.concatenate([a2[pl.ds(b, na, stride=ap), :], a2[pl.ds(nb + b, na, stride=ap), :]], axis=0)
            y = jnp.dot(g_ref[b], s.astype(BF16), preferred_element_type=F32)
            yp[0, pl.ds(b, ha, stride=xp), :] = y[:ha]
            yp[1, pl.ds(b, ha, stride=xp), :] = y[ha:]
            return c

        lax.fori_loop(0, nb, stage1, 0, unroll=HY_UNROLL_OUTER)
        lax.fori_loop(0, na // 2, stage2, 0, unroll=HY_UNROLL_INNER)
        lax.fori_loop(0, nb, stage3, 0, unroll=HY_UNROLL_OUTER)

    def gated(bias_row, write):
        for e in range(2):
            def body(a, c):
                rows = pl.ds(pl.multiple_of(a * xp, 8), nb)
                write(e, a, rows, gp[e, rows, :] * (yp[e, rows, :] + bias_row * vp[e, rows, :]))
                return c

            lax.fori_loop(0, ha, body, 0, unroll=HY_UNROLL_ROWS)

    def write_z(e, a, rows, val):
        vp[e, rows, :] = val

    def write_out(e, a, rows, val):
        o_ref[e, pl.ds(pl.multiple_of(a * nb, nb), nb), :] = val.astype(o_ref.dtype)

    short_conv_into(v_ref, 2, vp)
    short_conv_into(x1_ref, 0, gp)
    long_conv(kf1_ref)
    gated(bias_ref[0:1, :], write_z)
    short_conv_into(x2_ref, 1, gp)
    long_conv(kf2_ref)
    gated(bias_ref[1:2, :], write_out)


def _hyena_latent(hy_all, n_seq, conv_w, bias, kf):
    bn = hy_all.shape[0]
    ch = hy_all.shape[-1] // 3
    ct = HY_CT
    nct = ch // ct
    nb = HY_NB
    na = 2 * n_seq // nb
    assert bn % 2 == 0 and ch % ct == 0 and n_seq % (8 * nb) == 0
    f1, f2, f2i, g, _ = _dft_tables(n_seq)
    col = lambda which: pl.BlockSpec((2, n_seq, ct), lambda j, p: (p, 0, which * nct + j))
    kf_spec = lambda f: pl.BlockSpec((1, na, 2, nb, ct), lambda j, p: (f, 0, 0, 0, j), pipeline_mode=pl.Buffered(1))
    pad_rows = (na // 2) * HY_XPITCH
    return pl.pallas_call(
        functools.partial(_hyena_kernel, na),
        grid=(nct, bn // 2),
        in_specs=[col(0), col(1), col(2),
                  pl.BlockSpec((HY_SHORT, 3, ct), lambda j, p: (0, 0, j)),
                  pl.BlockSpec((2, ct), lambda j, p: (0, j)),
                  kf_spec(0), kf_spec(1),
                  _const_spec(f1.shape), _const_spec(f2.shape), _const_spec(f2i.shape), _const_spec(g.shape)],
        out_specs=pl.BlockSpec((2, n_seq, ct), lambda j, p: (p, 0, j)),
        out_shape=jax.ShapeDtypeStruct((bn, n_seq, ch), BF16),
        scratch_shapes=[pltpu.VMEM((2, n_seq + 2 * SUBLANES, ct), F32), pltpu.VMEM((2, pad_rows, ct), F32),
                        pltpu.VMEM((2, pad_rows, ct), F32), pltpu.VMEM((2, pad_rows, ct), F32),
                        pltpu.VMEM((na * HY_APITCH, ct), F32)],
        compiler_params=_params(2),
        name="hyena_latent",
    )(hy_all, hy_all, hy_all, conv_w.reshape(HY_SHORT, 3, ch), bias, kf, kf, f1, f2, f2i, g)


@functools.lru_cache(maxsize=None)
def _dense_dft_tables(n_seq):
    n = 2 * n_seq
    ang = 2.0 * np.pi * np.arange(n)[:, None] * np.arange(n_seq)[None, :] / n
    fd = np.concatenate([np.cos(ang), -np.sin(ang)], axis=0)
    gd = np.concatenate([np.cos(ang).T, -np.sin(ang).T], axis=1) / n
    return jnp.asarray(fd, dtype=BF16), jnp.asarray(gd, dtype=BF16)


def _hyena_ctx_kernel(n_seq, x1_ref, x2_ref, v_ref, cw_ref, bias_ref, kf1_ref, kf2_ref, fd_ref, gd_ref, o_ref):
    n = 2 * n_seq

    def short_conv(raw_ref, which):
        x = raw_ref[0].astype(F32)
        zero = jnp.zeros((1, x.shape[-1]), F32)
        prev = jnp.concatenate([zero, x[:-1]], axis=0)
        nxt = jnp.concatenate([x[1:], zero], axis=0)
        return cw_ref[0, which:which + 1, :] * prev + cw_ref[1, which:which + 1, :] * x + cw_ref[2, which:which + 1, :] * nxt

    def long_conv(u, kf_ref, bias_row):
        xf = jnp.dot(fd_ref[...], u.astype(BF16), preferred_element_type=F32)
        xr, xi = xf[:n], xf[n:]
        kr, ki = kf_ref[0], kf_ref[1]
        z = jnp.concatenate([xr * kr - xi * ki, xr * ki + xi * kr], axis=0).astype(BF16)
        return jnp.dot(gd_ref[...], z, preferred_element_type=F32) + bias_row * u

    x1, x2, v = short_conv(x1_ref, 0), short_conv(x2_ref, 1), short_conv(v_ref, 2)
    z = x1 * long_conv(v, kf1_ref, bias_ref[0:1, :])
    o_ref[0] = (x2 * long_conv(z, kf2_ref, bias_ref[1:2, :])).astype(o_ref.dtype)


def _hyena_context(hy_all, row_block, n_seq, conv_w, bias, kf1, kf2):
    bn = hy_all.shape[0]
    ch = hy_all.shape[-1] // 3
    fd, gd = _dense_dft_tables(n_seq)
    col = lambda which: pl.BlockSpec((1, n_seq, ch), lambda i: (i, row_block, which))
    split = lambda kf: jnp.stack([kf.real, kf.imag], axis=0)
    return pl.pallas_call(
        functools.partial(_hyena_ctx_kernel, n_seq),
        grid=(bn,),
        in_specs=[col(0), col(1), col(2), _const_spec((HY_SHORT, 3, ch)), _const_spec((2, ch)),
                  _const_spec((2, 2 * n_seq, ch)), _const_spec((2, 2 * n_seq, ch)),
                  _const_spec(fd.shape), _const_spec(gd.shape)],
        out_specs=pl.BlockSpec((1, n_seq, ch), lambda i: (i, 0, 0)),
        out_shape=jax.ShapeDtypeStruct((bn, n_seq, ch), BF16),
        compiler_params=_params(1),
        name="hyena_context",
    )(hy_all, hy_all, hy_all, conv_w.reshape(HY_SHORT, 3, ch), bias, split(kf1), split(kf2), fd, gd)


def kernel(x, c, ctx, c_ctx, w_mod, b_mod, g_norm1, g_norm2, w_in, hy_conv, hy_w1, hy_b1, hy_w2, hy_b2, hy_w3,
           hy_freq1, hy_freq2, hy_decay, hy_bias, gla_wa2, gla_ba, gla_gnorm, s5_a_re, s5_a_im, s5_log_dt, s5_b_re,
           s5_b_im, s5_c_re, s5_c_im, s5_d, s5_w_glu, s5_b_glu, w_br_hy, w_br_gla, w_br_s5, w_out, w_ffn_in,
           w_ffn_out, g_final):
    bn, n_lat, d = x.shape
    n_ctx = ctx.shape[1]
    depth = w_in.shape[0]
    d_hy = w_br_hy.shape[1]
    s5_ch = w_br_s5.shape[1]
    d_ff = w_ffn_out.shape[1]
    assert n_ctx == ROW_TILE and n_lat % ROW_TILE == 0
    n_lat_tiles = n_lat // ROW_TILE
    n_low = 2 * GLA_LOWRANK
    o_a = GLA_QK + GLA_V
    o_u = o_a + n_low
    o_q = o_u + s5_ch
    o_gate = o_q + GLA_QK + GLA_V + 3 * d_hy
    col_sizes = (GLA_QK, GLA_V, A_PAD, GLA_QK, GLA_V, 3 * d_hy)

    cond = jnp.concatenate([c_ctx[None], c], axis=0)
    cond = jnp.pad(cond, ((0, (-cond.shape[0]) % SUBLANES), (0, 0)))
    mod_all = _modulation_all(cond, w_mod, b_mod)

    s5w_all = jax.vmap(functools.partial(_s5_weights, tc=S5_CHUNK))(
        s5_a_re, s5_a_im, s5_log_dt, s5_b_re, s5_b_im, s5_c_re, s5_c_im)
    wa_all = jnp.zeros((depth, A_PAD, 2 * GLA_QK), F32)
    wa_all = wa_all.at[:, :GLA_LOWRANK, :GLA_QK].set(gla_wa2[:, 0]).at[:, GLA_LOWRANK:n_low, GLA_QK:].set(gla_wa2[:, 1])
    wa_all = wa_all.astype(BF16)

    xs = (x, ctx, 0)
    for l in range(depth):
        last = l == depth - 1
        mods = jnp.stack([jnp.broadcast_to(mod_all[l, 0], (bn, 6 * d)), mod_all[l, 1:1 + bn]], axis=1)
        mods = mods[:, :, None, :]
        wl = w_in[l]
        w_proj = jnp.concatenate(
            [wl[:, :o_u], jnp.zeros((d, A_PAD - n_low), F32), wl[:, o_q:o_gate]], axis=1).astype(BF16)
        g1 = g_norm1[l][None]
        w_u_t = wl[:, o_u:o_q].T.astype(BF16)
        k_a, v_a, a_a, q_a, r_a, hy_a, u_t = _project(xs, mods, g1, w_proj, col_sizes, w_u_t, n_lat_tiles)

        y_gla = _gla_mix(k_a, v_a, a_a, q_a, r_a, wa_all[l], gla_ba[l], gla_gnorm[l][None], n_ctx)

        ys = _s5_mix(u_t, n_ctx, s5w_all, l)

        hy_p = (hy_w1[l], hy_b1[l], hy_w2[l], hy_b2[l], hy_w3[l], hy_freq1[l], hy_freq2[l], hy_decay[l])
        y_hy = _hyena_latent(hy_a, n_lat, hy_conv[l], hy_bias[l], _hyena_spectra(_hyena_kernels(n_lat, hy_p)))
        if last:
            y_hy_ctx = y_hy
        else:
            kf_ctx = jnp.fft.fft(_hyena_kernels(n_ctx, hy_p), axis=1)
            y_hy_ctx = _hyena_context(hy_a, n_lat // n_ctx, n_ctx, hy_conv[l], hy_bias[l], kf_ctx[0], kf_ctx[1])

        wts = (wl[:, o_gate:].astype(BF16), w_br_hy[l].astype(BF16), w_br_gla[l].astype(BF16),
               w_br_s5[l].astype(BF16), w_out[l].astype(BF16), s5_d[l][:, None], s5_w_glu[l].astype(BF16),
               s5_b_glu[l][None])
        n_tiles = n_lat_tiles if last else n_lat_tiles + n_ctx // ROW_TILE
        x_mid = _merge(xs, mods, g1, y_hy, y_hy_ctx, y_gla, ys, u_t, wts, n_tiles, n_lat_tiles)
        wf = w_ffn_in[l]
        xc = _ffn(x_mid, mods, g_norm2[l][None], wf[:, :d_ff].astype(BF16), wf[:, d_ff:].astype(BF16),
                  w_ffn_out[l].astype(BF16), g_final[None], n_lat_tiles, last)
        xs = (xc, xc, n_lat_tiles)
    return xc
```
